```python
import jax, jax.numpy as jnp
from jax import lax
import numpy as np

D_MODEL = 1024
BATCH = 8
SEQ = 4096
DEPTH = 2

PL_DIM = 256
EPS = 1e-6
A_WIDTH = 512
A_GROUPS = 8
CONV_W = 3
SB_HEADS = 8
SB_HEAD_DIM = 64
SB_WIDTH = SB_HEADS * SB_HEAD_DIM
Q_BLOCK = 128
IN_EVEN = 3 * A_WIDTH + 3 * SB_WIDTH
MIX_EVEN = A_WIDTH + SB_WIDTH
C_WIDTH = 2048
C_GROUPS = 8
C_GROUP_DIM = C_WIDTH // C_GROUPS
CHUNK = 128
N_GROUPS = 4
EXP_PER_GROUP = 8
N_EXPERTS = N_GROUPS * EXP_PER_GROUP
TOP_K = 2
D_EXPERT = 256
N_EVEN = (DEPTH + 1) // 2
N_ODD = DEPTH // 2

kernel_name = "hybrid_conv_stickbreak_gmlp_hmoe"


def rmsnorm(x, g):
    xf = x.astype(jnp.float32)
    y = xf * lax.rsqrt(jnp.mean(xf * xf, axis=-1, keepdims=True) + EPS)
    return (y * g.astype(jnp.float32)).astype(x.dtype)


def short_conv_mixer(h, gb, gc, conv_w):
    z = gc * h
    kern = conv_w[:, None, :].astype(z.dtype)
    y = lax.conv_general_dilated(z, kern, window_strides=(1,), padding=[(CONV_W - 1, 0)],
                                 dimension_numbers=('NWC', 'WIO', 'NWC'),
                                 feature_group_count=A_WIDTH)
    return gb * y


def stick_breaking_attention(q, k, v):
    S = q.shape[1]
    scale = SB_HEAD_DIM ** -0.5
    outs = []
    for blk in range(S // Q_BLOCK):
        t0 = blk * Q_BLOCK
        L = t0 + Q_BLOCK
        z = jnp.einsum('bqhd,bkhd->bhqk', q[:, t0:L], k[:, :L],
                       preferred_element_type=jnp.float32) * scale
        t_idx = t0 + jnp.arange(Q_BLOCK)[:, None]
        s_idx = jnp.arange(L)[None, :]
        mask = s_idx < t_idx
        log_beta = jax.nn.log_sigmoid(z)
        log_rem = jnp.where(mask, jax.nn.log_sigmoid(-z), 0.0)
        later = lax.cumsum(log_rem, axis=3, reverse=True) - log_rem
        a = jnp.where(mask, jnp.exp(log_beta + later), 0.0)
        outs.append(jnp.einsum('bhqk,bkhd->bqhd', a.astype(v.dtype), v[:, :L]))
    return jnp.concatenate(outs, axis=1)


def even_mixer(xn, w_in, conv_w, w_out):
    B, S, _ = xn.shape
    proj = xn @ w_in
    cuts = [A_WIDTH, 2 * A_WIDTH, 3 * A_WIDTH, 3 * A_WIDTH + SB_WIDTH, 3 * A_WIDTH + 2 * SB_WIDTH]
    h, gb, gc, q, k, v = jnp.split(proj, cuts, axis=-1)
    y_a = short_conv_mixer(h, gb, gc, conv_w)
    shp = (B, S, SB_HEADS, SB_HEAD_DIM)
    y_b = stick_breaking_attention(q.reshape(shp), k.reshape(shp), v.reshape(shp))
    y = jnp.concatenate([y_a, y_b.reshape(B, S, SB_WIDTH)], axis=-1)
    return y @ w_out


def odd_mixer(xn, w_in, g_v, w_s, b_s, w_out):
    B, S, _ = xn.shape
    zc = jax.nn.gelu(xn @ w_in)
    u, v = jnp.split(zc, 2, axis=-1)
    v = rmsnorm(v, g_v)
    n_chunks = S // CHUNK
    v = v.reshape(B, n_chunks, CHUNK, C_GROUPS, C_GROUP_DIM)
    causal = jnp.tril(jnp.ones((CHUNK, CHUNK), dtype=w_s.dtype))
    w_m = w_s * causal
    gate = jnp.einsum('gts,bcsgd->bctgd', w_m, v) + b_s.T[None, None, :, :, None]
    y = u * gate.reshape(B, S, C_WIDTH)
    return y @ w_out


def hier_moe(xn, wc, bc, wf, bf, w1, w3, w2):
    B, S, D = xn.shape
    T = B * S
    xt = xn.reshape(T, D)
    pc = jax.nn.softmax((xt @ wc + bc).astype(jnp.float32), axis=-1)
    pg, gi = lax.top_k(pc, 1)
    lf = (xt @ wf + bf).astype(jnp.float32).reshape(T, N_GROUPS, EXP_PER_GROUP)
    idx = jnp.broadcast_to(gi[:, :, None], (T, 1, EXP_PER_GROUP))
    lf_sel = jnp.take_along_axis(lf, idx, axis=1)[:, 0]
    pf = jax.nn.softmax(lf_sel, axis=-1)
    wtop, ei = lax.top_k(pf, TOP_K)
    wtop = wtop / jnp.sum(wtop, axis=-1, keepdims=True)
    gates = pg * wtop
    eid = gi * EXP_PER_GROUP + ei
    comb = jnp.sum(jax.nn.one_hot(eid, N_EXPERTS, dtype=jnp.float32) * gates[..., None], axis=1)
    comb = comb.astype(xt.dtype)
    y = jnp.zeros_like(xt)
    for e in range(N_EXPERTS):
        he = jax.nn.silu(xt @ w1[e]) * (xt @ w3[e])
        y = y + comb[:, e:e + 1] * (he @ w2[e])
    return y.reshape(B, S, D)


def _nrm(k, shape, scale):
    return jax.random.normal(k, shape, jnp.float32) * scale


def _gain(k, shape):
    return 1.0 + 0.02 * jax.random.normal(k, shape, jnp.float32)


def setup_inputs(seed: int = 0) -> dict:
    key = jax.random.key(seed)
    ks = jax.random.split(key, 24)
    D = D_MODEL
    return {
        "x": _nrm(ks[0], (BATCH, SEQ, D), 1.0),
        "p": _nrm(ks[1], (DEPTH, BATCH, SEQ, PL_DIM), 1.0),
        "norm_mix": _gain(ks[2], (DEPTH, D)),
        "norm_ffn": _gain(ks[3], (DEPTH, D)),
        "norm_pl": _gain(ks[4], (DEPTH, D)),
        "final_norm": _gain(ks[5], (D,)),
        "w_in_even": _nrm(ks[6], (N_EVEN, D, IN_EVEN), D ** -0.5),
        "conv_w_even": _nrm(ks[7], (N_EVEN, CONV_W, A_WIDTH), CONV_W ** -0.5),
        "w_out_even": _nrm(ks[8], (N_EVEN, MIX_EVEN, D), MIX_EVEN ** -0.5),
        "w_in_odd": _nrm(ks[9], (N_ODD, D, 2 * C_WIDTH), D ** -0.5),
        "g_v_odd": _gain(ks[10], (N_ODD, C_WIDTH)),
        "w_s_odd": _nrm(ks[11], (N_ODD, C_GROUPS, CHUNK, CHUNK), CHUNK ** -0.5),
        "b_s_odd": 1.0 + _nrm(ks[12], (N_ODD, C_GROUPS, CHUNK), 0.02),
        "w_out_odd": _nrm(ks[13], (N_ODD, C_WIDTH, D), C_WIDTH ** -0.5),
        "router_c": _nrm(ks[14], (DEPTH, D, N_GROUPS), D ** -0.5),
        "router_c_b": _nrm(ks[15], (DEPTH, N_GROUPS), 0.01),
        "router_f": _nrm(ks[16], (DEPTH, D, N_EXPERTS), D ** -0.5),
        "router_f_b": _nrm(ks[17], (DEPTH, N_EXPERTS), 0.01),
        "moe_w1": _nrm(ks[18], (DEPTH, N_EXPERTS, D, D_EXPERT), D ** -0.5),
        "moe_w3": _nrm(ks[19], (DEPTH, N_EXPERTS, D, D_EXPERT), D ** -0.5),
        "moe_w2": _nrm(ks[20], (DEPTH, N_EXPERTS, D_EXPERT, D), D_EXPERT ** -0.5),
        "w_pe": _nrm(ks[21], (DEPTH, PL_DIM, D), PL_DIM ** -0.5),
        "w_pg": _nrm(ks[22], (DEPTH, D, D), D ** -0.5),
    }


def reference(x, p, norm_mix, norm_ffn, norm_pl, final_norm,
              w_in_even, conv_w_even, w_out_even,
              w_in_odd, g_v_odd, w_s_odd, b_s_odd, w_out_odd,
              router_c, router_c_b, router_f, router_f_b,
              moe_w1, moe_w3, moe_w2, w_pe, w_pg):
    h = x
    for i in range(DEPTH):
        j = i // 2
        xn = rmsnorm(h, norm_mix[i])
        if i % 2 == 0:
            h = h + even_mixer(xn, w_in_even[j], conv_w_even[j], w_out_even[j])
        else:
            h = h + odd_mixer(xn, w_in_odd[j], g_v_odd[j], w_s_odd[j], b_s_odd[j], w_out_odd[j])
        xn = rmsnorm(h, norm_ffn[i])
        h = h + hier_moe(xn, router_c[i], router_c_b[i], router_f[i], router_f_b[i],
                         moe_w1[i], moe_w3[i], moe_w2[i])
        gate = jax.nn.sigmoid(rmsnorm(h, norm_pl[i]) @ w_pg[i])
        h = h + gate * (p[i] @ w_pe[i])
    return rmsnorm(h, final_norm)
```

```python
import functools

import jax
import jax.numpy as jnp
from jax import lax
from jax.experimental import pallas as pl
from jax.experimental.pallas import tpu as pltpu

EPS = 1e-6
CONV_W = 3
HEAD_DIM = 64
HEADS_PER_SLAB = 2
CHUNK = 128
N_GROUPS = 4
EXP_PER_GROUP = 8
N_EXPERTS = N_GROUPS * EXP_PER_GROUP
ROUTER_LANES = 128
FINE_ROW0 = 8

V7X_VMEM_BYTES = 64 * 1024 * 1024
VMEM_LIMIT = V7X_VMEM_BYTES - 8 * 1024 * 1024
LOG_F32_TINY_BOUND = -88.0

F32 = jnp.float32
BF16 = jnp.bfloat16


def _rms(x, g):
    return x * lax.rsqrt(jnp.mean(x * x, axis=-1, keepdims=True) + EPS) * g


def _const_spec(shape):
    nd = len(shape)
    return pl.BlockSpec(shape, lambda *_: (0,) * nd)


def _params(*sem):
    return pltpu.CompilerParams(dimension_semantics=sem, vmem_limit_bytes=VMEM_LIMIT)


def _even_in_kernel(h_ref, g_ref, w_ref, cw_ref, ya_ref, q_ref, k_ref, v_ref, zs_ref,
                    *, tiles_per_seq, a_width, sb_width):
    tm = h_ref.shape[0]
    xn = _rms(h_ref[...], g_ref[...]).astype(BF16)
    a3 = 3 * a_width

    @pl.when(pl.program_id(0) % tiles_per_seq == 0)
    def _():
        zs_ref[0:8, :] = jnp.zeros((8, a_width), F32)

    pa = jnp.dot(xn, w_ref[:, 0:a3], preferred_element_type=F32)
    z = pa[:, 2 * a_width:a3] * pa[:, 0:a_width]
    zs_ref[8:tm + 8, :] = z
    z1 = zs_ref[7:tm + 7, :]
    z2 = zs_ref[6:tm + 6, :]
    cw = cw_ref[...]
    conv = cw[0:1, :] * z2 + cw[1:2, :] * z1 + cw[2:3, :] * z
    ya_ref[...] = (pa[:, a_width:2 * a_width] * conv).astype(BF16)
    zs_ref[0:8, :] = zs_ref[tm:tm + 8, :]

    pq = jnp.dot(xn, w_ref[:, a3:a3 + 3 * sb_width], preferred_element_type=F32)
    q_ref[...] = (pq[:, 0:sb_width] * (HEAD_DIM ** -0.5)).astype(BF16)
    k_ref[...] = pq[:, sb_width:2 * sb_width].astype(BF16)
    v_ref[...] = pq[:, 2 * sb_width:3 * sb_width].astype(BF16)


def _even_in(h, g, w_in, conv_w, *, seq, tm):
    t, d = h.shape
    a_width = conv_w.shape[1]
    sb_width = (w_in.shape[1] - 3 * a_width) // 3
    kern = functools.partial(_even_in_kernel, tiles_per_seq=seq // tm, a_width=a_width,
                             sb_width=sb_width)
    row = lambda w: pl.BlockSpec((tm, w), lambda i: (i, 0))
    return pl.pallas_call(
        kern,
        grid=(t // tm,),
        in_specs=[row(d), _const_spec((1, d)), _const_spec(w_in.shape), _const_spec(conv_w.shape)],
        out_specs=[row(a_width), row(sb_width), row(sb_width), row(sb_width)],
        out_shape=[jax.ShapeDtypeStruct((t, a_width), BF16)] +
                  [jax.ShapeDtypeStruct((t, sb_width), BF16)] * 3,
        scratch_shapes=[pltpu.VMEM((tm + 8, a_width), F32)],
        compiler_params=_params("arbitrary"),
        name="even_in",
    )(h, g, w_in, conv_w)


def _attn_kernel(q_ref, k_ref, v_ref, o_ref, acc_ref, c_ref, *, tq):
    qi = pl.program_id(2)
    rows = HEADS_PER_SLAB * tq
    q = q_ref[...]
    lane = lax.broadcasted_iota(jnp.int32, q.shape, 1)
    zero = jnp.zeros_like(q)
    qs = jnp.concatenate([jnp.where(lane < HEAD_DIM, q, zero),
                          jnp.where(lane >= HEAD_DIM, q, zero)], axis=0)
    r_i = lax.broadcasted_iota(jnp.int32, (tq, tq), 0)
    c_i = lax.broadcasted_iota(jnp.int32, (tq, tq), 1)
    upper = jnp.where(r_i > c_i, 1.0, 0.0).astype(BF16)
    t_loc = lax.broadcasted_iota(jnp.int32, (rows, tq), 0) % tq
    s_loc = lax.broadcasted_iota(jnp.int32, (rows, tq), 1)

    acc_ref[...] = jnp.zeros_like(acc_ref)
    c_ref[...] = jnp.zeros_like(c_ref)

    def cond(carry):
        kb, cmax = carry
        return jnp.logical_and(kb >= 0, cmax > LOG_F32_TINY_BOUND)

    def body(carry):
        kb, _ = carry
        start = pl.multiple_of(kb * tq, tq)
        kk = k_ref[pl.ds(start, tq), :]
        vv = v_ref[pl.ds(start, tq), :]
        z = lax.dot_general(qs, kk, (((1,), (1,)), ((), ())), preferred_element_type=F32)
        sp = jnp.maximum(z, 0.0) + jnp.log1p(jnp.exp(-jnp.abs(z)))
        mask = (s_loc + (kb - qi) * tq) < t_loc
        lr = jnp.where(mask, -sp, 0.0)
        hi = lr.astype(BF16)
        lo = (lr - hi.astype(F32)).astype(BF16)
        later = (jnp.dot(hi, upper, preferred_element_type=F32) +
                 jnp.dot(lo, upper, preferred_element_type=F32))
        c = c_ref[...]
        a = jnp.where(mask, jnp.exp(z - sp + later + c), 0.0)
        acc_ref[...] += jnp.dot(a.astype(BF16), vv, preferred_element_type=F32)
        c_new = c + jnp.sum(lr, axis=1, keepdims=True)
        c_ref[...] = c_new
        return kb - 1, jnp.max(c_new)

    lax.while_loop(cond, body, (qi, jnp.float32(0.0)))
    acc = acc_ref[...]
    o_ref[...] = jnp.where(lane < HEAD_DIM, acc[0:tq], acc[tq:rows]).astype(o_ref.dtype)


def _attention(q, k, v, *, batch, seq, tq):
    t, w = q.shape
    slab = HEADS_PER_SLAB * HEAD_DIM
    nq = seq // tq
    qspec = pl.BlockSpec((tq, slab), lambda b, s, i: (b * nq + i, s))
    kvspec = pl.BlockSpec((seq, slab), lambda b, s, i: (b, s))
    return pl.pallas_call(
        functools.partial(_attn_kernel, tq=tq),
        grid=(batch, w // slab, nq),
        in_specs=[qspec, kvspec, kvspec],
        out_specs=qspec,
        out_shape=jax.ShapeDtypeStruct((t, w), BF16),
        scratch_shapes=[pltpu.VMEM((HEADS_PER_SLAB * tq, slab), F32),
                        pltpu.VMEM((HEADS_PER_SLAB * tq, 1), F32)],
        compiler_params=_params("arbitrary", "arbitrary", "arbitrary"),
        name="stick_breaking_attention",
    )(q, k, v)


def _router_tail(h1, g_ref, wr_hi_ref, wr_lo_ref, br_ref, h1_ref, xn_ref, gate_ref, eid_ref):
    h1_ref[...] = h1
    xn = _rms(h1, g_ref[...])
    xn_ref[...] = xn.astype(BF16)
    x_hi = xn.astype(BF16)
    x_lo = (xn - x_hi.astype(F32)).astype(BF16)
    w_hi = wr_hi_ref[...]
    logits = (jnp.dot(x_hi, w_hi, preferred_element_type=F32) +
              jnp.dot(x_hi, wr_lo_ref[...], preferred_element_type=F32) +
              jnp.dot(x_lo, w_hi, preferred_element_type=F32) + br_ref[...])
    lt = logits.T
    tm = lt.shape[1]
    row = lax.broadcasted_iota(jnp.int32, (EXP_PER_GROUP, tm), 0)
    neg = jnp.float32(-jnp.inf)

    coarse = jnp.where(row < N_GROUPS, lt[0:EXP_PER_GROUP], neg)
    cmax = jnp.max(coarse, axis=0, keepdims=True)
    gi = jnp.min(jnp.where(coarse == cmax, row, EXP_PER_GROUP), axis=0, keepdims=True)
    pg = 1.0 / jnp.sum(jnp.exp(coarse - cmax), axis=0, keepdims=True)

    fine = lt[FINE_ROW0:FINE_ROW0 + EXP_PER_GROUP]
    for grp in range(1, N_GROUPS):
        lo_r = FINE_ROW0 + grp * EXP_PER_GROUP
        fine = jnp.where(gi == grp, lt[lo_r:lo_r + EXP_PER_GROUP], fine)
    m1 = jnp.max(fine, axis=0, keepdims=True)
    i1 = jnp.min(jnp.where(fine == m1, row, EXP_PER_GROUP), axis=0, keepdims=True)
    rest = jnp.where(row == i1, neg, fine)
    m2 = jnp.max(rest, axis=0, keepdims=True)
    i2 = jnp.min(jnp.where(rest == m2, row, EXP_PER_GROUP), axis=0, keepdims=True)
    e21 = jnp.exp(m2 - m1)
    w1 = 1.0 / (1.0 + e21)
    gate_ref[...] = jnp.concatenate([pg * w1, pg * (e21 * w1)], axis=0)
    eid_ref[...] = jnp.concatenate([gi * EXP_PER_GROUP + i1, gi * EXP_PER_GROUP + i2], axis=0)


def _tail_specs(t, d, tm):
    row = pl.BlockSpec((tm, d), lambda i: (i, 0))
    lanes = pl.BlockSpec((2, tm), lambda i: (0, i))
    out_specs = [row, row, lanes, lanes]
    out_shape = [jax.ShapeDtypeStruct((t, d), F32), jax.ShapeDtypeStruct((t, d), BF16),
                 jax.ShapeDtypeStruct((2, t), F32), jax.ShapeDtypeStruct((2, t), jnp.int32)]
    return out_specs, out_shape


def _even_out_kernel(ya_ref, yb_ref, h_ref, w_ref, g_ref, wr_hi_ref, wr_lo_ref, br_ref,
                     h1_ref, xn_ref, gate_ref, eid_ref):
    a_width = ya_ref.shape[1]
    mix = (jnp.dot(ya_ref[...], w_ref[0:a_width, :], preferred_element_type=F32) +
           jnp.dot(yb_ref[...], w_ref[a_width:, :], preferred_element_type=F32))
    _router_tail(h_ref[...] + mix, g_ref, wr_hi_ref, wr_lo_ref, br_ref,
                 h1_ref, xn_ref, gate_ref, eid_ref)


def _even_out(ya, yb, h, w_out, g_ffn, wr_hi, wr_lo, br, *, tm):
    t, d = h.shape
    out_specs, out_shape = _tail_specs(t, d, tm)
    row = lambda w: pl.BlockSpec((tm, w), lambda i: (i, 0))
    return pl.pallas_call(
        _even_out_kernel,
        grid=(t // tm,),
        in_specs=[row(ya.shape[1]), row(yb.shape[1]), row(d), _const_spec(w_out.shape),
                  _const_spec((1, d)), _const_spec(wr_hi.shape), _const_spec(wr_lo.shape),
                  _const_spec(br.shape)],
        out_specs=out_specs,
        out_shape=out_shape,
        compiler_params=_params("arbitrary"),
        name="even_out_router",
    )(ya, yb, h, w_out, g_ffn, wr_hi, wr_lo, br)


def _gelu_tanh(x):
    return 0.5 * x * (1.0 + jnp.tanh(0.7978845608028654 * (x + 0.044715 * (x * x * x))))


def _odd_kernel(h_ref, g_ref, w_in_ref, gv_ref, ws_ref, bs_ref, w_out_ref,
                gf_ref, wr_hi_ref, wr_lo_ref, br_ref,
                h1_ref, xn_ref, gate_ref, eid_ref, y_ref):
    tm = h_ref.shape[0]
    c_width = gv_ref.shape[1]
    n_grp = ws_ref.shape[0]
    gdim = c_width // n_grp
    h = h_ref[...]
    xn = _rms(h, g_ref[...]).astype(BF16)
    u = _gelu_tanh(jnp.dot(xn, w_in_ref[:, 0:c_width], preferred_element_type=F32))
    v = _gelu_tanh(jnp.dot(xn, w_in_ref[:, c_width:], preferred_element_type=F32))
    v = _rms(v, gv_ref[...]).astype(BF16)
    r_i = lax.broadcasted_iota(jnp.int32, (CHUNK, CHUNK), 0)
    c_i = lax.broadcasted_iota(jnp.int32, (CHUNK, CHUNK), 1)
    causal = c_i <= r_i
    bs = bs_ref[...]
    for grp in range(n_grp):
        w_m = jnp.where(causal, ws_ref[grp], 0.0).astype(BF16)
        bias = bs[:, grp:grp + 1]
        cols = slice(grp * gdim, (grp + 1) * gdim)
        for c in range(tm // CHUNK):
            rows = slice(c * CHUNK, (c + 1) * CHUNK)
            gate = jnp.dot(w_m, v[rows, cols], preferred_element_type=F32) + bias
            y_ref[rows, cols] = (u[rows, cols] * gate).astype(BF16)
    mix = jnp.dot(y_ref[...], w_out_ref[...], preferred_element_type=F32)
    _router_tail(h + mix, gf_ref, wr_hi_ref, wr_lo_ref, br_ref, h1_ref, xn_ref, gate_ref, eid_ref)


def _odd_mixer(h, g_mix, w_in, g_v, w_s, b_s_t, w_out, g_ffn, wr_hi, wr_lo, br, *, tm):
    t, d = h.shape
    out_specs, out_shape = _tail_specs(t, d, tm)
    row = pl.BlockSpec((tm, d), lambda i: (i, 0))
    consts = [g_mix, w_in, g_v, w_s, b_s_t, w_out, g_ffn, wr_hi, wr_lo, br]
    return pl.pallas_call(
        _odd_kernel,
        grid=(t // tm,),
        in_specs=[row] + [_const_spec(c.shape) for c in consts],
        out_specs=out_specs,
        out_shape=out_shape,
        scratch_shapes=[pltpu.VMEM((tm, g_v.shape[1]), BF16)],
        compiler_params=_params("arbitrary"),
        name="odd_mixer_router",
    )(h, *consts)


def _moe_kernel(x_ref, gate_ref, eid_ref, w1_ref, w3_ref, w2_ref, y_ref):
    e = pl.program_id(1)

    @pl.when(e == 0)
    def _():
        y_ref[...] = jnp.zeros_like(y_ref)

    x = x_ref[...]
    h1 = jnp.dot(x, w1_ref[0], preferred_element_type=F32)
    h3 = jnp.dot(x, w3_ref[0], preferred_element_type=F32)
    he = (h1 * jax.nn.sigmoid(h1)) * h3
    gates = gate_ref[...]
    eids = eid_ref[...]
    wgt = (jnp.where(eids[0:1] == e, gates[0:1], 0.0) +
           jnp.where(eids[1:2] == e, gates[1:2], 0.0))
    col = jnp.broadcast_to(wgt, (8, wgt.shape[1])).T[:, 0:1]
    y_ref[...] += col * jnp.dot(he.astype(BF16), w2_ref[0], preferred_element_type=F32)


def _moe_dense(xn, gates, eids, w1, w3, w2, *, tm):
    t, d = xn.shape
    n_e, _, d_e = w1.shape
    row = pl.BlockSpec((tm, d), lambda i, e: (i, 0))
    lanes = pl.BlockSpec((2, tm), lambda i, e: (0, i))
    return pl.pallas_call(
        _moe_kernel,
        grid=(t // tm, n_e),
        in_specs=[row, lanes, lanes,
                  pl.BlockSpec((1, d, d_e), lambda i, e: (e, 0, 0)),
                  pl.BlockSpec((1, d, d_e), lambda i, e: (e, 0, 0)),
                  pl.BlockSpec((1, d_e, d), lambda i, e: (e, 0, 0))],
        out_specs=row,
        out_shape=jax.ShapeDtypeStruct((t, d), F32),
        compiler_params=_params("arbitrary", "arbitrary"),
        name="moe_experts",
    )(xn, gates, eids, w1, w3, w2)


def _post_kernel(h1_ref, y_ref, p_ref, g_ref, wpg_ref, wpe_ref, gfin_ref, o_ref, *, final):
    h2 = h1_ref[...] + y_ref[...]
    gate = jax.nn.sigmoid(jnp.dot(_rms(h2, g_ref[...]).astype(BF16), wpg_ref[...],
                                  preferred_element_type=F32))
    pe = jnp.dot(p_ref[...].astype(BF16), wpe_ref[...], preferred_element_type=F32)
    h3 = h2 + gate * pe
    o_ref[...] = _rms(h3, gfin_ref[...]) if final else h3


def _post(h1, y, p, g_pl, w_pg, w_pe, g_fin, *, final, tm):
    t, d = h1.shape
    row = lambda w: pl.BlockSpec((tm, w), lambda i: (i, 0))
    return pl.pallas_call(
        functools.partial(_post_kernel, final=final),
        grid=(t // tm,),
        in_specs=[row(d), row(d), row(p.shape[1]), _const_spec((1, d)), _const_spec(w_pg.shape),
                  _const_spec(w_pe.shape), _const_spec((1, d))],
        out_specs=row(d),
        out_shape=jax.ShapeDtypeStruct((t, d), F32),
        compiler_params=_params("arbitrary"),
        name="layer_post",
    )(h1, y, p, g_pl, w_pg, w_pe, g_fin)


def _router_weights(wc, bc, wf, bf):
    d = wc.shape[0]
    w = jnp.zeros((d, ROUTER_LANES), F32)
    w = w.at[:, 0:N_GROUPS].set(wc).at[:, FINE_ROW0:FINE_ROW0 + N_EXPERTS].set(wf)
    b = jnp.zeros((1, ROUTER_LANES), F32)
    b = b.at[0, 0:N_GROUPS].set(bc).at[0, FINE_ROW0:FINE_ROW0 + N_EXPERTS].set(bf)
    hi = w.astype(BF16)
    lo = (w - hi.astype(F32)).astype(BF16)
    return hi, lo, b


def kernel(x, p, norm_mix, norm_ffn, norm_pl, final_norm, w_in_even, conv_w_even, w_out_even, w_in_odd, g_v_odd, w_s_odd, b_s_odd, w_out_odd, router_c, router_c_b, router_f, router_f_b, moe_w1, moe_w3, moe_w2, w_pe, w_pg):
    batch, seq, d = x.shape
    depth = p.shape[0]
    t = batch * seq
    tm = min(512, seq)
    tq = min(256, seq)
    h = x.reshape(t, d)
    row = lambda a: a.reshape(1, -1)
    for i in range(depth):
        j = i // 2
        wr_hi, wr_lo, br = _router_weights(router_c[i], router_c_b[i], router_f[i], router_f_b[i])
        if i % 2 == 0:
            ya, q, k, v = _even_in(h, row(norm_mix[i]), w_in_even[j].astype(BF16), conv_w_even[j],
                                   seq=seq, tm=tm)
            yb = _attention(q, k, v, batch=batch, seq=seq, tq=tq)
            h1, xn, gates, eids = _even_out(ya, yb, h, w_out_even[j].astype(BF16),
                                            row(norm_ffn[i]), wr_hi, wr_lo, br, tm=tm)
        else:
            h1, xn, gates, eids = _odd_mixer(
                h, row(norm_mix[i]), w_in_odd[j].astype(BF16), row(g_v_odd[j]), w_s_odd[j],
                b_s_odd[j].T, w_out_odd[j].astype(BF16), row(norm_ffn[i]), wr_hi, wr_lo, br,
                tm=min(256, seq))
        y = _moe_dense(xn, gates, eids, moe_w1[i].astype(BF16), moe_w3[i].astype(BF16),
                       moe_w2[i].astype(BF16), tm=min(2048, seq))
        h = _post(h1, y, p[i].reshape(t, -1), row(norm_pl[i]), w_pg[i].astype(BF16),
                  w_pe[i].astype(BF16), row(final_norm), final=(i == depth - 1), tm=tm)
    return h.reshape(batch, seq, d)
```

```python
import functools

import jax
import jax.numpy as jnp
from jax import lax
from jax.experimental import pallas as pl
from jax.experimental.pallas import tpu as pltpu
from jax.experimental.pallas import tpu_sc as plsc

EPS = 1e-6
HEAD_DIM = 64
HEADS_PER_SLAB = 2
CHUNK = 128
N_GROUPS = 4
EXP_PER_GROUP = 8
N_EXPERTS = N_GROUPS * EXP_PER_GROUP
TOP_K = 2
ROUTER_LANES = 128
FINE_ROW0 = 8

V7X_VMEM_BYTES = 64 * 1024 * 1024
VMEM_LIMIT = V7X_VMEM_BYTES - 8 * 1024 * 1024
V7X_SC_CORES = 2
V7X_SC_SUBCORES = 16
SC_WORKERS = V7X_SC_CORES * V7X_SC_SUBCORES
SC_ROWS = 32

TOKEN_TILE = 512
ODD_TOKEN_TILE = 256
Q_TILE = 256
SLOT_TILE = 256

LOG_F32_TINY_BOUND = -88.0

F32 = jnp.float32
BF16 = jnp.bfloat16
U32 = jnp.uint32
I32 = jnp.int32


def _rms(x, g):
    return x * lax.rsqrt(jnp.mean(x * x, axis=-1, keepdims=True) + EPS) * g


def _const_spec(shape):
    nd = len(shape)
    return pl.BlockSpec(shape, lambda *_: (0,) * nd)


def _params(*sem):
    return pltpu.CompilerParams(dimension_semantics=sem, vmem_limit_bytes=VMEM_LIMIT)


def _even_in_kernel(h_ref, g_ref, w_ref, cw_ref, ya_ref, q_ref, k_ref, v_ref, zs_ref,
                    *, tiles_per_seq, a_width, sb_width):
    tm = h_ref.shape[0]
    xn = _rms(h_ref[...], g_ref[...]).astype(BF16)
    a3 = 3 * a_width

    @pl.when(pl.program_id(0) % tiles_per_seq == 0)
    def _():
        zs_ref[0:8, :] = jnp.zeros((8, a_width), F32)

    pa = jnp.dot(xn, w_ref[:, 0:a3], preferred_element_type=F32)
    z = pa[:, 2 * a_width:a3] * pa[:, 0:a_width]
    zs_ref[8:tm + 8, :] = z
    z1 = zs_ref[7:tm + 7, :]
    z2 = zs_ref[6:tm + 6, :]
    cw = cw_ref[...]
    conv = cw[0:1, :] * z2 + cw[1:2, :] * z1 + cw[2:3, :] * z
    ya_ref[...] = (pa[:, a_width:2 * a_width] * conv).astype(BF16)
    zs_ref[0:8, :] = zs_ref[tm:tm + 8, :]

    pq = jnp.dot(xn, w_ref[:, a3:a3 + 3 * sb_width], preferred_element_type=F32)
    q_ref[...] = (pq[:, 0:sb_width] * (HEAD_DIM ** -0.5)).astype(BF16)
    k_ref[...] = pq[:, sb_width:2 * sb_width].astype(BF16)
    v_ref[...] = pq[:, 2 * sb_width:3 * sb_width].astype(BF16)


def _even_in(h, g, w_in, conv_w, *, seq, tm):
    t, d = h.shape
    a_width = conv_w.shape[1]
    sb_width = (w_in.shape[1] - 3 * a_width) // 3
    kern = functools.partial(_even_in_kernel, tiles_per_seq=seq // tm, a_width=a_width,
                             sb_width=sb_width)
    row = lambda w: pl.BlockSpec((tm, w), lambda i: (i, 0))
    return pl.pallas_call(
        kern,
        grid=(t // tm,),
        in_specs=[row(d), _const_spec((1, d)), _const_spec(w_in.shape), _const_spec(conv_w.shape)],
        out_specs=[row(a_width), row(sb_width), row(sb_width), row(sb_width)],
        out_shape=[jax.ShapeDtypeStruct((t, a_width), BF16)] +
                  [jax.ShapeDtypeStruct((t, sb_width), BF16)] * 3,
        scratch_shapes=[pltpu.VMEM((tm + 8, a_width), F32)],
        compiler_params=_params("arbitrary"),
        name="even_in",
    )(h, g, w_in, conv_w)


def _attn_kernel(q_ref, k_ref, v_ref, o_ref, acc_ref, c_ref, *, tq):
    qi = pl.program_id(2)
    rows = HEADS_PER_SLAB * tq
    q = q_ref[...]
    lane = lax.broadcasted_iota(I32, q.shape, 1)
    zero = jnp.zeros_like(q)
    qs = jnp.concatenate([jnp.where(lane < HEAD_DIM, q, zero),
                          jnp.where(lane >= HEAD_DIM, q, zero)], axis=0)
    r_i = lax.broadcasted_iota(I32, (tq, tq), 0)
    c_i = lax.broadcasted_iota(I32, (tq, tq), 1)
    upper = jnp.where(r_i > c_i, 1.0, 0.0).astype(BF16)
    t_loc = lax.broadcasted_iota(I32, (rows, tq), 0) % tq
    s_loc = lax.broadcasted_iota(I32, (rows, tq), 1)

    acc_ref[...] = jnp.zeros_like(acc_ref)
    c_ref[...] = jnp.zeros_like(c_ref)

    def cond(carry):
        kb, cmax = carry
        return jnp.logical_and(kb >= 0, cmax > LOG_F32_TINY_BOUND)

    def body(carry):
        kb, _ = carry
        start = pl.multiple_of(kb * tq, tq)
        kk = k_ref[pl.ds(start, tq), :]
        vv = v_ref[pl.ds(start, tq), :]
        z = lax.dot_general(qs, kk, (((1,), (1,)), ((), ())), preferred_element_type=F32)
        sp = jnp.maximum(z, 0.0) + jnp.log1p(jnp.exp(-jnp.abs(z)))
        mask = (s_loc + (kb - qi) * tq) < t_loc
        lr = jnp.where(mask, -sp, 0.0)
        hi = lr.astype(BF16)
        lo = (lr - hi.astype(F32)).astype(BF16)
        later = (jnp.dot(hi, upper, preferred_element_type=F32) +
                 jnp.dot(lo, upper, preferred_element_type=F32))
        c = c_ref[...]
        a = jnp.where(mask, jnp.exp(z - sp + later + c), 0.0)
        acc_ref[...] += jnp.dot(a.astype(BF16), vv, preferred_element_type=F32)
        c_new = c + jnp.sum(lr, axis=1, keepdims=True)
        c_ref[...] = c_new
        return kb - 1, jnp.max(c_new)

    lax.while_loop(cond, body, (qi, jnp.float32(0.0)))
    acc = acc_ref[...]
    o_ref[...] = jnp.where(lane < HEAD_DIM, acc[0:tq], acc[tq:rows]).astype(o_ref.dtype)


def _attention(q, k, v, *, batch, seq, tq):
    t, w = q.shape
    slab = HEADS_PER_SLAB * HEAD_DIM
    nq = seq // tq
    qspec = pl.BlockSpec((tq, slab), lambda b, s, i: (b * nq + i, s))
    kvspec = pl.BlockSpec((seq, slab), lambda b, s, i: (b, s))
    return pl.pallas_call(
        functools.partial(_attn_kernel, tq=tq),
        grid=(batch, w // slab, nq),
        in_specs=[qspec, kvspec, kvspec],
        out_specs=qspec,
        out_shape=jax.ShapeDtypeStruct((t, w), BF16),
        scratch_shapes=[pltpu.VMEM((HEADS_PER_SLAB * tq, slab), F32),
                        pltpu.VMEM((HEADS_PER_SLAB * tq, 1), F32)],
        compiler_params=_params("arbitrary", "arbitrary", "arbitrary"),
        name="stick_breaking_attention",
    )(q, k, v)


def _pack_bf16_pairs(x):
    n = x.shape[1] // 2
    bits = lax.bitcast_convert_type(x.astype(BF16).astype(F32), U32)
    return bits[:, 0:n] | (bits[:, n:] >> 16)


def _unpack_bf16_pairs(u):
    a = lax.bitcast_convert_type(u & jnp.uint32(0xFFFF0000), F32)
    b = lax.bitcast_convert_type(u << 16, F32)
    return jnp.concatenate([a, b], axis=1).astype(BF16)


def _router_tail(h1, g_ref, wr_hi_ref, wr_lo_ref, br_ref,
                 h1_ref, xp_ref, gate_ref, eid_ref, rank_ref, cnt_out_ref, cnt_ref):
    h1_ref[...] = h1
    xn = _rms(h1, g_ref[...])
    xp_ref[...] = _pack_bf16_pairs(xn)
    x_hi = xn.astype(BF16)
    x_lo = (xn - x_hi.astype(F32)).astype(BF16)
    w_hi = wr_hi_ref[...]
    logits = (jnp.dot(x_hi, w_hi, preferred_element_type=F32) +
              jnp.dot(x_hi, wr_lo_ref[...], preferred_element_type=F32) +
              jnp.dot(x_lo, w_hi, preferred_element_type=F32) + br_ref[...])
    lt = logits.T
    tm = lt.shape[1]
    row = lax.broadcasted_iota(I32, (EXP_PER_GROUP, tm), 0)
    neg = jnp.float32(-jnp.inf)

    coarse = jnp.where(row < N_GROUPS, lt[0:EXP_PER_GROUP], neg)
    cmax = jnp.max(coarse, axis=0, keepdims=True)
    gi = jnp.min(jnp.where(coarse == cmax, row, EXP_PER_GROUP), axis=0, keepdims=True)
    pg = 1.0 / jnp.sum(jnp.exp(coarse - cmax), axis=0, keepdims=True)

    fine = lt[FINE_ROW0:FINE_ROW0 + EXP_PER_GROUP]
    for grp in range(1, N_GROUPS):
        lo_r = FINE_ROW0 + grp * EXP_PER_GROUP
        fine = jnp.where(gi == grp, lt[lo_r:lo_r + EXP_PER_GROUP], fine)
    m1 = jnp.max(fine, axis=0, keepdims=True)
    i1 = jnp.min(jnp.where(fine == m1, row, EXP_PER_GROUP), axis=0, keepdims=True)
    rest = jnp.where(row == i1, neg, fine)
    m2 = jnp.max(rest, axis=0, keepdims=True)
    i2 = jnp.min(jnp.where(rest == m2, row, EXP_PER_GROUP), axis=0, keepdims=True)
    e21 = jnp.exp(m2 - m1)
    w1 = 1.0 / (1.0 + e21)
    gate_ref[...] = jnp.concatenate([pg * w1, pg * (e21 * w1)], axis=0)
    e0 = gi * EXP_PER_GROUP + i1
    e1 = gi * EXP_PER_GROUP + i2
    eid_ref[...] = jnp.concatenate([e0, e1], axis=0)

    @pl.when(pl.program_id(0) == 0)
    def _():
        cnt_ref[...] = jnp.zeros_like(cnt_ref)

    erow = lax.broadcasted_iota(I32, (N_EXPERTS, tm), 0)
    oh0 = jnp.where(erow == e0, 1.0, 0.0)
    oh1 = jnp.where(erow == e1, 1.0, 0.0)
    r_i = lax.broadcasted_iota(I32, (tm, tm), 0)
    c_i = lax.broadcasted_iota(I32, (tm, tm), 1)
    before = jnp.where(r_i < c_i, 1.0, 0.0).astype(BF16)
    pre0 = jnp.dot(oh0.astype(BF16), before, preferred_element_type=F32)
    pre1 = jnp.dot(oh1.astype(BF16), before, preferred_element_type=F32)
    tot0 = jnp.sum(oh0, axis=1, keepdims=True)
    tot1 = jnp.sum(oh1, axis=1, keepdims=True)
    base = cnt_ref[:, 0:1]
    r0 = jnp.sum(oh0 * (pre0 + base), axis=0, keepdims=True)
    r1 = jnp.sum(oh1 * (pre1 + (base + tot0)), axis=0, keepdims=True)
    rank_ref[...] = jnp.concatenate([r0, r1], axis=0).astype(I32)
    total = jnp.broadcast_to(base + tot0 + tot1, cnt_ref.shape)
    cnt_ref[...] = total
    cnt_out_ref[...] = total


def _tail_specs(t, d, tm):
    row = lambda w: pl.BlockSpec((tm, w), lambda i: (i, 0))
    lanes = pl.BlockSpec((TOP_K, tm), lambda i: (0, i))
    cnt = _const_spec((N_EXPERTS, ROUTER_LANES))
    out_specs = [row(d), row(d // 2), lanes, lanes, lanes, cnt]
    out_shape = [jax.ShapeDtypeStruct((t, d), F32), jax.ShapeDtypeStruct((t, d // 2), U32),
                 jax.ShapeDtypeStruct((TOP_K, t), F32), jax.ShapeDtypeStruct((TOP_K, t), I32),
                 jax.ShapeDtypeStruct((TOP_K, t), I32),
                 jax.ShapeDtypeStruct((N_EXPERTS, ROUTER_LANES), F32)]
    scratch = [pltpu.VMEM((N_EXPERTS, ROUTER_LANES), F32)]
    return out_specs, out_shape, scratch


def _even_out_kernel(ya_ref, yb_ref, h_ref, w_ref, g_ref, wr_hi_ref, wr_lo_ref, br_ref,
                     *tail_refs):
    a_width = ya_ref.shape[1]
    mix = (jnp.dot(ya_ref[...], w_ref[0:a_width, :], preferred_element_type=F32) +
           jnp.dot(yb_ref[...], w_ref[a_width:, :], preferred_element_type=F32))
    _router_tail(h_ref[...] + mix, g_ref, wr_hi_ref, wr_lo_ref, br_ref, *tail_refs)


def _even_out(ya, yb, h, w_out, g_ffn, wr_hi, wr_lo, br, *, tm):
    t, d = h.shape
    out_specs, out_shape, scratch = _tail_specs(t, d, tm)
    row = lambda w: pl.BlockSpec((tm, w), lambda i: (i, 0))
    return pl.pallas_call(
        _even_out_kernel,
        grid=(t // tm,),
        in_specs=[row(ya.shape[1]), row(yb.shape[1]), row(d), _const_spec(w_out.shape),
                  _const_spec((1, d)), _const_spec(wr_hi.shape), _const_spec(wr_lo.shape),
                  _const_spec(br.shape)],
        out_specs=out_specs,
        out_shape=out_shape,
        scratch_shapes=scratch,
        compiler_params=_params("arbitrary"),
        name="even_out_router",
    )(ya, yb, h, w_out, g_ffn, wr_hi, wr_lo, br)


def _gelu_tanh(x):
    return 0.5 * x * (1.0 + jnp.tanh(0.7978845608028654 * (x + 0.044715 * (x * x * x))))


def _odd_kernel(h_ref, g_ref, w_in_ref, gv_ref, ws_ref, bs_ref, w_out_ref,
                gf_ref, wr_hi_ref, wr_lo_ref, br_ref, *rest):
    *tail_refs, y_ref = rest
    tm = h_ref.shape[0]
    c_width = gv_ref.shape[1]
    n_grp = ws_ref.shape[0]
    gdim = c_width // n_grp
    h = h_ref[...]
    xn = _rms(h, g_ref[...]).astype(BF16)
    u = _gelu_tanh(jnp.dot(xn, w_in_ref[:, 0:c_width], preferred_element_type=F32))
    v = _gelu_tanh(jnp.dot(xn, w_in_ref[:, c_width:], preferred_element_type=F32))
    v = _rms(v, gv_ref[...]).astype(BF16)
    r_i = lax.broadcasted_iota(I32, (CHUNK, CHUNK), 0)
    c_i = lax.broadcasted_iota(I32, (CHUNK, CHUNK), 1)
    causal = c_i <= r_i
    bs = bs_ref[...]
    for grp in range(n_grp):
        w_m = jnp.where(causal, ws_ref[grp], 0.0).astype(BF16)
        bias = bs[:, grp:grp + 1]
        cols = slice(grp * gdim, (grp + 1) * gdim)
        for c in range(tm // CHUNK):
            rows = slice(c * CHUNK, (c + 1) * CHUNK)
            gate = jnp.dot(w_m, v[rows, cols], preferred_element_type=F32) + bias
            y_ref[rows, cols] = (u[rows, cols] * gate).astype(BF16)
    mix = jnp.dot(y_ref[...], w_out_ref[...], preferred_element_type=F32)
    _router_tail(h + mix, gf_ref, wr_hi_ref, wr_lo_ref, br_ref, *tail_refs)


def _odd_mixer(h, g_mix, w_in, g_v, w_s, b_s_t, w_out, g_ffn, wr_hi, wr_lo, br, *, tm):
    t, d = h.shape
    out_specs, out_shape, scratch = _tail_specs(t, d, tm)
    row = pl.BlockSpec((tm, d), lambda i: (i, 0))
    consts = [g_mix, w_in, g_v, w_s, b_s_t, w_out, g_ffn, wr_hi, wr_lo, br]
    return pl.pallas_call(
        _odd_kernel,
        grid=(t // tm,),
        in_specs=[row] + [_const_spec(c.shape) for c in consts],
        out_specs=out_specs,
        out_shape=out_shape,
        scratch_shapes=scratch + [pltpu.VMEM((tm, g_v.shape[1]), BF16)],
        compiler_params=_params("arbitrary"),
        name="odd_mixer_router",
    )(h, *consts)


def _sc_mesh():
    return plsc.VectorSubcoreMesh(core_axis_name="c", subcore_axis_name="s")


def _sc_worker_base(rows_per_worker):
    wid = lax.axis_index("s") * V7X_SC_CORES + lax.axis_index("c")
    return wid * rows_per_worker


def _dispatch(xp, dest, n_slots):
    t, w = xp.shape
    per_worker = t // SC_WORKERS
    n_chunks = per_worker // SC_ROWS
    assert per_worker * SC_WORKERS == t and n_chunks * SC_ROWS == per_worker and n_chunks % 2 == 0

    idx_t = pltpu.VMEM((SC_ROWS,), I32)

    @functools.partial(
        pl.kernel, mesh=_sc_mesh(),
        out_type=jax.ShapeDtypeStruct((n_slots, w), xp.dtype),
        scratch_types=[pltpu.VMEM((2, SC_ROWS, w), xp.dtype), idx_t, idx_t, idx_t, idx_t,
                       pltpu.SemaphoreType.DMA((2,)), pltpu.SemaphoreType.DMA((2,))],
    )
    def kern(x_hbm, d_hbm, out_hbm, rows_v, i00, i01, i10, i11, rsem, wsem):
        base = _sc_worker_base(per_worker)
        idx = ((i00, i01), (i10, i11))

        def load(c, b):
            off = pl.multiple_of(base + c * SC_ROWS, 8)
            pltpu.async_copy(x_hbm.at[pl.ds(off, SC_ROWS)], rows_v.at[b], rsem.at[b])
            for k in range(TOP_K):
                pltpu.sync_copy(d_hbm.at[k, pl.ds(off, SC_ROWS)], idx[b][k])

        def wait_load(b):
            pltpu.make_async_copy(x_hbm.at[pl.ds(0, SC_ROWS)], rows_v.at[b], rsem.at[b]).wait()

        def scatter(b):
            for k in range(TOP_K):
                pltpu.async_copy(rows_v.at[b], out_hbm.at[idx[b][k]], wsem.at[b])

        def wait_scatter(b):
            for k in range(TOP_K):
                pltpu.make_async_copy(rows_v.at[b], out_hbm.at[idx[b][k]], wsem.at[b]).wait()

        load(0, 0)

        @pl.loop(0, n_chunks, step=2)
        def _(c0):
            for b in range(2):
                c = c0 + b
                wait_load(b)
                scatter(b)

                @pl.when(c + 1 < n_chunks)
                def _():
                    @pl.when(c >= 1)
                    def _():
                        wait_scatter(1 - b)
                    load(c + 1, 1 - b)

        wait_scatter(0)
        wait_scatter(1)

    return kern(xp, dest)


def _combine(y, idx):
    n = idx.shape[0]
    w = y.shape[1]
    rows = SC_ROWS * 1024 // w if w > 512 else SC_ROWS * 2
    per_worker = n // SC_WORKERS
    n_chunks = per_worker // rows
    assert per_worker * SC_WORKERS == n and n_chunks * rows == per_worker and n_chunks % 2 == 0

    idx_t = pltpu.VMEM((rows,), I32)

    @functools.partial(
        pl.kernel, mesh=_sc_mesh(),
        out_type=jax.ShapeDtypeStruct((n, w), y.dtype),
        scratch_types=[pltpu.VMEM((2, rows, w), y.dtype), idx_t, idx_t,
                       pltpu.SemaphoreType.DMA((2,)), pltpu.SemaphoreType.DMA((2,))],
    )
    def kern(y_hbm, i_hbm, out_hbm, rows_v, i0, i1, gsem, wsem):
        base = _sc_worker_base(per_worker)
        ibuf = (i0, i1)

        def gather(c, b):
            off = pl.multiple_of(base + c * rows, 8)
            pltpu.sync_copy(i_hbm.at[pl.ds(off, rows)], ibuf[b])
            pltpu.async_copy(y_hbm.at[ibuf[b]], rows_v.at[b], gsem.at[b])

        def wait_gather(b):
            pltpu.make_async_copy(y_hbm.at[ibuf[b]], rows_v.at[b], gsem.at[b]).wait()

        def write(c, b):
            off = pl.multiple_of(base + c * rows, 8)
            pltpu.async_copy(rows_v.at[b], out_hbm.at[pl.ds(off, rows)], wsem.at[b])

        def wait_write(b):
            pltpu.make_async_copy(rows_v.at[b], out_hbm.at[pl.ds(0, rows)], wsem.at[b]).wait()

        gather(0, 0)

        @pl.loop(0, n_chunks, step=2)
        def _(c0):
            for b in range(2):
                c = c0 + b
                wait_gather(b)
                write(c, b)

                @pl.when(c + 1 < n_chunks)
                def _():
                    @pl.when(c >= 1)
                    def _():
                        wait_write(1 - b)
                    gather(c + 1, 1 - b)

        wait_write(0)
        wait_write(1)

    return kern(y, idx)


def _expert_kernel(te_ref, tv_ref, nu_ref, x_ref, w1_ref, w3_ref, w2_ref, y_ref,
                   w1b_ref, w3b_ref, w2b_ref):
    i = pl.program_id(0)

    @pl.when(i < nu_ref[0])
    def _():
        prev = te_ref[jnp.maximum(i - 1, 0)]

        @pl.when(jnp.logical_or(i == 0, te_ref[i] != prev))
        def _():
            w1b_ref[...] = w1_ref[0].astype(BF16)
            w3b_ref[...] = w3_ref[0].astype(BF16)
            w2b_ref[...] = w2_ref[0].astype(BF16)

        xu = x_ref[...]
        rowid = lax.broadcasted_iota(I32, xu.shape, 0)
        xu = jnp.where(rowid < tv_ref[i], xu, jnp.zeros_like(xu))
        x = _unpack_bf16_pairs(xu)
        h1 = jnp.dot(x, w1b_ref[...], preferred_element_type=F32)
        h3 = jnp.dot(x, w3b_ref[...], preferred_element_type=F32)
        he = (h1 * jax.nn.sigmoid(h1)) * h3
        y_ref[...] = jnp.dot(he.astype(BF16), w2b_ref[...], preferred_element_type=F32)


def _experts(x_sorted, tile_e, tile_valid, n_used, w1, w3, w2):
    n_slots, wp = x_sorted.shape
    n_e, d, d_e = w1.shape
    ts = SLOT_TILE
    last = lambda nu: jnp.maximum(nu[0] - 1, 0)
    xspec = pl.BlockSpec((ts, wp), lambda i, te, tv, nu: (jnp.minimum(i, last(nu)), 0))
    yspec = pl.BlockSpec((ts, d), lambda i, te, tv, nu: (jnp.minimum(i, last(nu)), 0))
    wspec = lambda a, b: pl.BlockSpec((1, a, b), lambda i, te, tv, nu: (te[i], 0, 0))
    return pl.pallas_call(
        _expert_kernel,
        grid_spec=pltpu.PrefetchScalarGridSpec(
            num_scalar_prefetch=3,
            grid=(n_slots // ts,),
            in_specs=[xspec, wspec(d, d_e), wspec(d, d_e), wspec(d_e, d)],
            out_specs=yspec,
            scratch_shapes=[pltpu.VMEM((d, d_e), BF16), pltpu.VMEM((d, d_e), BF16),
                            pltpu.VMEM((d_e, d), BF16)],
        ),
        out_shape=jax.ShapeDtypeStruct((n_slots, d), F32),
        compiler_params=_params("arbitrary"),
        name="moe_experts",
    )(tile_e, tile_valid, n_used, x_sorted, w1, w3, w2)


def _slot_plan(eids, ranks, counts, n_tiles):
    ts = SLOT_TILE
    cnt = counts[:, 0].astype(I32)
    padded = (cnt + ts - 1) // ts * ts
    ends = jnp.cumsum(padded)
    starts = ends - padded
    onehot = eids[..., None] == jnp.arange(N_EXPERTS, dtype=I32)
    dest = ranks + jnp.sum(jnp.where(onehot, starts, 0), axis=-1)
    tile_lo = jnp.arange(n_tiles, dtype=I32) * ts
    tile_e = jnp.minimum(jnp.sum(tile_lo[:, None] >= ends[None, :], axis=1), N_EXPERTS - 1)
    tile_e = tile_e.astype(I32)
    tile_valid = jnp.clip(starts[tile_e] + cnt[tile_e] - tile_lo, 0, ts).astype(I32)
    n_used = (ends[-1:] // ts).astype(I32)
    return dest, tile_e, tile_valid, n_used


def _post_kernel(h1_ref, y0_ref, y1_ref, gate_ref, p_ref, g_ref, wpg_ref, wpe_ref, gfin_ref,
                 o_ref, *, final):
    gates = gate_ref[...]
    tm = gates.shape[1]
    gt = jnp.concatenate([gates, jnp.zeros((8 - TOP_K, tm), F32)], axis=0).T
    h2 = h1_ref[...] + (gt[:, 0:1] * y0_ref[...] + gt[:, 1:2] * y1_ref[...])
    gate = jax.nn.sigmoid(jnp.dot(_rms(h2, g_ref[...]).astype(BF16), wpg_ref[...],
                                  preferred_element_type=F32))
    pe = jnp.dot(p_ref[...].astype(BF16), wpe_ref[...], preferred_element_type=F32)
    h3 = h2 + gate * pe
    o_ref[...] = _rms(h3, gfin_ref[...]) if final else h3


def _post(h1, yk, gates, p, g_pl, w_pg, w_pe, g_fin, *, final, tm):
    t, d = h1.shape
    nt = t // tm
    row = lambda w: pl.BlockSpec((tm, w), lambda i: (i, 0))
    return pl.pallas_call(
        functools.partial(_post_kernel, final=final),
        grid=(nt,),
        in_specs=[row(d), row(d), pl.BlockSpec((tm, d), lambda i: (i + nt, 0)),
                  pl.BlockSpec((TOP_K, tm), lambda i: (0, i)), row(p.shape[1]),
                  _const_spec((1, d)), _const_spec(w_pg.shape), _const_spec(w_pe.shape),
                  _const_spec((1, d))],
        out_specs=row(d),
        out_shape=jax.ShapeDtypeStruct((t, d), F32),
        compiler_params=_params("arbitrary"),
        name="layer_post",
    )(h1, yk, yk, gates, p, g_pl, w_pg, w_pe, g_fin)


def _router_weights(wc, bc, wf, bf):
    d = wc.shape[0]
    w = jnp.zeros((d, ROUTER_LANES), F32)
    w = w.at[:, 0:N_GROUPS].set(wc).at[:, FINE_ROW0:FINE_ROW0 + N_EXPERTS].set(wf)
    b = jnp.zeros((1, ROUTER_LANES), F32)
    b = b.at[0, 0:N_GROUPS].set(bc).at[0, FINE_ROW0:FINE_ROW0 + N_EXPERTS].set(bf)
    hi = w.astype(BF16)
    lo = (w - hi.astype(F32)).astype(BF16)
    return hi, lo, b


def _moe(xp, gates, eids, ranks, counts, w1, w3, w2):
    t = xp.shape[0]
    n_tiles = TOP_K * t // SLOT_TILE + N_EXPERTS
    dest, tile_e, tile_valid, n_used = _slot_plan(eids, ranks, counts, n_tiles)
    x_sorted = _dispatch(xp, dest, n_tiles * SLOT_TILE)
    y_sorted = _experts(x_sorted, tile_e, tile_valid, n_used, w1, w3, w2)
    return _combine(y_sorted, dest.reshape(-1))


def kernel(x, p, norm_mix, norm_ffn, norm_pl, final_norm, w_in_even, conv_w_even, w_out_even, w_in_odd, g_v_odd, w_s_odd, b_s_odd, w_out_odd, router_c, router_c_b, router_f, router_f_b, moe_w1, moe_w3, moe_w2, w_pe, w_pg):
    batch, seq, d = x.shape
    depth = p.shape[0]
    t = batch * seq
    tm = min(TOKEN_TILE, seq)
    h = x.reshape(t, d)
    row = lambda a: a.reshape(1, -1)
    for i in range(depth):
        j = i // 2
        wr_hi, wr_lo, br = _router_weights(router_c[i], router_c_b[i], router_f[i], router_f_b[i])
        if i % 2 == 0:
            ya, q, k, v = _even_in(h, row(norm_mix[i]), w_in_even[j].astype(BF16), conv_w_even[j],
                                   seq=seq, tm=tm)
            yb = _attention(q, k, v, batch=batch, seq=seq, tq=min(Q_TILE, seq))
            h1, xp, gates, eids, ranks, counts = _even_out(
                ya, yb, h, w_out_even[j].astype(BF16), row(norm_ffn[i]), wr_hi, wr_lo, br, tm=tm)
        else:
            h1, xp, gates, eids, ranks, counts = _odd_mixer(
                h, row(norm_mix[i]), w_in_odd[j].astype(BF16), row(g_v_odd[j]), w_s_odd[j],
                b_s_odd[j].T, w_out_odd[j].astype(BF16), row(norm_ffn[i]), wr_hi, wr_lo, br,
                tm=min(ODD_TOKEN_TILE, seq))
        yk = _moe(xp, gates, eids, ranks, counts, moe_w1[i], moe_w3[i], moe_w2[i])
        h = _post(h1, yk, gates, p[i].reshape(t, -1), row(norm_pl[i]), w_pg[i].astype(BF16),
                  w_pe[i].astype(BF16), row(final_norm), final=(i == depth - 1), tm=tm)
    return h.reshape(batch, seq, d)
```

```python
import functools

import jax
import jax.numpy as jnp
from jax import lax
from jax.experimental import pallas as pl
from jax.experimental.pallas import tpu as pltpu
from jax.experimental.pallas import tpu_sc as plsc

EPS = 1e-6
HEAD_DIM = 64
HEADS_PER_SLAB = 2
ATTN_SLABS = 4
CHUNK = 128
N_GROUPS = 4
EXP_PER_GROUP = 8
N_EXPERTS = N_GROUPS * EXP_PER_GROUP
TOP_K = 2
ROUTER_LANES = 128
FINE_ROW0 = 8

V7X_VMEM_BYTES = 64 * 1024 * 1024
VMEM_LIMIT = V7X_VMEM_BYTES - 8 * 1024 * 1024
V7X_SC_CORES = 2
V7X_SC_SUBCORES = 16
SC_WORKERS = V7X_SC_CORES * V7X_SC_SUBCORES
SC_ROWS = 32

TOKEN_TILE = 512
ODD_TOKEN_TILE = 512
Q_TILE = 128
SLOT_TILE = 512

LOG_F32_TINY_BOUND = -88.0
OFF_PENALTY = 1e30

F32 = jnp.float32
BF16 = jnp.bfloat16
U32 = jnp.uint32
I32 = jnp.int32


def _rms(x, g):
    return x * lax.rsqrt(jnp.mean(x * x, axis=-1, keepdims=True) + EPS) * g


def _const_spec(shape):
    nd = len(shape)
    return pl.BlockSpec(shape, lambda *_: (0,) * nd, pipeline_mode=pl.Buffered(1))


def _params(*sem):
    return pltpu.CompilerParams(dimension_semantics=sem, vmem_limit_bytes=VMEM_LIMIT)


def _even_in_kernel(h_ref, g_ref, w_ref, cw_ref, ya_ref, q_ref, k_ref, v_ref, zs_ref,
                    *, tiles_per_seq, a_width, sb_width):
    tm = h_ref.shape[0]
    xn = _rms(h_ref[...], g_ref[...]).astype(BF16)
    a3 = 3 * a_width

    @pl.when(pl.program_id(0) % tiles_per_seq == 0)
    def _():
        zs_ref[0:8, :] = jnp.zeros((8, a_width), F32)

    pa = jnp.dot(xn, w_ref[:, 0:a3], preferred_element_type=F32)
    z = pa[:, 2 * a_width:a3] * pa[:, 0:a_width]
    zs_ref[8:tm + 8, :] = z
    z1 = zs_ref[7:tm + 7, :]
    z2 = zs_ref[6:tm + 6, :]
    cw = cw_ref[...]
    conv = cw[0:1, :] * z2 + cw[1:2, :] * z1 + cw[2:3, :] * z
    ya_ref[...] = (pa[:, a_width:2 * a_width] * conv).astype(BF16)
    zs_ref[0:8, :] = zs_ref[tm:tm + 8, :]

    pq = jnp.dot(xn, w_ref[:, a3:a3 + 3 * sb_width], preferred_element_type=F32)
    q_ref[...] = (pq[:, 0:sb_width] * (HEAD_DIM ** -0.5)).astype(BF16)
    k_ref[...] = pq[:, sb_width:2 * sb_width].astype(BF16)
    v_ref[...] = pq[:, 2 * sb_width:3 * sb_width].astype(BF16)


def _even_in(h, g, w_in, conv_w, *, seq, tm):
    t, d = h.shape
    a_width = conv_w.shape[1]
    sb_width = (w_in.shape[1] - 3 * a_width) // 3
    kern = functools.partial(_even_in_kernel, tiles_per_seq=seq // tm, a_width=a_width,
                             sb_width=sb_width)
    row = lambda w: pl.BlockSpec((tm, w), lambda i: (i, 0))
    return pl.pallas_call(
        kern,
        grid=(t // tm,),
        in_specs=[row(d), _const_spec((1, d)), _const_spec(w_in.shape), _const_spec(conv_w.shape)],
        out_specs=[row(a_width), row(sb_width), row(sb_width), row(sb_width)],
        out_shape=[jax.ShapeDtypeStruct((t, a_width), BF16)] +
                  [jax.ShapeDtypeStruct((t, sb_width), BF16)] * 3,
        scratch_shapes=[pltpu.VMEM((tm + 8, a_width), F32)],
        compiler_params=_params("arbitrary"),
        name="even_in",
    )(h, g, w_in, conv_w)


def _attn_kernel(q_ref, k_ref, v_ref, o_ref, acc_ref, c_ref, *, tq):
    qi = pl.program_id(2)
    slab = HEADS_PER_SLAB * HEAD_DIM
    rows_slab = HEADS_PER_SLAB * tq
    rows = ATTN_SLABS * rows_slab
    lane = lax.broadcasted_iota(I32, (tq, slab), 1)
    qs = []
    for sl in range(ATTN_SLABS):
        q = q_ref[:, sl * slab:(sl + 1) * slab]
        zero = jnp.zeros_like(q)
        qs.append(jnp.concatenate([jnp.where(lane < HEAD_DIM, q, zero),
                                   jnp.where(lane >= HEAD_DIM, q, zero)], axis=0))
    r_i = lax.broadcasted_iota(I32, (tq, tq), 0)
    c_i = lax.broadcasted_iota(I32, (tq, tq), 1)
    upper = jnp.where(r_i > c_i, 1.0, 0.0).astype(BF16)
    t_loc = lax.broadcasted_iota(I32, (rows, tq), 0) & (tq - 1)
    s_loc = lax.broadcasted_iota(I32, (rows, tq), 1)
    causal = s_loc < t_loc

    def block(kb, c, mask):
        start = pl.multiple_of(kb * tq, tq)
        kk = k_ref[pl.ds(start, tq), :]
        vv = v_ref[pl.ds(start, tq), :]
        z = jnp.concatenate(
            [lax.dot_general(qs[sl], kk[:, sl * slab:(sl + 1) * slab], (((1,), (1,)), ((), ())),
                             preferred_element_type=F32) for sl in range(ATTN_SLABS)], axis=0)
        sp = jnp.maximum(z, 0.0) + jnp.log(1.0 + jnp.exp(-jnp.abs(z)))
        spm = sp if mask is None else jnp.where(mask, sp, 0.0)
        hi = spm.astype(BF16)
        lo = (spm - hi.astype(F32)).astype(BF16)
        later = (jnp.dot(hi, upper, preferred_element_type=F32) +
                 jnp.dot(lo, upper, preferred_element_type=F32))
        a = jnp.exp(z - sp - later - c)
        if mask is not None:
            a = jnp.where(mask, a, 0.0)
        a = a.astype(BF16)
        out = jnp.concatenate(
            [jnp.dot(a[sl * rows_slab:(sl + 1) * rows_slab], vv[:, sl * slab:(sl + 1) * slab],
                     preferred_element_type=F32) for sl in range(ATTN_SLABS)], axis=0)
        return out, jnp.sum(spm, axis=1, keepdims=True)

    out0, sum0 = block(qi, jnp.zeros((rows, 1), F32), causal)
    has_prev = qi >= 1
    out1, sum1 = block(jnp.maximum(qi - 1, 0), sum0 + jnp.where(has_prev, 0.0, OFF_PENALTY), None)
    c2 = sum0 + jnp.where(has_prev, sum1, 0.0)
    acc_ref[...] = out0 + out1
    c_ref[...] = c2

    def cond(carry):
        kb, cmin = carry
        return jnp.logical_and(kb >= 0, cmin < -LOG_F32_TINY_BOUND)

    def body(carry):
        kb, _ = carry
        c = c_ref[...]
        out, ssum = block(kb, c, None)
        acc_ref[...] += out
        c_new = c + ssum
        c_ref[...] = c_new
        return kb - 1, jnp.min(c_new)

    lax.while_loop(cond, body, (qi - 2, jnp.min(c2)))
    acc = acc_ref[...]
    for sl in range(ATTN_SLABS):
        lo_r = sl * rows_slab
        o_ref[:, sl * slab:(sl + 1) * slab] = jnp.where(
            lane < HEAD_DIM, acc[lo_r:lo_r + tq], acc[lo_r + tq:lo_r + 2 * tq]).astype(o_ref.dtype)


def _attention(q, k, v, *, batch, seq, tq):
    t, w = q.shape
    wblk = ATTN_SLABS * HEADS_PER_SLAB * HEAD_DIM
    nq = seq // tq
    qspec = pl.BlockSpec((tq, wblk), lambda b, s, i: (b * nq + i, s))
    kvspec = pl.BlockSpec((seq, wblk), lambda b, s, i: (b, s))
    rows = ATTN_SLABS * HEADS_PER_SLAB * tq
    return pl.pallas_call(
        functools.partial(_attn_kernel, tq=tq),
        grid=(batch, w // wblk, nq),
        in_specs=[qspec, kvspec, kvspec],
        out_specs=qspec,
        out_shape=jax.ShapeDtypeStruct((t, w), BF16),
        scratch_shapes=[pltpu.VMEM((rows, HEADS_PER_SLAB * HEAD_DIM), F32),
                        pltpu.VMEM((rows, 1), F32)],
        compiler_params=_params("arbitrary", "arbitrary", "arbitrary"),
        name="stick_breaking_attention",
    )(q, k, v)


def _pack_bf16_pairs(x):
    n = x.shape[1] // 2
    bits = lax.bitcast_convert_type(x.astype(BF16).astype(F32), U32)
    return bits[:, 0:n] | (bits[:, n:] >> 16)


def _unpack_bf16_pairs(u):
    a = lax.bitcast_convert_type(u & jnp.uint32(0xFFFF0000), F32)
    b = lax.bitcast_convert_type(u << 16, F32)
    return jnp.concatenate([a, b], axis=1)


def _router_tail(h1, g_ref, wr_hi_ref, wr_lo_ref, br_ref,
                 h1_ref, xp_ref, gate_ref, eid_ref, rank_ref, cnt_out_ref, cnt_ref):
    h1_ref[...] = h1
    xn = _rms(h1, g_ref[...])
    xp_ref[...] = _pack_bf16_pairs(xn)
    x_hi = xn.astype(BF16)
    x_lo = (xn - x_hi.astype(F32)).astype(BF16)
    w_hi = wr_hi_ref[...]
    logits = (jnp.dot(x_hi, w_hi, preferred_element_type=F32) +
              jnp.dot(x_hi, wr_lo_ref[...], preferred_element_type=F32) +
              jnp.dot(x_lo, w_hi, preferred_element_type=F32) + br_ref[...])
    lt = logits.T
    tm = lt.shape[1]
    row = lax.broadcasted_iota(I32, (EXP_PER_GROUP, tm), 0)
    neg = jnp.float32(-jnp.inf)

    coarse = jnp.where(row < N_GROUPS, lt[0:EXP_PER_GROUP], neg)
    cmax = jnp.max(coarse, axis=0, keepdims=True)
    gi = jnp.min(jnp.where(coarse == cmax, row, EXP_PER_GROUP), axis=0, keepdims=True)
    pg = 1.0 / jnp.sum(jnp.exp(coarse - cmax), axis=0, keepdims=True)

    fine = lt[FINE_ROW0:FINE_ROW0 + EXP_PER_GROUP]
    for grp in range(1, N_GROUPS):
        lo_r = FINE_ROW0 + grp * EXP_PER_GROUP
        fine = jnp.where(gi == grp, lt[lo_r:lo_r + EXP_PER_GROUP], fine)
    m1 = jnp.max(fine, axis=0, keepdims=True)
    i1 = jnp.min(jnp.where(fine == m1, row, EXP_PER_GROUP), axis=0, keepdims=True)
    rest = jnp.where(row == i1, neg, fine)
    m2 = jnp.max(rest, axis=0, keepdims=True)
    i2 = jnp.min(jnp.where(rest == m2, row, EXP_PER_GROUP), axis=0, keepdims=True)
    e21 = jnp.exp(m2 - m1)
    w1 = 1.0 / (1.0 + e21)
    gate_ref[...] = jnp.concatenate([pg * w1, pg * (e21 * w1)], axis=0)
    e0 = gi * EXP_PER_GROUP + i1
    e1 = gi * EXP_PER_GROUP + i2
    eid_ref[...] = jnp.concatenate([e0, e1], axis=0)

    @pl.when(pl.program_id(0) == 0)
    def _():
        cnt_ref[...] = jnp.zeros_like(cnt_ref)

    erow = lax.broadcasted_iota(I32, (N_EXPERTS, tm), 0)
    oh0 = jnp.where(erow == e0, 1.0, 0.0)
    oh1 = jnp.where(erow == e1, 1.0, 0.0)
    r_i = lax.broadcasted_iota(I32, (tm, tm), 0)
    c_i = lax.broadcasted_iota(I32, (tm, tm), 1)
    before = jnp.where(r_i < c_i, 1.0, 0.0).astype(BF16)
    pre0 = jnp.dot(oh0.astype(BF16), before, preferred_element_type=F32)
    pre1 = jnp.dot(oh1.astype(BF16), before, preferred_element_type=F32)
    tot0 = jnp.sum(oh0, axis=1, keepdims=True)
    tot1 = jnp.sum(oh1, axis=1, keepdims=True)
    base = cnt_ref[:, 0:1]
    r0 = jnp.sum(oh0 * (pre0 + base), axis=0, keepdims=True)
    r1 = jnp.sum(oh1 * (pre1 + (base + tot0)), axis=0, keepdims=True)
    rank_ref[...] = jnp.concatenate([r0, r1], axis=0).astype(I32)
    total = jnp.broadcast_to(base + tot0 + tot1, cnt_ref.shape)
    cnt_ref[...] = total
    cnt_out_ref[...] = total


def _tail_specs(t, d, tm):
    row = lambda w: pl.BlockSpec((tm, w), lambda i: (i, 0))
    lanes = pl.BlockSpec((TOP_K, tm), lambda i: (0, i))
    cnt = pl.BlockSpec((N_EXPERTS, ROUTER_LANES), lambda i: (0, 0))
    out_specs = [row(d), row(d // 2), lanes, lanes, lanes, cnt]
    out_shape = [jax.ShapeDtypeStruct((t, d), F32), jax.ShapeDtypeStruct((t, d // 2), U32),
                 jax.ShapeDtypeStruct((TOP_K, t), F32), jax.ShapeDtypeStruct((TOP_K, t), I32),
                 jax.ShapeDtypeStruct((TOP_K, t), I32),
                 jax.ShapeDtypeStruct((N_EXPERTS, ROUTER_LANES), F32)]
    scratch = [pltpu.VMEM((N_EXPERTS, ROUTER_LANES), F32)]
    return out_specs, out_shape, scratch


def _even_out_kernel(ya_ref, yb_ref, h_ref, w_ref, g_ref, wr_hi_ref, wr_lo_ref, br_ref,
                     *tail_refs):
    a_width = ya_ref.shape[1]
    mix = (jnp.dot(ya_ref[...], w_ref[0:a_width, :], preferred_element_type=F32) +
           jnp.dot(yb_ref[...], w_ref[a_width:, :], preferred_element_type=F32))
    _router_tail(h_ref[...] + mix, g_ref, wr_hi_ref, wr_lo_ref, br_ref, *tail_refs)


def _even_out(ya, yb, h, w_out, g_ffn, wr_hi, wr_lo, br, *, tm):
    t, d = h.shape
    out_specs, out_shape, scratch = _tail_specs(t, d, tm)
    row = lambda w: pl.BlockSpec((tm, w), lambda i: (i, 0))
    return pl.pallas_call(
        _even_out_kernel,
        grid=(t // tm,),
        in_specs=[row(ya.shape[1]), row(yb.shape[1]), row(d), _const_spec(w_out.shape),
                  _const_spec((1, d)), _const_spec(wr_hi.shape), _const_spec(wr_lo.shape),
                  _const_spec(br.shape)],
        out_specs=out_specs,
        out_shape=out_shape,
        scratch_shapes=scratch,
        compiler_params=_params("arbitrary"),
        name="even_out_router",
    )(ya, yb, h, w_out, g_ffn, wr_hi, wr_lo, br)


def _gelu_tanh(x):
    return 0.5 * x * (1.0 + jnp.tanh(0.7978845608028654 * (x + 0.044715 * (x * x * x))))


def _odd_kernel(h_ref, g_ref, w_in_ref, gv_ref, ws_ref, bs_ref, w_out_ref,
                gf_ref, wr_hi_ref, wr_lo_ref, br_ref, *rest):
    *tail_refs, y_ref = rest
    tm = h_ref.shape[0]
    c_width = gv_ref.shape[1]
    n_grp = ws_ref.shape[0]
    gdim = c_width // n_grp
    h = h_ref[...]
    xn = _rms(h, g_ref[...]).astype(BF16)
    u = _gelu_tanh(jnp.dot(xn, w_in_ref[:, 0:c_width], preferred_element_type=F32))
    v = _gelu_tanh(jnp.dot(xn, w_in_ref[:, c_width:], preferred_element_type=F32))
    v = _rms(v, gv_ref[...]).astype(BF16)
    r_i = lax.broadcasted_iota(I32, (CHUNK, CHUNK), 0)
    c_i = lax.broadcasted_iota(I32, (CHUNK, CHUNK), 1)
    causal = c_i <= r_i
    bs = bs_ref[...]
    for grp in range(n_grp):
        w_m = jnp.where(causal, ws_ref[grp], 0.0).astype(BF16)
        bias = bs[:, grp:grp + 1]
        cols = slice(grp * gdim, (grp + 1) * gdim)
        for c in range(tm // CHUNK):
            rows = slice(c * CHUNK, (c + 1) * CHUNK)
            gate = jnp.dot(w_m, v[rows, cols], preferred_element_type=F32) + bias
            y_ref[rows, cols] = (u[rows, cols] * gate).astype(BF16)
    mix = jnp.dot(y_ref[...], w_out_ref[...], preferred_element_type=F32)
    _router_tail(h + mix, gf_ref, wr_hi_ref, wr_lo_ref, br_ref, *tail_refs)


def _odd_mixer(h, g_mix, w_in, g_v, w_s, b_s_t, w_out, g_ffn, wr_hi, wr_lo, br, *, tm):
    t, d = h.shape
    out_specs, out_shape, scratch = _tail_specs(t, d, tm)
    row = pl.BlockSpec((tm, d), lambda i: (i, 0))
    consts = [g_mix, w_in, g_v, w_s, b_s_t, w_out, g_ffn, wr_hi, wr_lo, br]
    return pl.pallas_call(
        _odd_kernel,
        grid=(t // tm,),
        in_specs=[row] + [_const_spec(c.shape) for c in consts],
        out_specs=out_specs,
        out_shape=out_shape,
        scratch_shapes=scratch + [pltpu.VMEM((tm, g_v.shape[1]), BF16)],
        compiler_params=_params("arbitrary"),
        name="odd_mixer_router",
    )(h, *consts)


def _sc_mesh():
    return plsc.VectorSubcoreMesh(core_axis_name="c", subcore_axis_name="s")


def _sc_worker_base(rows_per_worker):
    wid = lax.axis_index("s") * V7X_SC_CORES + lax.axis_index("c")
    return wid * rows_per_worker


def _dispatch(xp, dest, n_slots):
    t, w = xp.shape
    per_worker = t // SC_WORKERS
    n_chunks = per_worker // SC_ROWS
    assert per_worker * SC_WORKERS == t and n_chunks * SC_ROWS == per_worker and n_chunks % 2 == 0

    idx_t = pltpu.VMEM((SC_ROWS,), I32)

    @functools.partial(
        pl.kernel, mesh=_sc_mesh(),
        out_type=jax.ShapeDtypeStruct((n_slots, w), xp.dtype),
        scratch_types=[pltpu.VMEM((2, SC_ROWS, w), xp.dtype), idx_t, idx_t, idx_t, idx_t,
                       pltpu.SemaphoreType.DMA((2,)), pltpu.SemaphoreType.DMA((2,))],
    )
    def kern(x_hbm, d_hbm, out_hbm, rows_v, i00, i01, i10, i11, rsem, wsem):
        base = _sc_worker_base(per_worker)
        idx = ((i00, i01), (i10, i11))

        def load(c, b):
            off = pl.multiple_of(base + c * SC_ROWS, 8)
            pltpu.async_copy(x_hbm.at[pl.ds(off, SC_ROWS)], rows_v.at[b], rsem.at[b])
            for k in range(TOP_K):
                pltpu.sync_copy(d_hbm.at[k, pl.ds(off, SC_ROWS)], idx[b][k])

        def wait_load(b):
            pltpu.make_async_copy(x_hbm.at[pl.ds(0, SC_ROWS)], rows_v.at[b], rsem.at[b]).wait()

        def scatter(b):
            for k in range(TOP_K):
                pltpu.async_copy(rows_v.at[b], out_hbm.at[idx[b][k]], wsem.at[b])

        def wait_scatter(b):
            for k in range(TOP_K):
                pltpu.make_async_copy(rows_v.at[b], out_hbm.at[idx[b][k]], wsem.at[b]).wait()

        load(0, 0)

        @pl.loop(0, n_chunks, step=2)
        def _(c0):
            for b in range(2):
                c = c0 + b
                wait_load(b)
                scatter(b)

                @pl.when(c + 1 < n_chunks)
                def _():
                    @pl.when(c >= 1)
                    def _():
                        wait_scatter(1 - b)
                    load(c + 1, 1 - b)

        wait_scatter(0)
        wait_scatter(1)

    return kern(xp, dest)


def _combine(y, idx):
    n = idx.shape[0]
    w = y.shape[1]
    rows = 2 * SC_ROWS
    per_worker = n // SC_WORKERS
    n_chunks = per_worker // rows
    assert per_worker * SC_WORKERS == n and n_chunks * rows == per_worker and n_chunks % 2 == 0

    idx_t = pltpu.VMEM((rows,), I32)

    @functools.partial(
        pl.kernel, mesh=_sc_mesh(),
        out_type=jax.ShapeDtypeStruct((n, w), y.dtype),
        scratch_types=[pltpu.VMEM((2, rows, w), y.dtype), idx_t, idx_t,
                       pltpu.SemaphoreType.DMA((2,)), pltpu.SemaphoreType.DMA((2,))],
    )
    def kern(y_hbm, i_hbm, out_hbm, rows_v, i0, i1, gsem, wsem):
        base = _sc_worker_base(per_worker)
        ibuf = (i0, i1)

        def gather(c, b):
            off = pl.multiple_of(base + c * rows, 8)
            pltpu.sync_copy(i_hbm.at[pl.ds(off, rows)], ibuf[b])
            pltpu.async_copy(y_hbm.at[ibuf[b]], rows_v.at[b], gsem.at[b])

        def wait_gather(b):
            pltpu.make_async_copy(y_hbm.at[ibuf[b]], rows_v.at[b], gsem.at[b]).wait()

        def write(c, b):
            off = pl.multiple_of(base + c * rows, 8)
            pltpu.async_copy(rows_v.at[b], out_hbm.at[pl.ds(off, rows)], wsem.at[b])

        def wait_write(b):
            pltpu.make_async_copy(rows_v.at[b], out_hbm.at[pl.ds(0, rows)], wsem.at[b]).wait()

        gather(0, 0)

        @pl.loop(0, n_chunks, step=2)
        def _(c0):
            for b in range(2):
                c = c0 + b
                wait_gather(b)
                write(c, b)

                @pl.when(c + 1 < n_chunks)
                def _():
                    @pl.when(c >= 1)
                    def _():
                        wait_write(1 - b)
                    gather(c + 1, 1 - b)

        wait_write(0)
        wait_write(1)

    return kern(y, idx)


def _expert_kernel(te_ref, tv_ref, nu_ref, x_ref, w1_ref, w3_ref, w2_ref, y_ref,
                   w1b_ref, w3b_ref, w2b_ref):
    i = pl.program_id(0)

    @pl.when(i < nu_ref[0])
    def _():
        prev = te_ref[jnp.maximum(i - 1, 0)]

        @pl.when(jnp.logical_or(i == 0, te_ref[i] != prev))
        def _():
            w1b_ref[...] = w1_ref[0, 0].astype(BF16)
            w3b_ref[...] = w3_ref[0, 0].astype(BF16)
            w2b_ref[...] = w2_ref[0, 0].astype(BF16)

        xu = x_ref[...]
        rowid = lax.broadcasted_iota(I32, xu.shape, 0)
        xu = jnp.where(rowid < tv_ref[i], xu, jnp.zeros_like(xu))
        x = _unpack_bf16_pairs(xu).astype(BF16)
        h1 = jnp.dot(x, w1b_ref[...], preferred_element_type=F32)
        h3 = jnp.dot(x, w3b_ref[...], preferred_element_type=F32)
        he = (h1 * jax.nn.sigmoid(h1)) * h3
        y = jnp.dot(he.astype(BF16), w2b_ref[...], preferred_element_type=F32)
        y_ref[...] = _pack_bf16_pairs(y)


def _experts(x_sorted, tile_e, tile_valid, n_used, w1, w3, w2, layer):
    n_slots, wp = x_sorted.shape
    _, n_e, d, d_e = w1.shape
    ts = SLOT_TILE
    last = lambda nu: jnp.maximum(nu[0] - 1, 0)
    xspec = pl.BlockSpec((ts, wp), lambda i, te, tv, nu: (jnp.minimum(i, last(nu)), 0))
    wspec = lambda a, b: pl.BlockSpec((1, 1, a, b), lambda i, te, tv, nu: (layer, te[i], 0, 0))
    return pl.pallas_call(
        _expert_kernel,
        grid_spec=pltpu.PrefetchScalarGridSpec(
            num_scalar_prefetch=3,
            grid=(n_slots // ts,),
            in_specs=[xspec, wspec(d, d_e), wspec(d, d_e), wspec(d_e, d)],
            out_specs=xspec,
            scratch_shapes=[pltpu.VMEM((d, d_e), BF16), pltpu.VMEM((d, d_e), BF16),
                            pltpu.VMEM((d_e, d), BF16)],
        ),
        out_shape=jax.ShapeDtypeStruct((n_slots, d // 2), U32),
        compiler_params=_params("arbitrary"),
        name="moe_experts",
    )(tile_e, tile_valid, n_used, x_sorted, w1, w3, w2)


def _slot_plan(eids, ranks, counts, n_tiles):
    ts = SLOT_TILE
    cnt = counts[:, 0].astype(I32)
    padded = (cnt + ts - 1) // ts * ts
    ends = jnp.cumsum(padded)
    starts = ends - padded
    onehot = eids[None] == jnp.arange(N_EXPERTS, dtype=I32)[:, None, None]
    dest = ranks + jnp.sum(jnp.where(onehot, starts[:, None, None], 0), axis=0)
    tile_lo = jnp.arange(n_tiles, dtype=I32) * ts
    tile_e = jnp.minimum(jnp.sum(tile_lo[:, None] >= ends[None, :], axis=1), N_EXPERTS - 1)
    tile_e = tile_e.astype(I32)
    tile_valid = jnp.clip(starts[tile_e] + cnt[tile_e] - tile_lo, 0, ts).astype(I32)
    n_used = (ends[-1:] // ts).astype(I32)
    return dest, tile_e, tile_valid, n_used


def _post_kernel(h1_ref, y0_ref, y1_ref, gate_ref, p_ref, g_ref, wpg_ref, wpe_ref, gfin_ref,
                 o_ref, *, final):
    gates = gate_ref[...]
    tm = gates.shape[1]
    gt = jnp.concatenate([gates, jnp.zeros((8 - TOP_K, tm), F32)], axis=0).T
    h2 = h1_ref[...] + (gt[:, 0:1] * _unpack_bf16_pairs(y0_ref[...]) +
                        gt[:, 1:2] * _unpack_bf16_pairs(y1_ref[...]))
    gate = jax.nn.sigmoid(jnp.dot(_rms(h2, g_ref[...]).astype(BF16), wpg_ref[...],
                                  preferred_element_type=F32))
    pe = jnp.dot(p_ref[...].astype(BF16), wpe_ref[...], preferred_element_type=F32)
    h3 = h2 + gate * pe
    o_ref[...] = _rms(h3, gfin_ref[...]) if final else h3


def _post(h1, yk, gates, p, g_pl, w_pg, w_pe, g_fin, *, layer, final, tm):
    t, d = h1.shape
    nt = t // tm
    wy = yk.shape[1]
    row = lambda w: pl.BlockSpec((tm, w), lambda i: (i, 0))
    return pl.pallas_call(
        functools.partial(_post_kernel, final=final),
        grid=(nt,),
        in_specs=[row(d), row(wy), pl.BlockSpec((tm, wy), lambda i: (i + nt, 0)),
                  pl.BlockSpec((TOP_K, tm), lambda i: (0, i)),
                  pl.BlockSpec((tm, p.shape[1]), lambda i: (i + layer * nt, 0)),
                  _const_spec((1, d)), _const_spec(w_pg.shape), _const_spec(w_pe.shape),
                  _const_spec((1, d))],
        out_specs=row(d),
        out_shape=jax.ShapeDtypeStruct((t, d), F32),
        compiler_params=_params("arbitrary"),
        name="layer_post",
    )(h1, yk, yk, gates, p, g_pl, w_pg, w_pe, g_fin)


def _router_weights(wc, bc, wf, bf):
    d = wc.shape[0]
    w = jnp.zeros((d, ROUTER_LANES), F32)
    w = w.at[:, 0:N_GROUPS].set(wc).at[:, FINE_ROW0:FINE_ROW0 + N_EXPERTS].set(wf)
    b = jnp.zeros((1, ROUTER_LANES), F32)
    b = b.at[0, 0:N_GROUPS].set(bc).at[0, FINE_ROW0:FINE_ROW0 + N_EXPERTS].set(bf)
    hi = w.astype(BF16)
    lo = (w - hi.astype(F32)).astype(BF16)
    return hi, lo, b


def _moe(xp, eids, ranks, counts, w1, w3, w2, layer):
    t = xp.shape[0]
    n_tiles = TOP_K * t // SLOT_TILE + N_EXPERTS
    dest, tile_e, tile_valid, n_used = _slot_plan(eids, ranks, counts, n_tiles)
    x_sorted = _dispatch(xp, dest, n_tiles * SLOT_TILE)
    y_sorted = _experts(x_sorted, tile_e, tile_valid, n_used, w1, w3, w2, layer)
    return _combine(y_sorted, dest.reshape(-1))


def kernel(x, p, norm_mix, norm_ffn, norm_pl, final_norm, w_in_even, conv_w_even, w_out_even, w_in_odd, g_v_odd, w_s_odd, b_s_odd, w_out_odd, router_c, router_c_b, router_f, router_f_b, moe_w1, moe_w3, moe_w2, w_pe, w_pg):
    batch, seq, d = x.shape
    depth = p.shape[0]
    t = batch * seq
    tm = min(TOKEN_TILE, seq)
    h = x.reshape(t, d)
    p_rows = p.reshape(depth * t, -1)
    row = lambda a: a.reshape(1, -1)
    for i in range(depth):
        j = i // 2
        wr_hi, wr_lo, br = _router_weights(router_c[i], router_c_b[i], router_f[i], router_f_b[i])
        if i % 2 == 0:
            ya, q, k, v = _even_in(h, row(norm_mix[i]), w_in_even[j].astype(BF16), conv_w_even[j],
                                   seq=seq, tm=tm)
            yb = _attention(q, k, v, batch=batch, seq=seq, tq=min(Q_TILE, seq))
            h1, xp, gates, eids, ranks, counts = _even_out(
                ya, yb, h, w_out_even[j].astype(BF16), row(norm_ffn[i]), wr_hi, wr_lo, br, tm=tm)
        else:
            h1, xp, gates, eids, ranks, counts = _odd_mixer(
                h, row(norm_mix[i]), w_in_odd[j].astype(BF16), row(g_v_odd[j]), w_s_odd[j],
                b_s_odd[j].T, w_out_odd[j].astype(BF16), row(norm_ffn[i]), wr_hi, wr_lo, br,
                tm=min(ODD_TOKEN_TILE, seq))
        yk = _moe(xp, eids, ranks, counts, moe_w1, moe_w3, moe_w2, i)
        h = _post(h1, yk, gates, p_rows, row(norm_pl[i]), w_pg[i].astype(BF16),
                  w_pe[i].astype(BF16), row(final_norm), layer=i, final=(i == depth - 1), tm=tm)
    return h.reshape(batch, seq, d)
```

```python
import functools

import jax
import jax.numpy as jnp
from jax import lax
from jax.experimental import pallas as pl
from jax.experimental.pallas import tpu as pltpu
from jax.experimental.pallas import tpu_sc as plsc

EPS = 1e-6
HEAD_DIM = 64
HEADS_PER_SLAB = 2
ATTN_SLABS = 4
CHUNK = 128
N_GROUPS = 4
EXP_PER_GROUP = 8
N_EXPERTS = N_GROUPS * EXP_PER_GROUP
TOP_K = 2
ROUTER_LANES = 128
FINE_ROW0 = 8

V7X_VMEM_BYTES = 64 * 1024 * 1024
VMEM_LIMIT = V7X_VMEM_BYTES - 8 * 1024 * 1024
V7X_SC_CORES = 2
V7X_SC_SUBCORES = 16
SC_WORKERS = V7X_SC_CORES * V7X_SC_SUBCORES
SC_ROWS = 32

N_STREAMS = 2
TOKEN_TILE = 512
ODD_TOKEN_TILE = 512
Q_TILE = 128
SLOT_TILE = 512

LOG_F32_TINY_BOUND = -88.0
OFF_PENALTY = 1e30

F32 = jnp.float32
BF16 = jnp.bfloat16
U32 = jnp.uint32
I32 = jnp.int32


def _rms(x, g):
    return x * lax.rsqrt(jnp.mean(x * x, axis=-1, keepdims=True) + EPS) * g


def _const_spec(shape):
    nd = len(shape)
    return pl.BlockSpec(shape, lambda *_: (0,) * nd, pipeline_mode=pl.Buffered(1))


def _params(*sem):
    return pltpu.CompilerParams(dimension_semantics=sem, vmem_limit_bytes=VMEM_LIMIT)


def _even_in_kernel(h_ref, g_ref, w_ref, cw_ref, ya_ref, q_ref, k_ref, v_ref, zs_ref,
                    *, tiles_per_seq, a_width, sb_width):
    tm = h_ref.shape[0]
    xn = _rms(h_ref[...], g_ref[...]).astype(BF16)
    a3 = 3 * a_width

    @pl.when(pl.program_id(0) % tiles_per_seq == 0)
    def _():
        zs_ref[0:8, :] = jnp.zeros((8, a_width), F32)

    pa = jnp.dot(xn, w_ref[:, 0:a3], preferred_element_type=F32)
    z = pa[:, 2 * a_width:a3] * pa[:, 0:a_width]
    zs_ref[8:tm + 8, :] = z
    z1 = zs_ref[7:tm + 7, :]
    z2 = zs_ref[6:tm + 6, :]
    cw = cw_ref[...]
    conv = cw[0:1, :] * z2 + cw[1:2, :] * z1 + cw[2:3, :] * z
    ya_ref[...] = (pa[:, a_width:2 * a_width] * conv).astype(BF16)
    zs_ref[0:8, :] = zs_ref[tm:tm + 8, :]

    pq = jnp.dot(xn, w_ref[:, a3:a3 + 3 * sb_width], preferred_element_type=F32)
    q_ref[...] = (pq[:, 0:sb_width] * (HEAD_DIM ** -0.5)).astype(BF16)
    k_ref[...] = pq[:, sb_width:2 * sb_width].astype(BF16)
    v_ref[...] = pq[:, 2 * sb_width:3 * sb_width].astype(BF16)


def _even_in(h, g, w_in, conv_w, *, seq, tm, t, row0):
    d = h.shape[1]
    a_width = conv_w.shape[1]
    sb_width = (w_in.shape[1] - 3 * a_width) // 3
    kern = functools.partial(_even_in_kernel, tiles_per_seq=seq // tm, a_width=a_width,
                             sb_width=sb_width)
    row = lambda w: pl.BlockSpec((tm, w), lambda i: (i, 0))
    blk0 = row0 // tm
    return pl.pallas_call(
        kern,
        grid=(t // tm,),
        in_specs=[pl.BlockSpec((tm, d), lambda i: (i + blk0, 0)), _const_spec((1, d)),
                  _const_spec(w_in.shape), _const_spec(conv_w.shape)],
        out_specs=[row(a_width), row(sb_width), row(sb_width), row(sb_width)],
        out_shape=[jax.ShapeDtypeStruct((t, a_width), BF16)] +
                  [jax.ShapeDtypeStruct((t, sb_width), BF16)] * 3,
        scratch_shapes=[pltpu.VMEM((tm + 8, a_width), F32)],
        compiler_params=_params("arbitrary"),
        name="even_in",
    )(h, g, w_in, conv_w)


def _attn_kernel(q_ref, k_ref, v_ref, o_ref, acc_ref, c_ref, *, tq):
    qi = pl.program_id(2)
    slab = HEADS_PER_SLAB * HEAD_DIM
    rows_slab = HEADS_PER_SLAB * tq
    rows = ATTN_SLABS * rows_slab
    lane = lax.broadcasted_iota(I32, (tq, slab), 1)
    qs = []
    for sl in range(ATTN_SLABS):
        q = q_ref[:, sl * slab:(sl + 1) * slab]
        zero = jnp.zeros_like(q)
        qs.append(jnp.concatenate([jnp.where(lane < HEAD_DIM, q, zero),
                                   jnp.where(lane >= HEAD_DIM, q, zero)], axis=0))
    r_i = lax.broadcasted_iota(I32, (tq, tq), 0)
    c_i = lax.broadcasted_iota(I32, (tq, tq), 1)
    upper = jnp.where(r_i > c_i, 1.0, 0.0).astype(BF16)
    t_loc = lax.broadcasted_iota(I32, (rows, tq), 0) & (tq - 1)
    s_loc = lax.broadcasted_iota(I32, (rows, tq), 1)
    causal = s_loc < t_loc

    def block(kb, c, mask):
        start = pl.multiple_of(kb * tq, tq)
        kk = k_ref[pl.ds(start, tq), :]
        vv = v_ref[pl.ds(start, tq), :]
        z = jnp.concatenate(
            [lax.dot_general(qs[sl], kk[:, sl * slab:(sl + 1) * slab], (((1,), (1,)), ((), ())),
                             preferred_element_type=F32) for sl in range(ATTN_SLABS)], axis=0)
        sp = jnp.maximum(z, 0.0) + jnp.log(1.0 + jnp.exp(-jnp.abs(z)))
        spm = sp if mask is None else jnp.where(mask, sp, 0.0)
        hi = spm.astype(BF16)
        lo = (spm - hi.astype(F32)).astype(BF16)
        later = (jnp.dot(hi, upper, preferred_element_type=F32) +
                 jnp.dot(lo, upper, preferred_element_type=F32))
        a = jnp.exp(z - sp - later - c)
        if mask is not None:
            a = jnp.where(mask, a, 0.0)
        a = a.astype(BF16)
        out = jnp.concatenate(
            [jnp.dot(a[sl * rows_slab:(sl + 1) * rows_slab], vv[:, sl * slab:(sl + 1) * slab],
                     preferred_element_type=F32) for sl in range(ATTN_SLABS)], axis=0)
        return out, jnp.sum(spm, axis=1, keepdims=True)

    out0, sum0 = block(qi, jnp.zeros((rows, 1), F32), causal)
    has_prev = qi >= 1
    out1, sum1 = block(jnp.maximum(qi - 1, 0), sum0 + jnp.where(has_prev, 0.0, OFF_PENALTY), None)
    c2 = sum0 + jnp.where(has_prev, sum1, 0.0)
    acc_ref[...] = out0 + out1
    c_ref[...] = c2

    def cond(carry):
        kb, cmin = carry
        return jnp.logical_and(kb >= 0, cmin < -LOG_F32_TINY_BOUND)

    def body(carry):
        kb, _ = carry
        c = c_ref[...]
        out, ssum = block(kb, c, None)
        acc_ref[...] += out
        c_new = c + ssum
        c_ref[...] = c_new
        return kb - 1, jnp.min(c_new)

    lax.while_loop(cond, body, (qi - 2, jnp.min(c2)))
    acc = acc_ref[...]
    for sl in range(ATTN_SLABS):
        lo_r = sl * rows_slab
        o_ref[:, sl * slab:(sl + 1) * slab] = jnp.where(
            lane < HEAD_DIM, acc[lo_r:lo_r + tq], acc[lo_r + tq:lo_r + 2 * tq]).astype(o_ref.dtype)


def _attention(q, k, v, *, batch, seq, tq):
    t, w = q.shape
    wblk = ATTN_SLABS * HEADS_PER_SLAB * HEAD_DIM
    nq = seq // tq
    qspec = pl.BlockSpec((tq, wblk), lambda b, s, i: (b * nq + i, s))
    kvspec = pl.BlockSpec((seq, wblk), lambda b, s, i: (b, s))
    rows = ATTN_SLABS * HEADS_PER_SLAB * tq
    return pl.pallas_call(
        functools.partial(_attn_kernel, tq=tq),
        grid=(batch, w // wblk, nq),
        in_specs=[qspec, kvspec, kvspec],
        out_specs=qspec,
        out_shape=jax.ShapeDtypeStruct((t, w), BF16),
        scratch_shapes=[pltpu.VMEM((rows, HEADS_PER_SLAB * HEAD_DIM), F32),
                        pltpu.VMEM((rows, 1), F32)],
        compiler_params=_params("arbitrary", "arbitrary", "arbitrary"),
        name="stick_breaking_attention",
    )(q, k, v)


def _pack_bf16_pairs(x):
    n = x.shape[1] // 2
    bits = lax.bitcast_convert_type(x.astype(BF16).astype(F32), U32)
    return bits[:, 0:n] | (bits[:, n:] >> 16)


def _unpack_bf16_pairs(u):
    a = lax.bitcast_convert_type(u & jnp.uint32(0xFFFF0000), F32)
    b = lax.bitcast_convert_type(u << 16, F32)
    return jnp.concatenate([a, b], axis=1)


def _router_tail(h1, g_ref, wr_ref, br_ref,
                 h1_ref, xp_ref, gate_ref, eid_ref, rank_ref, cnt_out_ref, cnt_ref):
    h1_ref[...] = h1
    xn = _rms(h1, g_ref[...])
    xp_ref[...] = _pack_bf16_pairs(xn)
    x_hi = xn.astype(BF16)
    x_lo = (xn - x_hi.astype(F32)).astype(BF16)
    both = jnp.dot(x_hi, wr_ref[...], preferred_element_type=F32)
    logits = (both[:, 0:ROUTER_LANES] + both[:, ROUTER_LANES:] +
              jnp.dot(x_lo, wr_ref[:, 0:ROUTER_LANES], preferred_element_type=F32) + br_ref[...])
    lt = logits.T
    tm = lt.shape[1]
    row = lax.broadcasted_iota(I32, (EXP_PER_GROUP, tm), 0)
    neg = jnp.float32(-jnp.inf)

    coarse = jnp.where(row < N_GROUPS, lt[0:EXP_PER_GROUP], neg)
    cmax = jnp.max(coarse, axis=0, keepdims=True)
    gi = jnp.min(jnp.where(coarse == cmax, row, EXP_PER_GROUP), axis=0, keepdims=True)
    pg = 1.0 / jnp.sum(jnp.exp(coarse - cmax), axis=0, keepdims=True)

    fine = lt[FINE_ROW0:FINE_ROW0 + EXP_PER_GROUP]
    for grp in range(1, N_GROUPS):
        lo_r = FINE_ROW0 + grp * EXP_PER_GROUP
        fine = jnp.where(gi == grp, lt[lo_r:lo_r + EXP_PER_GROUP], fine)
    m1 = jnp.max(fine, axis=0, keepdims=True)
    i1 = jnp.min(jnp.where(fine == m1, row, EXP_PER_GROUP), axis=0, keepdims=True)
    rest = jnp.where(row == i1, neg, fine)
    m2 = jnp.max(rest, axis=0, keepdims=True)
    i2 = jnp.min(jnp.where(rest == m2, row, EXP_PER_GROUP), axis=0, keepdims=True)
    e21 = jnp.exp(m2 - m1)
    w1 = 1.0 / (1.0 + e21)
    gate_ref[...] = jnp.concatenate([pg * w1, pg * (e21 * w1)], axis=0)
    e0 = gi * EXP_PER_GROUP + i1
    e1 = gi * EXP_PER_GROUP + i2
    eid_ref[...] = jnp.concatenate([e0, e1], axis=0)

    @pl.when(pl.program_id(0) == 0)
    def _():
        cnt_ref[...] = jnp.zeros_like(cnt_ref)

    erow = lax.broadcasted_iota(I32, (N_EXPERTS, tm), 0)
    oh0 = jnp.where(erow == e0, 1.0, 0.0)
    oh1 = jnp.where(erow == e1, 1.0, 0.0)
    r_i = lax.broadcasted_iota(I32, (tm, tm), 0)
    c_i = lax.broadcasted_iota(I32, (tm, tm), 1)
    before = jnp.where(r_i < c_i, 1.0, 0.0).astype(BF16)
    pre0 = jnp.dot(oh0.astype(BF16), before, preferred_element_type=F32)
    pre1 = jnp.dot(oh1.astype(BF16), before, preferred_element_type=F32)
    tot0 = jnp.sum(oh0, axis=1, keepdims=True)
    tot1 = jnp.sum(oh1, axis=1, keepdims=True)
    base = cnt_ref[:, 0:1]
    r0 = jnp.sum(oh0 * (pre0 + base), axis=0, keepdims=True)
    r1 = jnp.sum(oh1 * (pre1 + (base + tot0)), axis=0, keepdims=True)
    rank_ref[...] = jnp.concatenate([r0, r1], axis=0).astype(I32)
    total = jnp.broadcast_to(base + tot0 + tot1, cnt_ref.shape)
    cnt_ref[...] = total
    cnt_out_ref[...] = total


def _tail_specs(t, d, tm):
    row = lambda w: pl.BlockSpec((tm, w), lambda i: (i, 0))
    lanes = pl.BlockSpec((TOP_K, tm), lambda i: (0, i))
    cnt = pl.BlockSpec((N_EXPERTS, ROUTER_LANES), lambda i: (0, 0))
    out_specs = [row(d), row(d // 2), lanes, lanes, lanes, cnt]
    out_shape = [jax.ShapeDtypeStruct((t, d), F32), jax.ShapeDtypeStruct((t, d // 2), U32),
                 jax.ShapeDtypeStruct((TOP_K, t), F32), jax.ShapeDtypeStruct((TOP_K, t), I32),
                 jax.ShapeDtypeStruct((TOP_K, t), I32),
                 jax.ShapeDtypeStruct((N_EXPERTS, ROUTER_LANES), F32)]
    scratch = [pltpu.VMEM((N_EXPERTS, ROUTER_LANES), F32)]
    return out_specs, out_shape, scratch


def _even_out_kernel(ya_ref, yb_ref, h_ref, w_ref, g_ref, wr_ref, br_ref, *tail_refs):
    a_width = ya_ref.shape[1]
    mix = (jnp.dot(ya_ref[...], w_ref[0:a_width, :], preferred_element_type=F32) +
           jnp.dot(yb_ref[...], w_ref[a_width:, :], preferred_element_type=F32))
    _router_tail(h_ref[...] + mix, g_ref, wr_ref, br_ref, *tail_refs)


def _even_out(ya, yb, h, w_out, g_ffn, wr, br, *, tm, row0):
    t = ya.shape[0]
    d = h.shape[1]
    out_specs, out_shape, scratch = _tail_specs(t, d, tm)
    row = lambda w: pl.BlockSpec((tm, w), lambda i: (i, 0))
    blk0 = row0 // tm
    return pl.pallas_call(
        _even_out_kernel,
        grid=(t // tm,),
        in_specs=[row(ya.shape[1]), row(yb.shape[1]),
                  pl.BlockSpec((tm, d), lambda i: (i + blk0, 0)), _const_spec(w_out.shape),
                  _const_spec((1, d)), _const_spec(wr.shape), _const_spec(br.shape)],
        out_specs=out_specs,
        out_shape=out_shape,
        scratch_shapes=scratch,
        compiler_params=_params("arbitrary"),
        name="even_out_router",
    )(ya, yb, h, w_out, g_ffn, wr, br)


def _gelu_tanh(x):
    return 0.5 * x * (1.0 + jnp.tanh(0.7978845608028654 * (x + 0.044715 * (x * x * x))))


def _odd_kernel(h_ref, g_ref, w_in_ref, gv_ref, ws_ref, bs_ref, w_out_ref,
                gf_ref, wr_ref, br_ref, *rest):
    *tail_refs, y_ref = rest
    tm = h_ref.shape[0]
    c_width = gv_ref.shape[1]
    n_grp = ws_ref.shape[0]
    gdim = c_width // n_grp
    h = h_ref[...]
    xn = _rms(h, g_ref[...]).astype(BF16)
    u = _gelu_tanh(jnp.dot(xn, w_in_ref[:, 0:c_width], preferred_element_type=F32))
    v = _gelu_tanh(jnp.dot(xn, w_in_ref[:, c_width:], preferred_element_type=F32))
    v = _rms(v, gv_ref[...]).astype(BF16)
    r_i = lax.broadcasted_iota(I32, (CHUNK, CHUNK), 0)
    c_i = lax.broadcasted_iota(I32, (CHUNK, CHUNK), 1)
    causal = c_i <= r_i
    bs = bs_ref[...]
    for grp in range(n_grp):
        w_m = jnp.where(causal, ws_ref[grp], 0.0).astype(BF16)
        bias = bs[:, grp:grp + 1]
        cols = slice(grp * gdim, (grp + 1) * gdim)
        for c in range(tm // CHUNK):
            rows = slice(c * CHUNK, (c + 1) * CHUNK)
            gate = jnp.dot(w_m, v[rows, cols], preferred_element_type=F32) + bias
            y_ref[rows, cols] = (u[rows, cols] * gate).astype(BF16)
    mix = jnp.dot(y_ref[...], w_out_ref[...], preferred_element_type=F32)
    _router_tail(h + mix, gf_ref, wr_ref, br_ref, *tail_refs)


def _odd_mixer(h, g_mix, w_in, g_v, w_s, b_s_t, w_out, g_ffn, wr, br, *, tm):
    t, d = h.shape
    out_specs, out_shape, scratch = _tail_specs(t, d, tm)
    row = pl.BlockSpec((tm, d), lambda i: (i, 0))
    consts = [g_mix, w_in, g_v, w_s, b_s_t, w_out, g_ffn, wr, br]
    return pl.pallas_call(
        _odd_kernel,
        grid=(t // tm,),
        in_specs=[row] + [_const_spec(c.shape) for c in consts],
        out_specs=out_specs,
        out_shape=out_shape,
        scratch_shapes=scratch + [pltpu.VMEM((tm, g_v.shape[1]), BF16)],
        compiler_params=_params("arbitrary"),
        name="odd_mixer_router",
    )(h, *consts)


def _sc_mesh():
    return plsc.VectorSubcoreMesh(core_axis_name="c", subcore_axis_name="s")


def _sc_worker_base(rows_per_worker):
    wid = lax.axis_index("s") * V7X_SC_CORES + lax.axis_index("c")
    return wid * rows_per_worker


def _dispatch(xp, dest, n_slots):
    t, w = xp.shape
    per_worker = t // SC_WORKERS
    n_chunks = per_worker // SC_ROWS
    assert per_worker * SC_WORKERS == t and n_chunks * SC_ROWS == per_worker and n_chunks % 2 == 0

    idx_t = pltpu.VMEM((SC_ROWS,), I32)

    @functools.partial(
        pl.kernel, mesh=_sc_mesh(),
        out_type=jax.ShapeDtypeStruct((n_slots, w), xp.dtype),
        scratch_types=[pltpu.VMEM((2, SC_ROWS, w), xp.dtype), idx_t, idx_t, idx_t, idx_t,
                       pltpu.SemaphoreType.DMA((2,)), pltpu.SemaphoreType.DMA((2,))],
    )
    def kern(x_hbm, d_hbm, out_hbm, rows_v, i00, i01, i10, i11, rsem, wsem):
        base = _sc_worker_base(per_worker)
        idx = ((i00, i01), (i10, i11))

        def load(c, b):
            off = pl.multiple_of(base + c * SC_ROWS, 8)
            pltpu.async_copy(x_hbm.at[pl.ds(off, SC_ROWS)], rows_v.at[b], rsem.at[b])
            for k in range(TOP_K):
                pltpu.sync_copy(d_hbm.at[k, pl.ds(off, SC_ROWS)], idx[b][k])

        def wait_load(b):
            pltpu.make_async_copy(x_hbm.at[pl.ds(0, SC_ROWS)], rows_v.at[b], rsem.at[b]).wait()

        def scatter(b):
            for k in range(TOP_K):
                pltpu.async_copy(rows_v.at[b], out_hbm.at[idx[b][k]], wsem.at[b])

        def wait_scatter(b):
            for k in range(TOP_K):
                pltpu.make_async_copy(rows_v.at[b], out_hbm.at[idx[b][k]], wsem.at[b]).wait()

        load(0, 0)

        @pl.loop(0, n_chunks, step=2)
        def _(c0):
            for b in range(2):
                c = c0 + b
                wait_load(b)
                scatter(b)

                @pl.when(c + 1 < n_chunks)
                def _():
                    @pl.when(c >= 1)
                    def _():
                        wait_scatter(1 - b)
                    load(c + 1, 1 - b)

        wait_scatter(0)
        wait_scatter(1)

    return kern(xp, dest)


def _combine(y, idx):
    n = idx.shape[0]
    w = y.shape[1]
    rows = 2 * SC_ROWS
    per_worker = n // SC_WORKERS
    n_chunks = per_worker // rows
    assert per_worker * SC_WORKERS == n and n_chunks * rows == per_worker and n_chunks % 2 == 0

    idx_t = pltpu.VMEM((rows,), I32)

    @functools.partial(
        pl.kernel, mesh=_sc_mesh(),
        out_type=jax.ShapeDtypeStruct((n, w), y.dtype),
        scratch_types=[pltpu.VMEM((2, rows, w), y.dtype), idx_t, idx_t,
                       pltpu.SemaphoreType.DMA((2,)), pltpu.SemaphoreType.DMA((2,))],
    )
    def kern(y_hbm, i_hbm, out_hbm, rows_v, i0, i1, gsem, wsem):
        base = _sc_worker_base(per_worker)
        ibuf = (i0, i1)

        def gather(c, b):
            off = pl.multiple_of(base + c * rows, 8)
            pltpu.sync_copy(i_hbm.at[pl.ds(off, rows)], ibuf[b])
            pltpu.async_copy(y_hbm.at[ibuf[b]], rows_v.at[b], gsem.at[b])

        def wait_gather(b):
            pltpu.make_async_copy(y_hbm.at[ibuf[b]], rows_v.at[b], gsem.at[b]).wait()

        def write(c, b):
            off = pl.multiple_of(base + c * rows, 8)
            pltpu.async_copy(rows_v.at[b], out_hbm.at[pl.ds(off, rows)], wsem.at[b])

        def wait_write(b):
            pltpu.make_async_copy(rows_v.at[b], out_hbm.at[pl.ds(0, rows)], wsem.at[b]).wait()

        gather(0, 0)

        @pl.loop(0, n_chunks, step=2)
        def _(c0):
            for b in range(2):
                c = c0 + b
                wait_gather(b)
                write(c, b)

                @pl.when(c + 1 < n_chunks)
                def _():
                    @pl.when(c >= 1)
                    def _():
                        wait_write(1 - b)
                    gather(c + 1, 1 - b)

        wait_write(0)
        wait_write(1)

    return kern(y, idx)


def _expert_kernel(te_ref, tv_ref, nu_ref, x_ref, w1_ref, w3_ref, w2_ref, y_ref,
                   w1b_ref, w3b_ref, w2b_ref):
    i = pl.program_id(0)

    @pl.when(i < nu_ref[0])
    def _():
        prev = te_ref[jnp.maximum(i - 1, 0)]

        @pl.when(jnp.logical_or(i == 0, te_ref[i] != prev))
        def _():
            w1b_ref[...] = w1_ref[0, 0].astype(BF16)
            w3b_ref[...] = w3_ref[0, 0].astype(BF16)
            w2b_ref[...] = w2_ref[0, 0].astype(BF16)

        xu = x_ref[...]
        rowid = lax.broadcasted_iota(I32, xu.shape, 0)
        xu = jnp.where(rowid < tv_ref[i], xu, jnp.zeros_like(xu))
        x = _unpack_bf16_pairs(xu).astype(BF16)
        h1 = jnp.dot(x, w1b_ref[...], preferred_element_type=F32)
        h3 = jnp.dot(x, w3b_ref[...], preferred_element_type=F32)
        he = (h1 * jax.nn.sigmoid(h1)) * h3
        y = jnp.dot(he.astype(BF16), w2b_ref[...], preferred_element_type=F32)
        y_ref[...] = _pack_bf16_pairs(y)


def _experts(x_sorted, tile_e, tile_valid, n_used, w1, w3, w2, layer):
    n_slots, wp = x_sorted.shape
    _, n_e, d, d_e = w1.shape
    ts = SLOT_TILE
    last = lambda nu: jnp.maximum(nu[0] - 1, 0)
    xspec = pl.BlockSpec((ts, wp), lambda i, te, tv, nu: (jnp.minimum(i, last(nu)), 0))
    wspec = lambda a, b: pl.BlockSpec((1, 1, a, b), lambda i, te, tv, nu: (layer, te[i], 0, 0))
    return pl.pallas_call(
        _expert_kernel,
        grid_spec=pltpu.PrefetchScalarGridSpec(
            num_scalar_prefetch=3,
            grid=(n_slots // ts,),
            in_specs=[xspec, wspec(d, d_e), wspec(d, d_e), wspec(d_e, d)],
            out_specs=xspec,
            scratch_shapes=[pltpu.VMEM((d, d_e), BF16), pltpu.VMEM((d, d_e), BF16),
                            pltpu.VMEM((d_e, d), BF16)],
        ),
        out_shape=jax.ShapeDtypeStruct((n_slots, d // 2), U32),
        compiler_params=_params("arbitrary"),
        name="moe_experts",
    )(tile_e, tile_valid, n_used, x_sorted, w1, w3, w2)


def _slot_plan(eids, ranks, counts, n_tiles):
    ts = SLOT_TILE
    cnt = counts[:, 0].astype(I32)
    padded = (cnt + ts - 1) // ts * ts
    ends = jnp.cumsum(padded)
    starts = ends - padded
    onehot = eids[None] == jnp.arange(N_EXPERTS, dtype=I32)[:, None, None]
    dest = ranks + jnp.sum(jnp.where(onehot, starts[:, None, None], 0), axis=0)
    tile_lo = jnp.arange(n_tiles, dtype=I32) * ts
    tile_e = jnp.minimum(jnp.sum(tile_lo[:, None] >= ends[None, :], axis=1), N_EXPERTS - 1)
    tile_e = tile_e.astype(I32)
    tile_valid = jnp.clip(starts[tile_e] + cnt[tile_e] - tile_lo, 0, ts).astype(I32)
    n_used = (ends[-1:] // ts).astype(I32)
    return dest, tile_e, tile_valid, n_used


def _post_kernel(h1_ref, y0_ref, y1_ref, gate_ref, p_ref, g_ref, wpg_ref, wpe_ref, gfin_ref,
                 *rest, final):
    o_ref = rest[-1]
    gates = gate_ref[...]
    tm = gates.shape[1]
    gt = jnp.concatenate([gates, jnp.zeros((8 - TOP_K, tm), F32)], axis=0).T
    h2 = h1_ref[...] + (gt[:, 0:1] * _unpack_bf16_pairs(y0_ref[...]) +
                        gt[:, 1:2] * _unpack_bf16_pairs(y1_ref[...]))
    gate = jax.nn.sigmoid(jnp.dot(_rms(h2, g_ref[...]).astype(BF16), wpg_ref[...],
                                  preferred_element_type=F32))
    pe = jnp.dot(p_ref[...].astype(BF16), wpe_ref[...], preferred_element_type=F32)
    h3 = h2 + gate * pe
    o_ref[...] = _rms(h3, gfin_ref[...]) if final else h3


def _post(h1, yk, gates, p, g_pl, w_pg, w_pe, g_fin, *, p_row0, final, tm,
          out_rows=None, out_row0=0, out_buf=None):
    t, d = h1.shape
    nt = t // tm
    wy = yk.shape[1]
    row = lambda w: pl.BlockSpec((tm, w), lambda i: (i, 0))
    p_blk0 = p_row0 // tm
    o_blk0 = out_row0 // tm
    in_specs = [row(d), row(wy), pl.BlockSpec((tm, wy), lambda i: (i + nt, 0)),
                pl.BlockSpec((TOP_K, tm), lambda i: (0, i)),
                pl.BlockSpec((tm, p.shape[1]), lambda i: (i + p_blk0, 0)),
                _const_spec((1, d)), _const_spec(w_pg.shape), _const_spec(w_pe.shape),
                _const_spec((1, d))]
    args = [h1, yk, yk, gates, p, g_pl, w_pg, w_pe, g_fin]
    aliases = {}
    if out_buf is not None:
        aliases = {len(args): 0}
        in_specs.append(pl.BlockSpec(memory_space=pl.ANY))
        args.append(out_buf)
    return pl.pallas_call(
        functools.partial(_post_kernel, final=final),
        grid=(nt,),
        in_specs=in_specs,
        out_specs=pl.BlockSpec((tm, d), lambda i: (i + o_blk0, 0)),
        out_shape=jax.ShapeDtypeStruct((out_rows or t, d), F32),
        input_output_aliases=aliases,
        compiler_params=_params("arbitrary"),
        name="layer_post",
    )(*args)


def _router_weights(wc, bc, wf, bf):
    d = wc.shape[0]
    w = jnp.zeros((d, ROUTER_LANES), F32)
    w = w.at[:, 0:N_GROUPS].set(wc).at[:, FINE_ROW0:FINE_ROW0 + N_EXPERTS].set(wf)
    b = jnp.zeros((1, ROUTER_LANES), F32)
    b = b.at[0, 0:N_GROUPS].set(bc).at[0, FINE_ROW0:FINE_ROW0 + N_EXPERTS].set(bf)
    hi = w.astype(BF16)
    lo = (w - hi.astype(F32)).astype(BF16)
    return jnp.concatenate([hi, lo], axis=1), b


def _moe(xp, eids, ranks, counts, w1, w3, w2, layer):
    t = xp.shape[0]
    n_tiles = TOP_K * t // SLOT_TILE + N_EXPERTS
    dest, tile_e, tile_valid, n_used = _slot_plan(eids, ranks, counts, n_tiles)
    x_sorted = _dispatch(xp, dest, n_tiles * SLOT_TILE)
    y_sorted = _experts(x_sorted, tile_e, tile_valid, n_used, w1, w3, w2, layer)
    return _combine(y_sorted, dest.reshape(-1))


def kernel(x, p, norm_mix, norm_ffn, norm_pl, final_norm, w_in_even, conv_w_even, w_out_even, w_in_odd, g_v_odd, w_s_odd, b_s_odd, w_out_odd, router_c, router_c_b, router_f, router_f_b, moe_w1, moe_w3, moe_w2, w_pe, w_pg):
    batch, seq, d = x.shape
    depth = p.shape[0]
    t = batch * seq
    tm = min(TOKEN_TILE, seq)
    n_streams = N_STREAMS if batch % N_STREAMS == 0 else 1
    sbatch = batch // n_streams
    ts = sbatch * seq
    streams = range(n_streams)
    hs = [x.reshape(t, d)] * n_streams
    h_row0 = [s * ts for s in streams]
    p_rows = p.reshape(depth * t, -1)
    row = lambda a: a.reshape(1, -1)
    out = None
    for i in range(depth):
        j = i // 2
        wr, br = _router_weights(router_c[i], router_c_b[i], router_f[i], router_f_b[i])
        g_ffn = row(norm_ffn[i])
        if i % 2 == 0:
            w_in, w_out = w_in_even[j].astype(BF16), w_out_even[j].astype(BF16)
            mixed = []
            for s in streams:
                ya, q, k, v = _even_in(hs[s], row(norm_mix[i]), w_in, conv_w_even[j],
                                       seq=seq, tm=tm, t=ts, row0=h_row0[s])
                yb = _attention(q, k, v, batch=sbatch, seq=seq, tq=min(Q_TILE, seq))
                mixed.append(_even_out(ya, yb, hs[s], w_out, g_ffn, wr, br, tm=tm, row0=h_row0[s]))
        else:
            w_in, w_out = w_in_odd[j].astype(BF16), w_out_odd[j].astype(BF16)
            mixed = [_odd_mixer(hs[s], row(norm_mix[i]), w_in, row(g_v_odd[j]), w_s_odd[j],
                                b_s_odd[j].T, w_out, g_ffn, wr, br, tm=min(ODD_TOKEN_TILE, seq))
                     for s in streams]
        yks = [_moe(xp, eids, ranks, counts, moe_w1, moe_w3, moe_w2, i)
               for (_, xp, _, eids, ranks, counts) in mixed]
        w_g, w_e = w_pg[i].astype(BF16), w_pe[i].astype(BF16)
        final = i == depth - 1
        for s in streams:
            h1, _, gates = mixed[s][0:3]
            where = dict(out_rows=t, out_row0=s * ts, out_buf=out) if final else {}
            res = _post(h1, yks[s], gates, p_rows, row(norm_pl[i]), w_g, w_e, row(final_norm),
                        p_row0=i * t + s * ts, final=final, tm=tm, **where)
            if final:
                out = res
            else:
                hs[s] = res
        h_row0 = [0] * n_streams
    return out.reshape(batch, seq, d)
```

```python
import functools

import jax
import jax.numpy as jnp
from jax import lax
from jax.experimental import pallas as pl
from jax.experimental.pallas import tpu as pltpu
from jax.experimental.pallas import tpu_sc as plsc

EPS = 1e-6
HEAD_DIM = 64
HEADS_PER_SLAB = 2
ATTN_SLABS = 4
CHUNK = 128
N_GROUPS = 4
EXP_PER_GROUP = 8
N_EXPERTS = N_GROUPS * EXP_PER_GROUP
TOP_K = 2
ROUTER_LANES = 128
FINE_ROW0 = 8

V7X_VMEM_BYTES = 64 * 1024 * 1024
VMEM_LIMIT = V7X_VMEM_BYTES - 8 * 1024 * 1024
V7X_SC_CORES = 2
V7X_SC_SUBCORES = 16
SC_WORKERS = V7X_SC_CORES * V7X_SC_SUBCORES
SC_ROWS = 32

TOKEN_TILE = 512
Q_TILE = 128
SLOT_TILE = 512

LOG_F32_TINY_BOUND = -88.0
OFF_PENALTY = 1e30

F32 = jnp.float32
BF16 = jnp.bfloat16
U32 = jnp.uint32
I32 = jnp.int32


def _rms(x, g):
    return x * lax.rsqrt(jnp.mean(x * x, axis=-1, keepdims=True) + EPS) * g


def _const_spec(shape):
    nd = len(shape)
    return pl.BlockSpec(shape, lambda *_: (0,) * nd, pipeline_mode=pl.Buffered(1))


def _params(*sem):
    return pltpu.CompilerParams(dimension_semantics=sem, vmem_limit_bytes=VMEM_LIMIT)


def _even_in_kernel(h_ref, g_ref, w_ref, cw_ref, ya_ref, q_ref, k_ref, v_ref, zs_ref,
                    *, tiles_per_seq, a_width, sb_width):
    tm = h_ref.shape[0]
    xn = _rms(h_ref[...], g_ref[...]).astype(BF16)
    a3 = 3 * a_width

    @pl.when(pl.program_id(0) % tiles_per_seq == 0)
    def _():
        zs_ref[0:8, :] = jnp.zeros((8, a_width), F32)

    pa = jnp.dot(xn, w_ref[:, 0:a3], preferred_element_type=F32)
    z = pa[:, 2 * a_width:a3] * pa[:, 0:a_width]
    zs_ref[8:tm + 8, :] = z
    z1 = zs_ref[7:tm + 7, :]
    z2 = zs_ref[6:tm + 6, :]
    cw = cw_ref[...]
    conv = cw[0:1, :] * z2 + cw[1:2, :] * z1 + cw[2:3, :] * z
    ya_ref[...] = (pa[:, a_width:2 * a_width] * conv).astype(BF16)
    zs_ref[0:8, :] = zs_ref[tm:tm + 8, :]

    pq = jnp.dot(xn, w_ref[:, a3:a3 + 3 * sb_width], preferred_element_type=F32)
    q_ref[...] = (pq[:, 0:sb_width] * (HEAD_DIM ** -0.5)).astype(BF16)
    k_ref[...] = pq[:, sb_width:2 * sb_width].astype(BF16)
    v_ref[...] = pq[:, 2 * sb_width:3 * sb_width].astype(BF16)


def _even_in(h, g, w_in, conv_w, *, seq, tm):
    t, d = h.shape
    a_width = conv_w.shape[1]
    sb_width = (w_in.shape[1] - 3 * a_width) // 3
    kern = functools.partial(_even_in_kernel, tiles_per_seq=seq // tm, a_width=a_width,
                             sb_width=sb_width)
    row = lambda w: pl.BlockSpec((tm, w), lambda i: (i, 0))
    return pl.pallas_call(
        kern,
        grid=(t // tm,),
        in_specs=[row(d), _const_spec((1, d)), _const_spec(w_in.shape), _const_spec(conv_w.shape)],
        out_specs=[row(a_width), row(sb_width), row(sb_width), row(sb_width)],
        out_shape=[jax.ShapeDtypeStruct((t, a_width), BF16)] +
                  [jax.ShapeDtypeStruct((t, sb_width), BF16)] * 3,
        scratch_shapes=[pltpu.VMEM((tm + 8, a_width), F32)],
        compiler_params=_params("arbitrary"),
        name="even_in",
    )(h, g, w_in, conv_w)


def _attn_kernel(q_ref, k_ref, v_ref, o_ref, acc_ref, c_ref, *, tq):
    qi = pl.program_id(2)
    slab = HEADS_PER_SLAB * HEAD_DIM
    rows_slab = HEADS_PER_SLAB * tq
    rows = ATTN_SLABS * rows_slab
    lane = lax.broadcasted_iota(I32, (tq, slab), 1)
    qs = []
    for sl in range(ATTN_SLABS):
        q = q_ref[:, sl * slab:(sl + 1) * slab]
        zero = jnp.zeros_like(q)
        qs.append(jnp.concatenate([jnp.where(lane < HEAD_DIM, q, zero),
                                   jnp.where(lane >= HEAD_DIM, q, zero)], axis=0))
    r_i = lax.broadcasted_iota(I32, (tq, tq), 0)
    c_i = lax.broadcasted_iota(I32, (tq, tq), 1)
    upper = jnp.where(r_i > c_i, 1.0, 0.0).astype(BF16)
    t_loc = lax.broadcasted_iota(I32, (rows, tq), 0) & (tq - 1)
    s_loc = lax.broadcasted_iota(I32, (rows, tq), 1)
    causal = s_loc < t_loc

    def block(kb, c, mask):
        start = pl.multiple_of(kb * tq, tq)
        kk = k_ref[pl.ds(start, tq), :]
        vv = v_ref[pl.ds(start, tq), :]
        z = jnp.concatenate(
            [lax.dot_general(qs[sl], kk[:, sl * slab:(sl + 1) * slab], (((1,), (1,)), ((), ())),
                             preferred_element_type=F32) for sl in range(ATTN_SLABS)], axis=0)
        sp = jnp.maximum(z, 0.0) + jnp.log(1.0 + jnp.exp(-jnp.abs(z)))
        spm = sp if mask is None else jnp.where(mask, sp, 0.0)
        hi = spm.astype(BF16)
        lo = (spm - hi.astype(F32)).astype(BF16)
        later = (jnp.dot(hi, upper, preferred_element_type=F32) +
                 jnp.dot(lo, upper, preferred_element_type=F32))
        a = jnp.exp(z - sp - later - c)
        if mask is not None:
            a = jnp.where(mask, a, 0.0)
        a = a.astype(BF16)
        out = jnp.concatenate(
            [jnp.dot(a[sl * rows_slab:(sl + 1) * rows_slab], vv[:, sl * slab:(sl + 1) * slab],
                     preferred_element_type=F32) for sl in range(ATTN_SLABS)], axis=0)
        return out, jnp.sum(spm, axis=1, keepdims=True)

    out0, sum0 = block(qi, jnp.zeros((rows, 1), F32), causal)
    has_prev = qi >= 1
    out1, sum1 = block(jnp.maximum(qi - 1, 0), sum0 + jnp.where(has_prev, 0.0, OFF_PENALTY), None)
    c2 = sum0 + jnp.where(has_prev, sum1, 0.0)
    acc_ref[...] = out0 + out1
    c_ref[...] = c2

    def cond(carry):
        kb, cmin = carry
        return jnp.logical_and(kb >= 0, cmin < -LOG_F32_TINY_BOUND)

    def body(carry):
        kb, _ = carry
        c = c_ref[...]
        out, ssum = block(kb, c, None)
        acc_ref[...] += out
        c_new = c + ssum
        c_ref[...] = c_new
        return kb - 1, jnp.min(c_new)

    lax.while_loop(cond, body, (qi - 2, jnp.min(c2)))
    acc = acc_ref[...]
    for sl in range(ATTN_SLABS):
        lo_r = sl * rows_slab
        o_ref[:, sl * slab:(sl + 1) * slab] = jnp.where(
            lane < HEAD_DIM, acc[lo_r:lo_r + tq], acc[lo_r + tq:lo_r + 2 * tq]).astype(o_ref.dtype)


def _attention(q, k, v, *, batch, seq, tq):
    t, w = q.shape
    wblk = ATTN_SLABS * HEADS_PER_SLAB * HEAD_DIM
    nq = seq // tq
    qspec = pl.BlockSpec((tq, wblk), lambda b, s, i: (b * nq + i, s))
    kvspec = pl.BlockSpec((seq, wblk), lambda b, s, i: (b, s))
    rows = ATTN_SLABS * HEADS_PER_SLAB * tq
    return pl.pallas_call(
        functools.partial(_attn_kernel, tq=tq),
        grid=(batch, w // wblk, nq),
        in_specs=[qspec, kvspec, kvspec],
        out_specs=qspec,
        out_shape=jax.ShapeDtypeStruct((t, w), BF16),
        scratch_shapes=[pltpu.VMEM((rows, HEADS_PER_SLAB * HEAD_DIM), F32),
                        pltpu.VMEM((rows, 1), F32)],
        compiler_params=_params("arbitrary", "arbitrary", "arbitrary"),
        name="stick_breaking_attention",
    )(q, k, v)


def _pack_bf16_pairs(x):
    n = x.shape[1] // 2
    bits = lax.bitcast_convert_type(x.astype(BF16).astype(F32), U32)
    return bits[:, 0:n] | (bits[:, n:] >> 16)


def _unpack_bf16_pairs(u):
    a = lax.bitcast_convert_type(u & jnp.uint32(0xFFFF0000), F32)
    b = lax.bitcast_convert_type(u << 16, F32)
    return jnp.concatenate([a, b], axis=1)


def _router_tail(h1, g_ref, wr_ref, br_ref,
                 h1_ref, xp_ref, gate_ref, eid_ref, rank_ref, cnt_out_ref, cnt_ref):
    h1_ref[...] = h1
    xn = _rms(h1, g_ref[...])
    xp_ref[...] = _pack_bf16_pairs(xn)
    x_hi = xn.astype(BF16)
    x_lo = (xn - x_hi.astype(F32)).astype(BF16)
    both = jnp.dot(x_hi, wr_ref[...], preferred_element_type=F32)
    logits = (both[:, 0:ROUTER_LANES] + both[:, ROUTER_LANES:] +
              jnp.dot(x_lo, wr_ref[:, 0:ROUTER_LANES], preferred_element_type=F32) + br_ref[...])
    lt = logits.T
    tm = lt.shape[1]
    row = lax.broadcasted_iota(I32, (EXP_PER_GROUP, tm), 0)
    neg = jnp.float32(-jnp.inf)

    coarse = jnp.where(row < N_GROUPS, lt[0:EXP_PER_GROUP], neg)
    cmax = jnp.max(coarse, axis=0, keepdims=True)
    gi = jnp.min(jnp.where(coarse == cmax, row, EXP_PER_GROUP), axis=0, keepdims=True)
    pg = 1.0 / jnp.sum(jnp.exp(coarse - cmax), axis=0, keepdims=True)

    fine = lt[FINE_ROW0:FINE_ROW0 + EXP_PER_GROUP]
    for grp in range(1, N_GROUPS):
        lo_r = FINE_ROW0 + grp * EXP_PER_GROUP
        fine = jnp.where(gi == grp, lt[lo_r:lo_r + EXP_PER_GROUP], fine)
    m1 = jnp.max(fine, axis=0, keepdims=True)
    i1 = jnp.min(jnp.where(fine == m1, row, EXP_PER_GROUP), axis=0, keepdims=True)
    rest = jnp.where(row == i1, neg, fine)
    m2 = jnp.max(rest, axis=0, keepdims=True)
    i2 = jnp.min(jnp.where(rest == m2, row, EXP_PER_GROUP), axis=0, keepdims=True)
    e21 = jnp.exp(m2 - m1)
    w1 = 1.0 / (1.0 + e21)
    gate_ref[...] = jnp.concatenate([pg * w1, pg * (e21 * w1)], axis=0)
    e0 = gi * EXP_PER_GROUP + i1
    e1 = gi * EXP_PER_GROUP + i2
    eid_ref[...] = jnp.concatenate([e0, e1], axis=0)

    @pl.when(pl.program_id(0) == 0)
    def _():
        cnt_ref[...] = jnp.zeros_like(cnt_ref)

    erow = lax.broadcasted_iota(I32, (N_EXPERTS, tm), 0)
    oh0 = jnp.where(erow == e0, 1.0, 0.0)
    oh1 = jnp.where(erow == e1, 1.0, 0.0)
    r_i = lax.broadcasted_iota(I32, (tm, tm), 0)
    c_i = lax.broadcasted_iota(I32, (tm, tm), 1)
    before = jnp.where(r_i < c_i, 1.0, 0.0).astype(BF16)
    pre0 = jnp.dot(oh0.astype(BF16), before, preferred_element_type=F32)
    pre1 = jnp.dot(oh1.astype(BF16), before, preferred_element_type=F32)
    tot0 = jnp.sum(oh0, axis=1, keepdims=True)
    tot1 = jnp.sum(oh1, axis=1, keepdims=True)
    base = cnt_ref[:, 0:1]
    r0 = jnp.sum(oh0 * (pre0 + base), axis=0, keepdims=True)
    r1 = jnp.sum(oh1 * (pre1 + (base + tot0)), axis=0, keepdims=True)
    rank_ref[...] = jnp.concatenate([r0, r1], axis=0).astype(I32)
    total = jnp.broadcast_to(base + tot0 + tot1, cnt_ref.shape)
    cnt_ref[...] = total
    cnt_out_ref[...] = total


def _tail_specs(t, d, tm):
    row = lambda w: pl.BlockSpec((tm, w), lambda i: (i, 0))
    lanes = pl.BlockSpec((TOP_K, tm), lambda i: (0, i))
    cnt = pl.BlockSpec((N_EXPERTS, ROUTER_LANES), lambda i: (0, 0))
    out_specs = [row(d), row(d // 2), lanes, lanes, lanes, cnt]
    out_shape = [jax.ShapeDtypeStruct((t, d), F32), jax.ShapeDtypeStruct((t, d // 2), U32),
                 jax.ShapeDtypeStruct((TOP_K, t), F32), jax.ShapeDtypeStruct((TOP_K, t), I32),
                 jax.ShapeDtypeStruct((TOP_K, t), I32),
                 jax.ShapeDtypeStruct((N_EXPERTS, ROUTER_LANES), F32)]
    scratch = [pltpu.VMEM((N_EXPERTS, ROUTER_LANES), F32)]
    return out_specs, out_shape, scratch


def _even_out_kernel(ya_ref, yb_ref, h_ref, w_ref, g_ref, wr_ref, br_ref, *tail_refs):
    a_width = ya_ref.shape[1]
    mix = (jnp.dot(ya_ref[...], w_ref[0:a_width, :], preferred_element_type=F32) +
           jnp.dot(yb_ref[...], w_ref[a_width:, :], preferred_element_type=F32))
    _router_tail(h_ref[...] + mix, g_ref, wr_ref, br_ref, *tail_refs)


def _even_out(ya, yb, h, w_out, g_ffn, wr, br, *, tm):
    t, d = h.shape
    out_specs, out_shape, scratch = _tail_specs(t, d, tm)
    row = lambda w: pl.BlockSpec((tm, w), lambda i: (i, 0))
    return pl.pallas_call(
        _even_out_kernel,
        grid=(t // tm,),
        in_specs=[row(ya.shape[1]), row(yb.shape[1]), row(d), _const_spec(w_out.shape),
                  _const_spec((1, d)), _const_spec(wr.shape), _const_spec(br.shape)],
        out_specs=out_specs,
        out_shape=out_shape,
        scratch_shapes=scratch,
        compiler_params=_params("arbitrary"),
        name="even_out_router",
    )(ya, yb, h, w_out, g_ffn, wr, br)


def _gelu_tanh(x):
    return 0.5 * x * (1.0 + jnp.tanh(0.7978845608028654 * (x + 0.044715 * (x * x * x))))


def _odd_kernel(*refs, n_prev):
    h_refs, refs = refs[:max(n_prev, 1)], refs[max(n_prev, 1):]
    (g_ref, w_in_ref, gv_ref, ws_ref, bs_ref, w_out_ref, gf_ref, wr_ref, br_ref,
     *tail_refs, y_ref) = refs
    h = _post_value(*h_refs) if n_prev else h_refs[0][...]
    tm = h.shape[0]
    c_width = gv_ref.shape[1]
    n_grp = ws_ref.shape[0]
    gdim = c_width // n_grp
    xn = _rms(h, g_ref[...]).astype(BF16)
    u = _gelu_tanh(jnp.dot(xn, w_in_ref[:, 0:c_width], preferred_element_type=F32))
    v = _gelu_tanh(jnp.dot(xn, w_in_ref[:, c_width:], preferred_element_type=F32))
    v = _rms(v, gv_ref[...]).astype(BF16)
    r_i = lax.broadcasted_iota(I32, (CHUNK, CHUNK), 0)
    c_i = lax.broadcasted_iota(I32, (CHUNK, CHUNK), 1)
    causal = c_i <= r_i
    bs = bs_ref[...]
    for grp in range(n_grp):
        w_m = jnp.where(causal, ws_ref[grp], 0.0).astype(BF16)
        bias = bs[:, grp:grp + 1]
        cols = slice(grp * gdim, (grp + 1) * gdim)
        for c in range(tm // CHUNK):
            rows = slice(c * CHUNK, (c + 1) * CHUNK)
            gate = jnp.dot(w_m, v[rows, cols], preferred_element_type=F32) + bias
            y_ref[rows, cols] = (u[rows, cols] * gate).astype(BF16)
    mix = jnp.dot(y_ref[...], w_out_ref[...], preferred_element_type=F32)
    _router_tail(h + mix, gf_ref, wr_ref, br_ref, *tail_refs)


def _odd_mixer(h_or_prev, g_mix, w_in, g_v, w_s, b_s_t, w_out, g_ffn, wr, br, *, tm):
    consts = [g_mix, w_in, g_v, w_s, b_s_t, w_out, g_ffn, wr, br]
    if isinstance(h_or_prev, tuple):
        head_specs, head_args = h_or_prev
        t, d = head_args[0].shape
    else:
        t, d = h_or_prev.shape
        head_specs, head_args = [pl.BlockSpec((tm, d), lambda i: (i, 0))], [h_or_prev]
    out_specs, out_shape, scratch = _tail_specs(t, d, tm)
    n_prev = len(head_args) if isinstance(h_or_prev, tuple) else 0
    return pl.pallas_call(
        functools.partial(_odd_kernel, n_prev=n_prev),
        grid=(t // tm,),
        in_specs=head_specs + [_const_spec(c.shape) for c in consts],
        out_specs=out_specs,
        out_shape=out_shape,
        scratch_shapes=scratch + [pltpu.VMEM((tm, g_v.shape[1]), BF16)],
        compiler_params=_params("arbitrary"),
        name="odd_mixer_router",
    )(*head_args, *consts)


def _sc_mesh():
    return plsc.VectorSubcoreMesh(core_axis_name="c", subcore_axis_name="s")


def _sc_worker_base(rows_per_worker):
    wid = lax.axis_index("s") * V7X_SC_CORES + lax.axis_index("c")
    return wid * rows_per_worker


def _dispatch(xp, dest, n_slots):
    t, w = xp.shape
    per_worker = t // SC_WORKERS
    n_chunks = per_worker // SC_ROWS
    assert per_worker * SC_WORKERS == t and n_chunks * SC_ROWS == per_worker and n_chunks % 2 == 0

    idx_t = pltpu.VMEM((SC_ROWS,), I32)

    @functools.partial(
        pl.kernel, mesh=_sc_mesh(),
        out_type=jax.ShapeDtypeStruct((n_slots, w), xp.dtype),
        scratch_types=[pltpu.VMEM((2, SC_ROWS, w), xp.dtype), idx_t, idx_t, idx_t, idx_t,
                       pltpu.SemaphoreType.DMA((2,)), pltpu.SemaphoreType.DMA((2,))],
    )
    def kern(x_hbm, d_hbm, out_hbm, rows_v, i00, i01, i10, i11, rsem, wsem):
        base = _sc_worker_base(per_worker)
        idx = ((i00, i01), (i10, i11))

        def load(c, b):
            off = pl.multiple_of(base + c * SC_ROWS, 8)
            pltpu.async_copy(x_hbm.at[pl.ds(off, SC_ROWS)], rows_v.at[b], rsem.at[b])
            for k in range(TOP_K):
                pltpu.sync_copy(d_hbm.at[k, pl.ds(off, SC_ROWS)], idx[b][k])

        def wait_load(b):
            pltpu.make_async_copy(x_hbm.at[pl.ds(0, SC_ROWS)], rows_v.at[b], rsem.at[b]).wait()

        def scatter(b):
            for k in range(TOP_K):
                pltpu.async_copy(rows_v.at[b], out_hbm.at[idx[b][k]], wsem.at[b])

        def wait_scatter(b):
            for k in range(TOP_K):
                pltpu.make_async_copy(rows_v.at[b], out_hbm.at[idx[b][k]], wsem.at[b]).wait()

        load(0, 0)

        @pl.loop(0, n_chunks, step=2)
        def _(c0):
            for b in range(2):
                c = c0 + b
                wait_load(b)
                scatter(b)

                @pl.when(c + 1 < n_chunks)
                def _():
                    @pl.when(c >= 1)
                    def _():
                        wait_scatter(1 - b)
                    load(c + 1, 1 - b)

        wait_scatter(0)
        wait_scatter(1)

    return kern(xp, dest)


def _combine(y, idx):
    n = idx.shape[0]
    w = y.shape[1]
    rows = 2 * SC_ROWS
    per_worker = n // SC_WORKERS
    n_chunks = per_worker // rows
    assert per_worker * SC_WORKERS == n and n_chunks * rows == per_worker and n_chunks % 2 == 0

    idx_t = pltpu.VMEM((rows,), I32)

    @functools.partial(
        pl.kernel, mesh=_sc_mesh(),
        out_type=jax.ShapeDtypeStruct((n, w), y.dtype),
        scratch_types=[pltpu.VMEM((2, rows, w), y.dtype), idx_t, idx_t,
                       pltpu.SemaphoreType.DMA((2,)), pltpu.SemaphoreType.DMA((2,))],
    )
    def kern(y_hbm, i_hbm, out_hbm, rows_v, i0, i1, gsem, wsem):
        base = _sc_worker_base(per_worker)
        ibuf = (i0, i1)

        def gather(c, b):
            off = pl.multiple_of(base + c * rows, 8)
            pltpu.sync_copy(i_hbm.at[pl.ds(off, rows)], ibuf[b])
            pltpu.async_copy(y_hbm.at[ibuf[b]], rows_v.at[b], gsem.at[b])

        def wait_gather(b):
            pltpu.make_async_copy(y_hbm.at[ibuf[b]], rows_v.at[b], gsem.at[b]).wait()

        def write(c, b):
            off = pl.multiple_of(base + c * rows, 8)
            pltpu.async_copy(rows_v.at[b], out_hbm.at[pl.ds(off, rows)], wsem.at[b])

        def wait_write(b):
            pltpu.make_async_copy(rows_v.at[b], out_hbm.at[pl.ds(0, rows)], wsem.at[b]).wait()

        gather(0, 0)

        @pl.loop(0, n_chunks, step=2)
        def _(c0):
            for b in range(2):
                c = c0 + b
                wait_gather(b)
                write(c, b)

                @pl.when(c + 1 < n_chunks)
                def _():
                    @pl.when(c >= 1)
                    def _():
                        wait_write(1 - b)
                    gather(c + 1, 1 - b)

        wait_write(0)
        wait_write(1)

    return kern(y, idx)


def _expert_kernel(te_ref, tv_ref, nu_ref, x_ref, w1_ref, w3_ref, w2_ref, y_ref):
    i = pl.program_id(0)

    @pl.when(i < nu_ref[0])
    def _():
        xu = x_ref[...]
        rowid = lax.broadcasted_iota(I32, xu.shape, 0)
        xu = jnp.where(rowid < tv_ref[i], xu, jnp.zeros_like(xu))
        x = _unpack_bf16_pairs(xu).astype(BF16)
        h1 = jnp.dot(x, w1_ref[0, 0], preferred_element_type=F32)
        h3 = jnp.dot(x, w3_ref[0, 0], preferred_element_type=F32)
        he = (h1 * jax.nn.sigmoid(h1)) * h3
        y = jnp.dot(he.astype(BF16), w2_ref[0, 0], preferred_element_type=F32)
        y_ref[...] = _pack_bf16_pairs(y)


def _experts(x_sorted, tile_e, tile_valid, n_used, w1, w3, w2, layer):
    n_slots, wp = x_sorted.shape
    _, n_e, d, d_e = w1.shape
    ts = SLOT_TILE
    last = lambda nu: jnp.maximum(nu[0] - 1, 0)
    xspec = pl.BlockSpec((ts, wp), lambda i, te, tv, nu: (jnp.minimum(i, last(nu)), 0))
    wspec = lambda a, b: pl.BlockSpec((1, 1, a, b), lambda i, te, tv, nu: (layer, te[i], 0, 0))
    return pl.pallas_call(
        _expert_kernel,
        grid_spec=pltpu.PrefetchScalarGridSpec(
            num_scalar_prefetch=3,
            grid=(n_slots // ts,),
            in_specs=[xspec, wspec(d, d_e), wspec(d, d_e), wspec(d_e, d)],
            out_specs=xspec,
        ),
        out_shape=jax.ShapeDtypeStruct((n_slots, d // 2), U32),
        compiler_params=_params("arbitrary"),
        name="moe_experts",
    )(tile_e, tile_valid, n_used, x_sorted, w1, w3, w2)


def _slot_plan(eids, ranks, counts, n_tiles):
    ts = SLOT_TILE
    cnt = counts[:, 0].astype(I32)
    padded = (cnt + ts - 1) // ts * ts
    ends = jnp.cumsum(padded)
    starts = ends - padded
    onehot = eids[None] == jnp.arange(N_EXPERTS, dtype=I32)[:, None, None]
    dest = ranks + jnp.sum(jnp.where(onehot, starts[:, None, None], 0), axis=0)
    tile_lo = jnp.arange(n_tiles, dtype=I32) * ts
    tile_e = jnp.minimum(jnp.sum(tile_lo[:, None] >= ends[None, :], axis=1), N_EXPERTS - 1)
    tile_e = tile_e.astype(I32)
    tile_valid = jnp.clip(starts[tile_e] + cnt[tile_e] - tile_lo, 0, ts).astype(I32)
    n_used = (ends[-1:] // ts).astype(I32)
    return dest, tile_e, tile_valid, n_used


def _post_value(h1_ref, y0_ref, y1_ref, gate_ref, p_ref, g_ref, wpg_ref, wpe_ref):
    gates = gate_ref[...]
    tm = gates.shape[1]
    gt = jnp.concatenate([gates, jnp.zeros((8 - TOP_K, tm), F32)], axis=0).T
    h2 = h1_ref[...] + (gt[:, 0:1] * _unpack_bf16_pairs(y0_ref[...]) +
                        gt[:, 1:2] * _unpack_bf16_pairs(y1_ref[...]))
    gate = jax.nn.sigmoid(jnp.dot(_rms(h2, g_ref[...]).astype(BF16), wpg_ref[...],
                                  preferred_element_type=F32))
    pe = jnp.dot(p_ref[...].astype(BF16), wpe_ref[...], preferred_element_type=F32)
    return h2 + gate * pe


def _post_operands(h1, yk, gates, p, g_pl, w_pg, w_pe, *, p_row0, tm):
    t, d = h1.shape
    nt = t // tm
    wy = yk.shape[1]
    p_blk0 = p_row0 // tm
    row = lambda w: pl.BlockSpec((tm, w), lambda i: (i, 0))
    specs = [row(d), row(wy), pl.BlockSpec((tm, wy), lambda i: (i + nt, 0)),
             pl.BlockSpec((TOP_K, tm), lambda i: (0, i)),
             pl.BlockSpec((tm, p.shape[1]), lambda i: (i + p_blk0, 0)),
             _const_spec((1, d)), _const_spec(w_pg.shape), _const_spec(w_pe.shape)]
    return specs, [h1, yk, yk, gates, p, g_pl, w_pg, w_pe]


def _post_kernel(*refs):
    *post_refs, o_ref = refs
    o_ref[...] = _post_value(*post_refs)


def _post(post_operands, *, tm):
    specs, args = post_operands
    t, d = args[0].shape
    return pl.pallas_call(
        _post_kernel,
        grid=(t // tm,),
        in_specs=specs,
        out_specs=pl.BlockSpec((tm, d), lambda i: (i, 0)),
        out_shape=jax.ShapeDtypeStruct((t, d), F32),
        compiler_params=_params("arbitrary"),
        name="layer_post",
    )(*args)


def _final_kernel(*refs):
    *post_refs, gfin_ref, o_ref = refs
    o_ref[...] = _rms(_post_value(*post_refs), gfin_ref[...])


def _final(post_operands, g_fin, *, tm):
    specs, args = post_operands
    t, d = args[0].shape
    return pl.pallas_call(
        _final_kernel,
        grid=(t // tm,),
        in_specs=specs + [_const_spec((1, d))],
        out_specs=pl.BlockSpec((tm, d), lambda i: (i, 0)),
        out_shape=jax.ShapeDtypeStruct((t, d), F32),
        compiler_params=_params("arbitrary"),
        name="final_post",
    )(*args, g_fin)


def _router_weights(wc, bc, wf, bf):
    d = wc.shape[0]
    w = jnp.zeros((d, ROUTER_LANES), F32)
    w = w.at[:, 0:N_GROUPS].set(wc).at[:, FINE_ROW0:FINE_ROW0 + N_EXPERTS].set(wf)
    b = jnp.zeros((1, ROUTER_LANES), F32)
    b = b.at[0, 0:N_GROUPS].set(bc).at[0, FINE_ROW0:FINE_ROW0 + N_EXPERTS].set(bf)
    hi = w.astype(BF16)
    lo = (w - hi.astype(F32)).astype(BF16)
    return jnp.concatenate([hi, lo], axis=1), b


def _moe(xp, eids, ranks, counts, w1, w3, w2, layer):
    t = xp.shape[0]
    n_tiles = TOP_K * t // SLOT_TILE + N_EXPERTS
    dest, tile_e, tile_valid, n_used = _slot_plan(eids, ranks, counts, n_tiles)
    x_sorted = _dispatch(xp, dest, n_tiles * SLOT_TILE)
    y_sorted = _experts(x_sorted, tile_e, tile_valid, n_used, w1, w3, w2, layer)
    return _combine(y_sorted, dest.reshape(-1))


def kernel(x, p, norm_mix, norm_ffn, norm_pl, final_norm, w_in_even, conv_w_even, w_out_even, w_in_odd, g_v_odd, w_s_odd, b_s_odd, w_out_odd, router_c, router_c_b, router_f, router_f_b, moe_w1, moe_w3, moe_w2, w_pe, w_pg):
    batch, seq, d = x.shape
    depth = p.shape[0]
    t = batch * seq
    tm = min(TOKEN_TILE, seq)
    p_rows = p.reshape(depth * t, -1)
    row = lambda a: a.reshape(1, -1)
    w1, w3, w2 = moe_w1.astype(BF16), moe_w3.astype(BF16), moe_w2.astype(BF16)
    h = x.reshape(t, d)
    for i in range(depth):
        j = i // 2
        wr, br = _router_weights(router_c[i], router_c_b[i], router_f[i], router_f_b[i])
        g_ffn = row(norm_ffn[i])
        if i % 2 == 0:
            if isinstance(h, tuple):
                h = _post(h, tm=tm)
            ya, q, k, v = _even_in(h, row(norm_mix[i]), w_in_even[j].astype(BF16), conv_w_even[j],
                                   seq=seq, tm=tm)
            yb = _attention(q, k, v, batch=batch, seq=seq, tq=min(Q_TILE, seq))
            h1, xp, gates, eids, ranks, counts = _even_out(
                ya, yb, h, w_out_even[j].astype(BF16), g_ffn, wr, br, tm=tm)
        else:
            h1, xp, gates, eids, ranks, counts = _odd_mixer(
                h, row(norm_mix[i]), w_in_odd[j].astype(BF16), row(g_v_odd[j]), w_s_odd[j],
                b_s_odd[j].T, w_out_odd[j].astype(BF16), g_ffn, wr, br, tm=tm)
        yk = _moe(xp, eids, ranks, counts, w1, w3, w2, i)
        h = _post_operands(h1, yk, gates, p_rows, row(norm_pl[i]), w_pg[i].astype(BF16),
                           w_pe[i].astype(BF16), p_row0=i * t, tm=tm)
    return _final(h, row(final_norm), tm=tm).reshape(batch, seq, d)
```

```python
import functools

import jax
import jax.numpy as jnp
from jax import lax
from jax.experimental import pallas as pl
from jax.experimental.pallas import tpu as pltpu
from jax.experimental.pallas import tpu_sc as plsc

EPS = 1e-6
HEAD_DIM = 64
HEADS_PER_SLAB = 2
ATTN_SLABS = 4
CHUNK = 128
N_GROUPS = 4
EXP_PER_GROUP = 8
N_EXPERTS = N_GROUPS * EXP_PER_GROUP
TOP_K = 2
ROUTER_LANES = 128
FINE_ROW0 = 8

V7X_VMEM_BYTES = 64 * 1024 * 1024
VMEM_LIMIT = V7X_VMEM_BYTES - 8 * 1024 * 1024
V7X_SC_CORES = 2
V7X_SC_SUBCORES = 16
SC_WORKERS = V7X_SC_CORES * V7X_SC_SUBCORES
SC_ROWS = 32

TOKEN_TILE = 512
Q_TILE = 128
SLOT_TILE = 512

LOG_F32_TINY_BOUND = -88.0
OFF_PENALTY = 1e30

F32 = jnp.float32
BF16 = jnp.bfloat16
U32 = jnp.uint32
I32 = jnp.int32


def _rms(x, g):
    return x * lax.rsqrt(jnp.mean(x * x, axis=-1, keepdims=True) + EPS) * g


def _const_spec(shape):
    nd = len(shape)
    return pl.BlockSpec(shape, lambda *_: (0,) * nd, pipeline_mode=pl.Buffered(1))


def _params(*sem):
    return pltpu.CompilerParams(dimension_semantics=sem, vmem_limit_bytes=VMEM_LIMIT)


def _even_in_kernel(h_ref, g_ref, w_ref, cw_ref, ya_ref, q_ref, k_ref, v_ref, zs_ref,
                    *, tiles_per_seq, a_width, sb_width):
    tm = h_ref.shape[0]
    xn = _rms(h_ref[...], g_ref[...]).astype(BF16)
    a3 = 3 * a_width

    @pl.when(pl.program_id(0) % tiles_per_seq == 0)
    def _():
        zs_ref[0:8, :] = jnp.zeros((8, a_width), F32)

    pa = jnp.dot(xn, w_ref[:, 0:a3], preferred_element_type=F32)
    z = pa[:, 2 * a_width:a3] * pa[:, 0:a_width]
    zs_ref[8:tm + 8, :] = z
    z1 = zs_ref[7:tm + 7, :]
    z2 = zs_ref[6:tm + 6, :]
    cw = cw_ref[...]
    conv = cw[0:1, :] * z2 + cw[1:2, :] * z1 + cw[2:3, :] * z
    ya_ref[...] = (pa[:, a_width:2 * a_width] * conv).astype(BF16)
    zs_ref[0:8, :] = zs_ref[tm:tm + 8, :]

    pq = jnp.dot(xn, w_ref[:, a3:a3 + 3 * sb_width], preferred_element_type=F32)
    q_ref[...] = (pq[:, 0:sb_width] * (HEAD_DIM ** -0.5)).astype(BF16)
    k_ref[...] = pq[:, sb_width:2 * sb_width].astype(BF16)
    v_ref[...] = pq[:, 2 * sb_width:3 * sb_width].astype(BF16)


def _even_in(h, g, w_in, conv_w, *, seq, tm):
    t, d = h.shape
    a_width = conv_w.shape[1]
    sb_width = (w_in.shape[1] - 3 * a_width) // 3
    kern = functools.partial(_even_in_kernel, tiles_per_seq=seq // tm, a_width=a_width,
                             sb_width=sb_width)
    row = lambda w: pl.BlockSpec((tm, w), lambda i: (i, 0))
    return pl.pallas_call(
        kern,
        grid=(t // tm,),
        in_specs=[row(d), _const_spec((1, d)), _const_spec(w_in.shape), _const_spec(conv_w.shape)],
        out_specs=[row(a_width), row(sb_width), row(sb_width), row(sb_width)],
        out_shape=[jax.ShapeDtypeStruct((t, a_width), BF16)] +
                  [jax.ShapeDtypeStruct((t, sb_width), BF16)] * 3,
        scratch_shapes=[pltpu.VMEM((tm + 8, a_width), F32)],
        compiler_params=_params("arbitrary"),
        name="even_in",
    )(h, g, w_in, conv_w)


def _attn_kernel(q_ref, k_ref, v_ref, o_ref, acc_ref, c_ref, *, tq):
    qi = pl.program_id(2)
    slab = HEADS_PER_SLAB * HEAD_DIM
    rows_slab = HEADS_PER_SLAB * tq
    rows = ATTN_SLABS * rows_slab
    lane = lax.broadcasted_iota(I32, (tq, slab), 1)
    qs = []
    for sl in range(ATTN_SLABS):
        q = q_ref[:, sl * slab:(sl + 1) * slab]
        zero = jnp.zeros_like(q)
        qs.append(jnp.concatenate([jnp.where(lane < HEAD_DIM, q, zero),
                                   jnp.where(lane >= HEAD_DIM, q, zero)], axis=0))
    r_i = lax.broadcasted_iota(I32, (tq, tq), 0)
    c_i = lax.broadcasted_iota(I32, (tq, tq), 1)
    upper = jnp.where(r_i > c_i, 1.0, 0.0).astype(BF16)
    t_loc = lax.broadcasted_iota(I32, (rows, tq), 0) & (tq - 1)
    s_loc = lax.broadcasted_iota(I32, (rows, tq), 1)
    causal = s_loc < t_loc

    def block(kb, c, mask):
        start = pl.multiple_of(kb * tq, tq)
        kk = k_ref[pl.ds(start, tq), :]
        vv = v_ref[pl.ds(start, tq), :]
        z = jnp.concatenate(
            [lax.dot_general(qs[sl], kk[:, sl * slab:(sl + 1) * slab], (((1,), (1,)), ((), ())),
                             preferred_element_type=F32) for sl in range(ATTN_SLABS)], axis=0)
        sp = jnp.maximum(z, 0.0) + jnp.log(1.0 + jnp.exp(-jnp.abs(z)))
        spm = sp if mask is None else jnp.where(mask, sp, 0.0)
        hi = spm.astype(BF16)
        lo = (spm - hi.astype(F32)).astype(BF16)
        later = (jnp.dot(hi, upper, preferred_element_type=F32) +
                 jnp.dot(lo, upper, preferred_element_type=F32))
        a = jnp.exp(z - sp - later - c)
        if mask is not None:
            a = jnp.where(mask, a, 0.0)
        a = a.astype(BF16)
        out = jnp.concatenate(
            [jnp.dot(a[sl * rows_slab:(sl + 1) * rows_slab], vv[:, sl * slab:(sl + 1) * slab],
                     preferred_element_type=F32) for sl in range(ATTN_SLABS)], axis=0)
        return out, jnp.sum(spm, axis=1, keepdims=True)

    out0, sum0 = block(qi, jnp.zeros((rows, 1), F32), causal)
    has_prev = qi >= 1
    out1, sum1 = block(jnp.maximum(qi - 1, 0), sum0 + jnp.where(has_prev, 0.0, OFF_PENALTY), None)
    c2 = sum0 + jnp.where(has_prev, sum1, 0.0)
    acc_ref[...] = out0 + out1
    c_ref[...] = c2

    def cond(carry):
        kb, cmin = carry
        return jnp.logical_and(kb >= 0, cmin < -LOG_F32_TINY_BOUND)

    def body(carry):
        kb, _ = carry
        c = c_ref[...]
        out, ssum = block(kb, c, None)
        acc_ref[...] += out
        c_new = c + ssum
        c_ref[...] = c_new
        return kb - 1, jnp.min(c_new)

    lax.while_loop(cond, body, (qi - 2, jnp.min(c2)))
    acc = acc_ref[...]
    for sl in range(ATTN_SLABS):
        lo_r = sl * rows_slab
        o_ref[:, sl * slab:(sl + 1) * slab] = jnp.where(
            lane < HEAD_DIM, acc[lo_r:lo_r + tq], acc[lo_r + tq:lo_r + 2 * tq]).astype(o_ref.dtype)


def _attention(q, k, v, *, batch, seq, tq):
    t, w = q.shape
    wblk = ATTN_SLABS * HEADS_PER_SLAB * HEAD_DIM
    nq = seq // tq
    qspec = pl.BlockSpec((tq, wblk), lambda b, s, i: (b * nq + i, s))
    kvspec = pl.BlockSpec((seq, wblk), lambda b, s, i: (b, s))
    rows = ATTN_SLABS * HEADS_PER_SLAB * tq
    return pl.pallas_call(
        functools.partial(_attn_kernel, tq=tq),
        grid=(batch, w // wblk, nq),
        in_specs=[qspec, kvspec, kvspec],
        out_specs=qspec,
        out_shape=jax.ShapeDtypeStruct((t, w), BF16),
        scratch_shapes=[pltpu.VMEM((rows, HEADS_PER_SLAB * HEAD_DIM), F32),
                        pltpu.VMEM((rows, 1), F32)],
        compiler_params=_params("arbitrary", "arbitrary", "arbitrary"),
        name="stick_breaking_attention",
    )(q, k, v)


def _pack_bf16_pairs(x):
    n = x.shape[1] // 2
    bits = lax.bitcast_convert_type(x.astype(BF16).astype(F32), U32)
    return bits[:, 0:n] | (bits[:, n:] >> 16)


def _unpack_bf16_pairs(u):
    a = lax.bitcast_convert_type(u & jnp.uint32(0xFFFF0000), F32)
    b = lax.bitcast_convert_type(u << 16, F32)
    return jnp.concatenate([a, b], axis=1)


def _router_tail(h1, g_ref, wr_ref, br_ref,
                 h1_ref, xp_ref, gate_ref, eid_ref, rank_ref, cnt_out_ref, cnt_ref):
    h1_ref[...] = h1
    xn = _rms(h1, g_ref[...])
    xp_ref[...] = _pack_bf16_pairs(xn)
    x_hi = xn.astype(BF16)
    x_lo = (xn - x_hi.astype(F32)).astype(BF16)
    both = jnp.dot(x_hi, wr_ref[...], preferred_element_type=F32)
    logits = (both[:, 0:ROUTER_LANES] + both[:, ROUTER_LANES:] +
              jnp.dot(x_lo, wr_ref[:, 0:ROUTER_LANES], preferred_element_type=F32) + br_ref[...])
    lt = logits.T
    tm = lt.shape[1]
    row = lax.broadcasted_iota(I32, (EXP_PER_GROUP, tm), 0)
    neg = jnp.float32(-jnp.inf)

    coarse = jnp.where(row < N_GROUPS, lt[0:EXP_PER_GROUP], neg)
    cmax = jnp.max(coarse, axis=0, keepdims=True)
    gi = jnp.min(jnp.where(coarse == cmax, row, EXP_PER_GROUP), axis=0, keepdims=True)
    pg = 1.0 / jnp.sum(jnp.exp(coarse - cmax), axis=0, keepdims=True)

    fine = lt[FINE_ROW0:FINE_ROW0 + EXP_PER_GROUP]
    for grp in range(1, N_GROUPS):
        lo_r = FINE_ROW0 + grp * EXP_PER_GROUP
        fine = jnp.where(gi == grp, lt[lo_r:lo_r + EXP_PER_GROUP], fine)
    m1 = jnp.max(fine, axis=0, keepdims=True)
    i1 = jnp.min(jnp.where(fine == m1, row, EXP_PER_GROUP), axis=0, keepdims=True)
    rest = jnp.where(row == i1, neg, fine)
    m2 = jnp.max(rest, axis=0, keepdims=True)
    i2 = jnp.min(jnp.where(rest == m2, row, EXP_PER_GROUP), axis=0, keepdims=True)
    e21 = jnp.exp(m2 - m1)
    w1 = 1.0 / (1.0 + e21)
    gate_ref[...] = jnp.concatenate([pg * w1, pg * (e21 * w1)], axis=0)
    e0 = gi * EXP_PER_GROUP + i1
    e1 = gi * EXP_PER_GROUP + i2
    eid_ref[...] = jnp.concatenate([e0, e1], axis=0)

    @pl.when(pl.program_id(0) == 0)
    def _():
        cnt_ref[...] = jnp.zeros_like(cnt_ref)

    erow = lax.broadcasted_iota(I32, (N_EXPERTS, tm), 0)
    oh0 = jnp.where(erow == e0, 1.0, 0.0)
    oh1 = jnp.where(erow == e1, 1.0, 0.0)
    r_i = lax.broadcasted_iota(I32, (tm, tm), 0)
    c_i = lax.broadcasted_iota(I32, (tm, tm), 1)
    before = jnp.where(r_i < c_i, 1.0, 0.0).astype(BF16)
    pre0 = jnp.dot(oh0.astype(BF16), before, preferred_element_type=F32)
    pre1 = jnp.dot(oh1.astype(BF16), before, preferred_element_type=F32)
    tot0 = jnp.sum(oh0, axis=1, keepdims=True)
    tot1 = jnp.sum(oh1, axis=1, keepdims=True)
    base = cnt_ref[:, 0:1]
    r0 = jnp.sum(oh0 * (pre0 + base), axis=0, keepdims=True)
    r1 = jnp.sum(oh1 * (pre1 + (base + tot0)), axis=0, keepdims=True)
    rank_ref[...] = jnp.concatenate([r0, r1], axis=0).astype(I32)
    total = jnp.broadcast_to(base + tot0 + tot1, cnt_ref.shape)
    cnt_ref[...] = total
    cnt_out_ref[...] = total


def _tail_specs(t, d, tm):
    row = lambda w: pl.BlockSpec((tm, w), lambda i: (i, 0))
    lanes = pl.BlockSpec((TOP_K, tm), lambda i: (0, i))
    cnt = pl.BlockSpec((N_EXPERTS, ROUTER_LANES), lambda i: (0, 0))
    out_specs = [row(d), row(d // 2), lanes, lanes, lanes, cnt]
    out_shape = [jax.ShapeDtypeStruct((t, d), F32), jax.ShapeDtypeStruct((t, d // 2), U32),
                 jax.ShapeDtypeStruct((TOP_K, t), F32), jax.ShapeDtypeStruct((TOP_K, t), I32),
                 jax.ShapeDtypeStruct((TOP_K, t), I32),
                 jax.ShapeDtypeStruct((N_EXPERTS, ROUTER_LANES), F32)]
    scratch = [pltpu.VMEM((N_EXPERTS, ROUTER_LANES), F32)]
    return out_specs, out_shape, scratch


def _even_out_kernel(ya_ref, yb_ref, h_ref, w_ref, g_ref, wr_ref, br_ref, *tail_refs):
    a_width = ya_ref.shape[1]
    mix = (jnp.dot(ya_ref[...], w_ref[0:a_width, :], preferred_element_type=F32) +
           jnp.dot(yb_ref[...], w_ref[a_width:, :], preferred_element_type=F32))
    _router_tail(h_ref[...] + mix, g_ref, wr_ref, br_ref, *tail_refs)


def _even_out(ya, yb, h, w_out, g_ffn, wr, br, *, tm):
    t, d = h.shape
    out_specs, out_shape, scratch = _tail_specs(t, d, tm)
    row = lambda w: pl.BlockSpec((tm, w), lambda i: (i, 0))
    return pl.pallas_call(
        _even_out_kernel,
        grid=(t // tm,),
        in_specs=[row(ya.shape[1]), row(yb.shape[1]), row(d), _const_spec(w_out.shape),
                  _const_spec((1, d)), _const_spec(wr.shape), _const_spec(br.shape)],
        out_specs=out_specs,
        out_shape=out_shape,
        scratch_shapes=scratch,
        compiler_params=_params("arbitrary"),
        name="even_out_router",
    )(ya, yb, h, w_out, g_ffn, wr, br)


def _gelu_tanh(x):
    return 0.5 * x * (1.0 + jnp.tanh(0.7978845608028654 * (x + 0.044715 * (x * x * x))))


def _odd_kernel(*refs, n_prev):
    h_refs, refs = refs[:max(n_prev, 1)], refs[max(n_prev, 1):]
    (g_ref, w_in_ref, gv_ref, ws_ref, bs_ref, w_out_ref, gf_ref, wr_ref, br_ref,
     *tail_refs, y_ref) = refs
    h = _post_value(*h_refs) if n_prev else h_refs[0][...]
    tm = h.shape[0]
    c_width = gv_ref.shape[1]
    n_grp = ws_ref.shape[0]
    gdim = c_width // n_grp
    xn = _rms(h, g_ref[...]).astype(BF16)
    u = _gelu_tanh(jnp.dot(xn, w_in_ref[:, 0:c_width], preferred_element_type=F32))
    v = _gelu_tanh(jnp.dot(xn, w_in_ref[:, c_width:], preferred_element_type=F32))
    v = _rms(v, gv_ref[...]).astype(BF16)
    r_i = lax.broadcasted_iota(I32, (CHUNK, CHUNK), 0)
    c_i = lax.broadcasted_iota(I32, (CHUNK, CHUNK), 1)
    causal = c_i <= r_i
    bs = bs_ref[...]
    for grp in range(n_grp):
        w_m = jnp.where(causal, ws_ref[grp], 0.0).astype(BF16)
        bias = bs[:, grp:grp + 1]
        cols = slice(grp * gdim, (grp + 1) * gdim)
        for c in range(tm // CHUNK):
            rows = slice(c * CHUNK, (c + 1) * CHUNK)
            gate = jnp.dot(w_m, v[rows, cols], preferred_element_type=F32) + bias
            y_ref[rows, cols] = (u[rows, cols] * gate).astype(BF16)
    mix = jnp.dot(y_ref[...], w_out_ref[...], preferred_element_type=F32)
    _router_tail(h + mix, gf_ref, wr_ref, br_ref, *tail_refs)


def _odd_mixer(h_or_prev, g_mix, w_in, g_v, w_s, b_s_t, w_out, g_ffn, wr, br, *, tm):
    consts = [g_mix, w_in, g_v, w_s, b_s_t, w_out, g_ffn, wr, br]
    if isinstance(h_or_prev, tuple):
        head_specs, head_args = h_or_prev
        t, d = head_args[0].shape
    else:
        t, d = h_or_prev.shape
        head_specs, head_args = [pl.BlockSpec((tm, d), lambda i: (i, 0))], [h_or_prev]
    out_specs, out_shape, scratch = _tail_specs(t, d, tm)
    n_prev = len(head_args) if isinstance(h_or_prev, tuple) else 0
    return pl.pallas_call(
        functools.partial(_odd_kernel, n_prev=n_prev),
        grid=(t // tm,),
        in_specs=head_specs + [_const_spec(c.shape) for c in consts],
        out_specs=out_specs,
        out_shape=out_shape,
        scratch_shapes=scratch + [pltpu.VMEM((tm, g_v.shape[1]), BF16)],
        compiler_params=_params("arbitrary"),
        name="odd_mixer_router",
    )(*head_args, *consts)


def _sc_mesh():
    return plsc.VectorSubcoreMesh(core_axis_name="c", subcore_axis_name="s")


def _sc_worker_base(rows_per_worker):
    wid = lax.axis_index("s") * V7X_SC_CORES + lax.axis_index("c")
    return wid * rows_per_worker


def _dispatch(xp, dest, n_slots):
    t, w = xp.shape
    per_worker = t // SC_WORKERS
    n_chunks = per_worker // SC_ROWS
    assert per_worker * SC_WORKERS == t and n_chunks * SC_ROWS == per_worker and n_chunks % 2 == 0

    idx_t = pltpu.VMEM((SC_ROWS,), I32)

    @functools.partial(
        pl.kernel, mesh=_sc_mesh(),
        out_type=jax.ShapeDtypeStruct((n_slots, w), xp.dtype),
        scratch_types=[pltpu.VMEM((2, SC_ROWS, w), xp.dtype), idx_t, idx_t, idx_t, idx_t,
                       pltpu.SemaphoreType.DMA((2,)), pltpu.SemaphoreType.DMA((2,))],
    )
    def kern(x_hbm, d_hbm, out_hbm, rows_v, i00, i01, i10, i11, rsem, wsem):
        base = _sc_worker_base(per_worker)
        idx = ((i00, i01), (i10, i11))

        def load(c, b):
            off = pl.multiple_of(base + c * SC_ROWS, 8)
            pltpu.async_copy(x_hbm.at[pl.ds(off, SC_ROWS)], rows_v.at[b], rsem.at[b])
            for k in range(TOP_K):
                pltpu.sync_copy(d_hbm.at[k, pl.ds(off, SC_ROWS)], idx[b][k])

        def wait_load(b):
            pltpu.make_async_copy(x_hbm.at[pl.ds(0, SC_ROWS)], rows_v.at[b], rsem.at[b]).wait()

        def scatter(b):
            for k in range(TOP_K):
                pltpu.async_copy(rows_v.at[b], out_hbm.at[idx[b][k]], wsem.at[b])

        def wait_scatter(b):
            for k in range(TOP_K):
                pltpu.make_async_copy(rows_v.at[b], out_hbm.at[idx[b][k]], wsem.at[b]).wait()

        load(0, 0)

        @pl.loop(0, n_chunks, step=2)
        def _(c0):
            for b in range(2):
                c = c0 + b
                wait_load(b)
                scatter(b)

                @pl.when(c + 1 < n_chunks)
                def _():
                    @pl.when(c >= 1)
                    def _():
                        wait_scatter(1 - b)
                    load(c + 1, 1 - b)

        wait_scatter(0)
        wait_scatter(1)

    return kern(xp, dest)


def _combine(y, idx):
    n = idx.shape[0]
    w = y.shape[1]
    rows = 2 * SC_ROWS
    per_worker = n // SC_WORKERS
    n_chunks = per_worker // rows
    assert per_worker * SC_WORKERS == n and n_chunks * rows == per_worker and n_chunks % 2 == 0

    idx_t = pltpu.VMEM((rows,), I32)

    @functools.partial(
        pl.kernel, mesh=_sc_mesh(),
        out_type=jax.ShapeDtypeStruct((n, w), y.dtype),
        scratch_types=[pltpu.VMEM((2, rows, w), y.dtype), idx_t, idx_t,
                       pltpu.SemaphoreType.DMA((2,)), pltpu.SemaphoreType.DMA((2,))],
    )
    def kern(y_hbm, i_hbm, out_hbm, rows_v, i0, i1, gsem, wsem):
        base = _sc_worker_base(per_worker)
        ibuf = (i0, i1)

        def gather(c, b):
            off = pl.multiple_of(base + c * rows, 8)
            pltpu.sync_copy(i_hbm.at[pl.ds(off, rows)], ibuf[b])
            pltpu.async_copy(y_hbm.at[ibuf[b]], rows_v.at[b], gsem.at[b])

        def wait_gather(b):
            pltpu.make_async_copy(y_hbm.at[ibuf[b]], rows_v.at[b], gsem.at[b]).wait()

        def write(c, b):
            off = pl.multiple_of(base + c * rows, 8)
            pltpu.async_copy(rows_v.at[b], out_hbm.at[pl.ds(off, rows)], wsem.at[b])

        def wait_write(b):
            pltpu.make_async_copy(rows_v.at[b], out_hbm.at[pl.ds(0, rows)], wsem.at[b]).wait()

        gather(0, 0)

        @pl.loop(0, n_chunks, step=2)
        def _(c0):
            for b in range(2):
                c = c0 + b
                wait_gather(b)
                write(c, b)

                @pl.when(c + 1 < n_chunks)
                def _():
                    @pl.when(c >= 1)
                    def _():
                        wait_write(1 - b)
                    gather(c + 1, 1 - b)

        wait_write(0)
        wait_write(1)

    return kern(y, idx)


def _expert_kernel(te_ref, tv_ref, nu_ref, x_ref, w1_ref, w3_ref, w2_ref, y_ref,
                   w1b_ref, w3b_ref, w2b_ref):
    i = pl.program_id(0)

    @pl.when(i < nu_ref[0])
    def _():
        prev = te_ref[jnp.maximum(i - 1, 0)]

        @pl.when(jnp.logical_or(i == 0, te_ref[i] != prev))
        def _():
            w1b_ref[...] = w1_ref[0, 0].astype(BF16)
            w3b_ref[...] = w3_ref[0, 0].astype(BF16)
            w2b_ref[...] = w2_ref[0, 0].astype(BF16)

        xu = x_ref[...]
        rowid = lax.broadcasted_iota(I32, xu.shape, 0)
        xu = jnp.where(rowid < tv_ref[i], xu, jnp.zeros_like(xu))
        x = _unpack_bf16_pairs(xu).astype(BF16)
        h1 = jnp.dot(x, w1b_ref[...], preferred_element_type=F32)
        h3 = jnp.dot(x, w3b_ref[...], preferred_element_type=F32)
        he = (h1 * jax.nn.sigmoid(h1)) * h3
        y = jnp.dot(he.astype(BF16), w2b_ref[...], preferred_element_type=F32)
        y_ref[...] = _pack_bf16_pairs(y)


def _experts(x_sorted, tile_e, tile_valid, n_used, w1, w3, w2, layer):
    n_slots, wp = x_sorted.shape
    _, n_e, d, d_e = w1.shape
    ts = SLOT_TILE
    last = lambda nu: jnp.maximum(nu[0] - 1, 0)
    xspec = pl.BlockSpec((ts, wp), lambda i, te, tv, nu: (jnp.minimum(i, last(nu)), 0))
    wspec = lambda a, b: pl.BlockSpec((1, 1, a, b), lambda i, te, tv, nu: (layer, te[i], 0, 0))
    return pl.pallas_call(
        _expert_kernel,
        grid_spec=pltpu.PrefetchScalarGridSpec(
            num_scalar_prefetch=3,
            grid=(n_slots // ts,),
            in_specs=[xspec, wspec(d, d_e), wspec(d, d_e), wspec(d_e, d)],
            out_specs=xspec,
            scratch_shapes=[pltpu.VMEM((d, d_e), BF16), pltpu.VMEM((d, d_e), BF16),
                            pltpu.VMEM((d_e, d), BF16)],
        ),
        out_shape=jax.ShapeDtypeStruct((n_slots, d // 2), U32),
        compiler_params=_params("arbitrary"),
        name="moe_experts",
    )(tile_e, tile_valid, n_used, x_sorted, w1, w3, w2)


def _slot_plan(eids, ranks, counts, n_tiles):
    ts = SLOT_TILE
    cnt = counts[:, 0].astype(I32)
    padded = (cnt + ts - 1) // ts * ts
    e_ids = jnp.arange(N_EXPERTS, dtype=I32)
    ends = jnp.sum(jnp.where(e_ids[:, None] <= e_ids[None, :], padded[:, None], 0), axis=0)
    starts = ends - padded
    onehot = eids[None] == e_ids[:, None, None]
    dest = ranks + jnp.sum(jnp.where(onehot, starts[:, None, None], 0), axis=0)
    tile_lo = jnp.arange(n_tiles, dtype=I32) * ts
    tile_e = jnp.minimum(jnp.sum(tile_lo[:, None] >= ends[None, :], axis=1), N_EXPERTS - 1)
    tile_e = tile_e.astype(I32)
    valid_end = jnp.sum(jnp.where(tile_e[:, None] == e_ids[None, :], (starts + cnt)[None, :], 0),
                        axis=1)
    tile_valid = jnp.clip(valid_end - tile_lo, 0, ts).astype(I32)
    n_used = (ends[-1:] // ts).astype(I32)
    return dest, tile_e, tile_valid, n_used


def _post_value(h1_ref, y0_ref, y1_ref, gate_ref, p_ref, g_ref, wpg_ref, wpe_ref):
    gates = gate_ref[...]
    tm = gates.shape[1]
    gt = jnp.concatenate([gates, jnp.zeros((8 - TOP_K, tm), F32)], axis=0).T
    h2 = h1_ref[...] + (gt[:, 0:1] * _unpack_bf16_pairs(y0_ref[...]) +
                        gt[:, 1:2] * _unpack_bf16_pairs(y1_ref[...]))
    gate = jax.nn.sigmoid(jnp.dot(_rms(h2, g_ref[...]).astype(BF16), wpg_ref[...],
                                  preferred_element_type=F32))
    pe = jnp.dot(p_ref[...].astype(BF16), wpe_ref[...], preferred_element_type=F32)
    return h2 + gate * pe


def _post_operands(h1, yk, gates, p, g_pl, w_pg, w_pe, *, p_row0, tm):
    t, d = h1.shape
    nt = t // tm
    wy = yk.shape[1]
    p_blk0 = p_row0 // tm
    row = lambda w: pl.BlockSpec((tm, w), lambda i: (i, 0))
    specs = [row(d), row(wy), pl.BlockSpec((tm, wy), lambda i: (i + nt, 0)),
             pl.BlockSpec((TOP_K, tm), lambda i: (0, i)),
             pl.BlockSpec((tm, p.shape[1]), lambda i: (i + p_blk0, 0)),
             _const_spec((1, d)), _const_spec(w_pg.shape), _const_spec(w_pe.shape)]
    return specs, [h1, yk, yk, gates, p, g_pl, w_pg, w_pe]


def _post_kernel(*refs):
    *post_refs, o_ref = refs
    o_ref[...] = _post_value(*post_refs)


def _post(post_operands, *, tm):
    specs, args = post_operands
    t, d = args[0].shape
    return pl.pallas_call(
        _post_kernel,
        grid=(t // tm,),
        in_specs=specs,
        out_specs=pl.BlockSpec((tm, d), lambda i: (i, 0)),
        out_shape=jax.ShapeDtypeStruct((t, d), F32),
        compiler_params=_params("arbitrary"),
        name="layer_post",
    )(*args)


def _final_kernel(*refs):
    *post_refs, gfin_ref, o_ref = refs
    o_ref[...] = _rms(_post_value(*post_refs), gfin_ref[...])


def _final(post_operands, g_fin, *, tm):
    specs, args = post_operands
    t, d = args[0].shape
    return pl.pallas_call(
        _final_kernel,
        grid=(t // tm,),
        in_specs=specs + [_const_spec((1, d))],
        out_specs=pl.BlockSpec((tm, d), lambda i: (i, 0)),
        out_shape=jax.ShapeDtypeStruct((t, d), F32),
        compiler_params=_params("arbitrary"),
        name="final_post",
    )(*args, g_fin)


def _router_weights(wc, bc, wf, bf):
    d = wc.shape[0]
    w = jnp.zeros((d, ROUTER_LANES), F32)
    w = w.at[:, 0:N_GROUPS].set(wc).at[:, FINE_ROW0:FINE_ROW0 + N_EXPERTS].set(wf)
    b = jnp.zeros((1, ROUTER_LANES), F32)
    b = b.at[0, 0:N_GROUPS].set(bc).at[0, FINE_ROW0:FINE_ROW0 + N_EXPERTS].set(bf)
    hi = w.astype(BF16)
    lo = (w - hi.astype(F32)).astype(BF16)
    return jnp.concatenate([hi, lo], axis=1), b


def _moe(xp, eids, ranks, counts, w1, w3, w2, layer):
    t = xp.shape[0]
    n_tiles = TOP_K * t // SLOT_TILE + N_EXPERTS
    dest, tile_e, tile_valid, n_used = _slot_plan(eids, ranks, counts, n_tiles)
    x_sorted = _dispatch(xp, dest, n_tiles * SLOT_TILE)
    y_sorted = _experts(x_sorted, tile_e, tile_valid, n_used, w1, w3, w2, layer)
    return _combine(y_sorted, dest.reshape(-1))


def kernel(x, p, norm_mix, norm_ffn, norm_pl, final_norm, w_in_even, conv_w_even, w_out_even, w_in_odd, g_v_odd, w_s_odd, b_s_odd, w_out_odd, router_c, router_c_b, router_f, router_f_b, moe_w1, moe_w3, moe_w2, w_pe, w_pg):
    batch, seq, d = x.shape
    depth = p.shape[0]
    t = batch * seq
    tm = min(TOKEN_TILE, seq)
    p_rows = p.reshape(depth * t, -1)
    row = lambda a: a.reshape(1, -1)
    h = x.reshape(t, d)
    for i in range(depth):
        j = i // 2
        wr, br = _router_weights(router_c[i], router_c_b[i], router_f[i], router_f_b[i])
        g_ffn = row(norm_ffn[i])
        if i % 2 == 0:
            if isinstance(h, tuple):
                h = _post(h, tm=tm)
            ya, q, k, v = _even_in(h, row(norm_mix[i]), w_in_even[j].astype(BF16), conv_w_even[j],
                                   seq=seq, tm=tm)
            yb = _attention(q, k, v, batch=batch, seq=seq, tq=min(Q_TILE, seq))
            h1, xp, gates, eids, ranks, counts = _even_out(
                ya, yb, h, w_out_even[j].astype(BF16), g_ffn, wr, br, tm=tm)
        else:
            h1, xp, gates, eids, ranks, counts = _odd_mixer(
                h, row(norm_mix[i]), w_in_odd[j].astype(BF16), row(g_v_odd[j]), w_s_odd[j],
                b_s_odd[j].T, w_out_odd[j].astype(BF16), g_ffn, wr, br, tm=tm)
        yk = _moe(xp, eids, ranks, counts, moe_w1, moe_w3, moe_w2, i)
        h = _post_operands(h1, yk, gates, p_rows, row(norm_pl[i]), w_pg[i].astype(BF16),
                           w_pe[i].astype(BF16), p_row0=i * t, tm=tm)
    return _final(h, row(final_norm), tm=tm).reshape(batch, seq, d)
```

```python
import functools

import jax
import jax.numpy as jnp
from jax import lax
from jax.experimental import pallas as pl
from jax.experimental.pallas import tpu as pltpu
from jax.experimental.pallas import tpu_sc as plsc

EPS = 1e-6
HEAD_DIM = 64
HEADS_PER_SLAB = 2
ATTN_SLABS = 4
CHUNK = 128
N_GROUPS = 4
EXP_PER_GROUP = 8
N_EXPERTS = N_GROUPS * EXP_PER_GROUP
TOP_K = 2
ROUTER_LANES = 128
FINE_ROW0 = 8

V7X_VMEM_BYTES = 64 * 1024 * 1024
VMEM_LIMIT = V7X_VMEM_BYTES - 8 * 1024 * 1024
V7X_SC_CORES = 2
V7X_SC_SUBCORES = 16
SC_WORKERS = V7X_SC_CORES * V7X_SC_SUBCORES
SC_ROWS = 32

TOKEN_TILE = 512
Q_TILE = 128
SLOT_TILE = 512
X_RING = 3

LOG_F32_TINY_BOUND = -88.0
OFF_PENALTY = 1e30

F32 = jnp.float32
BF16 = jnp.bfloat16
U32 = jnp.uint32
I32 = jnp.int32


def _rms(x, g):
    return x * lax.rsqrt(jnp.mean(x * x, axis=-1, keepdims=True) + EPS) * g


def _const_spec(shape):
    nd = len(shape)
    return pl.BlockSpec(shape, lambda *_: (0,) * nd, pipeline_mode=pl.Buffered(1))


def _params(*sem):
    return pltpu.CompilerParams(dimension_semantics=sem, vmem_limit_bytes=VMEM_LIMIT)


def _even_in_kernel(h_ref, g_ref, w_ref, cw_ref, ya_ref, q_ref, k_ref, v_ref, zs_ref,
                    *, tiles_per_seq, a_width, sb_width):
    tm = h_ref.shape[0]
    xn = _rms(h_ref[...], g_ref[...]).astype(BF16)
    a3 = 3 * a_width

    @pl.when(pl.program_id(0) % tiles_per_seq == 0)
    def _():
        zs_ref[0:8, :] = jnp.zeros((8, a_width), F32)

    pa = jnp.dot(xn, w_ref[:, 0:a3], preferred_element_type=F32)
    z = pa[:, 2 * a_width:a3] * pa[:, 0:a_width]
    zs_ref[8:tm + 8, :] = z
    z1 = zs_ref[7:tm + 7, :]
    z2 = zs_ref[6:tm + 6, :]
    cw = cw_ref[...]
    conv = cw[0:1, :] * z2 + cw[1:2, :] * z1 + cw[2:3, :] * z
    ya_ref[...] = (pa[:, a_width:2 * a_width] * conv).astype(BF16)
    zs_ref[0:8, :] = zs_ref[tm:tm + 8, :]

    pq = jnp.dot(xn, w_ref[:, a3:a3 + 3 * sb_width], preferred_element_type=F32)
    q_ref[...] = (pq[:, 0:sb_width] * (HEAD_DIM ** -0.5)).astype(BF16)
    k_ref[...] = pq[:, sb_width:2 * sb_width].astype(BF16)
    v_ref[...] = pq[:, 2 * sb_width:3 * sb_width].astype(BF16)


def _even_in(h, g, w_in, conv_w, *, seq, tm):
    t, d = h.shape
    a_width = conv_w.shape[1]
    sb_width = (w_in.shape[1] - 3 * a_width) // 3
    kern = functools.partial(_even_in_kernel, tiles_per_seq=seq // tm, a_width=a_width,
                             sb_width=sb_width)
    row = lambda w: pl.BlockSpec((tm, w), lambda i: (i, 0))
    return pl.pallas_call(
        kern,
        grid=(t // tm,),
        in_specs=[row(d), _const_spec((1, d)), _const_spec(w_in.shape), _const_spec(conv_w.shape)],
        out_specs=[row(a_width), row(sb_width), row(sb_width), row(sb_width)],
        out_shape=[jax.ShapeDtypeStruct((t, a_width), BF16)] +
                  [jax.ShapeDtypeStruct((t, sb_width), BF16)] * 3,
        scratch_shapes=[pltpu.VMEM((tm + 8, a_width), F32)],
        compiler_params=_params("arbitrary"),
        name="even_in",
    )(h, g, w_in, conv_w)


def _attn_kernel(q_ref, k_ref, v_ref, o_ref, acc_ref, c_ref, *, tq):
    qi = pl.program_id(2)
    slab = HEADS_PER_SLAB * HEAD_DIM
    rows_slab = HEADS_PER_SLAB * tq
    rows = ATTN_SLABS * rows_slab
    lane = lax.broadcasted_iota(I32, (tq, slab), 1)
    qs = []
    for sl in range(ATTN_SLABS):
        q = q_ref[:, sl * slab:(sl + 1) * slab]
        zero = jnp.zeros_like(q)
        qs.append(jnp.concatenate([jnp.where(lane < HEAD_DIM, q, zero),
                                   jnp.where(lane >= HEAD_DIM, q, zero)], axis=0))
    r_i = lax.broadcasted_iota(I32, (tq, tq), 0)
    c_i = lax.broadcasted_iota(I32, (tq, tq), 1)
    upper = jnp.where(r_i > c_i, 1.0, 0.0).astype(BF16)
    t_loc = lax.broadcasted_iota(I32, (rows, tq), 0) & (tq - 1)
    s_loc = lax.broadcasted_iota(I32, (rows, tq), 1)
    causal = s_loc < t_loc

    def block(kb, c, mask):
        start = pl.multiple_of(kb * tq, tq)
        kk = k_ref[pl.ds(start, tq), :]
        vv = v_ref[pl.ds(start, tq), :]
        z = jnp.concatenate(
            [lax.dot_general(qs[sl], kk[:, sl * slab:(sl + 1) * slab], (((1,), (1,)), ((), ())),
                             preferred_element_type=F32) for sl in range(ATTN_SLABS)], axis=0)
        sp = jnp.maximum(z, 0.0) + jnp.log(1.0 + jnp.exp(-jnp.abs(z)))
        spm = sp if mask is None else jnp.where(mask, sp, 0.0)
        hi = spm.astype(BF16)
        lo = (spm - hi.astype(F32)).astype(BF16)
        later = (jnp.dot(hi, upper, preferred_element_type=F32) +
                 jnp.dot(lo, upper, preferred_element_type=F32))
        a = jnp.exp(z - sp - later - c)
        if mask is not None:
            a = jnp.where(mask, a, 0.0)
        a = a.astype(BF16)
        out = jnp.concatenate(
            [jnp.dot(a[sl * rows_slab:(sl + 1) * rows_slab], vv[:, sl * slab:(sl + 1) * slab],
                     preferred_element_type=F32) for sl in range(ATTN_SLABS)], axis=0)
        return out, jnp.sum(spm, axis=1, keepdims=True)

    out0, sum0 = block(qi, jnp.zeros((rows, 1), F32), causal)
    has_prev = qi >= 1
    out1, sum1 = block(jnp.maximum(qi - 1, 0), sum0 + jnp.where(has_prev, 0.0, OFF_PENALTY), None)
    c2 = sum0 + jnp.where(has_prev, sum1, 0.0)
    acc_ref[...] = out0 + out1
    c_ref[...] = c2

    def cond(carry):
        kb, cmin = carry
        return jnp.logical_and(kb >= 0, cmin < -LOG_F32_TINY_BOUND)

    def body(carry):
        kb, _ = carry
        c = c_ref[...]
        out, ssum = block(kb, c, None)
        acc_ref[...] += out
        c_new = c + ssum
        c_ref[...] = c_new
        return kb - 1, jnp.min(c_new)

    lax.while_loop(cond, body, (qi - 2, jnp.min(c2)))
    acc = acc_ref[...]
    for sl in range(ATTN_SLABS):
        lo_r = sl * rows_slab
        o_ref[:, sl * slab:(sl + 1) * slab] = jnp.where(
            lane < HEAD_DIM, acc[lo_r:lo_r + tq], acc[lo_r + tq:lo_r + 2 * tq]).astype(o_ref.dtype)


def _attention(q, k, v, *, batch, seq, tq):
    t, w = q.shape
    wblk = ATTN_SLABS * HEADS_PER_SLAB * HEAD_DIM
    nq = seq // tq
    qspec = pl.BlockSpec((tq, wblk), lambda b, s, i: (b * nq + i, s))
    kvspec = pl.BlockSpec((seq, wblk), lambda b, s, i: (b, s))
    rows = ATTN_SLABS * HEADS_PER_SLAB * tq
    return pl.pallas_call(
        functools.partial(_attn_kernel, tq=tq),
        grid=(batch, w // wblk, nq),
        in_specs=[qspec, kvspec, kvspec],
        out_specs=qspec,
        out_shape=jax.ShapeDtypeStruct((t, w), BF16),
        scratch_shapes=[pltpu.VMEM((rows, HEADS_PER_SLAB * HEAD_DIM), F32),
                        pltpu.VMEM((rows, 1), F32)],
        compiler_params=_params("arbitrary", "arbitrary", "arbitrary"),
        name="stick_breaking_attention",
    )(q, k, v)


def _pack_bf16_pairs(x):
    n = x.shape[1] // 2
    bits = lax.bitcast_convert_type(x.astype(BF16).astype(F32), U32)
    return bits[:, 0:n] | (bits[:, n:] >> 16)


def _unpack_bf16_pairs(u):
    a = lax.bitcast_convert_type(u & jnp.uint32(0xFFFF0000), F32)
    b = lax.bitcast_convert_type(u << 16, F32)
    return jnp.concatenate([a, b], axis=1)


def _router_tail(h1, g_ref, wr_ref, br_ref,
                 h1_ref, xp_ref, gate_ref, eid_ref, rank_ref, cnt_out_ref, cnt_ref):
    h1_ref[...] = h1
    xn = _rms(h1, g_ref[...])
    xp_ref[...] = _pack_bf16_pairs(xn)
    x_hi = xn.astype(BF16)
    x_lo = (xn - x_hi.astype(F32)).astype(BF16)
    both = jnp.dot(x_hi, wr_ref[...], preferred_element_type=F32)
    logits = (both[:, 0:ROUTER_LANES] + both[:, ROUTER_LANES:] +
              jnp.dot(x_lo, wr_ref[:, 0:ROUTER_LANES], preferred_element_type=F32) + br_ref[...])
    lt = logits.T
    tm = lt.shape[1]
    row = lax.broadcasted_iota(I32, (EXP_PER_GROUP, tm), 0)
    neg = jnp.float32(-jnp.inf)

    coarse = jnp.where(row < N_GROUPS, lt[0:EXP_PER_GROUP], neg)
    cmax = jnp.max(coarse, axis=0, keepdims=True)
    gi = jnp.min(jnp.where(coarse == cmax, row, EXP_PER_GROUP), axis=0, keepdims=True)
    pg = 1.0 / jnp.sum(jnp.exp(coarse - cmax), axis=0, keepdims=True)

    fine = lt[FINE_ROW0:FINE_ROW0 + EXP_PER_GROUP]
    for grp in range(1, N_GROUPS):
        lo_r = FINE_ROW0 + grp * EXP_PER_GROUP
        fine = jnp.where(gi == grp, lt[lo_r:lo_r + EXP_PER_GROUP], fine)
    m1 = jnp.max(fine, axis=0, keepdims=True)
    i1 = jnp.min(jnp.where(fine == m1, row, EXP_PER_GROUP), axis=0, keepdims=True)
    rest = jnp.where(row == i1, neg, fine)
    m2 = jnp.max(rest, axis=0, keepdims=True)
    i2 = jnp.min(jnp.where(rest == m2, row, EXP_PER_GROUP), axis=0, keepdims=True)
    e21 = jnp.exp(m2 - m1)
    w1 = 1.0 / (1.0 + e21)
    gate_ref[...] = jnp.concatenate([pg * w1, pg * (e21 * w1)], axis=0)
    e0 = gi * EXP_PER_GROUP + i1
    e1 = gi * EXP_PER_GROUP + i2
    eid_ref[...] = jnp.concatenate([e0, e1], axis=0)

    @pl.when(pl.program_id(0) == 0)
    def _():
        cnt_ref[...] = jnp.zeros_like(cnt_ref)

    erow = lax.broadcasted_iota(I32, (N_EXPERTS, tm), 0)
    oh0 = jnp.where(erow == e0, 1.0, 0.0)
    oh1 = jnp.where(erow == e1, 1.0, 0.0)
    r_i = lax.broadcasted_iota(I32, (tm, tm), 0)
    c_i = lax.broadcasted_iota(I32, (tm, tm), 1)
    before = jnp.where(r_i < c_i, 1.0, 0.0).astype(BF16)
    pre0 = jnp.dot(oh0.astype(BF16), before, preferred_element_type=F32)
    pre1 = jnp.dot(oh1.astype(BF16), before, preferred_element_type=F32)
    tot0 = jnp.sum(oh0, axis=1, keepdims=True)
    tot1 = jnp.sum(oh1, axis=1, keepdims=True)
    base = cnt_ref[:, 0:1]
    r0 = jnp.sum(oh0 * (pre0 + base), axis=0, keepdims=True)
    r1 = jnp.sum(oh1 * (pre1 + (base + tot0)), axis=0, keepdims=True)
    rank_ref[...] = jnp.concatenate([r0, r1], axis=0).astype(I32)
    total = jnp.broadcast_to(base + tot0 + tot1, cnt_ref.shape)
    cnt_ref[...] = total
    cnt_out_ref[...] = total


def _tail_specs(t, d, tm):
    row = lambda w: pl.BlockSpec((tm, w), lambda i: (i, 0))
    lanes = pl.BlockSpec((TOP_K, tm), lambda i: (0, i))
    cnt = pl.BlockSpec((N_EXPERTS, ROUTER_LANES), lambda i: (0, 0))
    out_specs = [row(d), row(d // 2), lanes, lanes, lanes, cnt]
    out_shape = [jax.ShapeDtypeStruct((t, d), F32), jax.ShapeDtypeStruct((t, d // 2), U32),
                 jax.ShapeDtypeStruct((TOP_K, t), F32), jax.ShapeDtypeStruct((TOP_K, t), I32),
                 jax.ShapeDtypeStruct((TOP_K, t), I32),
                 jax.ShapeDtypeStruct((N_EXPERTS, ROUTER_LANES), F32)]
    scratch = [pltpu.VMEM((N_EXPERTS, ROUTER_LANES), F32)]
    return out_specs, out_shape, scratch


def _even_out_kernel(ya_ref, yb_ref, h_ref, w_ref, g_ref, wr_ref, br_ref, *tail_refs):
    a_width = ya_ref.shape[1]
    mix = (jnp.dot(ya_ref[...], w_ref[0:a_width, :], preferred_element_type=F32) +
           jnp.dot(yb_ref[...], w_ref[a_width:, :], preferred_element_type=F32))
    _router_tail(h_ref[...] + mix, g_ref, wr_ref, br_ref, *tail_refs)


def _even_out(ya, yb, h, w_out, g_ffn, wr, br, *, tm):
    t, d = h.shape
    out_specs, out_shape, scratch = _tail_specs(t, d, tm)
    row = lambda w: pl.BlockSpec((tm, w), lambda i: (i, 0))
    return pl.pallas_call(
        _even_out_kernel,
        grid=(t // tm,),
        in_specs=[row(ya.shape[1]), row(yb.shape[1]), row(d), _const_spec(w_out.shape),
                  _const_spec((1, d)), _const_spec(wr.shape), _const_spec(br.shape)],
        out_specs=out_specs,
        out_shape=out_shape,
        scratch_shapes=scratch,
        compiler_params=_params("arbitrary"),
        name="even_out_router",
    )(ya, yb, h, w_out, g_ffn, wr, br)


def _gelu_tanh(x):
    return 0.5 * x * (1.0 + jnp.tanh(0.7978845608028654 * (x + 0.044715 * (x * x * x))))


def _odd_kernel(*refs, n_prev):
    h_refs, refs = refs[:max(n_prev, 1)], refs[max(n_prev, 1):]
    (g_ref, w_in_ref, gv_ref, ws_ref, bs_ref, w_out_ref, gf_ref, wr_ref, br_ref,
     *tail_refs, y_ref) = refs
    h = _post_value(*h_refs) if n_prev else h_refs[0][...]
    tm = h.shape[0]
    c_width = gv_ref.shape[1]
    n_grp = ws_ref.shape[0]
    gdim = c_width // n_grp
    xn = _rms(h, g_ref[...]).astype(BF16)
    u = _gelu_tanh(jnp.dot(xn, w_in_ref[:, 0:c_width], preferred_element_type=F32))
    v = _gelu_tanh(jnp.dot(xn, w_in_ref[:, c_width:], preferred_element_type=F32))
    v = _rms(v, gv_ref[...]).astype(BF16)
    r_i = lax.broadcasted_iota(I32, (CHUNK, CHUNK), 0)
    c_i = lax.broadcasted_iota(I32, (CHUNK, CHUNK), 1)
    causal = c_i <= r_i
    bs = bs_ref[...]
    for grp in range(n_grp):
        w_m = jnp.where(causal, ws_ref[grp], 0.0).astype(BF16)
        bias = bs[:, grp:grp + 1]
        cols = slice(grp * gdim, (grp + 1) * gdim)
        for c in range(tm // CHUNK):
            rows = slice(c * CHUNK, (c + 1) * CHUNK)
            gate = jnp.dot(w_m, v[rows, cols], preferred_element_type=F32) + bias
            y_ref[rows, cols] = (u[rows, cols] * gate).astype(BF16)
    mix = jnp.dot(y_ref[...], w_out_ref[...], preferred_element_type=F32)
    _router_tail(h + mix, gf_ref, wr_ref, br_ref, *tail_refs)


def _odd_mixer(h_or_prev, g_mix, w_in, g_v, w_s, b_s_t, w_out, g_ffn, wr, br, *, tm):
    consts = [g_mix, w_in, g_v, w_s, b_s_t, w_out, g_ffn, wr, br]
    if isinstance(h_or_prev, tuple):
        head_specs, head_args = h_or_prev
        t, d = head_args[0].shape
    else:
        t, d = h_or_prev.shape
        head_specs, head_args = [pl.BlockSpec((tm, d), lambda i: (i, 0))], [h_or_prev]
    out_specs, out_shape, scratch = _tail_specs(t, d, tm)
    n_prev = len(head_args) if isinstance(h_or_prev, tuple) else 0
    return pl.pallas_call(
        functools.partial(_odd_kernel, n_prev=n_prev),
        grid=(t // tm,),
        in_specs=head_specs + [_const_spec(c.shape) for c in consts],
        out_specs=out_specs,
        out_shape=out_shape,
        scratch_shapes=scratch + [pltpu.VMEM((tm, g_v.shape[1]), BF16)],
        compiler_params=_params("arbitrary"),
        name="odd_mixer_router",
    )(*head_args, *consts)


def _sc_mesh():
    return plsc.VectorSubcoreMesh(core_axis_name="c", subcore_axis_name="s")


def _sc_worker_base(rows_per_worker):
    wid = lax.axis_index("s") * V7X_SC_CORES + lax.axis_index("c")
    return wid * rows_per_worker


def _dispatch(xp, dest, n_slots):
    t, w = xp.shape
    per_worker = t // SC_WORKERS
    n_chunks = per_worker // SC_ROWS
    assert per_worker * SC_WORKERS == t and n_chunks * SC_ROWS == per_worker and n_chunks % 2 == 0

    idx_t = pltpu.VMEM((SC_ROWS,), I32)

    @functools.partial(
        pl.kernel, mesh=_sc_mesh(),
        out_type=jax.ShapeDtypeStruct((n_slots, w), xp.dtype),
        scratch_types=[pltpu.VMEM((2, SC_ROWS, w), xp.dtype), idx_t, idx_t, idx_t, idx_t,
                       pltpu.SemaphoreType.DMA((2,)), pltpu.SemaphoreType.DMA((2,))],
    )
    def kern(x_hbm, d_hbm, out_hbm, rows_v, i00, i01, i10, i11, rsem, wsem):
        base = _sc_worker_base(per_worker)
        idx = ((i00, i01), (i10, i11))

        def load(c, b):
            off = pl.multiple_of(base + c * SC_ROWS, 8)
            pltpu.async_copy(x_hbm.at[pl.ds(off, SC_ROWS)], rows_v.at[b], rsem.at[b])
            for k in range(TOP_K):
                pltpu.sync_copy(d_hbm.at[k, pl.ds(off, SC_ROWS)], idx[b][k])

        def wait_load(b):
            pltpu.make_async_copy(x_hbm.at[pl.ds(0, SC_ROWS)], rows_v.at[b], rsem.at[b]).wait()

        def scatter(b):
            for k in range(TOP_K):
                pltpu.async_copy(rows_v.at[b], out_hbm.at[idx[b][k]], wsem.at[b])

        def wait_scatter(b):
            for k in range(TOP_K):
                pltpu.make_async_copy(rows_v.at[b], out_hbm.at[idx[b][k]], wsem.at[b]).wait()

        load(0, 0)

        @pl.loop(0, n_chunks, step=2)
        def _(c0):
            for b in range(2):
                c = c0 + b
                wait_load(b)
                scatter(b)

                @pl.when(c + 1 < n_chunks)
                def _():
                    @pl.when(c >= 1)
                    def _():
                        wait_scatter(1 - b)
                    load(c + 1, 1 - b)

        wait_scatter(0)
        wait_scatter(1)

    return kern(xp, dest)


def _combine(y, idx):
    n = idx.shape[0]
    w = y.shape[1]
    rows = 2 * SC_ROWS
    per_worker = n // SC_WORKERS
    n_chunks = per_worker // rows
    assert per_worker * SC_WORKERS == n and n_chunks * rows == per_worker and n_chunks % 2 == 0

    idx_t = pltpu.VMEM((rows,), I32)

    @functools.partial(
        pl.kernel, mesh=_sc_mesh(),
        out_type=jax.ShapeDtypeStruct((n, w), y.dtype),
        scratch_types=[pltpu.VMEM((2, rows, w), y.dtype), idx_t, idx_t,
                       pltpu.SemaphoreType.DMA((2,)), pltpu.SemaphoreType.DMA((2,))],
    )
    def kern(y_hbm, i_hbm, out_hbm, rows_v, i0, i1, gsem, wsem):
        base = _sc_worker_base(per_worker)
        ibuf = (i0, i1)

        def gather(c, b):
            off = pl.multiple_of(base + c * rows, 8)
            pltpu.sync_copy(i_hbm.at[pl.ds(off, rows)], ibuf[b])
            pltpu.async_copy(y_hbm.at[ibuf[b]], rows_v.at[b], gsem.at[b])

        def wait_gather(b):
            pltpu.make_async_copy(y_hbm.at[ibuf[b]], rows_v.at[b], gsem.at[b]).wait()

        def write(c, b):
            off = pl.multiple_of(base + c * rows, 8)
            pltpu.async_copy(rows_v.at[b], out_hbm.at[pl.ds(off, rows)], wsem.at[b])

        def wait_write(b):
            pltpu.make_async_copy(rows_v.at[b], out_hbm.at[pl.ds(0, rows)], wsem.at[b]).wait()

        gather(0, 0)

        @pl.loop(0, n_chunks, step=2)
        def _(c0):
            for b in range(2):
                c = c0 + b
                wait_gather(b)
                write(c, b)

                @pl.when(c + 1 < n_chunks)
                def _():
                    @pl.when(c >= 1)
                    def _():
                        wait_write(1 - b)
                    gather(c + 1, 1 - b)

        wait_write(0)
        wait_write(1)

    return kern(y, idx)


def _expert_kernel(te_ref, tv_ref, nu_ref, x_hbm, w1_ref, w3_ref, w2_ref, y_ref,
                   xbuf_ref, xsem, w1b_ref, w3b_ref, w2b_ref):
    i = pl.program_id(0)
    n_used = nu_ref[0]
    ts = xbuf_ref.shape[1]

    def x_copy(tile):
        slot = tile % X_RING
        rows = pl.ds(pl.multiple_of(tile * ts, ts), ts)
        return pltpu.make_async_copy(x_hbm.at[rows], xbuf_ref.at[slot], xsem.at[slot])

    @pl.when(i == 0)
    def _():
        for tile in range(X_RING - 1):
            @pl.when(tile < n_used)
            def _():
                x_copy(tile).start()

    @pl.when(i < n_used)
    def _():
        @pl.when(i + (X_RING - 1) < n_used)
        def _():
            x_copy(i + (X_RING - 1)).start()

        prev = te_ref[jnp.maximum(i - 1, 0)]

        @pl.when(jnp.logical_or(i == 0, te_ref[i] != prev))
        def _():
            w1b_ref[...] = w1_ref[0, 0].astype(BF16)
            w3b_ref[...] = w3_ref[0, 0].astype(BF16)
            w2b_ref[...] = w2_ref[0, 0].astype(BF16)

        x_copy(i).wait()
        xu = xbuf_ref[i % X_RING]
        rowid = lax.broadcasted_iota(I32, xu.shape, 0)
        xu = jnp.where(rowid < tv_ref[i], xu, jnp.zeros_like(xu))
        x = _unpack_bf16_pairs(xu).astype(BF16)
        h1 = jnp.dot(x, w1b_ref[...], preferred_element_type=F32)
        h3 = jnp.dot(x, w3b_ref[...], preferred_element_type=F32)
        he = (h1 * jax.nn.sigmoid(h1)) * h3
        y = jnp.dot(he.astype(BF16), w2b_ref[...], preferred_element_type=F32)
        y_ref[...] = _pack_bf16_pairs(y)


def _experts(x_sorted, tile_e, tile_valid, n_used, w1, w3, w2, layer):
    n_slots, wp = x_sorted.shape
    _, n_e, d, d_e = w1.shape
    ts = SLOT_TILE
    last = lambda nu: jnp.maximum(nu[0] - 1, 0)
    yspec = pl.BlockSpec((ts, wp), lambda i, te, tv, nu: (jnp.minimum(i, last(nu)), 0))
    wspec = lambda a, b: pl.BlockSpec((1, 1, a, b), lambda i, te, tv, nu: (layer, te[i], 0, 0))
    return pl.pallas_call(
        _expert_kernel,
        grid_spec=pltpu.PrefetchScalarGridSpec(
            num_scalar_prefetch=3,
            grid=(n_slots // ts,),
            in_specs=[pl.BlockSpec(memory_space=pl.ANY), wspec(d, d_e), wspec(d, d_e),
                      wspec(d_e, d)],
            out_specs=yspec,
            scratch_shapes=[pltpu.VMEM((X_RING, ts, wp), x_sorted.dtype),
                            pltpu.SemaphoreType.DMA((X_RING,)),
                            pltpu.VMEM((d, d_e), BF16), pltpu.VMEM((d, d_e), BF16),
                            pltpu.VMEM((d_e, d), BF16)],
        ),
        out_shape=jax.ShapeDtypeStruct((n_slots, d // 2), U32),
        compiler_params=_params("arbitrary"),
        name="moe_experts",
    )(tile_e, tile_valid, n_used, x_sorted, w1, w3, w2)


def _slot_plan(eids, ranks, counts, n_tiles):
    ts = SLOT_TILE
    cnt = counts[:, 0].astype(I32)
    padded = (cnt + ts - 1) // ts * ts
    e_ids = jnp.arange(N_EXPERTS, dtype=I32)
    ends = jnp.sum(jnp.where(e_ids[:, None] <= e_ids[None, :], padded[:, None], 0), axis=0)
    starts = ends - padded
    onehot = eids[None] == e_ids[:, None, None]
    dest = ranks + jnp.sum(jnp.where(onehot, starts[:, None, None], 0), axis=0)
    tile_lo = jnp.arange(n_tiles, dtype=I32) * ts
    tile_e = jnp.minimum(jnp.sum(tile_lo[:, None] >= ends[None, :], axis=1), N_EXPERTS - 1)
    tile_e = tile_e.astype(I32)
    valid_end = jnp.sum(jnp.where(tile_e[:, None] == e_ids[None, :], (starts + cnt)[None, :], 0),
                        axis=1)
    tile_valid = jnp.clip(valid_end - tile_lo, 0, ts).astype(I32)
    n_used = (ends[-1:] // ts).astype(I32)
    return dest, tile_e, tile_valid, n_used


def _post_value(h1_ref, y0_ref, y1_ref, gate_ref, p_ref, g_ref, wpg_ref, wpe_ref):
    gates = gate_ref[...]
    tm = gates.shape[1]
    gt = jnp.concatenate([gates, jnp.zeros((8 - TOP_K, tm), F32)], axis=0).T
    h2 = h1_ref[...] + (gt[:, 0:1] * _unpack_bf16_pairs(y0_ref[...]) +
                        gt[:, 1:2] * _unpack_bf16_pairs(y1_ref[...]))
    gate = jax.nn.sigmoid(jnp.dot(_rms(h2, g_ref[...]).astype(BF16), wpg_ref[...],
                                  preferred_element_type=F32))
    pe = jnp.dot(p_ref[...].astype(BF16), wpe_ref[...], preferred_element_type=F32)
    return h2 + gate * pe


def _post_operands(h1, yk, gates, p, g_pl, w_pg, w_pe, *, p_row0, tm):
    t, d = h1.shape
    nt = t // tm
    wy = yk.shape[1]
    p_blk0 = p_row0 // tm
    row = lambda w: pl.BlockSpec((tm, w), lambda i: (i, 0))
    specs = [row(d), row(wy), pl.BlockSpec((tm, wy), lambda i: (i + nt, 0)),
             pl.BlockSpec((TOP_K, tm), lambda i: (0, i)),
             pl.BlockSpec((tm, p.shape[1]), lambda i: (i + p_blk0, 0)),
             _const_spec((1, d)), _const_spec(w_pg.shape), _const_spec(w_pe.shape)]
    return specs, [h1, yk, yk, gates, p, g_pl, w_pg, w_pe]


def _post_kernel(*refs):
    *post_refs, o_ref = refs
    o_ref[...] = _post_value(*post_refs)


def _post(post_operands, *, tm):
    specs, args = post_operands
    t, d = args[0].shape
    return pl.pallas_call(
        _post_kernel,
        grid=(t // tm,),
        in_specs=specs,
        out_specs=pl.BlockSpec((tm, d), lambda i: (i, 0)),
        out_shape=jax.ShapeDtypeStruct((t, d), F32),
        compiler_params=_params("arbitrary"),
        name="layer_post",
    )(*args)


def _final_kernel(*refs):
    *post_refs, gfin_ref, o_ref = refs
    o_ref[...] = _rms(_post_value(*post_refs), gfin_ref[...])


def _final(post_operands, g_fin, *, tm):
    specs, args = post_operands
    t, d = args[0].shape
    return pl.pallas_call(
        _final_kernel,
        grid=(t // tm,),
        in_specs=specs + [_const_spec((1, d))],
        out_specs=pl.BlockSpec((tm, d), lambda i: (i, 0)),
        out_shape=jax.ShapeDtypeStruct((t, d), F32),
        compiler_params=_params("arbitrary"),
        name="final_post",
    )(*args, g_fin)


def _router_weights(wc, bc, wf, bf):
    d = wc.shape[0]
    w = jnp.zeros((d, ROUTER_LANES), F32)
    w = w.at[:, 0:N_GROUPS].set(wc).at[:, FINE_ROW0:FINE_ROW0 + N_EXPERTS].set(wf)
    b = jnp.zeros((1, ROUTER_LANES), F32)
    b = b.at[0, 0:N_GROUPS].set(bc).at[0, FINE_ROW0:FINE_ROW0 + N_EXPERTS].set(bf)
    hi = w.astype(BF16)
    lo = (w - hi.astype(F32)).astype(BF16)
    return jnp.concatenate([hi, lo], axis=1), b


def _moe(xp, eids, ranks, counts, w1, w3, w2, layer):
    t = xp.shape[0]
    n_tiles = TOP_K * t // SLOT_TILE + N_EXPERTS
    dest, tile_e, tile_valid, n_used = _slot_plan(eids, ranks, counts, n_tiles)
    x_sorted = _dispatch(xp, dest, n_tiles * SLOT_TILE)
    y_sorted = _experts(x_sorted, tile_e, tile_valid, n_used, w1, w3, w2, layer)
    return _combine(y_sorted, dest.reshape(-1))


def kernel(x, p, norm_mix, norm_ffn, norm_pl, final_norm, w_in_even, conv_w_even, w_out_even, w_in_odd, g_v_odd, w_s_odd, b_s_odd, w_out_odd, router_c, router_c_b, router_f, router_f_b, moe_w1, moe_w3, moe_w2, w_pe, w_pg):
    batch, seq, d = x.shape
    depth = p.shape[0]
    t = batch * seq
    tm = min(TOKEN_TILE, seq)
    p_rows = p.reshape(depth * t, -1)
    row = lambda a: a.reshape(1, -1)
    h = x.reshape(t, d)
    for i in range(depth):
        j = i // 2
        wr, br = _router_weights(router_c[i], router_c_b[i], router_f[i], router_f_b[i])
        g_ffn = row(norm_ffn[i])
        if i % 2 == 0:
            if isinstance(h, tuple):
                h = _post(h, tm=tm)
            ya, q, k, v = _even_in(h, row(norm_mix[i]), w_in_even[j].astype(BF16), conv_w_even[j],
                                   seq=seq, tm=tm)
            yb = _attention(q, k, v, batch=batch, seq=seq, tq=min(Q_TILE, seq))
            h1, xp, gates, eids, ranks, counts = _even_out(
                ya, yb, h, w_out_even[j].astype(BF16), g_ffn, wr, br, tm=tm)
        else:
            h1, xp, gates, eids, ranks, counts = _odd_mixer(
                h, row(norm_mix[i]), w_in_odd[j].astype(BF16), row(g_v_odd[j]), w_s_odd[j],
                b_s_odd[j].T, w_out_odd[j].astype(BF16), g_ffn, wr, br, tm=tm)
        yk = _moe(xp, eids, ranks, counts, moe_w1, moe_w3, moe_w2, i)
        h = _post_operands(h1, yk, gates, p_rows, row(norm_pl[i]), w_pg[i].astype(BF16),
                           w_pe[i].astype(BF16), p_row0=i * t, tm=tm)
    return _final(h, row(final_norm), tm=tm).reshape(batch, seq, d)
```

```python
import functools

import jax
import jax.numpy as jnp
from jax import lax
from jax.experimental import pallas as pl
from jax.experimental.pallas import tpu as pltpu
from jax.experimental.pallas import tpu_sc as plsc

EPS = 1e-6
HEAD_DIM = 64
HEADS_PER_SLAB = 2
ATTN_SLABS = 4
CHUNK = 128
N_GROUPS = 4
EXP_PER_GROUP = 8
N_EXPERTS = N_GROUPS * EXP_PER_GROUP
TOP_K = 2
ROUTER_LANES = 128
FINE_ROW0 = 8

V7X_VMEM_BYTES = 64 * 1024 * 1024
VMEM_LIMIT = V7X_VMEM_BYTES - 8 * 1024 * 1024
V7X_SC_CORES = 2
V7X_SC_SUBCORES = 16
SC_WORKERS = V7X_SC_CORES * V7X_SC_SUBCORES
SC_ROWS = 32

TOKEN_TILE = 512
LIGHT_TOKEN_TILE = 1024
Q_TILE = 128
SLOT_TILE = 512
X_RING = 3

LOG_F32_TINY_BOUND = -88.0
OFF_PENALTY = 1e30

F32 = jnp.float32
BF16 = jnp.bfloat16
U32 = jnp.uint32
I32 = jnp.int32


def _rms(x, g):
    return x * lax.rsqrt(jnp.mean(x * x, axis=-1, keepdims=True) + EPS) * g


def _const_spec(shape):
    nd = len(shape)
    return pl.BlockSpec(shape, lambda *_: (0,) * nd, pipeline_mode=pl.Buffered(1))


def _params(*sem):
    return pltpu.CompilerParams(dimension_semantics=sem, vmem_limit_bytes=VMEM_LIMIT)


def _even_in_kernel(h_ref, g_ref, w_ref, cw_ref, ya_ref, q_ref, k_ref, v_ref, zs_ref,
                    *, tiles_per_seq, a_width, sb_width):
    tm = h_ref.shape[0]
    xn = _rms(h_ref[...], g_ref[...]).astype(BF16)
    a3 = 3 * a_width

    @pl.when(pl.program_id(0) % tiles_per_seq == 0)
    def _():
        zs_ref[0:8, :] = jnp.zeros((8, a_width), F32)

    pa = jnp.dot(xn, w_ref[:, 0:a3], preferred_element_type=F32)
    z = pa[:, 2 * a_width:a3] * pa[:, 0:a_width]
    zs_ref[8:tm + 8, :] = z
    z1 = zs_ref[7:tm + 7, :]
    z2 = zs_ref[6:tm + 6, :]
    cw = cw_ref[...]
    conv = cw[0:1, :] * z2 + cw[1:2, :] * z1 + cw[2:3, :] * z
    ya_ref[...] = (pa[:, a_width:2 * a_width] * conv).astype(BF16)
    zs_ref[0:8, :] = zs_ref[tm:tm + 8, :]

    pq = jnp.dot(xn, w_ref[:, a3:a3 + 3 * sb_width], preferred_element_type=F32)
    q_ref[...] = (pq[:, 0:sb_width] * (HEAD_DIM ** -0.5)).astype(BF16)
    k_ref[...] = pq[:, sb_width:2 * sb_width].astype(BF16)
    v_ref[...] = pq[:, 2 * sb_width:3 * sb_width].astype(BF16)


def _even_in(h, g, w_in, conv_w, *, seq, tm):
    t, d = h.shape
    a_width = conv_w.shape[1]
    sb_width = (w_in.shape[1] - 3 * a_width) // 3
    kern = functools.partial(_even_in_kernel, tiles_per_seq=seq // tm, a_width=a_width,
                             sb_width=sb_width)
    row = lambda w: pl.BlockSpec((tm, w), lambda i: (i, 0))
    return pl.pallas_call(
        kern,
        grid=(t // tm,),
        in_specs=[row(d), _const_spec((1, d)), _const_spec(w_in.shape), _const_spec(conv_w.shape)],
        out_specs=[row(a_width), row(sb_width), row(sb_width), row(sb_width)],
        out_shape=[jax.ShapeDtypeStruct((t, a_width), BF16)] +
                  [jax.ShapeDtypeStruct((t, sb_width), BF16)] * 3,
        scratch_shapes=[pltpu.VMEM((tm + 8, a_width), F32)],
        compiler_params=_params("arbitrary"),
        name="even_in",
    )(h, g, w_in, conv_w)


def _attn_kernel(q_ref, k_ref, v_ref, o_ref, acc_ref, c_ref, *, tq):
    qi = pl.program_id(2)
    slab = HEADS_PER_SLAB * HEAD_DIM
    rows_slab = HEADS_PER_SLAB * tq
    rows = ATTN_SLABS * rows_slab
    lane = lax.broadcasted_iota(I32, (tq, slab), 1)
    qs = []
    for sl in range(ATTN_SLABS):
        q = q_ref[:, sl * slab:(sl + 1) * slab]
        zero = jnp.zeros_like(q)
        qs.append(jnp.concatenate([jnp.where(lane < HEAD_DIM, q, zero),
                                   jnp.where(lane >= HEAD_DIM, q, zero)], axis=0))
    r_i = lax.broadcasted_iota(I32, (tq, tq), 0)
    c_i = lax.broadcasted_iota(I32, (tq, tq), 1)
    upper = jnp.where(r_i > c_i, 1.0, 0.0).astype(BF16)
    t_loc = lax.broadcasted_iota(I32, (rows, tq), 0) & (tq - 1)
    s_loc = lax.broadcasted_iota(I32, (rows, tq), 1)
    causal = s_loc < t_loc

    def block(kb, c, mask):
        start = pl.multiple_of(kb * tq, tq)
        kk = k_ref[pl.ds(start, tq), :]
        vv = v_ref[pl.ds(start, tq), :]
        z = jnp.concatenate(
            [lax.dot_general(qs[sl], kk[:, sl * slab:(sl + 1) * slab], (((1,), (1,)), ((), ())),
                             preferred_element_type=F32) for sl in range(ATTN_SLABS)], axis=0)
        sp = jnp.maximum(z, 0.0) + jnp.log(1.0 + jnp.exp(-jnp.abs(z)))
        spm = sp if mask is None else jnp.where(mask, sp, 0.0)
        hi = spm.astype(BF16)
        lo = (spm - hi.astype(F32)).astype(BF16)
        later = (jnp.dot(hi, upper, preferred_element_type=F32) +
                 jnp.dot(lo, upper, preferred_element_type=F32))
        a = jnp.exp(z - sp - later - c)
        if mask is not None:
            a = jnp.where(mask, a, 0.0)
        a = a.astype(BF16)
        out = jnp.concatenate(
            [jnp.dot(a[sl * rows_slab:(sl + 1) * rows_slab], vv[:, sl * slab:(sl + 1) * slab],
                     preferred_element_type=F32) for sl in range(ATTN_SLABS)], axis=0)
        return out, jnp.sum(spm, axis=1, keepdims=True)

    out0, sum0 = block(qi, jnp.zeros((rows, 1), F32), causal)
    has_prev = qi >= 1
    out1, sum1 = block(jnp.maximum(qi - 1, 0), sum0 + jnp.where(has_prev, 0.0, OFF_PENALTY), None)
    c2 = sum0 + jnp.where(has_prev, sum1, 0.0)
    acc_ref[...] = out0 + out1
    c_ref[...] = c2

    def cond(carry):
        kb, cmin = carry
        return jnp.logical_and(kb >= 0, cmin < -LOG_F32_TINY_BOUND)

    def body(carry):
        kb, _ = carry
        c = c_ref[...]
        out, ssum = block(kb, c, None)
        acc_ref[...] += out
        c_new = c + ssum
        c_ref[...] = c_new
        return kb - 1, jnp.min(c_new)

    lax.while_loop(cond, body, (qi - 2, jnp.min(c2)))
    acc = acc_ref[...]
    for sl in range(ATTN_SLABS):
        lo_r = sl * rows_slab
        o_ref[:, sl * slab:(sl + 1) * slab] = jnp.where(
            lane < HEAD_DIM, acc[lo_r:lo_r + tq], acc[lo_r + tq:lo_r + 2 * tq]).astype(o_ref.dtype)


def _attention(q, k, v, *, batch, seq, tq):
    t, w = q.shape
    wblk = ATTN_SLABS * HEADS_PER_SLAB * HEAD_DIM
    nq = seq // tq
    qspec = pl.BlockSpec((tq, wblk), lambda b, s, i: (b * nq + i, s))
    kvspec = pl.BlockSpec((seq, wblk), lambda b, s, i: (b, s))
    rows = ATTN_SLABS * HEADS_PER_SLAB * tq
    return pl.pallas_call(
        functools.partial(_attn_kernel, tq=tq),
        grid=(batch, w // wblk, nq),
        in_specs=[qspec, kvspec, kvspec],
        out_specs=qspec,
        out_shape=jax.ShapeDtypeStruct((t, w), BF16),
        scratch_shapes=[pltpu.VMEM((rows, HEADS_PER_SLAB * HEAD_DIM), F32),
                        pltpu.VMEM((rows, 1), F32)],
        compiler_params=_params("arbitrary", "arbitrary", "arbitrary"),
        name="stick_breaking_attention",
    )(q, k, v)


def _pack_bf16_pairs(x):
    n = x.shape[1] // 2
    bits = lax.bitcast_convert_type(x.astype(BF16).astype(F32), U32)
    return bits[:, 0:n] | (bits[:, n:] >> 16)


def _unpack_bf16_pairs(u):
    a = lax.bitcast_convert_type(u & jnp.uint32(0xFFFF0000), F32)
    b = lax.bitcast_convert_type(u << 16, F32)
    return jnp.concatenate([a, b], axis=1)


def _router_tail(h1, g_ref, wr_ref, br_ref,
                 h1_ref, xp_ref, gate_ref, eid_ref, rank_ref, cnt_out_ref, cnt_ref):
    h1_ref[...] = h1
    xn = _rms(h1, g_ref[...])
    xp_ref[...] = _pack_bf16_pairs(xn)
    x_hi = xn.astype(BF16)
    x_lo = (xn - x_hi.astype(F32)).astype(BF16)
    both = jnp.dot(x_hi, wr_ref[...], preferred_element_type=F32)
    logits = (both[:, 0:ROUTER_LANES] + both[:, ROUTER_LANES:] +
              jnp.dot(x_lo, wr_ref[:, 0:ROUTER_LANES], preferred_element_type=F32) + br_ref[...])
    lt = logits.T
    tm = lt.shape[1]
    row = lax.broadcasted_iota(I32, (EXP_PER_GROUP, tm), 0)
    neg = jnp.float32(-jnp.inf)

    coarse = jnp.where(row < N_GROUPS, lt[0:EXP_PER_GROUP], neg)
    cmax = jnp.max(coarse, axis=0, keepdims=True)
    gi = jnp.min(jnp.where(coarse == cmax, row, EXP_PER_GROUP), axis=0, keepdims=True)
    pg = 1.0 / jnp.sum(jnp.exp(coarse - cmax), axis=0, keepdims=True)

    fine = lt[FINE_ROW0:FINE_ROW0 + EXP_PER_GROUP]
    for grp in range(1, N_GROUPS):
        lo_r = FINE_ROW0 + grp * EXP_PER_GROUP
        fine = jnp.where(gi == grp, lt[lo_r:lo_r + EXP_PER_GROUP], fine)
    m1 = jnp.max(fine, axis=0, keepdims=True)
    i1 = jnp.min(jnp.where(fine == m1, row, EXP_PER_GROUP), axis=0, keepdims=True)
    rest = jnp.where(row == i1, neg, fine)
    m2 = jnp.max(rest, axis=0, keepdims=True)
    i2 = jnp.min(jnp.where(rest == m2, row, EXP_PER_GROUP), axis=0, keepdims=True)
    e21 = jnp.exp(m2 - m1)
    w1 = 1.0 / (1.0 + e21)
    gate_ref[...] = jnp.concatenate([pg * w1, pg * (e21 * w1)], axis=0)
    e0 = gi * EXP_PER_GROUP + i1
    e1 = gi * EXP_PER_GROUP + i2
    eid_ref[...] = jnp.concatenate([e0, e1], axis=0)

    @pl.when(pl.program_id(0) == 0)
    def _():
        cnt_ref[...] = jnp.zeros_like(cnt_ref)

    erow = lax.broadcasted_iota(I32, (N_EXPERTS, tm), 0)
    oh0 = jnp.where(erow == e0, 1.0, 0.0)
    oh1 = jnp.where(erow == e1, 1.0, 0.0)
    r_i = lax.broadcasted_iota(I32, (tm, tm), 0)
    c_i = lax.broadcasted_iota(I32, (tm, tm), 1)
    before = jnp.where(r_i < c_i, 1.0, 0.0).astype(BF16)
    pre0 = jnp.dot(oh0.astype(BF16), before, preferred_element_type=F32)
    pre1 = jnp.dot(oh1.astype(BF16), before, preferred_element_type=F32)
    tot0 = jnp.sum(oh0, axis=1, keepdims=True)
    tot1 = jnp.sum(oh1, axis=1, keepdims=True)
    base = cnt_ref[:, 0:1]
    r0 = jnp.sum(oh0 * (pre0 + base), axis=0, keepdims=True)
    r1 = jnp.sum(oh1 * (pre1 + (base + tot0)), axis=0, keepdims=True)
    rank_ref[...] = jnp.concatenate([r0, r1], axis=0).astype(I32)
    total = jnp.broadcast_to(base + tot0 + tot1, cnt_ref.shape)
    cnt_ref[...] = total
    cnt_out_ref[...] = total


def _tail_specs(t, d, tm):
    row = lambda w: pl.BlockSpec((tm, w), lambda i: (i, 0))
    lanes = pl.BlockSpec((TOP_K, tm), lambda i: (0, i))
    cnt = pl.BlockSpec((N_EXPERTS, ROUTER_LANES), lambda i: (0, 0))
    out_specs = [row(d), row(d // 2), lanes, lanes, lanes, cnt]
    out_shape = [jax.ShapeDtypeStruct((t, d), F32), jax.ShapeDtypeStruct((t, d // 2), U32),
                 jax.ShapeDtypeStruct((TOP_K, t), F32), jax.ShapeDtypeStruct((TOP_K, t), I32),
                 jax.ShapeDtypeStruct((TOP_K, t), I32),
                 jax.ShapeDtypeStruct((N_EXPERTS, ROUTER_LANES), F32)]
    scratch = [pltpu.VMEM((N_EXPERTS, ROUTER_LANES), F32)]
    return out_specs, out_shape, scratch


def _even_out_kernel(ya_ref, yb_ref, h_ref, w_ref, g_ref, wr_ref, br_ref, *tail_refs):
    a_width = ya_ref.shape[1]
    mix = (jnp.dot(ya_ref[...], w_ref[0:a_width, :], preferred_element_type=F32) +
           jnp.dot(yb_ref[...], w_ref[a_width:, :], preferred_element_type=F32))
    _router_tail(h_ref[...] + mix, g_ref, wr_ref, br_ref, *tail_refs)


def _even_out(ya, yb, h, w_out, g_ffn, wr, br, *, tm):
    t, d = h.shape
    out_specs, out_shape, scratch = _tail_specs(t, d, tm)
    row = lambda w: pl.BlockSpec((tm, w), lambda i: (i, 0))
    return pl.pallas_call(
        _even_out_kernel,
        grid=(t // tm,),
        in_specs=[row(ya.shape[1]), row(yb.shape[1]), row(d), _const_spec(w_out.shape),
                  _const_spec((1, d)), _const_spec(wr.shape), _const_spec(br.shape)],
        out_specs=out_specs,
        out_shape=out_shape,
        scratch_shapes=scratch,
        compiler_params=_params("arbitrary"),
        name="even_out_router",
    )(ya, yb, h, w_out, g_ffn, wr, br)


def _gelu_tanh(x):
    return 0.5 * x * (1.0 + jnp.tanh(0.7978845608028654 * (x + 0.044715 * (x * x * x))))


def _odd_kernel(*refs, n_prev):
    h_refs, refs = refs[:max(n_prev, 1)], refs[max(n_prev, 1):]
    (g_ref, w_in_ref, gv_ref, ws_ref, bs_ref, w_out_ref, gf_ref, wr_ref, br_ref,
     *tail_refs, y_ref) = refs
    h = _post_value(*h_refs) if n_prev else h_refs[0][...]
    tm = h.shape[0]
    c_width = gv_ref.shape[1]
    n_grp = ws_ref.shape[0]
    gdim = c_width // n_grp
    xn = _rms(h, g_ref[...]).astype(BF16)
    u = _gelu_tanh(jnp.dot(xn, w_in_ref[:, 0:c_width], preferred_element_type=F32))
    v = _gelu_tanh(jnp.dot(xn, w_in_ref[:, c_width:], preferred_element_type=F32))
    v = _rms(v, gv_ref[...]).astype(BF16)
    r_i = lax.broadcasted_iota(I32, (CHUNK, CHUNK), 0)
    c_i = lax.broadcasted_iota(I32, (CHUNK, CHUNK), 1)
    causal = c_i <= r_i
    bs = bs_ref[...]
    for grp in range(n_grp):
        w_m = jnp.where(causal, ws_ref[grp], 0.0).astype(BF16)
        bias = bs[:, grp:grp + 1]
        cols = slice(grp * gdim, (grp + 1) * gdim)
        for c in range(tm // CHUNK):
            rows = slice(c * CHUNK, (c + 1) * CHUNK)
            gate = jnp.dot(w_m, v[rows, cols], preferred_element_type=F32) + bias
            y_ref[rows, cols] = (u[rows, cols] * gate).astype(BF16)
    mix = jnp.dot(y_ref[...], w_out_ref[...], preferred_element_type=F32)
    _router_tail(h + mix, gf_ref, wr_ref, br_ref, *tail_refs)


def _odd_mixer(h_or_prev, g_mix, w_in, g_v, w_s, b_s_t, w_out, g_ffn, wr, br, *, tm):
    consts = [g_mix, w_in, g_v, w_s, b_s_t, w_out, g_ffn, wr, br]
    if callable(h_or_prev):
        head_specs, head_args = h_or_prev(tm)
        t, d = head_args[0].shape
    else:
        t, d = h_or_prev.shape
        head_specs, head_args = [pl.BlockSpec((tm, d), lambda i: (i, 0))], [h_or_prev]
    out_specs, out_shape, scratch = _tail_specs(t, d, tm)
    n_prev = len(head_args) if callable(h_or_prev) else 0
    return pl.pallas_call(
        functools.partial(_odd_kernel, n_prev=n_prev),
        grid=(t // tm,),
        in_specs=head_specs + [_const_spec(c.shape) for c in consts],
        out_specs=out_specs,
        out_shape=out_shape,
        scratch_shapes=scratch + [pltpu.VMEM((tm, g_v.shape[1]), BF16)],
        compiler_params=_params("arbitrary"),
        name="odd_mixer_router",
    )(*head_args, *consts)


def _sc_mesh():
    return plsc.VectorSubcoreMesh(core_axis_name="c", subcore_axis_name="s")


def _sc_worker_base(rows_per_worker):
    wid = lax.axis_index("s") * V7X_SC_CORES + lax.axis_index("c")
    return wid * rows_per_worker


def _dispatch(xp, dest, n_slots):
    t, w = xp.shape
    per_worker = t // SC_WORKERS
    n_chunks = per_worker // SC_ROWS
    assert per_worker * SC_WORKERS == t and n_chunks * SC_ROWS == per_worker and n_chunks % 2 == 0

    idx_t = pltpu.VMEM((SC_ROWS,), I32)

    @functools.partial(
        pl.kernel, mesh=_sc_mesh(),
        out_type=jax.ShapeDtypeStruct((n_slots, w), xp.dtype),
        scratch_types=[pltpu.VMEM((2, SC_ROWS, w), xp.dtype), idx_t, idx_t, idx_t, idx_t,
                       pltpu.SemaphoreType.DMA((2,)), pltpu.SemaphoreType.DMA((2,))],
    )
    def kern(x_hbm, d_hbm, out_hbm, rows_v, i00, i01, i10, i11, rsem, wsem):
        base = _sc_worker_base(per_worker)
        idx = ((i00, i01), (i10, i11))

        def load(c, b):
            off = pl.multiple_of(base + c * SC_ROWS, 8)
            pltpu.async_copy(x_hbm.at[pl.ds(off, SC_ROWS)], rows_v.at[b], rsem.at[b])
            for k in range(TOP_K):
                pltpu.sync_copy(d_hbm.at[k, pl.ds(off, SC_ROWS)], idx[b][k])

        def wait_load(b):
            pltpu.make_async_copy(x_hbm.at[pl.ds(0, SC_ROWS)], rows_v.at[b], rsem.at[b]).wait()

        def scatter(b):
            for k in range(TOP_K):
                pltpu.async_copy(rows_v.at[b], out_hbm.at[idx[b][k]], wsem.at[b])

        def wait_scatter(b):
            for k in range(TOP_K):
                pltpu.make_async_copy(rows_v.at[b], out_hbm.at[idx[b][k]], wsem.at[b]).wait()

        load(0, 0)

        @pl.loop(0, n_chunks, step=2)
        def _(c0):
            for b in range(2):
                c = c0 + b
                wait_load(b)
                scatter(b)

                @pl.when(c + 1 < n_chunks)
                def _():
                    @pl.when(c >= 1)
                    def _():
                        wait_scatter(1 - b)
                    load(c + 1, 1 - b)

        wait_scatter(0)
        wait_scatter(1)

    return kern(xp, dest)


def _combine(y, idx):
    n = idx.shape[0]
    w = y.shape[1]
    rows = 2 * SC_ROWS
    per_worker = n // SC_WORKERS
    n_chunks = per_worker // rows
    assert per_worker * SC_WORKERS == n and n_chunks * rows == per_worker and n_chunks % 2 == 0

    idx_t = pltpu.VMEM((rows,), I32)

    @functools.partial(
        pl.kernel, mesh=_sc_mesh(),
        out_type=jax.ShapeDtypeStruct((n, w), y.dtype),
        scratch_types=[pltpu.VMEM((2, rows, w), y.dtype), idx_t, idx_t,
                       pltpu.SemaphoreType.DMA((2,)), pltpu.SemaphoreType.DMA((2,))],
    )
    def kern(y_hbm, i_hbm, out_hbm, rows_v, i0, i1, gsem, wsem):
        base = _sc_worker_base(per_worker)
        ibuf = (i0, i1)

        def gather(c, b):
            off = pl.multiple_of(base + c * rows, 8)
            pltpu.sync_copy(i_hbm.at[pl.ds(off, rows)], ibuf[b])
            pltpu.async_copy(y_hbm.at[ibuf[b]], rows_v.at[b], gsem.at[b])

        def wait_gather(b):
            pltpu.make_async_copy(y_hbm.at[ibuf[b]], rows_v.at[b], gsem.at[b]).wait()

        def write(c, b):
            off = pl.multiple_of(base + c * rows, 8)
            pltpu.async_copy(rows_v.at[b], out_hbm.at[pl.ds(off, rows)], wsem.at[b])

        def wait_write(b):
            pltpu.make_async_copy(rows_v.at[b], out_hbm.at[pl.ds(0, rows)], wsem.at[b]).wait()

        gather(0, 0)

        @pl.loop(0, n_chunks, step=2)
        def _(c0):
            for b in range(2):
                c = c0 + b
                wait_gather(b)
                write(c, b)

                @pl.when(c + 1 < n_chunks)
                def _():
                    @pl.when(c >= 1)
                    def _():
                        wait_write(1 - b)
                    gather(c + 1, 1 - b)

        wait_write(0)
        wait_write(1)

    return kern(y, idx)


def _expert_kernel(te_ref, tv_ref, nu_ref, tg_ref, tn_ref, x_hbm, w1_hbm, w3_hbm, w2_hbm, y_ref,
                   xbuf_ref, xsem, w1f_ref, w3f_ref, w2f_ref, wsem, w1b_ref, w3b_ref, w2b_ref,
                   *, layer):
    i = pl.program_id(0)
    n_used = nu_ref[0]
    ts = xbuf_ref.shape[1]

    def x_copy(tile):
        slot = tile % X_RING
        rows = pl.ds(pl.multiple_of(tile * ts, ts), ts)
        return pltpu.make_async_copy(x_hbm.at[rows], xbuf_ref.at[slot], xsem.at[slot])

    def w_copies(expert, slot):
        pairs = ((w1_hbm, w1f_ref), (w3_hbm, w3f_ref), (w2_hbm, w2f_ref))
        return [pltpu.make_async_copy(src.at[layer, expert], dst.at[slot], wsem.at[slot, j])
                for j, (src, dst) in enumerate(pairs)]

    @pl.when(jnp.logical_and(i == 0, n_used > 0))
    def _():
        for tile in range(X_RING - 1):
            @pl.when(tile < n_used)
            def _():
                x_copy(tile).start()
        for c in w_copies(te_ref[0], 0):
            c.start()

    @pl.when(i < n_used)
    def _():
        @pl.when(i + (X_RING - 1) < n_used)
        def _():
            x_copy(i + (X_RING - 1)).start()

        prev = te_ref[jnp.maximum(i - 1, 0)]

        @pl.when(jnp.logical_or(i == 0, te_ref[i] != prev))
        def _():
            slot = tg_ref[i] % 2
            for c in w_copies(te_ref[i], slot):
                c.wait()
            nxt = tn_ref[i]

            @pl.when(nxt >= 0)
            def _():
                for c in w_copies(nxt, 1 - slot):
                    c.start()

            w1b_ref[...] = w1f_ref[slot].astype(BF16)
            w3b_ref[...] = w3f_ref[slot].astype(BF16)
            w2b_ref[...] = w2f_ref[slot].astype(BF16)

        x_copy(i).wait()
        xu = xbuf_ref[i % X_RING]
        rowid = lax.broadcasted_iota(I32, xu.shape, 0)
        xu = jnp.where(rowid < tv_ref[i], xu, jnp.zeros_like(xu))
        x = _unpack_bf16_pairs(xu).astype(BF16)
        h1 = jnp.dot(x, w1b_ref[...], preferred_element_type=F32)
        h3 = jnp.dot(x, w3b_ref[...], preferred_element_type=F32)
        he = (h1 * jax.nn.sigmoid(h1)) * h3
        y = jnp.dot(he.astype(BF16), w2b_ref[...], preferred_element_type=F32)
        y_ref[...] = _pack_bf16_pairs(y)


def _experts(x_sorted, plan, w1, w3, w2, layer):
    n_slots, wp = x_sorted.shape
    _, n_e, d, d_e = w1.shape
    ts = SLOT_TILE
    any_spec = pl.BlockSpec(memory_space=pl.ANY)
    yspec = pl.BlockSpec((ts, wp),
                         lambda i, te, tv, nu, tg, tn: (jnp.minimum(i, jnp.maximum(nu[0] - 1, 0)), 0))
    return pl.pallas_call(
        functools.partial(_expert_kernel, layer=layer),
        grid_spec=pltpu.PrefetchScalarGridSpec(
            num_scalar_prefetch=len(plan),
            grid=(n_slots // ts,),
            in_specs=[any_spec] * 4,
            out_specs=yspec,
            scratch_shapes=[pltpu.VMEM((X_RING, ts, wp), x_sorted.dtype),
                            pltpu.SemaphoreType.DMA((X_RING,)),
                            pltpu.VMEM((2, d, d_e), F32), pltpu.VMEM((2, d, d_e), F32),
                            pltpu.VMEM((2, d_e, d), F32), pltpu.SemaphoreType.DMA((2, 3)),
                            pltpu.VMEM((d, d_e), BF16), pltpu.VMEM((d, d_e), BF16),
                            pltpu.VMEM((d_e, d), BF16)],
        ),
        out_shape=jax.ShapeDtypeStruct((n_slots, d // 2), U32),
        compiler_params=_params("arbitrary"),
        name="moe_experts",
    )(*plan, x_sorted, w1, w3, w2)


def _slot_plan(eids, ranks, counts, n_tiles):
    ts = SLOT_TILE
    cnt = counts[:, 0].astype(I32)
    padded = (cnt + ts - 1) // ts * ts
    e_ids = jnp.arange(N_EXPERTS, dtype=I32)
    ends = jnp.sum(jnp.where(e_ids[:, None] <= e_ids[None, :], padded[:, None], 0), axis=0)
    starts = ends - padded
    onehot = eids[None] == e_ids[:, None, None]
    dest = ranks + jnp.sum(jnp.where(onehot, starts[:, None, None], 0), axis=0)
    tile_lo = jnp.arange(n_tiles, dtype=I32) * ts
    tile_e = jnp.minimum(jnp.sum(tile_lo[:, None] >= ends[None, :], axis=1), N_EXPERTS - 1)
    tile_e = tile_e.astype(I32)
    valid_end = jnp.sum(jnp.where(tile_e[:, None] == e_ids[None, :], (starts + cnt)[None, :], 0),
                        axis=1)
    tile_valid = jnp.clip(valid_end - tile_lo, 0, ts).astype(I32)
    n_used = (ends[-1:] // ts).astype(I32)
    nonempty = cnt > 0
    earlier = jnp.logical_and(nonempty[None, :], e_ids[None, :] < tile_e[:, None])
    tile_grp = jnp.sum(earlier, axis=1).astype(I32)
    later = jnp.logical_and(nonempty[None, :], e_ids[None, :] > tile_e[:, None])
    nxt = jnp.min(jnp.where(later, e_ids[None, :], N_EXPERTS), axis=1)
    tile_next = jnp.where(nxt < N_EXPERTS, nxt, -1).astype(I32)
    return dest, (tile_e, tile_valid, n_used, tile_grp, tile_next)


def _post_value(h1_ref, y0_ref, y1_ref, gate_ref, p_ref, g_ref, wpg_ref, wpe_ref):
    gates = gate_ref[...]
    tm = gates.shape[1]
    gt = jnp.concatenate([gates, jnp.zeros((8 - TOP_K, tm), F32)], axis=0).T
    h2 = h1_ref[...] + (gt[:, 0:1] * _unpack_bf16_pairs(y0_ref[...]) +
                        gt[:, 1:2] * _unpack_bf16_pairs(y1_ref[...]))
    gate = jax.nn.sigmoid(jnp.dot(_rms(h2, g_ref[...]).astype(BF16), wpg_ref[...],
                                  preferred_element_type=F32))
    pe = jnp.dot(p_ref[...].astype(BF16), wpe_ref[...], preferred_element_type=F32)
    return h2 + gate * pe


def _post_operands(h1, yk, gates, p, g_pl, w_pg, w_pe, *, p_row0):
    t, d = h1.shape
    wy = yk.shape[1]

    def operands(tm):
        nt = t // tm
        p_blk0 = p_row0 // tm
        row = lambda w: pl.BlockSpec((tm, w), lambda i: (i, 0))
        specs = [row(d), row(wy), pl.BlockSpec((tm, wy), lambda i: (i + nt, 0)),
                 pl.BlockSpec((TOP_K, tm), lambda i: (0, i)),
                 pl.BlockSpec((tm, p.shape[1]), lambda i: (i + p_blk0, 0)),
                 _const_spec((1, d)), _const_spec(w_pg.shape), _const_spec(w_pe.shape)]
        return specs, [h1, yk, yk, gates, p, g_pl, w_pg, w_pe]

    return operands


def _post_kernel(*refs):
    *post_refs, o_ref = refs
    o_ref[...] = _post_value(*post_refs)


def _post(pending, *, tm):
    specs, args = pending(tm)
    t, d = args[0].shape
    return pl.pallas_call(
        _post_kernel,
        grid=(t // tm,),
        in_specs=specs,
        out_specs=pl.BlockSpec((tm, d), lambda i: (i, 0)),
        out_shape=jax.ShapeDtypeStruct((t, d), F32),
        compiler_params=_params("arbitrary"),
        name="layer_post",
    )(*args)


def _final_kernel(*refs):
    *post_refs, gfin_ref, o_ref = refs
    o_ref[...] = _rms(_post_value(*post_refs), gfin_ref[...])


def _final(pending, g_fin, *, tm):
    specs, args = pending(tm)
    t, d = args[0].shape
    return pl.pallas_call(
        _final_kernel,
        grid=(t // tm,),
        in_specs=specs + [_const_spec((1, d))],
        out_specs=pl.BlockSpec((tm, d), lambda i: (i, 0)),
        out_shape=jax.ShapeDtypeStruct((t, d), F32),
        compiler_params=_params("arbitrary"),
        name="final_post",
    )(*args, g_fin)


def _router_weights(wc, bc, wf, bf):
    d = wc.shape[0]
    w = jnp.zeros((d, ROUTER_LANES), F32)
    w = w.at[:, 0:N_GROUPS].set(wc).at[:, FINE_ROW0:FINE_ROW0 + N_EXPERTS].set(wf)
    b = jnp.zeros((1, ROUTER_LANES), F32)
    b = b.at[0, 0:N_GROUPS].set(bc).at[0, FINE_ROW0:FINE_ROW0 + N_EXPERTS].set(bf)
    hi = w.astype(BF16)
    lo = (w - hi.astype(F32)).astype(BF16)
    return jnp.concatenate([hi, lo], axis=1), b


def _moe(xp, eids, ranks, counts, w1, w3, w2, layer):
    t = xp.shape[0]
    n_tiles = TOP_K * t // SLOT_TILE + N_EXPERTS
    dest, plan = _slot_plan(eids, ranks, counts, n_tiles)
    x_sorted = _dispatch(xp, dest, n_tiles * SLOT_TILE)
    y_sorted = _experts(x_sorted, plan, w1, w3, w2, layer)
    return _combine(y_sorted, dest.reshape(-1))


def kernel(x, p, norm_mix, norm_ffn, norm_pl, final_norm, w_in_even, conv_w_even, w_out_even, w_in_odd, g_v_odd, w_s_odd, b_s_odd, w_out_odd, router_c, router_c_b, router_f, router_f_b, moe_w1, moe_w3, moe_w2, w_pe, w_pg):
    batch, seq, d = x.shape
    depth = p.shape[0]
    t = batch * seq
    tm = min(TOKEN_TILE, seq)
    tm_big = min(LIGHT_TOKEN_TILE, seq)
    p_rows = p.reshape(depth * t, -1)
    row = lambda a: a.reshape(1, -1)
    h = x.reshape(t, d)
    for i in range(depth):
        j = i // 2
        wr, br = _router_weights(router_c[i], router_c_b[i], router_f[i], router_f_b[i])
        g_ffn = row(norm_ffn[i])
        if i % 2 == 0:
            if callable(h):
                h = _post(h, tm=tm_big)
            ya, q, k, v = _even_in(h, row(norm_mix[i]), w_in_even[j].astype(BF16), conv_w_even[j],
                                   seq=seq, tm=tm_big)
            yb = _attention(q, k, v, batch=batch, seq=seq, tq=min(Q_TILE, seq))
            h1, xp, gates, eids, ranks, counts = _even_out(
                ya, yb, h, w_out_even[j].astype(BF16), g_ffn, wr, br, tm=tm_big)
        else:
            h1, xp, gates, eids, ranks, counts = _odd_mixer(
                h, row(norm_mix[i]), w_in_odd[j].astype(BF16), row(g_v_odd[j]), w_s_odd[j],
                b_s_odd[j].T, w_out_odd[j].astype(BF16), g_ffn, wr, br, tm=tm)
        yk = _moe(xp, eids, ranks, counts, moe_w1, moe_w3, moe_w2, i)
        h = _post_operands(h1, yk, gates, p_rows, row(norm_pl[i]), w_pg[i].astype(BF16),
                           w_pe[i].astype(BF16), p_row0=i * t)
    return _final(h, row(final_norm), tm=tm_big).reshape(batch, seq, d)
```

```python
import functools
from typing import Callable, NamedTuple

import jax
import jax.numpy as jnp
from jax import lax
from jax.experimental import pallas as pl
from jax.experimental.pallas import tpu as pltpu
from jax.experimental.pallas import tpu_sc as plsc

EPS = 1e-6
HEAD_DIM = 64
HEADS_PER_SLAB = 2
ATTN_SLABS = 4
CHUNK = 128
N_GROUPS = 4
EXP_PER_GROUP = 8
N_EXPERTS = N_GROUPS * EXP_PER_GROUP
TOP_K = 2
ROUTER_LANES = 128
FINE_ROW0 = 8

V7X_VMEM_BYTES = 64 * 1024 * 1024
VMEM_LIMIT = V7X_VMEM_BYTES - 8 * 1024 * 1024
V7X_SC_CORES = 2
V7X_SC_SUBCORES = 16
SC_WORKERS = V7X_SC_CORES * V7X_SC_SUBCORES
SC_ROWS = 32

TOKEN_TILE = 512
LIGHT_TOKEN_TILE = 1024
Q_TILE = 128
SLOT_TILE = 512
X_RING = 3
FINAL_PARTS = 2

LOG_F32_TINY_BOUND = -88.0
OFF_PENALTY = 1e30

F32 = jnp.float32
BF16 = jnp.bfloat16
U32 = jnp.uint32
I32 = jnp.int32


def _rms(x, g):
    return x * lax.rsqrt(jnp.mean(x * x, axis=-1, keepdims=True) + EPS) * g


def _const_spec(shape):
    nd = len(shape)
    return pl.BlockSpec(shape, lambda *_: (0,) * nd, pipeline_mode=pl.Buffered(1))


def _params(*sem):
    return pltpu.CompilerParams(dimension_semantics=sem, vmem_limit_bytes=VMEM_LIMIT)


def _even_in_kernel(h_ref, g_ref, w_ref, cw_ref, ya_ref, q_ref, k_ref, v_ref, zs_ref,
                    *, tiles_per_seq, a_width, sb_width):
    tm = h_ref.shape[0]
    xn = _rms(h_ref[...], g_ref[...]).astype(BF16)
    a3 = 3 * a_width

    @pl.when(pl.program_id(0) % tiles_per_seq == 0)
    def _():
        zs_ref[0:8, :] = jnp.zeros((8, a_width), F32)

    pa = jnp.dot(xn, w_ref[:, 0:a3], preferred_element_type=F32)
    z = pa[:, 2 * a_width:a3] * pa[:, 0:a_width]
    zs_ref[8:tm + 8, :] = z
    z1 = zs_ref[7:tm + 7, :]
    z2 = zs_ref[6:tm + 6, :]
    cw = cw_ref[...]
    conv = cw[0:1, :] * z2 + cw[1:2, :] * z1 + cw[2:3, :] * z
    ya_ref[...] = (pa[:, a_width:2 * a_width] * conv).astype(BF16)
    zs_ref[0:8, :] = zs_ref[tm:tm + 8, :]

    pq = jnp.dot(xn, w_ref[:, a3:a3 + 3 * sb_width], preferred_element_type=F32)
    q_ref[...] = (pq[:, 0:sb_width] * (HEAD_DIM ** -0.5)).astype(BF16)
    k_ref[...] = pq[:, sb_width:2 * sb_width].astype(BF16)
    v_ref[...] = pq[:, 2 * sb_width:3 * sb_width].astype(BF16)


def _even_in(h, g, w_in, conv_w, *, seq, tm):
    t, d = h.shape
    a_width = conv_w.shape[1]
    sb_width = (w_in.shape[1] - 3 * a_width) // 3
    kern = functools.partial(_even_in_kernel, tiles_per_seq=seq // tm, a_width=a_width,
                             sb_width=sb_width)
    row = lambda w: pl.BlockSpec((tm, w), lambda i: (i, 0))
    return pl.pallas_call(
        kern,
        grid=(t // tm,),
        in_specs=[row(d), _const_spec((1, d)), _const_spec(w_in.shape), _const_spec(conv_w.shape)],
        out_specs=[row(a_width), row(sb_width), row(sb_width), row(sb_width)],
        out_shape=[jax.ShapeDtypeStruct((t, a_width), BF16)] +
                  [jax.ShapeDtypeStruct((t, sb_width), BF16)] * 3,
        scratch_shapes=[pltpu.VMEM((tm + 8, a_width), F32)],
        compiler_params=_params("arbitrary"),
        name="even_in",
    )(h, g, w_in, conv_w)


def _attn_kernel(q_ref, k_ref, v_ref, o_ref, acc_ref, c_ref, *, tq):
    qi = pl.program_id(2)
    slab = HEADS_PER_SLAB * HEAD_DIM
    rows_slab = HEADS_PER_SLAB * tq
    rows = ATTN_SLABS * rows_slab
    lane = lax.broadcasted_iota(I32, (tq, slab), 1)
    qs = []
    for sl in range(ATTN_SLABS):
        q = q_ref[:, sl * slab:(sl + 1) * slab]
        zero = jnp.zeros_like(q)
        qs.append(jnp.concatenate([jnp.where(lane < HEAD_DIM, q, zero),
                                   jnp.where(lane >= HEAD_DIM, q, zero)], axis=0))
    r_i = lax.broadcasted_iota(I32, (tq, tq), 0)
    c_i = lax.broadcasted_iota(I32, (tq, tq), 1)
    upper = jnp.where(r_i > c_i, 1.0, 0.0).astype(BF16)
    t_loc = lax.broadcasted_iota(I32, (rows, tq), 0) & (tq - 1)
    s_loc = lax.broadcasted_iota(I32, (rows, tq), 1)
    causal = s_loc < t_loc

    def block(kb, c, mask):
        start = pl.multiple_of(kb * tq, tq)
        kk = k_ref[pl.ds(start, tq), :]
        vv = v_ref[pl.ds(start, tq), :]
        z = jnp.concatenate(
            [lax.dot_general(qs[sl], kk[:, sl * slab:(sl + 1) * slab], (((1,), (1,)), ((), ())),
                             preferred_element_type=F32) for sl in range(ATTN_SLABS)], axis=0)
        sp = jnp.maximum(z, 0.0) + jnp.log(1.0 + jnp.exp(-jnp.abs(z)))
        spm = sp if mask is None else jnp.where(mask, sp, 0.0)
        hi = spm.astype(BF16)
        lo = (spm - hi.astype(F32)).astype(BF16)
        later = (jnp.dot(hi, upper, preferred_element_type=F32) +
                 jnp.dot(lo, upper, preferred_element_type=F32))
        a = jnp.exp(z - sp - later - c)
        if mask is not None:
            a = jnp.where(mask, a, 0.0)
        a = a.astype(BF16)
        out = jnp.concatenate(
            [jnp.dot(a[sl * rows_slab:(sl + 1) * rows_slab], vv[:, sl * slab:(sl + 1) * slab],
                     preferred_element_type=F32) for sl in range(ATTN_SLABS)], axis=0)
        return out, jnp.sum(spm, axis=1, keepdims=True)

    out0, sum0 = block(qi, jnp.zeros((rows, 1), F32), causal)
    has_prev = qi >= 1
    out1, sum1 = block(jnp.maximum(qi - 1, 0), sum0 + jnp.where(has_prev, 0.0, OFF_PENALTY), None)
    c2 = sum0 + jnp.where(has_prev, sum1, 0.0)
    acc_ref[...] = out0 + out1
    c_ref[...] = c2

    def cond(carry):
        kb, cmin = carry
        return jnp.logical_and(kb >= 0, cmin < -LOG_F32_TINY_BOUND)

    def body(carry):
        kb, _ = carry
        c = c_ref[...]
        out, ssum = block(kb, c, None)
        acc_ref[...] += out
        c_new = c + ssum
        c_ref[...] = c_new
        return kb - 1, jnp.min(c_new)

    lax.while_loop(cond, body, (qi - 2, jnp.min(c2)))
    acc = acc_ref[...]
    for sl in range(ATTN_SLABS):
        lo_r = sl * rows_slab
        o_ref[:, sl * slab:(sl + 1) * slab] = jnp.where(
            lane < HEAD_DIM, acc[lo_r:lo_r + tq], acc[lo_r + tq:lo_r + 2 * tq]).astype(o_ref.dtype)


def _attention(q, k, v, *, batch, seq, tq):
    t, w = q.shape
    wblk = ATTN_SLABS * HEADS_PER_SLAB * HEAD_DIM
    nq = seq // tq
    qspec = pl.BlockSpec((tq, wblk), lambda b, s, i: (b * nq + i, s))
    kvspec = pl.BlockSpec((seq, wblk), lambda b, s, i: (b, s))
    rows = ATTN_SLABS * HEADS_PER_SLAB * tq
    return pl.pallas_call(
        functools.partial(_attn_kernel, tq=tq),
        grid=(batch, w // wblk, nq),
        in_specs=[qspec, kvspec, kvspec],
        out_specs=qspec,
        out_shape=jax.ShapeDtypeStruct((t, w), BF16),
        scratch_shapes=[pltpu.VMEM((rows, HEADS_PER_SLAB * HEAD_DIM), F32),
                        pltpu.VMEM((rows, 1), F32)],
        compiler_params=_params("arbitrary", "arbitrary", "arbitrary"),
        name="stick_breaking_attention",
    )(q, k, v)


def _pack_bf16_pairs(x):
    n = x.shape[1] // 2
    bits = lax.bitcast_convert_type(x.astype(BF16).astype(F32), U32)
    return bits[:, 0:n] | (bits[:, n:] >> 16)


def _unpack_bf16_pairs(u):
    a = lax.bitcast_convert_type(u & jnp.uint32(0xFFFF0000), F32)
    b = lax.bitcast_convert_type(u << 16, F32)
    return jnp.concatenate([a, b], axis=1)


def _router_tail(h1, g_ref, wr_ref, br_ref,
                 h1_ref, xp_ref, gate_ref, eid_ref, rank_ref, cnt_out_ref, cnt_ref):
    h1_ref[...] = h1
    xn = _rms(h1, g_ref[...])
    xp_ref[...] = _pack_bf16_pairs(xn)
    x_hi = xn.astype(BF16)
    x_lo = (xn - x_hi.astype(F32)).astype(BF16)
    both = jnp.dot(x_hi, wr_ref[...], preferred_element_type=F32)
    logits = (both[:, 0:ROUTER_LANES] + both[:, ROUTER_LANES:] +
              jnp.dot(x_lo, wr_ref[:, 0:ROUTER_LANES], preferred_element_type=F32) + br_ref[...])
    lt = logits.T
    tm = lt.shape[1]
    row = lax.broadcasted_iota(I32, (EXP_PER_GROUP, tm), 0)
    neg = jnp.float32(-jnp.inf)

    coarse = jnp.where(row < N_GROUPS, lt[0:EXP_PER_GROUP], neg)
    cmax = jnp.max(coarse, axis=0, keepdims=True)
    gi = jnp.min(jnp.where(coarse == cmax, row, EXP_PER_GROUP), axis=0, keepdims=True)
    pg = 1.0 / jnp.sum(jnp.exp(coarse - cmax), axis=0, keepdims=True)

    fine = lt[FINE_ROW0:FINE_ROW0 + EXP_PER_GROUP]
    for grp in range(1, N_GROUPS):
        lo_r = FINE_ROW0 + grp * EXP_PER_GROUP
        fine = jnp.where(gi == grp, lt[lo_r:lo_r + EXP_PER_GROUP], fine)
    m1 = jnp.max(fine, axis=0, keepdims=True)
    i1 = jnp.min(jnp.where(fine == m1, row, EXP_PER_GROUP), axis=0, keepdims=True)
    rest = jnp.where(row == i1, neg, fine)
    m2 = jnp.max(rest, axis=0, keepdims=True)
    i2 = jnp.min(jnp.where(rest == m2, row, EXP_PER_GROUP), axis=0, keepdims=True)
    e21 = jnp.exp(m2 - m1)
    w1 = 1.0 / (1.0 + e21)
    gate_ref[...] = jnp.concatenate([pg * w1, pg * (e21 * w1)], axis=0)
    e0 = gi * EXP_PER_GROUP + i1
    e1 = gi * EXP_PER_GROUP + i2
    eid_ref[...] = jnp.concatenate([e0, e1], axis=0)

    @pl.when(pl.program_id(0) == 0)
    def _():
        cnt_ref[...] = jnp.zeros_like(cnt_ref)

    erow = lax.broadcasted_iota(I32, (N_EXPERTS, tm), 0)
    oh0 = jnp.where(erow == e0, 1.0, 0.0)
    oh1 = jnp.where(erow == e1, 1.0, 0.0)
    r_i = lax.broadcasted_iota(I32, (tm, tm), 0)
    c_i = lax.broadcasted_iota(I32, (tm, tm), 1)
    before = jnp.where(r_i < c_i, 1.0, 0.0).astype(BF16)
    pre0 = jnp.dot(oh0.astype(BF16), before, preferred_element_type=F32)
    pre1 = jnp.dot(oh1.astype(BF16), before, preferred_element_type=F32)
    tot0 = jnp.sum(oh0, axis=1, keepdims=True)
    tot1 = jnp.sum(oh1, axis=1, keepdims=True)
    base = cnt_ref[:, 0:1]
    r0 = jnp.sum(oh0 * (pre0 + base), axis=0, keepdims=True)
    r1 = jnp.sum(oh1 * (pre1 + (base + tot0)), axis=0, keepdims=True)
    rank_ref[...] = jnp.concatenate([r0, r1], axis=0).astype(I32)
    total = jnp.broadcast_to(base + tot0 + tot1, cnt_ref.shape)
    cnt_ref[...] = total
    cnt_out_ref[...] = total


def _tail_specs(t, d, tm):
    row = lambda w: pl.BlockSpec((tm, w), lambda i: (i, 0))
    lanes = pl.BlockSpec((TOP_K, tm), lambda i: (0, i))
    cnt = pl.BlockSpec((N_EXPERTS, ROUTER_LANES), lambda i: (0, 0))
    out_specs = [row(d), row(d // 2), lanes, lanes, lanes, cnt]
    out_shape = [jax.ShapeDtypeStruct((t, d), F32), jax.ShapeDtypeStruct((t, d // 2), U32),
                 jax.ShapeDtypeStruct((TOP_K, t), F32), jax.ShapeDtypeStruct((TOP_K, t), I32),
                 jax.ShapeDtypeStruct((TOP_K, t), I32),
                 jax.ShapeDtypeStruct((N_EXPERTS, ROUTER_LANES), F32)]
    scratch = [pltpu.VMEM((N_EXPERTS, ROUTER_LANES), F32)]
    return out_specs, out_shape, scratch


def _even_out_kernel(ya_ref, yb_ref, h_ref, w_ref, g_ref, wr_ref, br_ref, *tail_refs):
    a_width = ya_ref.shape[1]
    mix = (jnp.dot(ya_ref[...], w_ref[0:a_width, :], preferred_element_type=F32) +
           jnp.dot(yb_ref[...], w_ref[a_width:, :], preferred_element_type=F32))
    _router_tail(h_ref[...] + mix, g_ref, wr_ref, br_ref, *tail_refs)


def _even_out(ya, yb, h, w_out, g_ffn, wr, br, *, tm):
    t, d = h.shape
    out_specs, out_shape, scratch = _tail_specs(t, d, tm)
    row = lambda w: pl.BlockSpec((tm, w), lambda i: (i, 0))
    return pl.pallas_call(
        _even_out_kernel,
        grid=(t // tm,),
        in_specs=[row(ya.shape[1]), row(yb.shape[1]), row(d), _const_spec(w_out.shape),
                  _const_spec((1, d)), _const_spec(wr.shape), _const_spec(br.shape)],
        out_specs=out_specs,
        out_shape=out_shape,
        scratch_shapes=scratch,
        compiler_params=_params("arbitrary"),
        name="even_out_router",
    )(ya, yb, h, w_out, g_ffn, wr, br)


def _gelu_tanh(x):
    return 0.5 * x * (1.0 + jnp.tanh(0.7978845608028654 * (x + 0.044715 * (x * x * x))))


def _odd_kernel(*refs, n_prev):
    h_refs, refs = refs[:max(n_prev, 1)], refs[max(n_prev, 1):]
    (g_ref, w_in_ref, gv_ref, ws_ref, bs_ref, w_out_ref, gf_ref, wr_ref, br_ref,
     *tail_refs, y_ref) = refs
    h = _post_value(*h_refs) if n_prev else h_refs[0][...]
    tm = h.shape[0]
    c_width = gv_ref.shape[1]
    n_grp = ws_ref.shape[0]
    gdim = c_width // n_grp
    xn = _rms(h, g_ref[...]).astype(BF16)
    u = _gelu_tanh(jnp.dot(xn, w_in_ref[:, 0:c_width], preferred_element_type=F32))
    v = _gelu_tanh(jnp.dot(xn, w_in_ref[:, c_width:], preferred_element_type=F32))
    v = _rms(v, gv_ref[...]).astype(BF16)
    r_i = lax.broadcasted_iota(I32, (CHUNK, CHUNK), 0)
    c_i = lax.broadcasted_iota(I32, (CHUNK, CHUNK), 1)
    causal = c_i <= r_i
    bs = bs_ref[...]
    for grp in range(n_grp):
        w_m = jnp.where(causal, ws_ref[grp], 0.0).astype(BF16)
        bias = bs[:, grp:grp + 1]
        cols = slice(grp * gdim, (grp + 1) * gdim)
        for c in range(tm // CHUNK):
            rows = slice(c * CHUNK, (c + 1) * CHUNK)
            gate = jnp.dot(w_m, v[rows, cols], preferred_element_type=F32) + bias
            y_ref[rows, cols] = (u[rows, cols] * gate).astype(BF16)
    mix = jnp.dot(y_ref[...], w_out_ref[...], preferred_element_type=F32)
    _router_tail(h + mix, gf_ref, wr_ref, br_ref, *tail_refs)


def _odd_mixer(h_or_prev, g_mix, w_in, g_v, w_s, b_s_t, w_out, g_ffn, wr, br, *, tm):
    consts = [g_mix, w_in, g_v, w_s, b_s_t, w_out, g_ffn, wr, br]
    pending = isinstance(h_or_prev, PendingTail)
    if pending:
        head_specs, head_args = h_or_prev.operands(tm)
        t, d = h_or_prev.rows, h_or_prev.width
    else:
        t, d = h_or_prev.shape
        head_specs, head_args = [pl.BlockSpec((tm, d), lambda i: (i, 0))], [h_or_prev]
    out_specs, out_shape, scratch = _tail_specs(t, d, tm)
    n_prev = len(head_args) if pending else 0
    return pl.pallas_call(
        functools.partial(_odd_kernel, n_prev=n_prev),
        grid=(t // tm,),
        in_specs=head_specs + [_const_spec(c.shape) for c in consts],
        out_specs=out_specs,
        out_shape=out_shape,
        scratch_shapes=scratch + [pltpu.VMEM((tm, g_v.shape[1]), BF16)],
        compiler_params=_params("arbitrary"),
        name="odd_mixer_router",
    )(*head_args, *consts)


def _sc_mesh():
    return plsc.VectorSubcoreMesh(core_axis_name="c", subcore_axis_name="s")


def _sc_worker_base(rows_per_worker):
    wid = lax.axis_index("s") * V7X_SC_CORES + lax.axis_index("c")
    return wid * rows_per_worker


def _dispatch(xp, dest, n_slots):
    t, w = xp.shape
    per_worker = t // SC_WORKERS
    n_chunks = per_worker // SC_ROWS
    assert per_worker * SC_WORKERS == t and n_chunks * SC_ROWS == per_worker and n_chunks % 2 == 0

    idx_t = pltpu.VMEM((SC_ROWS,), I32)

    @functools.partial(
        pl.kernel, mesh=_sc_mesh(),
        out_type=jax.ShapeDtypeStruct((n_slots, w), xp.dtype),
        scratch_types=[pltpu.VMEM((2, SC_ROWS, w), xp.dtype), idx_t, idx_t, idx_t, idx_t,
                       pltpu.SemaphoreType.DMA((2,)), pltpu.SemaphoreType.DMA((2,))],
    )
    def kern(x_hbm, d_hbm, out_hbm, rows_v, i00, i01, i10, i11, rsem, wsem):
        base = _sc_worker_base(per_worker)
        idx = ((i00, i01), (i10, i11))

        def load(c, b):
            off = pl.multiple_of(base + c * SC_ROWS, 8)
            pltpu.async_copy(x_hbm.at[pl.ds(off, SC_ROWS)], rows_v.at[b], rsem.at[b])
            for k in range(TOP_K):
                pltpu.sync_copy(d_hbm.at[k, pl.ds(off, SC_ROWS)], idx[b][k])

        def wait_load(b):
            pltpu.make_async_copy(x_hbm.at[pl.ds(0, SC_ROWS)], rows_v.at[b], rsem.at[b]).wait()

        def scatter(b):
            for k in range(TOP_K):
                pltpu.async_copy(rows_v.at[b], out_hbm.at[idx[b][k]], wsem.at[b])

        def wait_scatter(b):
            for k in range(TOP_K):
                pltpu.make_async_copy(rows_v.at[b], out_hbm.at[idx[b][k]], wsem.at[b]).wait()

        load(0, 0)

        @pl.loop(0, n_chunks, step=2)
        def _(c0):
            for b in range(2):
                c = c0 + b
                wait_load(b)
                scatter(b)

                @pl.when(c + 1 < n_chunks)
                def _():
                    @pl.when(c >= 1)
                    def _():
                        wait_scatter(1 - b)
                    load(c + 1, 1 - b)

        wait_scatter(0)
        wait_scatter(1)

    return kern(xp, dest)


def _combine(y, idx):
    n = idx.shape[0]
    w = y.shape[1]
    rows = 2 * SC_ROWS
    per_worker = n // SC_WORKERS
    n_chunks = per_worker // rows
    assert per_worker * SC_WORKERS == n and n_chunks * rows == per_worker and n_chunks % 2 == 0

    idx_t = pltpu.VMEM((rows,), I32)

    @functools.partial(
        pl.kernel, mesh=_sc_mesh(),
        out_type=jax.ShapeDtypeStruct((n, w), y.dtype),
        scratch_types=[pltpu.VMEM((2, rows, w), y.dtype), idx_t, idx_t,
                       pltpu.SemaphoreType.DMA((2,)), pltpu.SemaphoreType.DMA((2,))],
    )
    def kern(y_hbm, i_hbm, out_hbm, rows_v, i0, i1, gsem, wsem):
        base = _sc_worker_base(per_worker)
        ibuf = (i0, i1)

        def gather(c, b):
            off = pl.multiple_of(base + c * rows, 8)
            pltpu.sync_copy(i_hbm.at[pl.ds(off, rows)], ibuf[b])
            pltpu.async_copy(y_hbm.at[ibuf[b]], rows_v.at[b], gsem.at[b])

        def wait_gather(b):
            pltpu.make_async_copy(y_hbm.at[ibuf[b]], rows_v.at[b], gsem.at[b]).wait()

        def write(c, b):
            off = pl.multiple_of(base + c * rows, 8)
            pltpu.async_copy(rows_v.at[b], out_hbm.at[pl.ds(off, rows)], wsem.at[b])

        def wait_write(b):
            pltpu.make_async_copy(rows_v.at[b], out_hbm.at[pl.ds(0, rows)], wsem.at[b]).wait()

        gather(0, 0)

        @pl.loop(0, n_chunks, step=2)
        def _(c0):
            for b in range(2):
                c = c0 + b

                @pl.when(c + 1 < n_chunks)
                def _():
                    @pl.when(c >= 1)
                    def _():
                        wait_write(1 - b)
                    gather(c + 1, 1 - b)

                wait_gather(b)
                write(c, b)

        wait_write(0)
        wait_write(1)

    return kern(y, idx)


def _expert_kernel(te_ref, tv_ref, nu_ref, tg_ref, tn_ref, x_hbm, w1_hbm, w3_hbm, w2_hbm, y_ref,
                   xbuf_ref, xsem, w1f_ref, w3f_ref, w2f_ref, wsem, w1b_ref, w3b_ref, w2b_ref,
                   *, layer):
    i = pl.program_id(0)
    n_used = nu_ref[0]
    ts = xbuf_ref.shape[1]

    def x_copy(tile):
        slot = tile % X_RING
        rows = pl.ds(pl.multiple_of(tile * ts, ts), ts)
        return pltpu.make_async_copy(x_hbm.at[rows], xbuf_ref.at[slot], xsem.at[slot])

    def w_copies(expert, slot):
        pairs = ((w1_hbm, w1f_ref), (w3_hbm, w3f_ref), (w2_hbm, w2f_ref))
        return [pltpu.make_async_copy(src.at[layer, expert], dst.at[slot], wsem.at[slot, j])
                for j, (src, dst) in enumerate(pairs)]

    @pl.when(jnp.logical_and(i == 0, n_used > 0))
    def _():
        for tile in range(X_RING - 1):
            @pl.when(tile < n_used)
            def _():
                x_copy(tile).start()
        for c in w_copies(te_ref[0], 0):
            c.start()

    @pl.when(i < n_used)
    def _():
        @pl.when(i + (X_RING - 1) < n_used)
        def _():
            x_copy(i + (X_RING - 1)).start()

        prev = te_ref[jnp.maximum(i - 1, 0)]

        @pl.when(jnp.logical_or(i == 0, te_ref[i] != prev))
        def _():
            slot = tg_ref[i] % 2
            for c in w_copies(te_ref[i], slot):
                c.wait()
            nxt = tn_ref[i]

            @pl.when(nxt >= 0)
            def _():
                for c in w_copies(nxt, 1 - slot):
                    c.start()

            w1b_ref[...] = w1f_ref[slot].astype(BF16)
            w3b_ref[...] = w3f_ref[slot].astype(BF16)
            w2b_ref[...] = w2f_ref[slot].astype(BF16)

        x_copy(i).wait()
        xu = xbuf_ref[i % X_RING]
        rowid = lax.broadcasted_iota(I32, xu.shape, 0)
        xu = jnp.where(rowid < tv_ref[i], xu, jnp.zeros_like(xu))
        x = _unpack_bf16_pairs(xu).astype(BF16)
        h1 = jnp.dot(x, w1b_ref[...], preferred_element_type=F32)
        h3 = jnp.dot(x, w3b_ref[...], preferred_element_type=F32)
        he = (h1 * jax.nn.sigmoid(h1)) * h3
        y = jnp.dot(he.astype(BF16), w2b_ref[...], preferred_element_type=F32)
        y_ref[...] = _pack_bf16_pairs(y)


def _experts(x_sorted, plan, w1, w3, w2, layer):
    n_slots, wp = x_sorted.shape
    _, n_e, d, d_e = w1.shape
    ts = SLOT_TILE
    any_spec = pl.BlockSpec(memory_space=pl.ANY)
    yspec = pl.BlockSpec((ts, wp),
                         lambda i, te, tv, nu, tg, tn: (jnp.minimum(i, jnp.maximum(nu[0] - 1, 0)), 0))
    return pl.pallas_call(
        functools.partial(_expert_kernel, layer=layer),
        grid_spec=pltpu.PrefetchScalarGridSpec(
            num_scalar_prefetch=len(plan),
            grid=(n_slots // ts,),
            in_specs=[any_spec] * 4,
            out_specs=yspec,
            scratch_shapes=[pltpu.VMEM((X_RING, ts, wp), x_sorted.dtype),
                            pltpu.SemaphoreType.DMA((X_RING,)),
                            pltpu.VMEM((2, d, d_e), F32), pltpu.VMEM((2, d, d_e), F32),
                            pltpu.VMEM((2, d_e, d), F32), pltpu.SemaphoreType.DMA((2, 3)),
                            pltpu.VMEM((d, d_e), BF16), pltpu.VMEM((d, d_e), BF16),
                            pltpu.VMEM((d_e, d), BF16)],
        ),
        out_shape=jax.ShapeDtypeStruct((n_slots, d // 2), U32),
        compiler_params=_params("arbitrary"),
        name="moe_experts",
    )(*plan, x_sorted, w1, w3, w2)


def _slot_plan(eids, ranks, counts, n_tiles):
    ts = SLOT_TILE
    cnt = counts[:, 0].astype(I32)
    padded = (cnt + ts - 1) // ts * ts
    e_ids = jnp.arange(N_EXPERTS, dtype=I32)
    ends = jnp.sum(jnp.where(e_ids[:, None] <= e_ids[None, :], padded[:, None], 0), axis=0)
    starts = ends - padded
    onehot = eids[None] == e_ids[:, None, None]
    dest = ranks + jnp.sum(jnp.where(onehot, starts[:, None, None], 0), axis=0)
    tile_lo = jnp.arange(n_tiles, dtype=I32) * ts
    tile_e = jnp.minimum(jnp.sum(tile_lo[:, None] >= ends[None, :], axis=1), N_EXPERTS - 1)
    tile_e = tile_e.astype(I32)
    valid_end = jnp.sum(jnp.where(tile_e[:, None] == e_ids[None, :], (starts + cnt)[None, :], 0),
                        axis=1)
    tile_valid = jnp.clip(valid_end - tile_lo, 0, ts).astype(I32)
    n_used = (ends[-1:] // ts).astype(I32)
    nonempty = cnt > 0
    earlier = jnp.logical_and(nonempty[None, :], e_ids[None, :] < tile_e[:, None])
    tile_grp = jnp.sum(earlier, axis=1).astype(I32)
    later = jnp.logical_and(nonempty[None, :], e_ids[None, :] > tile_e[:, None])
    nxt = jnp.min(jnp.where(later, e_ids[None, :], N_EXPERTS), axis=1)
    tile_next = jnp.where(nxt < N_EXPERTS, nxt, -1).astype(I32)
    return dest, (tile_e, tile_valid, n_used, tile_grp, tile_next)


class PendingTail(NamedTuple):
    operands: Callable
    rows: int
    row0: int
    width: int


def _post_value(h1_ref, y0_ref, y1_ref, gate_ref, p_ref, g_ref, wpg_ref, wpe_ref):
    gates = gate_ref[...]
    tm = gates.shape[1]
    gt = jnp.concatenate([gates, jnp.zeros((8 - TOP_K, tm), F32)], axis=0).T
    h2 = h1_ref[...] + (gt[:, 0:1] * _unpack_bf16_pairs(y0_ref[...]) +
                        gt[:, 1:2] * _unpack_bf16_pairs(y1_ref[...]))
    gate = jax.nn.sigmoid(jnp.dot(_rms(h2, g_ref[...]).astype(BF16), wpg_ref[...],
                                  preferred_element_type=F32))
    pe = jnp.dot(p_ref[...].astype(BF16), wpe_ref[...], preferred_element_type=F32)
    return h2 + gate * pe


def _post_operands(h1, yk, gates, p, g_pl, w_pg, w_pe, *, p_row0, row0=0):
    d = h1.shape[1]
    t = yk.shape[0] // TOP_K
    wy = yk.shape[1]

    def operands(tm):
        nt = t // tm
        blk0 = row0 // tm
        p_blk0 = (p_row0 + row0) // tm
        specs = [pl.BlockSpec((tm, d), lambda i: (i + blk0, 0)),
                 pl.BlockSpec((tm, wy), lambda i: (i, 0)),
                 pl.BlockSpec((tm, wy), lambda i: (i + nt, 0)),
                 pl.BlockSpec((TOP_K, tm), lambda i: (0, i + blk0)),
                 pl.BlockSpec((tm, p.shape[1]), lambda i: (i + p_blk0, 0)),
                 _const_spec((1, d)), _const_spec(w_pg.shape), _const_spec(w_pe.shape)]
        return specs, [h1, yk, yk, gates, p, g_pl, w_pg, w_pe]

    return PendingTail(operands, t, row0, d)


def _post_kernel(*refs):
    *post_refs, o_ref = refs
    o_ref[...] = _post_value(*post_refs)


def _post(pending, *, tm):
    specs, args = pending.operands(tm)
    t, d = pending.rows, pending.width
    return pl.pallas_call(
        _post_kernel,
        grid=(t // tm,),
        in_specs=specs,
        out_specs=pl.BlockSpec((tm, d), lambda i: (i, 0)),
        out_shape=jax.ShapeDtypeStruct((t, d), F32),
        compiler_params=_params("arbitrary"),
        name="layer_post",
    )(*args)


def _final_kernel(*refs, n_post):
    post_refs, gfin_ref, o_ref = refs[:n_post], refs[n_post], refs[-1]
    o_ref[...] = _rms(_post_value(*post_refs), gfin_ref[...])


def _final(pending, g_fin, *, tm, total_rows, out=None):
    specs, args = pending.operands(tm)
    d = pending.width
    blk0 = pending.row0 // tm
    in_specs = specs + [_const_spec((1, d))]
    operands = [*args, g_fin]
    aliases = {}
    if out is not None:
        aliases = {len(operands): 0}
        in_specs.append(pl.BlockSpec(memory_space=pl.ANY))
        operands.append(out)
    return pl.pallas_call(
        functools.partial(_final_kernel, n_post=len(args)),
        grid=(pending.rows // tm,),
        in_specs=in_specs,
        out_specs=pl.BlockSpec((tm, d), lambda i: (i + blk0, 0)),
        out_shape=jax.ShapeDtypeStruct((total_rows, d), F32),
        input_output_aliases=aliases,
        compiler_params=_params("arbitrary"),
        name="final_post",
    )(*operands)


def _router_weights(wc, bc, wf, bf):
    d = wc.shape[0]
    w = jnp.zeros((d, ROUTER_LANES), F32)
    w = w.at[:, 0:N_GROUPS].set(wc).at[:, FINE_ROW0:FINE_ROW0 + N_EXPERTS].set(wf)
    b = jnp.zeros((1, ROUTER_LANES), F32)
    b = b.at[0, 0:N_GROUPS].set(bc).at[0, FINE_ROW0:FINE_ROW0 + N_EXPERTS].set(bf)
    hi = w.astype(BF16)
    lo = (w - hi.astype(F32)).astype(BF16)
    return jnp.concatenate([hi, lo], axis=1), b


def _moe(xp, eids, ranks, counts, w1, w3, w2, layer, n_parts):
    t = xp.shape[0]
    n_tiles = TOP_K * t // SLOT_TILE + N_EXPERTS
    dest, plan = _slot_plan(eids, ranks, counts, n_tiles)
    x_sorted = _dispatch(xp, dest, n_tiles * SLOT_TILE)
    y_sorted = _experts(x_sorted, plan, w1, w3, w2, layer)
    tp = t // n_parts
    return [_combine(y_sorted, dest[:, s * tp:(s + 1) * tp].reshape(-1)) for s in range(n_parts)]


def kernel(x, p, norm_mix, norm_ffn, norm_pl, final_norm, w_in_even, conv_w_even, w_out_even, w_in_odd, g_v_odd, w_s_odd, b_s_odd, w_out_odd, router_c, router_c_b, router_f, router_f_b, moe_w1, moe_w3, moe_w2, w_pe, w_pg):
    batch, seq, d = x.shape
    depth = p.shape[0]
    t = batch * seq
    tm = min(TOKEN_TILE, seq)
    tm_big = min(LIGHT_TOKEN_TILE, seq)
    p_rows = p.reshape(depth * t, -1)
    row = lambda a: a.reshape(1, -1)
    h = x.reshape(t, d)
    for i in range(depth):
        j = i // 2
        wr, br = _router_weights(router_c[i], router_c_b[i], router_f[i], router_f_b[i])
        g_ffn = row(norm_ffn[i])
        if i % 2 == 0:
            if isinstance(h, PendingTail):
                h = _post(h, tm=tm_big)
            ya, q, k, v = _even_in(h, row(norm_mix[i]), w_in_even[j].astype(BF16), conv_w_even[j],
                                   seq=seq, tm=tm_big)
            yb = _attention(q, k, v, batch=batch, seq=seq, tq=min(Q_TILE, seq))
            h1, xp, gates, eids, ranks, counts = _even_out(
                ya, yb, h, w_out_even[j].astype(BF16), g_ffn, wr, br, tm=tm_big)
        else:
            h1, xp, gates, eids, ranks, counts = _odd_mixer(
                h, row(norm_mix[i]), w_in_odd[j].astype(BF16), row(g_v_odd[j]), w_s_odd[j],
                b_s_odd[j].T, w_out_odd[j].astype(BF16), g_ffn, wr, br, tm=tm)
        last = i == depth - 1
        n_parts = FINAL_PARTS if last and t % (FINAL_PARTS * tm_big) == 0 else 1
        yks = _moe(xp, eids, ranks, counts, moe_w1, moe_w3, moe_w2, i, n_parts)
        tails = [_post_operands(h1, yk, gates, p_rows, row(norm_pl[i]), w_pg[i].astype(BF16),
                                w_pe[i].astype(BF16), p_row0=i * t, row0=s * (t // n_parts))
                 for s, yk in enumerate(yks)]
        h = tails[0]
    out = None
    for tail in tails:
        out = _final(tail, row(final_norm), tm=tm_big, total_rows=t, out=out)
    return out.reshape(batch, seq, d)
```

```python
import functools
from typing import Callable, NamedTuple

import jax
import jax.numpy as jnp
from jax import lax
from jax.experimental import pallas as pl
from jax.experimental.pallas import tpu as pltpu
from jax.experimental.pallas import tpu_sc as plsc

EPS = 1e-6
HEAD_DIM = 64
HEADS_PER_SLAB = 2
ATTN_SLABS = 4
CHUNK = 128
N_GROUPS = 4
EXP_PER_GROUP = 8
N_EXPERTS = N_GROUPS * EXP_PER_GROUP
TOP_K = 2
ROUTER_LANES = 128
FINE_ROW0 = 8

V7X_VMEM_BYTES = 64 * 1024 * 1024
VMEM_LIMIT = V7X_VMEM_BYTES - 8 * 1024 * 1024
V7X_SC_CORES = 2
V7X_SC_SUBCORES = 16
SC_WORKERS = V7X_SC_CORES * V7X_SC_SUBCORES
SC_ROWS = 32

TOKEN_TILE = 512
LIGHT_TOKEN_TILE = 1024
Q_TILE = 128
SLOT_TILE = 512
X_RING = 3
TAIL_PARTS = 2

LOG_F32_TINY_BOUND = -88.0
OFF_PENALTY = 1e30

F32 = jnp.float32
BF16 = jnp.bfloat16
U32 = jnp.uint32
I32 = jnp.int32


def _rms(x, g):
    return x * lax.rsqrt(jnp.mean(x * x, axis=-1, keepdims=True) + EPS) * g


def _const_spec(shape):
    nd = len(shape)
    return pl.BlockSpec(shape, lambda *_: (0,) * nd, pipeline_mode=pl.Buffered(1))


def _params(*sem):
    return pltpu.CompilerParams(dimension_semantics=sem, vmem_limit_bytes=VMEM_LIMIT)


def _even_in_kernel(h_ref, g_ref, w_ref, cw_ref, ya_ref, q_ref, k_ref, v_ref, zs_ref,
                    *, tiles_per_seq, a_width, sb_width):
    tm = h_ref.shape[0]
    xn = _rms(h_ref[...], g_ref[...]).astype(BF16)
    a3 = 3 * a_width

    @pl.when(pl.program_id(0) % tiles_per_seq == 0)
    def _():
        zs_ref[0:8, :] = jnp.zeros((8, a_width), F32)

    pa = jnp.dot(xn, w_ref[:, 0:a3], preferred_element_type=F32)
    z = pa[:, 2 * a_width:a3] * pa[:, 0:a_width]
    zs_ref[8:tm + 8, :] = z
    z1 = zs_ref[7:tm + 7, :]
    z2 = zs_ref[6:tm + 6, :]
    cw = cw_ref[...]
    conv = cw[0:1, :] * z2 + cw[1:2, :] * z1 + cw[2:3, :] * z
    ya_ref[...] = (pa[:, a_width:2 * a_width] * conv).astype(BF16)
    zs_ref[0:8, :] = zs_ref[tm:tm + 8, :]

    pq = jnp.dot(xn, w_ref[:, a3:a3 + 3 * sb_width], preferred_element_type=F32)
    q_ref[...] = (pq[:, 0:sb_width] * (HEAD_DIM ** -0.5)).astype(BF16)
    k_ref[...] = pq[:, sb_width:2 * sb_width].astype(BF16)
    v_ref[...] = pq[:, 2 * sb_width:3 * sb_width].astype(BF16)


def _even_in(h, g, w_in, conv_w, *, seq, tm):
    t, d = h.shape
    a_width = conv_w.shape[1]
    sb_width = (w_in.shape[1] - 3 * a_width) // 3
    kern = functools.partial(_even_in_kernel, tiles_per_seq=seq // tm, a_width=a_width,
                             sb_width=sb_width)
    row = lambda w: pl.BlockSpec((tm, w), lambda i: (i, 0))
    return pl.pallas_call(
        kern,
        grid=(t // tm,),
        in_specs=[row(d), _const_spec((1, d)), _const_spec(w_in.shape), _const_spec(conv_w.shape)],
        out_specs=[row(a_width), row(sb_width), row(sb_width), row(sb_width)],
        out_shape=[jax.ShapeDtypeStruct((t, a_width), BF16)] +
                  [jax.ShapeDtypeStruct((t, sb_width), BF16)] * 3,
        scratch_shapes=[pltpu.VMEM((tm + 8, a_width), F32)],
        compiler_params=_params("arbitrary"),
        name="even_in",
    )(h, g, w_in, conv_w)


def _attn_kernel(q_ref, k_ref, v_ref, o_ref, acc_ref, c_ref, *, tq):
    qi = pl.program_id(2)
    slab = HEADS_PER_SLAB * HEAD_DIM
    rows_slab = HEADS_PER_SLAB * tq
    rows = ATTN_SLABS * rows_slab
    lane = lax.broadcasted_iota(I32, (tq, slab), 1)
    qs = []
    for sl in range(ATTN_SLABS):
        q = q_ref[:, sl * slab:(sl + 1) * slab]
        zero = jnp.zeros_like(q)
        qs.append(jnp.concatenate([jnp.where(lane < HEAD_DIM, q, zero),
                                   jnp.where(lane >= HEAD_DIM, q, zero)], axis=0))
    r_i = lax.broadcasted_iota(I32, (tq, tq), 0)
    c_i = lax.broadcasted_iota(I32, (tq, tq), 1)
    upper = jnp.where(r_i > c_i, 1.0, 0.0).astype(BF16)
    t_loc = lax.broadcasted_iota(I32, (rows, tq), 0) & (tq - 1)
    s_loc = lax.broadcasted_iota(I32, (rows, tq), 1)
    causal = s_loc < t_loc

    def block(kb, c, mask):
        start = pl.multiple_of(kb * tq, tq)
        kk = k_ref[pl.ds(start, tq), :]
        vv = v_ref[pl.ds(start, tq), :]
        z = jnp.concatenate(
            [lax.dot_general(qs[sl], kk[:, sl * slab:(sl + 1) * slab], (((1,), (1,)), ((), ())),
                             preferred_element_type=F32) for sl in range(ATTN_SLABS)], axis=0)
        sp = jnp.maximum(z, 0.0) + jnp.log(1.0 + jnp.exp(-jnp.abs(z)))
        spm = sp if mask is None else jnp.where(mask, sp, 0.0)
        hi = spm.astype(BF16)
        lo = (spm - hi.astype(F32)).astype(BF16)
        later = (jnp.dot(hi, upper, preferred_element_type=F32) +
                 jnp.dot(lo, upper, preferred_element_type=F32))
        a = jnp.exp(z - sp - later - c)
        if mask is not None:
            a = jnp.where(mask, a, 0.0)
        a = a.astype(BF16)
        out = jnp.concatenate(
            [jnp.dot(a[sl * rows_slab:(sl + 1) * rows_slab], vv[:, sl * slab:(sl + 1) * slab],
                     preferred_element_type=F32) for sl in range(ATTN_SLABS)], axis=0)
        return out, jnp.sum(spm, axis=1, keepdims=True)

    out0, sum0 = block(qi, jnp.zeros((rows, 1), F32), causal)
    has_prev = qi >= 1
    out1, sum1 = block(jnp.maximum(qi - 1, 0), sum0 + jnp.where(has_prev, 0.0, OFF_PENALTY), None)
    c2 = sum0 + jnp.where(has_prev, sum1, 0.0)
    acc_ref[...] = out0 + out1
    c_ref[...] = c2

    def cond(carry):
        kb, cmin = carry
        return jnp.logical_and(kb >= 0, cmin < -LOG_F32_TINY_BOUND)

    def body(carry):
        kb, _ = carry
        c = c_ref[...]
        out, ssum = block(kb, c, None)
        acc_ref[...] += out
        c_new = c + ssum
        c_ref[...] = c_new
        return kb - 1, jnp.min(c_new)

    lax.while_loop(cond, body, (qi - 2, jnp.min(c2)))
    acc = acc_ref[...]
    for sl in range(ATTN_SLABS):
        lo_r = sl * rows_slab
        o_ref[:, sl * slab:(sl + 1) * slab] = jnp.where(
            lane < HEAD_DIM, acc[lo_r:lo_r + tq], acc[lo_r + tq:lo_r + 2 * tq]).astype(o_ref.dtype)


def _attention(q, k, v, *, batch, seq, tq):
    t, w = q.shape
    wblk = ATTN_SLABS * HEADS_PER_SLAB * HEAD_DIM
    nq = seq // tq
    qspec = pl.BlockSpec((tq, wblk), lambda b, s, i: (b * nq + i, s))
    kvspec = pl.BlockSpec((seq, wblk), lambda b, s, i: (b, s))
    rows = ATTN_SLABS * HEADS_PER_SLAB * tq
    return pl.pallas_call(
        functools.partial(_attn_kernel, tq=tq),
        grid=(batch, w // wblk, nq),
        in_specs=[qspec, kvspec, kvspec],
        out_specs=qspec,
        out_shape=jax.ShapeDtypeStruct((t, w), BF16),
        scratch_shapes=[pltpu.VMEM((rows, HEADS_PER_SLAB * HEAD_DIM), F32),
                        pltpu.VMEM((rows, 1), F32)],
        compiler_params=_params("arbitrary", "arbitrary", "arbitrary"),
        name="stick_breaking_attention",
    )(q, k, v)


def _pack_bf16_pairs(x):
    n = x.shape[1] // 2
    bits = lax.bitcast_convert_type(x.astype(BF16).astype(F32), U32)
    return bits[:, 0:n] | (bits[:, n:] >> 16)


def _unpack_bf16_pairs(u):
    a = lax.bitcast_convert_type(u & jnp.uint32(0xFFFF0000), F32)
    b = lax.bitcast_convert_type(u << 16, F32)
    return jnp.concatenate([a, b], axis=1)


def _router_tail(h1, g_ref, wr_ref, br_ref, cnt0_ref,
                 h1_ref, xp_ref, gate_ref, eid_ref, rank_ref, cnt_out_ref, cnt_ref):
    h1_ref[...] = h1
    xn = _rms(h1, g_ref[...])
    xp_ref[...] = _pack_bf16_pairs(xn)
    x_hi = xn.astype(BF16)
    x_lo = (xn - x_hi.astype(F32)).astype(BF16)
    both = jnp.dot(x_hi, wr_ref[...], preferred_element_type=F32)
    logits = (both[:, 0:ROUTER_LANES] + both[:, ROUTER_LANES:] +
              jnp.dot(x_lo, wr_ref[:, 0:ROUTER_LANES], preferred_element_type=F32) + br_ref[...])
    lt = logits.T
    tm = lt.shape[1]
    row = lax.broadcasted_iota(I32, (EXP_PER_GROUP, tm), 0)
    neg = jnp.float32(-jnp.inf)

    coarse = jnp.where(row < N_GROUPS, lt[0:EXP_PER_GROUP], neg)
    cmax = jnp.max(coarse, axis=0, keepdims=True)
    gi = jnp.min(jnp.where(coarse == cmax, row, EXP_PER_GROUP), axis=0, keepdims=True)
    pg = 1.0 / jnp.sum(jnp.exp(coarse - cmax), axis=0, keepdims=True)

    fine = lt[FINE_ROW0:FINE_ROW0 + EXP_PER_GROUP]
    for grp in range(1, N_GROUPS):
        lo_r = FINE_ROW0 + grp * EXP_PER_GROUP
        fine = jnp.where(gi == grp, lt[lo_r:lo_r + EXP_PER_GROUP], fine)
    m1 = jnp.max(fine, axis=0, keepdims=True)
    i1 = jnp.min(jnp.where(fine == m1, row, EXP_PER_GROUP), axis=0, keepdims=True)
    rest = jnp.where(row == i1, neg, fine)
    m2 = jnp.max(rest, axis=0, keepdims=True)
    i2 = jnp.min(jnp.where(rest == m2, row, EXP_PER_GROUP), axis=0, keepdims=True)
    e21 = jnp.exp(m2 - m1)
    w1 = 1.0 / (1.0 + e21)
    gate_ref[...] = jnp.concatenate([pg * w1, pg * (e21 * w1)], axis=0)
    e0 = gi * EXP_PER_GROUP + i1
    e1 = gi * EXP_PER_GROUP + i2
    eid_ref[...] = jnp.concatenate([e0, e1], axis=0)

    @pl.when(pl.program_id(0) == 0)
    def _():
        cnt_ref[...] = cnt0_ref[...]

    erow = lax.broadcasted_iota(I32, (N_EXPERTS, tm), 0)
    oh0 = jnp.where(erow == e0, 1.0, 0.0)
    oh1 = jnp.where(erow == e1, 1.0, 0.0)
    r_i = lax.broadcasted_iota(I32, (tm, tm), 0)
    c_i = lax.broadcasted_iota(I32, (tm, tm), 1)
    before = jnp.where(r_i < c_i, 1.0, 0.0).astype(BF16)
    pre0 = jnp.dot(oh0.astype(BF16), before, preferred_element_type=F32)
    pre1 = jnp.dot(oh1.astype(BF16), before, preferred_element_type=F32)
    tot0 = jnp.sum(oh0, axis=1, keepdims=True)
    tot1 = jnp.sum(oh1, axis=1, keepdims=True)
    base = cnt_ref[:, 0:1]
    r0 = jnp.sum(oh0 * (pre0 + base), axis=0, keepdims=True)
    r1 = jnp.sum(oh1 * (pre1 + (base + tot0)), axis=0, keepdims=True)
    rank_ref[...] = jnp.concatenate([r0, r1], axis=0).astype(I32)
    total = jnp.broadcast_to(base + tot0 + tot1, cnt_ref.shape)
    cnt_ref[...] = total
    cnt_out_ref[...] = total


N_TAIL_ROW_OUTPUTS = 5


def _tail_specs(total_rows, d, tm, row0=0):
    blk0 = row0 // tm
    t = total_rows
    row = lambda w: pl.BlockSpec((tm, w), lambda i: (i + blk0, 0))
    lanes = pl.BlockSpec((TOP_K, tm), lambda i: (0, i + blk0))
    cnt = pl.BlockSpec((N_EXPERTS, ROUTER_LANES), lambda i: (0, 0))
    out_specs = [row(d), row(d // 2), lanes, lanes, lanes, cnt]
    out_shape = [jax.ShapeDtypeStruct((t, d), F32), jax.ShapeDtypeStruct((t, d // 2), U32),
                 jax.ShapeDtypeStruct((TOP_K, t), F32), jax.ShapeDtypeStruct((TOP_K, t), I32),
                 jax.ShapeDtypeStruct((TOP_K, t), I32),
                 jax.ShapeDtypeStruct((N_EXPERTS, ROUTER_LANES), F32)]
    scratch = [pltpu.VMEM((N_EXPERTS, ROUTER_LANES), F32)]
    return out_specs, out_shape, scratch


def _no_slots():
    return jnp.zeros((N_EXPERTS, ROUTER_LANES), F32)


def _even_out_kernel(ya_ref, yb_ref, h_ref, w_ref, g_ref, wr_ref, br_ref, cnt0_ref, *tail_refs):
    a_width = ya_ref.shape[1]
    mix = (jnp.dot(ya_ref[...], w_ref[0:a_width, :], preferred_element_type=F32) +
           jnp.dot(yb_ref[...], w_ref[a_width:, :], preferred_element_type=F32))
    _router_tail(h_ref[...] + mix, g_ref, wr_ref, br_ref, cnt0_ref, *tail_refs)


def _even_out(ya, yb, h, w_out, g_ffn, wr, br, *, tm):
    t, d = h.shape
    out_specs, out_shape, scratch = _tail_specs(t, d, tm)
    row = lambda w: pl.BlockSpec((tm, w), lambda i: (i, 0))
    cnt0 = _no_slots()
    return pl.pallas_call(
        _even_out_kernel,
        grid=(t // tm,),
        in_specs=[row(ya.shape[1]), row(yb.shape[1]), row(d), _const_spec(w_out.shape),
                  _const_spec((1, d)), _const_spec(wr.shape), _const_spec(br.shape),
                  _const_spec(cnt0.shape)],
        out_specs=out_specs,
        out_shape=out_shape,
        scratch_shapes=scratch,
        compiler_params=_params("arbitrary"),
        name="even_out_router",
    )(ya, yb, h, w_out, g_ffn, wr, br, cnt0)


def _gelu_tanh(x):
    return 0.5 * x * (1.0 + jnp.tanh(0.7978845608028654 * (x + 0.044715 * (x * x * x))))


def _odd_kernel(*refs, n_prev, n_alias):
    h_refs, refs = refs[:max(n_prev, 1)], refs[max(n_prev, 1):]
    (g_ref, w_in_ref, gv_ref, ws_ref, bs_ref, w_out_ref, gf_ref, wr_ref, br_ref, cnt0_ref,
     *rest) = refs
    *tail_refs, y_ref = rest[n_alias:]
    h = _post_value(*h_refs) if n_prev else h_refs[0][...]
    tm = h.shape[0]
    c_width = gv_ref.shape[1]
    n_grp = ws_ref.shape[0]
    gdim = c_width // n_grp
    xn = _rms(h, g_ref[...]).astype(BF16)
    u = _gelu_tanh(jnp.dot(xn, w_in_ref[:, 0:c_width], preferred_element_type=F32))
    v = _gelu_tanh(jnp.dot(xn, w_in_ref[:, c_width:], preferred_element_type=F32))
    v = _rms(v, gv_ref[...]).astype(BF16)
    r_i = lax.broadcasted_iota(I32, (CHUNK, CHUNK), 0)
    c_i = lax.broadcasted_iota(I32, (CHUNK, CHUNK), 1)
    causal = c_i <= r_i
    bs = bs_ref[...]
    for grp in range(n_grp):
        w_m = jnp.where(causal, ws_ref[grp], 0.0).astype(BF16)
        bias = bs[:, grp:grp + 1]
        cols = slice(grp * gdim, (grp + 1) * gdim)
        for c in range(tm // CHUNK):
            rows = slice(c * CHUNK, (c + 1) * CHUNK)
            gate = jnp.dot(w_m, v[rows, cols], preferred_element_type=F32) + bias
            y_ref[rows, cols] = (u[rows, cols] * gate).astype(BF16)
    mix = jnp.dot(y_ref[...], w_out_ref[...], preferred_element_type=F32)
    _router_tail(h + mix, gf_ref, wr_ref, br_ref, cnt0_ref, *tail_refs)


def _odd_mixer(h_or_prev, g_mix, w_in, g_v, w_s, b_s_t, w_out, g_ffn, wr, br, *, tm,
               total_rows=None, earlier=None):
    cnt0 = _no_slots() if earlier is None else earlier[N_TAIL_ROW_OUTPUTS]
    consts = [g_mix, w_in, g_v, w_s, b_s_t, w_out, g_ffn, wr, br, cnt0]
    pending = isinstance(h_or_prev, PendingTail)
    if pending:
        head_specs, head_args = h_or_prev.operands(tm)
        t, d, row0 = h_or_prev.rows, h_or_prev.width, h_or_prev.row0
    else:
        t, d = h_or_prev.shape
        row0 = 0
        head_specs, head_args = [pl.BlockSpec((tm, d), lambda i: (i, 0))], [h_or_prev]
    out_specs, out_shape, scratch = _tail_specs(total_rows or t, d, tm, row0)
    n_prev = len(head_args) if pending else 0
    in_specs = head_specs + [_const_spec(c.shape) for c in consts]
    operands = [*head_args, *consts]
    aliases = {}
    if earlier is not None:
        aliases = {len(operands) + k: k for k in range(N_TAIL_ROW_OUTPUTS)}
        in_specs += [pl.BlockSpec(memory_space=pl.ANY)] * N_TAIL_ROW_OUTPUTS
        operands += list(earlier[:N_TAIL_ROW_OUTPUTS])
    return pl.pallas_call(
        functools.partial(_odd_kernel, n_prev=n_prev, n_alias=len(aliases)),
        grid=(t // tm,),
        in_specs=in_specs,
        out_specs=out_specs,
        out_shape=out_shape,
        input_output_aliases=aliases,
        scratch_shapes=scratch + [pltpu.VMEM((tm, g_v.shape[1]), BF16)],
        compiler_params=_params("arbitrary"),
        name="odd_mixer_router",
    )(*operands)


def _sc_mesh():
    return plsc.VectorSubcoreMesh(core_axis_name="c", subcore_axis_name="s")


def _sc_worker_base(rows_per_worker):
    wid = lax.axis_index("s") * V7X_SC_CORES + lax.axis_index("c")
    return wid * rows_per_worker


def _dispatch(xp, dest, n_slots):
    t, w = xp.shape
    per_worker = t // SC_WORKERS
    n_chunks = per_worker // SC_ROWS
    assert per_worker * SC_WORKERS == t and n_chunks * SC_ROWS == per_worker and n_chunks % 2 == 0

    idx_t = pltpu.VMEM((SC_ROWS,), I32)

    @functools.partial(
        pl.kernel, mesh=_sc_mesh(),
        out_type=jax.ShapeDtypeStruct((n_slots, w), xp.dtype),
        scratch_types=[pltpu.VMEM((2, SC_ROWS, w), xp.dtype), idx_t, idx_t, idx_t, idx_t,
                       pltpu.SemaphoreType.DMA((2,)), pltpu.SemaphoreType.DMA((2,))],
    )
    def kern(x_hbm, d_hbm, out_hbm, rows_v, i00, i01, i10, i11, rsem, wsem):
        base = _sc_worker_base(per_worker)
        idx = ((i00, i01), (i10, i11))

        def load(c, b):
            off = pl.multiple_of(base + c * SC_ROWS, 8)
            pltpu.async_copy(x_hbm.at[pl.ds(off, SC_ROWS)], rows_v.at[b], rsem.at[b])
            for k in range(TOP_K):
                pltpu.sync_copy(d_hbm.at[k, pl.ds(off, SC_ROWS)], idx[b][k])

        def wait_load(b):
            pltpu.make_async_copy(x_hbm.at[pl.ds(0, SC_ROWS)], rows_v.at[b], rsem.at[b]).wait()

        def scatter(b):
            for k in range(TOP_K):
                pltpu.async_copy(rows_v.at[b], out_hbm.at[idx[b][k]], wsem.at[b])

        def wait_scatter(b):
            for k in range(TOP_K):
                pltpu.make_async_copy(rows_v.at[b], out_hbm.at[idx[b][k]], wsem.at[b]).wait()

        load(0, 0)

        @pl.loop(0, n_chunks, step=2)
        def _(c0):
            for b in range(2):
                c = c0 + b
                wait_load(b)
                scatter(b)

                @pl.when(c + 1 < n_chunks)
                def _():
                    @pl.when(c >= 1)
                    def _():
                        wait_scatter(1 - b)
                    load(c + 1, 1 - b)

        wait_scatter(0)
        wait_scatter(1)

    return kern(xp, dest)


def _combine(y, idx):
    n = idx.shape[0]
    w = y.shape[1]
    rows = 2 * SC_ROWS
    per_worker = n // SC_WORKERS
    n_chunks = per_worker // rows
    assert per_worker * SC_WORKERS == n and n_chunks * rows == per_worker and n_chunks % 2 == 0

    idx_t = pltpu.VMEM((rows,), I32)

    @functools.partial(
        pl.kernel, mesh=_sc_mesh(),
        out_type=jax.ShapeDtypeStruct((n, w), y.dtype),
        scratch_types=[pltpu.VMEM((2, rows, w), y.dtype), idx_t, idx_t,
                       pltpu.SemaphoreType.DMA((2,)), pltpu.SemaphoreType.DMA((2,))],
    )
    def kern(y_hbm, i_hbm, out_hbm, rows_v, i0, i1, gsem, wsem):
        base = _sc_worker_base(per_worker)
        ibuf = (i0, i1)

        def gather(c, b):
            off = pl.multiple_of(base + c * rows, 8)
            pltpu.sync_copy(i_hbm.at[pl.ds(off, rows)], ibuf[b])
            pltpu.async_copy(y_hbm.at[ibuf[b]], rows_v.at[b], gsem.at[b])

        def wait_gather(b):
            pltpu.make_async_copy(y_hbm.at[ibuf[b]], rows_v.at[b], gsem.at[b]).wait()

        def write(c, b):
            off = pl.multiple_of(base + c * rows, 8)
            pltpu.async_copy(rows_v.at[b], out_hbm.at[pl.ds(off, rows)], wsem.at[b])

        def wait_write(b):
            pltpu.make_async_copy(rows_v.at[b], out_hbm.at[pl.ds(0, rows)], wsem.at[b]).wait()

        gather(0, 0)

        @pl.loop(0, n_chunks, step=2)
        def _(c0):
            for b in range(2):
                c = c0 + b

                @pl.when(c + 1 < n_chunks)
                def _():
                    @pl.when(c >= 1)
                    def _():
                        wait_write(1 - b)
                    gather(c + 1, 1 - b)

                wait_gather(b)
                write(c, b)

        wait_write(0)
        wait_write(1)

    return kern(y, idx)


def _expert_kernel(te_ref, tv_ref, nu_ref, tg_ref, tn_ref, x_hbm, w1_hbm, w3_hbm, w2_hbm, y_ref,
                   xbuf_ref, xsem, w1f_ref, w3f_ref, w2f_ref, wsem, w1b_ref, w3b_ref, w2b_ref,
                   *, layer):
    i = pl.program_id(0)
    n_used = nu_ref[0]
    ts = xbuf_ref.shape[1]

    def x_copy(tile):
        slot = tile % X_RING
        rows = pl.ds(pl.multiple_of(tile * ts, ts), ts)
        return pltpu.make_async_copy(x_hbm.at[rows], xbuf_ref.at[slot], xsem.at[slot])

    def w_copies(expert, slot):
        pairs = ((w1_hbm, w1f_ref), (w3_hbm, w3f_ref), (w2_hbm, w2f_ref))
        return [pltpu.make_async_copy(src.at[layer, expert], dst.at[slot], wsem.at[slot, j])
                for j, (src, dst) in enumerate(pairs)]

    @pl.when(jnp.logical_and(i == 0, n_used > 0))
    def _():
        for tile in range(X_RING - 1):
            @pl.when(tile < n_used)
            def _():
                x_copy(tile).start()
        for c in w_copies(te_ref[0], 0):
            c.start()

    @pl.when(i < n_used)
    def _():
        @pl.when(i + (X_RING - 1) < n_used)
        def _():
            x_copy(i + (X_RING - 1)).start()

        prev = te_ref[jnp.maximum(i - 1, 0)]

        @pl.when(jnp.logical_or(i == 0, te_ref[i] != prev))
        def _():
            slot = tg_ref[i] % 2
            for c in w_copies(te_ref[i], slot):
                c.wait()
            nxt = tn_ref[i]

            @pl.when(nxt >= 0)
            def _():
                for c in w_copies(nxt, 1 - slot):
                    c.start()

            w1b_ref[...] = w1f_ref[slot].astype(BF16)
            w3b_ref[...] = w3f_ref[slot].astype(BF16)
            w2b_ref[...] = w2f_ref[slot].astype(BF16)

        x_copy(i).wait()
        xu = xbuf_ref[i % X_RING]
        rowid = lax.broadcasted_iota(I32, xu.shape, 0)
        xu = jnp.where(rowid < tv_ref[i], xu, jnp.zeros_like(xu))
        x = _unpack_bf16_pairs(xu).astype(BF16)
        h1 = jnp.dot(x, w1b_ref[...], preferred_element_type=F32)
        h3 = jnp.dot(x, w3b_ref[...], preferred_element_type=F32)
        he = (h1 * jax.nn.sigmoid(h1)) * h3
        y = jnp.dot(he.astype(BF16), w2b_ref[...], preferred_element_type=F32)
        y_ref[...] = _pack_bf16_pairs(y)


def _experts(x_sorted, plan, w1, w3, w2, layer):
    n_slots, wp = x_sorted.shape
    _, n_e, d, d_e = w1.shape
    ts = SLOT_TILE
    any_spec = pl.BlockSpec(memory_space=pl.ANY)
    yspec = pl.BlockSpec((ts, wp),
                         lambda i, te, tv, nu, tg, tn: (jnp.minimum(i, jnp.maximum(nu[0] - 1, 0)), 0))
    return pl.pallas_call(
        functools.partial(_expert_kernel, layer=layer),
        grid_spec=pltpu.PrefetchScalarGridSpec(
            num_scalar_prefetch=len(plan),
            grid=(n_slots // ts,),
            in_specs=[any_spec] * 4,
            out_specs=yspec,
            scratch_shapes=[pltpu.VMEM((X_RING, ts, wp), x_sorted.dtype),
                            pltpu.SemaphoreType.DMA((X_RING,)),
                            pltpu.VMEM((2, d, d_e), F32), pltpu.VMEM((2, d, d_e), F32),
                            pltpu.VMEM((2, d_e, d), F32), pltpu.SemaphoreType.DMA((2, 3)),
                            pltpu.VMEM((d, d_e), BF16), pltpu.VMEM((d, d_e), BF16),
                            pltpu.VMEM((d_e, d), BF16)],
        ),
        out_shape=jax.ShapeDtypeStruct((n_slots, d // 2), U32),
        compiler_params=_params("arbitrary"),
        name="moe_experts",
    )(*plan, x_sorted, w1, w3, w2)


def _slot_plan(eids, ranks, counts, n_tiles):
    ts = SLOT_TILE
    cnt = counts[:, 0].astype(I32)
    padded = (cnt + ts - 1) // ts * ts
    e_ids = jnp.arange(N_EXPERTS, dtype=I32)
    ends = jnp.sum(jnp.where(e_ids[:, None] <= e_ids[None, :], padded[:, None], 0), axis=0)
    starts = ends - padded
    onehot = eids[None] == e_ids[:, None, None]
    dest = ranks + jnp.sum(jnp.where(onehot, starts[:, None, None], 0), axis=0)
    tile_lo = jnp.arange(n_tiles, dtype=I32) * ts
    tile_e = jnp.minimum(jnp.sum(tile_lo[:, None] >= ends[None, :], axis=1), N_EXPERTS - 1)
    tile_e = tile_e.astype(I32)
    valid_end = jnp.sum(jnp.where(tile_e[:, None] == e_ids[None, :], (starts + cnt)[None, :], 0),
                        axis=1)
    tile_valid = jnp.clip(valid_end - tile_lo, 0, ts).astype(I32)
    n_used = (ends[-1:] // ts).astype(I32)
    nonempty = cnt > 0
    earlier = jnp.logical_and(nonempty[None, :], e_ids[None, :] < tile_e[:, None])
    tile_grp = jnp.sum(earlier, axis=1).astype(I32)
    later = jnp.logical_and(nonempty[None, :], e_ids[None, :] > tile_e[:, None])
    nxt = jnp.min(jnp.where(later, e_ids[None, :], N_EXPERTS), axis=1)
    tile_next = jnp.where(nxt < N_EXPERTS, nxt, -1).astype(I32)
    return dest, (tile_e, tile_valid, n_used, tile_grp, tile_next)


class PendingTail(NamedTuple):
    operands: Callable
    rows: int
    row0: int
    width: int


def _post_value(h1_ref, y0_ref, y1_ref, gate_ref, p_ref, g_ref, wpg_ref, wpe_ref):
    gates = gate_ref[...]
    tm = gates.shape[1]
    gt = jnp.concatenate([gates, jnp.zeros((8 - TOP_K, tm), F32)], axis=0).T
    h2 = h1_ref[...] + (gt[:, 0:1] * _unpack_bf16_pairs(y0_ref[...]) +
                        gt[:, 1:2] * _unpack_bf16_pairs(y1_ref[...]))
    gate = jax.nn.sigmoid(jnp.dot(_rms(h2, g_ref[...]).astype(BF16), wpg_ref[...],
                                  preferred_element_type=F32))
    pe = jnp.dot(p_ref[...].astype(BF16), wpe_ref[...], preferred_element_type=F32)
    return h2 + gate * pe


def _post_operands(h1, yk, gates, p, g_pl, w_pg, w_pe, *, p_row0, row0=0):
    d = h1.shape[1]
    t = yk.shape[0] // TOP_K
    wy = yk.shape[1]

    def operands(tm):
        nt = t // tm
        blk0 = row0 // tm
        p_blk0 = (p_row0 + row0) // tm
        specs = [pl.BlockSpec((tm, d), lambda i: (i + blk0, 0)),
                 pl.BlockSpec((tm, wy), lambda i: (i, 0)),
                 pl.BlockSpec((tm, wy), lambda i: (i + nt, 0)),
                 pl.BlockSpec((TOP_K, tm), lambda i: (0, i + blk0)),
                 pl.BlockSpec((tm, p.shape[1]), lambda i: (i + p_blk0, 0)),
                 _const_spec((1, d)), _const_spec(w_pg.shape), _const_spec(w_pe.shape)]
        return specs, [h1, yk, yk, gates, p, g_pl, w_pg, w_pe]

    return PendingTail(operands, t, row0, d)


def _post_kernel(*refs):
    *post_refs, o_ref = refs
    o_ref[...] = _post_value(*post_refs)


def _post(pending, *, tm):
    specs, args = pending.operands(tm)
    t, d = pending.rows, pending.width
    return pl.pallas_call(
        _post_kernel,
        grid=(t // tm,),
        in_specs=specs,
        out_specs=pl.BlockSpec((tm, d), lambda i: (i, 0)),
        out_shape=jax.ShapeDtypeStruct((t, d), F32),
        compiler_params=_params("arbitrary"),
        name="layer_post",
    )(*args)


def _final_kernel(*refs, n_post):
    post_refs, gfin_ref, o_ref = refs[:n_post], refs[n_post], refs[-1]
    o_ref[...] = _rms(_post_value(*post_refs), gfin_ref[...])


def _final(pending, g_fin, *, tm, total_rows, out=None):
    specs, args = pending.operands(tm)
    d = pending.width
    blk0 = pending.row0 // tm
    in_specs = specs + [_const_spec((1, d))]
    operands = [*args, g_fin]
    aliases = {}
    if out is not None:
        aliases = {len(operands): 0}
        in_specs.append(pl.BlockSpec(memory_space=pl.ANY))
        operands.append(out)
    return pl.pallas_call(
        functools.partial(_final_kernel, n_post=len(args)),
        grid=(pending.rows // tm,),
        in_specs=in_specs,
        out_specs=pl.BlockSpec((tm, d), lambda i: (i + blk0, 0)),
        out_shape=jax.ShapeDtypeStruct((total_rows, d), F32),
        input_output_aliases=aliases,
        compiler_params=_params("arbitrary"),
        name="final_post",
    )(*operands)


def _router_weights(wc, bc, wf, bf):
    d = wc.shape[0]
    w = jnp.zeros((d, ROUTER_LANES), F32)
    w = w.at[:, 0:N_GROUPS].set(wc).at[:, FINE_ROW0:FINE_ROW0 + N_EXPERTS].set(wf)
    b = jnp.zeros((1, ROUTER_LANES), F32)
    b = b.at[0, 0:N_GROUPS].set(bc).at[0, FINE_ROW0:FINE_ROW0 + N_EXPERTS].set(bf)
    hi = w.astype(BF16)
    lo = (w - hi.astype(F32)).astype(BF16)
    return jnp.concatenate([hi, lo], axis=1), b


def _moe(xp, eids, ranks, counts, w1, w3, w2, layer, n_parts):
    t = xp.shape[0]
    n_tiles = TOP_K * t // SLOT_TILE + N_EXPERTS
    dest, plan = _slot_plan(eids, ranks, counts, n_tiles)
    x_sorted = _dispatch(xp, dest, n_tiles * SLOT_TILE)
    y_sorted = _experts(x_sorted, plan, w1, w3, w2, layer)
    tp = t // n_parts
    return [_combine(y_sorted, dest[:, s * tp:(s + 1) * tp].reshape(-1)) for s in range(n_parts)]


def kernel(x, p, norm_mix, norm_ffn, norm_pl, final_norm, w_in_even, conv_w_even, w_out_even, w_in_odd, g_v_odd, w_s_odd, b_s_odd, w_out_odd, router_c, router_c_b, router_f, router_f_b, moe_w1, moe_w3, moe_w2, w_pe, w_pg):
    batch, seq, d = x.shape
    depth = p.shape[0]
    t = batch * seq
    tm = min(TOKEN_TILE, seq)
    tm_big = min(LIGHT_TOKEN_TILE, seq)
    p_rows = p.reshape(depth * t, -1)
    row = lambda a: a.reshape(1, -1)
    h = x.reshape(t, d)
    for i in range(depth):
        j = i // 2
        wr, br = _router_weights(router_c[i], router_c_b[i], router_f[i], router_f_b[i])
        g_ffn = row(norm_ffn[i])
        if i % 2 == 0:
            if isinstance(h, list):
                h = _post(h[0], tm=tm_big)
            ya, q, k, v = _even_in(h, row(norm_mix[i]), w_in_even[j].astype(BF16), conv_w_even[j],
                                   seq=seq, tm=tm_big)
            yb = _attention(q, k, v, batch=batch, seq=seq, tq=min(Q_TILE, seq))
            mixed = _even_out(ya, yb, h, w_out_even[j].astype(BF16), g_ffn, wr, br, tm=tm_big)
        else:
            mixed = None
            for part in (h if isinstance(h, list) else [h]):
                mixed = _odd_mixer(part, row(norm_mix[i]), w_in_odd[j].astype(BF16),
                                   row(g_v_odd[j]), w_s_odd[j], b_s_odd[j].T,
                                   w_out_odd[j].astype(BF16), g_ffn, wr, br, tm=tm,
                                   total_rows=t, earlier=mixed)
        h1, xp, gates, eids, ranks, counts = mixed
        consumer_tile = tm_big if i == depth - 1 else tm if i % 2 == 0 else None
        split = consumer_tile is not None and t % (TAIL_PARTS * consumer_tile) == 0
        n_parts = TAIL_PARTS if split else 1
        yks = _moe(xp, eids, ranks, counts, moe_w1, moe_w3, moe_w2, i, n_parts)
        h = [_post_operands(h1, yk, gates, p_rows, row(norm_pl[i]), w_pg[i].astype(BF16),
                            w_pe[i].astype(BF16), p_row0=i * t, row0=s * (t // n_parts))
             for s, yk in enumerate(yks)]
    out = None
    for tail in h:
        out = _final(tail, row(final_norm), tm=tm_big, total_rows=t, out=out)
    return out.reshape(batch, seq, d)
```

```python
import functools
from typing import Callable, NamedTuple

import jax
import jax.numpy as jnp
from jax import lax
from jax.experimental import pallas as pl
from jax.experimental.pallas import tpu as pltpu
from jax.experimental.pallas import tpu_sc as plsc

EPS = 1e-6
HEAD_DIM = 64
HEADS_PER_SLAB = 2
ATTN_SLABS = 4
CHUNK = 128
N_GROUPS = 4
EXP_PER_GROUP = 8
N_EXPERTS = N_GROUPS * EXP_PER_GROUP
TOP_K = 2
ROUTER_LANES = 128
FINE_ROW0 = 8

V7X_VMEM_BYTES = 64 * 1024 * 1024
VMEM_LIMIT = V7X_VMEM_BYTES - 8 * 1024 * 1024
V7X_SC_CORES = 2
V7X_SC_SUBCORES = 16
SC_WORKERS = V7X_SC_CORES * V7X_SC_SUBCORES
SC_ROWS = 32

TOKEN_TILE = 512
LIGHT_TOKEN_TILE = 1024
Q_TILE = 128
SLOT_TILE = 512
X_RING = 3
TAIL_PARTS = 2

LOG_F32_TINY_BOUND = -88.0

F32 = jnp.float32
BF16 = jnp.bfloat16
U32 = jnp.uint32
I32 = jnp.int32


def _rms(x, g):
    return x * lax.rsqrt(jnp.mean(x * x, axis=-1, keepdims=True) + EPS) * g


def _const_spec(shape):
    nd = len(shape)
    return pl.BlockSpec(shape, lambda *_: (0,) * nd, pipeline_mode=pl.Buffered(1))


def _params(*sem):
    return pltpu.CompilerParams(dimension_semantics=sem, vmem_limit_bytes=VMEM_LIMIT)


def _even_in_kernel(h_ref, g_ref, w_ref, cw_ref, ya_ref, q_ref, k_ref, v_ref, zs_ref,
                    *, tiles_per_seq, a_width, sb_width):
    tm = h_ref.shape[0]
    xn = _rms(h_ref[...], g_ref[...]).astype(BF16)
    a3 = 3 * a_width

    @pl.when(pl.program_id(0) % tiles_per_seq == 0)
    def _():
        zs_ref[0:8, :] = jnp.zeros((8, a_width), F32)

    pa = jnp.dot(xn, w_ref[:, 0:a3], preferred_element_type=F32)
    z = pa[:, 2 * a_width:a3] * pa[:, 0:a_width]
    zs_ref[8:tm + 8, :] = z
    z1 = zs_ref[7:tm + 7, :]
    z2 = zs_ref[6:tm + 6, :]
    cw = cw_ref[...]
    conv = cw[0:1, :] * z2 + cw[1:2, :] * z1 + cw[2:3, :] * z
    ya_ref[...] = (pa[:, a_width:2 * a_width] * conv).astype(BF16)
    zs_ref[0:8, :] = zs_ref[tm:tm + 8, :]

    pq = jnp.dot(xn, w_ref[:, a3:a3 + 3 * sb_width], preferred_element_type=F32)
    q_ref[...] = (pq[:, 0:sb_width] * (HEAD_DIM ** -0.5)).astype(BF16)
    k_ref[...] = pq[:, sb_width:2 * sb_width].astype(BF16)
    v_ref[...] = pq[:, 2 * sb_width:3 * sb_width].astype(BF16)


def _even_in(h, g, w_in, conv_w, *, seq, tm):
    t, d = h.shape
    a_width = conv_w.shape[1]
    sb_width = (w_in.shape[1] - 3 * a_width) // 3
    kern = functools.partial(_even_in_kernel, tiles_per_seq=seq // tm, a_width=a_width,
                             sb_width=sb_width)
    row = lambda w: pl.BlockSpec((tm, w), lambda i: (i, 0))
    return pl.pallas_call(
        kern,
        grid=(t // tm,),
        in_specs=[row(d), _const_spec((1, d)), _const_spec(w_in.shape), _const_spec(conv_w.shape)],
        out_specs=[row(a_width), row(sb_width), row(sb_width), row(sb_width)],
        out_shape=[jax.ShapeDtypeStruct((t, a_width), BF16)] +
                  [jax.ShapeDtypeStruct((t, sb_width), BF16)] * 3,
        scratch_shapes=[pltpu.VMEM((tm + 8, a_width), F32)],
        compiler_params=_params("arbitrary"),
        name="even_in",
    )(h, g, w_in, conv_w)


def _attn_kernel(q_ref, k_ref, v_ref, o_ref, acc_ref, c_ref, *, tq):
    qi = pl.program_id(2)
    slab = HEADS_PER_SLAB * HEAD_DIM
    rows_slab = HEADS_PER_SLAB * tq
    rows = ATTN_SLABS * rows_slab
    lane = lax.broadcasted_iota(I32, (tq, slab), 1)
    qs = []
    for sl in range(ATTN_SLABS):
        q = q_ref[:, sl * slab:(sl + 1) * slab]
        zero = jnp.zeros_like(q)
        qs.append(jnp.concatenate([jnp.where(lane < HEAD_DIM, q, zero),
                                   jnp.where(lane >= HEAD_DIM, q, zero)], axis=0))
    def later_keys(width):
        r_i = lax.broadcasted_iota(I32, (width, width), 0)
        c_i = lax.broadcasted_iota(I32, (width, width), 1)
        return jnp.where(r_i > c_i, 1.0, 0.0).astype(BF16)

    def block(start, width, c, mask):
        start = pl.multiple_of(start, tq)
        upper = later_keys(width)
        kk = k_ref[pl.ds(start, width), :]
        vv = v_ref[pl.ds(start, width), :]
        z = jnp.concatenate(
            [lax.dot_general(qs[sl], kk[:, sl * slab:(sl + 1) * slab], (((1,), (1,)), ((), ())),
                             preferred_element_type=F32) for sl in range(ATTN_SLABS)], axis=0)
        sp = jnp.maximum(z, 0.0) + jnp.log(1.0 + jnp.exp(-jnp.abs(z)))
        spm = sp if mask is None else jnp.where(mask, sp, 0.0)
        hi = spm.astype(BF16)
        lo = (spm - hi.astype(F32)).astype(BF16)
        later = (jnp.dot(hi, upper, preferred_element_type=F32) +
                 jnp.dot(lo, upper, preferred_element_type=F32))
        a = jnp.exp(z - sp - later - c)
        if mask is not None:
            a = jnp.where(mask, a, 0.0)
        a = a.astype(BF16)
        out = jnp.concatenate(
            [jnp.dot(a[sl * rows_slab:(sl + 1) * rows_slab], vv[:, sl * slab:(sl + 1) * slab],
                     preferred_element_type=F32) for sl in range(ATTN_SLABS)], axis=0)
        return out, jnp.sum(spm, axis=1, keepdims=True)

    first = jnp.maximum(qi - 1, 0) * tq
    key_pos = lax.broadcasted_iota(I32, (tq, 2 * tq), 1) + (first - qi * tq)
    mask = jnp.concatenate([key_pos < lax.broadcasted_iota(I32, (tq, 2 * tq), 0)] * (rows // tq),
                           axis=0)
    out0, c2 = block(first, 2 * tq, jnp.zeros((rows, 1), F32), mask)
    acc_ref[...] = out0
    c_ref[...] = c2

    def cond(carry):
        kb, cmin = carry
        return jnp.logical_and(kb >= 0, cmin < -LOG_F32_TINY_BOUND)

    def body(carry):
        kb, _ = carry
        c = c_ref[...]
        out, ssum = block(kb * tq, tq, c, None)
        acc_ref[...] += out
        c_new = c + ssum
        c_ref[...] = c_new
        return kb - 1, jnp.min(c_new)

    lax.while_loop(cond, body, (qi - 2, jnp.min(c2)))
    acc = acc_ref[...]
    for sl in range(ATTN_SLABS):
        lo_r = sl * rows_slab
        o_ref[:, sl * slab:(sl + 1) * slab] = jnp.where(
            lane < HEAD_DIM, acc[lo_r:lo_r + tq], acc[lo_r + tq:lo_r + 2 * tq]).astype(o_ref.dtype)


def _attention(q, k, v, *, batch, seq, tq):
    t, w = q.shape
    wblk = ATTN_SLABS * HEADS_PER_SLAB * HEAD_DIM
    nq = seq // tq
    qspec = pl.BlockSpec((tq, wblk), lambda b, s, i: (b * nq + i, s))
    kvspec = pl.BlockSpec((seq, wblk), lambda b, s, i: (b, s))
    rows = ATTN_SLABS * HEADS_PER_SLAB * tq
    return pl.pallas_call(
        functools.partial(_attn_kernel, tq=tq),
        grid=(batch, w // wblk, nq),
        in_specs=[qspec, kvspec, kvspec],
        out_specs=qspec,
        out_shape=jax.ShapeDtypeStruct((t, w), BF16),
        scratch_shapes=[pltpu.VMEM((rows, HEADS_PER_SLAB * HEAD_DIM), F32),
                        pltpu.VMEM((rows, 1), F32)],
        compiler_params=_params("arbitrary", "arbitrary", "arbitrary"),
        name="stick_breaking_attention",
    )(q, k, v)


def _pack_bf16_pairs(x):
    n = x.shape[1] // 2
    bits = lax.bitcast_convert_type(x.astype(BF16).astype(F32), U32)
    return bits[:, 0:n] | (bits[:, n:] >> 16)


def _unpack_bf16_pairs(u):
    a = lax.bitcast_convert_type(u & jnp.uint32(0xFFFF0000), F32)
    b = lax.bitcast_convert_type(u << 16, F32)
    return jnp.concatenate([a, b], axis=1)


def _router_tail(h1, g_ref, wr_ref, br_ref, cnt0_ref,
                 h1_ref, xp_ref, gate_ref, eid_ref, rank_ref, cnt_out_ref, cnt_ref):
    h1_ref[...] = h1
    xn = _rms(h1, g_ref[...])
    xp_ref[...] = _pack_bf16_pairs(xn)
    x_hi = xn.astype(BF16)
    x_lo = (xn - x_hi.astype(F32)).astype(BF16)
    both = jnp.dot(x_hi, wr_ref[...], preferred_element_type=F32)
    logits = (both[:, 0:ROUTER_LANES] + both[:, ROUTER_LANES:] +
              jnp.dot(x_lo, wr_ref[:, 0:ROUTER_LANES], preferred_element_type=F32) + br_ref[...])
    lt = logits.T
    tm = lt.shape[1]
    row = lax.broadcasted_iota(I32, (EXP_PER_GROUP, tm), 0)
    neg = jnp.float32(-jnp.inf)

    coarse = jnp.where(row < N_GROUPS, lt[0:EXP_PER_GROUP], neg)
    cmax = jnp.max(coarse, axis=0, keepdims=True)
    gi = jnp.min(jnp.where(coarse == cmax, row, EXP_PER_GROUP), axis=0, keepdims=True)
    pg = 1.0 / jnp.sum(jnp.exp(coarse - cmax), axis=0, keepdims=True)

    fine = lt[FINE_ROW0:FINE_ROW0 + EXP_PER_GROUP]
    for grp in range(1, N_GROUPS):
        lo_r = FINE_ROW0 + grp * EXP_PER_GROUP
        fine = jnp.where(gi == grp, lt[lo_r:lo_r + EXP_PER_GROUP], fine)
    m1 = jnp.max(fine, axis=0, keepdims=True)
    i1 = jnp.min(jnp.where(fine == m1, row, EXP_PER_GROUP), axis=0, keepdims=True)
    rest = jnp.where(row == i1, neg, fine)
    m2 = jnp.max(rest, axis=0, keepdims=True)
    i2 = jnp.min(jnp.where(rest == m2, row, EXP_PER_GROUP), axis=0, keepdims=True)
    e21 = jnp.exp(m2 - m1)
    w1 = 1.0 / (1.0 + e21)
    gate_ref[...] = jnp.concatenate([pg * w1, pg * (e21 * w1)], axis=0)
    e0 = gi * EXP_PER_GROUP + i1
    e1 = gi * EXP_PER_GROUP + i2
    eid_ref[...] = jnp.concatenate([e0, e1], axis=0)

    @pl.when(pl.program_id(0) == 0)
    def _():
        cnt_ref[...] = cnt0_ref[...]

    erow = lax.broadcasted_iota(I32, (N_EXPERTS, tm), 0)
    oh0 = jnp.where(erow == e0, 1.0, 0.0)
    oh1 = jnp.where(erow == e1, 1.0, 0.0)
    r_i = lax.broadcasted_iota(I32, (tm, tm), 0)
    c_i = lax.broadcasted_iota(I32, (tm, tm), 1)
    before = jnp.where(r_i < c_i, 1.0, 0.0).astype(BF16)
    pre0 = jnp.dot(oh0.astype(BF16), before, preferred_element_type=F32)
    pre1 = jnp.dot(oh1.astype(BF16), before, preferred_element_type=F32)
    tot0 = jnp.sum(oh0, axis=1, keepdims=True)
    tot1 = jnp.sum(oh1, axis=1, keepdims=True)
    base = cnt_ref[:, 0:1]
    r0 = jnp.sum(oh0 * (pre0 + base), axis=0, keepdims=True)
    r1 = jnp.sum(oh1 * (pre1 + (base + tot0)), axis=0, keepdims=True)
    rank_ref[...] = jnp.concatenate([r0, r1], axis=0).astype(I32)
    total = jnp.broadcast_to(base + tot0 + tot1, cnt_ref.shape)
    cnt_ref[...] = total
    cnt_out_ref[...] = total


N_TAIL_ROW_OUTPUTS = 5


def _tail_specs(total_rows, d, tm, row0=0):
    blk0 = row0 // tm
    t = total_rows
    row = lambda w: pl.BlockSpec((tm, w), lambda i: (i + blk0, 0))
    lanes = pl.BlockSpec((TOP_K, tm), lambda i: (0, i + blk0))
    cnt = pl.BlockSpec((N_EXPERTS, ROUTER_LANES), lambda i: (0, 0))
    out_specs = [row(d), row(d // 2), lanes, lanes, lanes, cnt]
    out_shape = [jax.ShapeDtypeStruct((t, d), F32), jax.ShapeDtypeStruct((t, d // 2), U32),
                 jax.ShapeDtypeStruct((TOP_K, t), F32), jax.ShapeDtypeStruct((TOP_K, t), I32),
                 jax.ShapeDtypeStruct((TOP_K, t), I32),
                 jax.ShapeDtypeStruct((N_EXPERTS, ROUTER_LANES), F32)]
    scratch = [pltpu.VMEM((N_EXPERTS, ROUTER_LANES), F32)]
    return out_specs, out_shape, scratch


def _no_slots():
    return jnp.zeros((N_EXPERTS, ROUTER_LANES), F32)


def _even_out_kernel(ya_ref, yb_ref, h_ref, w_ref, g_ref, wr_ref, br_ref, cnt0_ref, *tail_refs):
    a_width = ya_ref.shape[1]
    mix = (jnp.dot(ya_ref[...], w_ref[0:a_width, :], preferred_element_type=F32) +
           jnp.dot(yb_ref[...], w_ref[a_width:, :], preferred_element_type=F32))
    _router_tail(h_ref[...] + mix, g_ref, wr_ref, br_ref, cnt0_ref, *tail_refs)


def _even_out(ya, yb, h, w_out, g_ffn, wr, br, *, tm):
    t, d = h.shape
    out_specs, out_shape, scratch = _tail_specs(t, d, tm)
    row = lambda w: pl.BlockSpec((tm, w), lambda i: (i, 0))
    cnt0 = _no_slots()
    return pl.pallas_call(
        _even_out_kernel,
        grid=(t // tm,),
        in_specs=[row(ya.shape[1]), row(yb.shape[1]), row(d), _const_spec(w_out.shape),
                  _const_spec((1, d)), _const_spec(wr.shape), _const_spec(br.shape),
                  _const_spec(cnt0.shape)],
        out_specs=out_specs,
        out_shape=out_shape,
        scratch_shapes=scratch,
        compiler_params=_params("arbitrary"),
        name="even_out_router",
    )(ya, yb, h, w_out, g_ffn, wr, br, cnt0)


def _gelu_tanh(x):
    return 0.5 * x * (1.0 + jnp.tanh(0.7978845608028654 * (x + 0.044715 * (x * x * x))))


def _odd_kernel(*refs, n_prev, n_alias):
    h_refs, refs = refs[:max(n_prev, 1)], refs[max(n_prev, 1):]
    (g_ref, w_in_ref, gv_ref, ws_ref, bs_ref, w_out_ref, gf_ref, wr_ref, br_ref, cnt0_ref,
     *rest) = refs
    *tail_refs, y_ref = rest[n_alias:]
    h = _post_value(*h_refs) if n_prev else h_refs[0][...]
    tm = h.shape[0]
    c_width = gv_ref.shape[1]
    n_grp = ws_ref.shape[0]
    gdim = c_width // n_grp
    xn = _rms(h, g_ref[...]).astype(BF16)
    u = _gelu_tanh(jnp.dot(xn, w_in_ref[:, 0:c_width], preferred_element_type=F32))
    v = _gelu_tanh(jnp.dot(xn, w_in_ref[:, c_width:], preferred_element_type=F32))
    v = _rms(v, gv_ref[...]).astype(BF16)
    r_i = lax.broadcasted_iota(I32, (CHUNK, CHUNK), 0)
    c_i = lax.broadcasted_iota(I32, (CHUNK, CHUNK), 1)
    causal = c_i <= r_i
    bs = bs_ref[...]
    for grp in range(n_grp):
        w_m = jnp.where(causal, ws_ref[grp], 0.0).astype(BF16)
        bias = bs[:, grp:grp + 1]
        cols = slice(grp * gdim, (grp + 1) * gdim)
        for c in range(tm // CHUNK):
            rows = slice(c * CHUNK, (c + 1) * CHUNK)
            gate = jnp.dot(w_m, v[rows, cols], preferred_element_type=F32) + bias
            y_ref[rows, cols] = (u[rows, cols] * gate).astype(BF16)
    mix = jnp.dot(y_ref[...], w_out_ref[...], preferred_element_type=F32)
    _router_tail(h + mix, gf_ref, wr_ref, br_ref, cnt0_ref, *tail_refs)


def _odd_mixer(h_or_prev, g_mix, w_in, g_v, w_s, b_s_t, w_out, g_ffn, wr, br, *, tm,
               total_rows=None, earlier=None):
    cnt0 = _no_slots() if earlier is None else earlier[N_TAIL_ROW_OUTPUTS]
    consts = [g_mix, w_in, g_v, w_s, b_s_t, w_out, g_ffn, wr, br, cnt0]
    pending = isinstance(h_or_prev, PendingTail)
    if pending:
        head_specs, head_args = h_or_prev.operands(tm)
        t, d, row0 = h_or_prev.rows, h_or_prev.width, h_or_prev.row0
    else:
        t, d = h_or_prev.shape
        row0 = 0
        head_specs, head_args = [pl.BlockSpec((tm, d), lambda i: (i, 0))], [h_or_prev]
    out_specs, out_shape, scratch = _tail_specs(total_rows or t, d, tm, row0)
    n_prev = len(head_args) if pending else 0
    in_specs = head_specs + [_const_spec(c.shape) for c in consts]
    operands = [*head_args, *consts]
    aliases = {}
    if earlier is not None:
        aliases = {len(operands) + k: k for k in range(N_TAIL_ROW_OUTPUTS)}
        in_specs += [pl.BlockSpec(memory_space=pl.ANY)] * N_TAIL_ROW_OUTPUTS
        operands += list(earlier[:N_TAIL_ROW_OUTPUTS])
    return pl.pallas_call(
        functools.partial(_odd_kernel, n_prev=n_prev, n_alias=len(aliases)),
        grid=(t // tm,),
        in_specs=in_specs,
        out_specs=out_specs,
        out_shape=out_shape,
        input_output_aliases=aliases,
        scratch_shapes=scratch + [pltpu.VMEM((tm, g_v.shape[1]), BF16)],
        compiler_params=_params("arbitrary"),
        name="odd_mixer_router",
    )(*operands)


def _sc_mesh():
    return plsc.VectorSubcoreMesh(core_axis_name="c", subcore_axis_name="s")


def _sc_worker_base(rows_per_worker):
    wid = lax.axis_index("s") * V7X_SC_CORES + lax.axis_index("c")
    return wid * rows_per_worker


def _dispatch(xp, dest, n_slots):
    t, w = xp.shape
    per_worker = t // SC_WORKERS
    n_chunks = per_worker // SC_ROWS
    assert per_worker * SC_WORKERS == t and n_chunks * SC_ROWS == per_worker and n_chunks % 2 == 0

    idx_t = pltpu.VMEM((SC_ROWS,), I32)

    @functools.partial(
        pl.kernel, mesh=_sc_mesh(),
        out_type=jax.ShapeDtypeStruct((n_slots, w), xp.dtype),
        scratch_types=[pltpu.VMEM((2, SC_ROWS, w), xp.dtype), idx_t, idx_t, idx_t, idx_t,
                       pltpu.SemaphoreType.DMA((2,)), pltpu.SemaphoreType.DMA((2,))],
    )
    def kern(x_hbm, d_hbm, out_hbm, rows_v, i00, i01, i10, i11, rsem, wsem):
        base = _sc_worker_base(per_worker)
        idx = ((i00, i01), (i10, i11))

        def load(c, b):
            off = pl.multiple_of(base + c * SC_ROWS, 8)
            pltpu.async_copy(x_hbm.at[pl.ds(off, SC_ROWS)], rows_v.at[b], rsem.at[b])
            for k in range(TOP_K):
                pltpu.sync_copy(d_hbm.at[k, pl.ds(off, SC_ROWS)], idx[b][k])

        def wait_load(b):
            pltpu.make_async_copy(x_hbm.at[pl.ds(0, SC_ROWS)], rows_v.at[b], rsem.at[b]).wait()

        def scatter(b):
            for k in range(TOP_K):
                pltpu.async_copy(rows_v.at[b], out_hbm.at[idx[b][k]], wsem.at[b])

        def wait_scatter(b):
            for k in range(TOP_K):
                pltpu.make_async_copy(rows_v.at[b], out_hbm.at[idx[b][k]], wsem.at[b]).wait()

        load(0, 0)

        @pl.loop(0, n_chunks, step=2)
        def _(c0):
            for b in range(2):
                c = c0 + b
                wait_load(b)
                scatter(b)

                @pl.when(c + 1 < n_chunks)
                def _():
                    @pl.when(c >= 1)
                    def _():
                        wait_scatter(1 - b)
                    load(c + 1, 1 - b)

        wait_scatter(0)
        wait_scatter(1)

    return kern(xp, dest)


def _combine(y, idx):
    n = idx.shape[0]
    w = y.shape[1]
    rows = 2 * SC_ROWS
    per_worker = n // SC_WORKERS
    n_chunks = per_worker // rows
    assert per_worker * SC_WORKERS == n and n_chunks * rows == per_worker and n_chunks % 2 == 0

    idx_t = pltpu.VMEM((rows,), I32)

    @functools.partial(
        pl.kernel, mesh=_sc_mesh(),
        out_type=jax.ShapeDtypeStruct((n, w), y.dtype),
        scratch_types=[pltpu.VMEM((2, rows, w), y.dtype), idx_t, idx_t,
                       pltpu.SemaphoreType.DMA((2,)), pltpu.SemaphoreType.DMA((2,))],
    )
    def kern(y_hbm, i_hbm, out_hbm, rows_v, i0, i1, gsem, wsem):
        base = _sc_worker_base(per_worker)
        ibuf = (i0, i1)

        def gather(c, b):
            off = pl.multiple_of(base + c * rows, 8)
            pltpu.sync_copy(i_hbm.at[pl.ds(off, rows)], ibuf[b])
            pltpu.async_copy(y_hbm.at[ibuf[b]], rows_v.at[b], gsem.at[b])

        def wait_gather(b):
            pltpu.make_async_copy(y_hbm.at[ibuf[b]], rows_v.at[b], gsem.at[b]).wait()

        def write(c, b):
            off = pl.multiple_of(base + c * rows, 8)
            pltpu.async_copy(rows_v.at[b], out_hbm.at[pl.ds(off, rows)], wsem.at[b])

        def wait_write(b):
            pltpu.make_async_copy(rows_v.at[b], out_hbm.at[pl.ds(0, rows)], wsem.at[b]).wait()

        gather(0, 0)

        @pl.loop(0, n_chunks, step=2)
        def _(c0):
            for b in range(2):
                c = c0 + b

                @pl.when(c + 1 < n_chunks)
                def _():
                    @pl.when(c >= 1)
                    def _():
                        wait_write(1 - b)
                    gather(c + 1, 1 - b)

                wait_gather(b)
                write(c, b)

        wait_write(0)
        wait_write(1)

    return kern(y, idx)


def _expert_kernel(te_ref, tv_ref, nu_ref, tg_ref, tn_ref, x_hbm, w1_hbm, w3_hbm, w2_hbm, y_ref,
                   xbuf_ref, xsem, w1f_ref, w3f_ref, w2f_ref, wsem, w1b_ref, w3b_ref, w2b_ref,
                   *, layer):
    i = pl.program_id(0)
    n_used = nu_ref[0]
    ts = xbuf_ref.shape[1]

    def x_copy(tile):
        slot = tile % X_RING
        rows = pl.ds(pl.multiple_of(tile * ts, ts), ts)
        return pltpu.make_async_copy(x_hbm.at[rows], xbuf_ref.at[slot], xsem.at[slot])

    def w_copies(expert, slot):
        pairs = ((w1_hbm, w1f_ref), (w3_hbm, w3f_ref), (w2_hbm, w2f_ref))
        return [pltpu.make_async_copy(src.at[layer, expert], dst.at[slot], wsem.at[slot, j])
                for j, (src, dst) in enumerate(pairs)]

    @pl.when(jnp.logical_and(i == 0, n_used > 0))
    def _():
        for tile in range(X_RING - 1):
            @pl.when(tile < n_used)
            def _():
                x_copy(tile).start()
        for c in w_copies(te_ref[0], 0):
            c.start()

    @pl.when(i < n_used)
    def _():
        @pl.when(i + (X_RING - 1) < n_used)
        def _():
            x_copy(i + (X_RING - 1)).start()

        prev = te_ref[jnp.maximum(i - 1, 0)]

        @pl.when(jnp.logical_or(i == 0, te_ref[i] != prev))
        def _():
            slot = tg_ref[i] % 2
            for c in w_copies(te_ref[i], slot):
                c.wait()
            nxt = tn_ref[i]

            @pl.when(nxt >= 0)
            def _():
                for c in w_copies(nxt, 1 - slot):
                    c.start()

            w1b_ref[...] = w1f_ref[slot].astype(BF16)
            w3b_ref[...] = w3f_ref[slot].astype(BF16)
            w2b_ref[...] = w2f_ref[slot].astype(BF16)

        x_copy(i).wait()
        xu = xbuf_ref[i % X_RING]
        rowid = lax.broadcasted_iota(I32, xu.shape, 0)
        xu = jnp.where(rowid < tv_ref[i], xu, jnp.zeros_like(xu))
        x = _unpack_bf16_pairs(xu).astype(BF16)
        h1 = jnp.dot(x, w1b_ref[...], preferred_element_type=F32)
        h3 = jnp.dot(x, w3b_ref[...], preferred_element_type=F32)
        he = (h1 * jax.nn.sigmoid(h1)) * h3
        y = jnp.dot(he.astype(BF16), w2b_ref[...], preferred_element_type=F32)
        y_ref[...] = _pack_bf16_pairs(y)


def _experts(x_sorted, plan, w1, w3, w2, layer):
    n_slots, wp = x_sorted.shape
    _, n_e, d, d_e = w1.shape
    ts = SLOT_TILE
    any_spec = pl.BlockSpec(memory_space=pl.ANY)
    yspec = pl.BlockSpec((ts, wp),
                         lambda i, te, tv, nu, tg, tn: (jnp.minimum(i, jnp.maximum(nu[0] - 1, 0)), 0))
    return pl.pallas_call(
        functools.partial(_expert_kernel, layer=layer),
        grid_spec=pltpu.PrefetchScalarGridSpec(
            num_scalar_prefetch=len(plan),
            grid=(n_slots // ts,),
            in_specs=[any_spec] * 4,
            out_specs=yspec,
            scratch_shapes=[pltpu.VMEM((X_RING, ts, wp), x_sorted.dtype),
                            pltpu.SemaphoreType.DMA((X_RING,)),
                            pltpu.VMEM((2, d, d_e), F32), pltpu.VMEM((2, d, d_e), F32),
                            pltpu.VMEM((2, d_e, d), F32), pltpu.SemaphoreType.DMA((2, 3)),
                            pltpu.VMEM((d, d_e), BF16), pltpu.VMEM((d, d_e), BF16),
                            pltpu.VMEM((d_e, d), BF16)],
        ),
        out_shape=jax.ShapeDtypeStruct((n_slots, d // 2), U32),
        compiler_params=_params("arbitrary"),
        name="moe_experts",
    )(*plan, x_sorted, w1, w3, w2)


def _slot_plan(eids, ranks, counts, n_tiles):
    ts = SLOT_TILE
    cnt = counts[:, 0].astype(I32)
    padded = (cnt + ts - 1) // ts * ts
    e_ids = jnp.arange(N_EXPERTS, dtype=I32)
    ends = jnp.sum(jnp.where(e_ids[:, None] <= e_ids[None, :], padded[:, None], 0), axis=0)
    starts = ends - padded
    onehot = eids[None] == e_ids[:, None, None]
    dest = ranks + jnp.sum(jnp.where(onehot, starts[:, None, None], 0), axis=0)
    tile_lo = jnp.arange(n_tiles, dtype=I32) * ts
    tile_e = jnp.minimum(jnp.sum(tile_lo[:, None] >= ends[None, :], axis=1), N_EXPERTS - 1)
    tile_e = tile_e.astype(I32)
    valid_end = jnp.sum(jnp.where(tile_e[:, None] == e_ids[None, :], (starts + cnt)[None, :], 0),
                        axis=1)
    tile_valid = jnp.clip(valid_end - tile_lo, 0, ts).astype(I32)
    n_used = (ends[-1:] // ts).astype(I32)
    nonempty = cnt > 0
    earlier = jnp.logical_and(nonempty[None, :], e_ids[None, :] < tile_e[:, None])
    tile_grp = jnp.sum(earlier, axis=1).astype(I32)
    later = jnp.logical_and(nonempty[None, :], e_ids[None, :] > tile_e[:, None])
    nxt = jnp.min(jnp.where(later, e_ids[None, :], N_EXPERTS), axis=1)
    tile_next = jnp.where(nxt < N_EXPERTS, nxt, -1).astype(I32)
    return dest, (tile_e, tile_valid, n_used, tile_grp, tile_next)


class PendingTail(NamedTuple):
    operands: Callable
    rows: int
    row0: int
    width: int


def _post_value(h1_ref, y0_ref, y1_ref, gate_ref, p_ref, g_ref, wpg_ref, wpe_ref):
    gates = gate_ref[...]
    tm = gates.shape[1]
    gt = jnp.concatenate([gates, jnp.zeros((8 - TOP_K, tm), F32)], axis=0).T
    h2 = h1_ref[...] + (gt[:, 0:1] * _unpack_bf16_pairs(y0_ref[...]) +
                        gt[:, 1:2] * _unpack_bf16_pairs(y1_ref[...]))
    gate = jax.nn.sigmoid(jnp.dot(_rms(h2, g_ref[...]).astype(BF16), wpg_ref[...],
                                  preferred_element_type=F32))
    pe = jnp.dot(p_ref[...].astype(BF16), wpe_ref[...], preferred_element_type=F32)
    return h2 + gate * pe


def _post_operands(h1, yk, gates, p, g_pl, w_pg, w_pe, *, p_row0, row0=0):
    d = h1.shape[1]
    t = yk.shape[0] // TOP_K
    wy = yk.shape[1]

    def operands(tm):
        nt = t // tm
        blk0 = row0 // tm
        p_blk0 = (p_row0 + row0) // tm
        specs = [pl.BlockSpec((tm, d), lambda i: (i + blk0, 0)),
                 pl.BlockSpec((tm, wy), lambda i: (i, 0)),
                 pl.BlockSpec((tm, wy), lambda i: (i + nt, 0)),
                 pl.BlockSpec((TOP_K, tm), lambda i: (0, i + blk0)),
                 pl.BlockSpec((tm, p.shape[1]), lambda i: (i + p_blk0, 0)),
                 _const_spec((1, d)), _const_spec(w_pg.shape), _const_spec(w_pe.shape)]
        return specs, [h1, yk, yk, gates, p, g_pl, w_pg, w_pe]

    return PendingTail(operands, t, row0, d)


def _post_kernel(*refs):
    *post_refs, o_ref = refs
    o_ref[...] = _post_value(*post_refs)


def _post(pending, *, tm):
    specs, args = pending.operands(tm)
    t, d = pending.rows, pending.width
    return pl.pallas_call(
        _post_kernel,
        grid=(t // tm,),
        in_specs=specs,
        out_specs=pl.BlockSpec((tm, d), lambda i: (i, 0)),
        out_shape=jax.ShapeDtypeStruct((t, d), F32),
        compiler_params=_params("arbitrary"),
        name="layer_post",
    )(*args)


def _final_kernel(*refs, n_post):
    post_refs, gfin_ref, o_ref = refs[:n_post], refs[n_post], refs[-1]
    o_ref[...] = _rms(_post_value(*post_refs), gfin_ref[...])


def _final(pending, g_fin, *, tm, total_rows, out=None):
    specs, args = pending.operands(tm)
    d = pending.width
    blk0 = pending.row0 // tm
    in_specs = specs + [_const_spec((1, d))]
    operands = [*args, g_fin]
    aliases = {}
    if out is not None:
        aliases = {len(operands): 0}
        in_specs.append(pl.BlockSpec(memory_space=pl.ANY))
        operands.append(out)
    return pl.pallas_call(
        functools.partial(_final_kernel, n_post=len(args)),
        grid=(pending.rows // tm,),
        in_specs=in_specs,
        out_specs=pl.BlockSpec((tm, d), lambda i: (i + blk0, 0)),
        out_shape=jax.ShapeDtypeStruct((total_rows, d), F32),
        input_output_aliases=aliases,
        compiler_params=_params("arbitrary"),
        name="final_post",
    )(*operands)


def _router_weights(wc, bc, wf, bf):
    d = wc.shape[0]
    w = jnp.zeros((d, ROUTER_LANES), F32)
    w = w.at[:, 0:N_GROUPS].set(wc).at[:, FINE_ROW0:FINE_ROW0 + N_EXPERTS].set(wf)
    b = jnp.zeros((1, ROUTER_LANES), F32)
    b = b.at[0, 0:N_GROUPS].set(bc).at[0, FINE_ROW0:FINE_ROW0 + N_EXPERTS].set(bf)
    hi = w.astype(BF16)
    lo = (w - hi.astype(F32)).astype(BF16)
    return jnp.concatenate([hi, lo], axis=1), b


def _moe(xp, eids, ranks, counts, w1, w3, w2, layer, n_parts):
    t = xp.shape[0]
    n_tiles = TOP_K * t // SLOT_TILE + N_EXPERTS
    dest, plan = _slot_plan(eids, ranks, counts, n_tiles)
    x_sorted = _dispatch(xp, dest, n_tiles * SLOT_TILE)
    y_sorted = _experts(x_sorted, plan, w1, w3, w2, layer)
    tp = t // n_parts
    return [_combine(y_sorted, dest[:, s * tp:(s + 1) * tp].reshape(-1)) for s in range(n_parts)]


def kernel(x, p, norm_mix, norm_ffn, norm_pl, final_norm, w_in_even, conv_w_even, w_out_even, w_in_odd, g_v_odd, w_s_odd, b_s_odd, w_out_odd, router_c, router_c_b, router_f, router_f_b, moe_w1, moe_w3, moe_w2, w_pe, w_pg):
    batch, seq, d = x.shape
    depth = p.shape[0]
    t = batch * seq
    tm = min(TOKEN_TILE, seq)
    tm_big = min(LIGHT_TOKEN_TILE, seq)
    p_rows = p.reshape(depth * t, -1)
    row = lambda a: a.reshape(1, -1)
    h = x.reshape(t, d)
    for i in range(depth):
        j = i // 2
        wr, br = _router_weights(router_c[i], router_c_b[i], router_f[i], router_f_b[i])
        g_ffn = row(norm_ffn[i])
        if i % 2 == 0:
            if isinstance(h, list):
                h = _post(h[0], tm=tm_big)
            ya, q, k, v = _even_in(h, row(norm_mix[i]), w_in_even[j].astype(BF16), conv_w_even[j],
                                   seq=seq, tm=tm_big)
            yb = _attention(q, k, v, batch=batch, seq=seq, tq=min(Q_TILE, seq))
            mixed = _even_out(ya, yb, h, w_out_even[j].astype(BF16), g_ffn, wr, br, tm=tm_big)
        else:
            mixed = None
            for part in (h if isinstance(h, list) else [h]):
                mixed = _odd_mixer(part, row(norm_mix[i]), w_in_odd[j].astype(BF16),
                                   row(g_v_odd[j]), w_s_odd[j], b_s_odd[j].T,
                                   w_out_odd[j].astype(BF16), g_ffn, wr, br, tm=tm,
                                   total_rows=t, earlier=mixed)
        h1, xp, gates, eids, ranks, counts = mixed
        consumer_tile = tm_big if i == depth - 1 else tm if i % 2 == 0 else None
        split = consumer_tile is not None and t % (TAIL_PARTS * consumer_tile) == 0
        n_parts = TAIL_PARTS if split else 1
        yks = _moe(xp, eids, ranks, counts, moe_w1, moe_w3, moe_w2, i, n_parts)
        h = [_post_operands(h1, yk, gates, p_rows, row(norm_pl[i]), w_pg[i].astype(BF16),
                            w_pe[i].astype(BF16), p_row0=i * t, row0=s * (t // n_parts))
             for s, yk in enumerate(yks)]
    out = None
    for tail in h:
        out = _final(tail, row(final_norm), tm=tm_big, total_rows=t, out=out)
    return out.reshape(batch, seq, d)
```

```python
import functools
from typing import Callable, NamedTuple

import jax
import jax.numpy as jnp
from jax import lax
from jax.experimental import pallas as pl
from jax.experimental.pallas import tpu as pltpu
from jax.experimental.pallas import tpu_sc as plsc

EPS = 1e-6
HEAD_DIM = 64
HEADS_PER_SLAB = 2
ATTN_SLABS = 4
CHUNK = 128
N_GROUPS = 4
EXP_PER_GROUP = 8
N_EXPERTS = N_GROUPS * EXP_PER_GROUP
TOP_K = 2
ROUTER_LANES = 128
FINE_ROW0 = 8

V7X_VMEM_BYTES = 64 * 1024 * 1024
VMEM_LIMIT = V7X_VMEM_BYTES - 8 * 1024 * 1024
V7X_SC_CORES = 2
V7X_SC_SUBCORES = 16
SC_WORKERS = V7X_SC_CORES * V7X_SC_SUBCORES
SC_ROWS = 32

TOKEN_TILE = 512
LIGHT_TOKEN_TILE = 1024
Q_TILE = 128
SLOT_TILE = 512
X_RING = 3
TAIL_PARTS = 2

LOG_F32_TINY_BOUND = -88.0
OFF_PENALTY = 1e30
Q_SUBTILES = 2

F32 = jnp.float32
BF16 = jnp.bfloat16
U32 = jnp.uint32
I32 = jnp.int32


def _rms(x, g):
    return x * lax.rsqrt(jnp.mean(x * x, axis=-1, keepdims=True) + EPS) * g


def _const_spec(shape):
    nd = len(shape)
    return pl.BlockSpec(shape, lambda *_: (0,) * nd, pipeline_mode=pl.Buffered(1))


def _params(*sem):
    return pltpu.CompilerParams(dimension_semantics=sem, vmem_limit_bytes=VMEM_LIMIT)


def _even_in_kernel(h_ref, g_ref, w_ref, cw_ref, ya_ref, q_ref, k_ref, v_ref, zs_ref,
                    *, tiles_per_seq, a_width, sb_width):
    tm = h_ref.shape[0]
    xn = _rms(h_ref[...], g_ref[...]).astype(BF16)
    a3 = 3 * a_width

    @pl.when(pl.program_id(0) % tiles_per_seq == 0)
    def _():
        zs_ref[0:8, :] = jnp.zeros((8, a_width), F32)

    pa = jnp.dot(xn, w_ref[:, 0:a3], preferred_element_type=F32)
    z = pa[:, 2 * a_width:a3] * pa[:, 0:a_width]
    zs_ref[8:tm + 8, :] = z
    z1 = zs_ref[7:tm + 7, :]
    z2 = zs_ref[6:tm + 6, :]
    cw = cw_ref[...]
    conv = cw[0:1, :] * z2 + cw[1:2, :] * z1 + cw[2:3, :] * z
    ya_ref[...] = (pa[:, a_width:2 * a_width] * conv).astype(BF16)
    zs_ref[0:8, :] = zs_ref[tm:tm + 8, :]

    pq = jnp.dot(xn, w_ref[:, a3:a3 + 3 * sb_width], preferred_element_type=F32)
    q_ref[...] = (pq[:, 0:sb_width] * (HEAD_DIM ** -0.5)).astype(BF16)
    k_ref[...] = pq[:, sb_width:2 * sb_width].astype(BF16)
    v_ref[...] = pq[:, 2 * sb_width:3 * sb_width].astype(BF16)


def _even_in(h, g, w_in, conv_w, *, seq, tm):
    t, d = h.shape
    a_width = conv_w.shape[1]
    sb_width = (w_in.shape[1] - 3 * a_width) // 3
    kern = functools.partial(_even_in_kernel, tiles_per_seq=seq // tm, a_width=a_width,
                             sb_width=sb_width)
    row = lambda w: pl.BlockSpec((tm, w), lambda i: (i, 0))
    return pl.pallas_call(
        kern,
        grid=(t // tm,),
        in_specs=[row(d), _const_spec((1, d)), _const_spec(w_in.shape), _const_spec(conv_w.shape)],
        out_specs=[row(a_width), row(sb_width), row(sb_width), row(sb_width)],
        out_shape=[jax.ShapeDtypeStruct((t, a_width), BF16)] +
                  [jax.ShapeDtypeStruct((t, sb_width), BF16)] * 3,
        scratch_shapes=[pltpu.VMEM((tm + 8, a_width), F32)],
        compiler_params=_params("arbitrary"),
        name="even_in",
    )(h, g, w_in, conv_w)


def _attn_kernel(q_ref, k_ref, v_ref, o_ref, acc_ref, c_ref, *, tq):
    qi = pl.program_id(2)
    slab = HEADS_PER_SLAB * HEAD_DIM
    rows_unit = HEADS_PER_SLAB * tq
    rows_tile = ATTN_SLABS * rows_unit
    rows = Q_SUBTILES * rows_tile
    units = [(j, sl) for j in range(Q_SUBTILES) for sl in range(ATTN_SLABS)]
    lane = lax.broadcasted_iota(I32, (tq, slab), 1)
    qs = []
    for j, sl in units:
        q = q_ref[j * tq:(j + 1) * tq, sl * slab:(sl + 1) * slab]
        zero = jnp.zeros_like(q)
        qs.append(jnp.concatenate([jnp.where(lane < HEAD_DIM, q, zero),
                                   jnp.where(lane >= HEAD_DIM, q, zero)], axis=0))

    def later_keys(width):
        r_i = lax.broadcasted_iota(I32, (width, width), 0)
        c_i = lax.broadcasted_iota(I32, (width, width), 1)
        return jnp.where(r_i > c_i, 1.0, 0.0).astype(BF16)

    def per_tile_column(vals):
        return jnp.concatenate([jnp.full((rows_tile, 1), v, F32) for v in vals], axis=0)

    def block(starts, width, c, mask):
        upper = later_keys(width)
        keys = [pl.ds(pl.multiple_of(s, tq), width) for s in starts]
        z = jnp.concatenate(
            [lax.dot_general(qs[u], k_ref[keys[j], sl * slab:(sl + 1) * slab],
                             (((1,), (1,)), ((), ())), preferred_element_type=F32)
             for u, (j, sl) in enumerate(units)], axis=0)
        sp = jnp.maximum(z, 0.0) + jnp.log(1.0 + jnp.exp(-jnp.abs(z)))
        spm = sp if mask is None else jnp.where(mask, sp, 0.0)
        hi = spm.astype(BF16)
        lo = (spm - hi.astype(F32)).astype(BF16)
        later = (jnp.dot(hi, upper, preferred_element_type=F32) +
                 jnp.dot(lo, upper, preferred_element_type=F32))
        a = jnp.exp(z - sp - later - c)
        if mask is not None:
            a = jnp.where(mask, a, 0.0)
        a = a.astype(BF16)
        out = jnp.concatenate(
            [jnp.dot(a[u * rows_unit:(u + 1) * rows_unit], v_ref[keys[j], sl * slab:(sl + 1) * slab],
                     preferred_element_type=F32) for u, (j, sl) in enumerate(units)], axis=0)
        return out, jnp.sum(spm, axis=1, keepdims=True)

    tile0 = qi * Q_SUBTILES
    firsts = [jnp.maximum(tile0 + j - 1, 0) * tq for j in range(Q_SUBTILES)]
    col = lax.broadcasted_iota(I32, (tq, 2 * tq), 1)
    row = lax.broadcasted_iota(I32, (tq, 2 * tq), 0)
    mask = jnp.concatenate(
        [col + (firsts[j] - (tile0 + j) * tq) < row
         for j in range(Q_SUBTILES) for _ in range(rows_tile // tq)], axis=0)
    out0, c2 = block(firsts, 2 * tq, jnp.zeros((rows, 1), F32), mask)
    acc_ref[...] = out0
    c_ref[...] = c2

    def cond(carry):
        n, cmin = carry
        return jnp.logical_and(tile0 + Q_SUBTILES - 3 - n >= 0, cmin < -LOG_F32_TINY_BOUND)

    def body(carry):
        n, _ = carry
        kbs = [tile0 + j - 2 - n for j in range(Q_SUBTILES)]
        c = c_ref[...]
        off = per_tile_column([jnp.where(kb >= 0, 0.0, OFF_PENALTY) for kb in kbs])
        out, ssum = block([jnp.maximum(kb, 0) * tq for kb in kbs], tq, c + off, None)
        acc_ref[...] += out
        c_new = c + ssum
        c_ref[...] = c_new
        return n + 1, jnp.min(c_new)

    lax.while_loop(cond, body, (jnp.int32(0), jnp.min(c2)))
    acc = acc_ref[...]
    for u, (j, sl) in enumerate(units):
        lo_r = u * rows_unit
        o_ref[j * tq:(j + 1) * tq, sl * slab:(sl + 1) * slab] = jnp.where(
            lane < HEAD_DIM, acc[lo_r:lo_r + tq], acc[lo_r + tq:lo_r + 2 * tq]).astype(o_ref.dtype)


def _attention(q, k, v, *, batch, seq, tq):
    t, w = q.shape
    wblk = ATTN_SLABS * HEADS_PER_SLAB * HEAD_DIM
    nq = seq // (Q_SUBTILES * tq)
    qspec = pl.BlockSpec((Q_SUBTILES * tq, wblk), lambda b, s, i: (b * nq + i, s))
    kvspec = pl.BlockSpec((seq, wblk), lambda b, s, i: (b, s))
    rows = Q_SUBTILES * ATTN_SLABS * HEADS_PER_SLAB * tq
    return pl.pallas_call(
        functools.partial(_attn_kernel, tq=tq),
        grid=(batch, w // wblk, nq),
        in_specs=[qspec, kvspec, kvspec],
        out_specs=qspec,
        out_shape=jax.ShapeDtypeStruct((t, w), BF16),
        scratch_shapes=[pltpu.VMEM((rows, HEADS_PER_SLAB * HEAD_DIM), F32),
                        pltpu.VMEM((rows, 1), F32)],
        compiler_params=_params("arbitrary", "arbitrary", "arbitrary"),
        name="stick_breaking_attention",
    )(q, k, v)


def _pack_bf16_pairs(x):
    n = x.shape[1] // 2
    bits = lax.bitcast_convert_type(x.astype(BF16).astype(F32), U32)
    return bits[:, 0:n] | (bits[:, n:] >> 16)


def _unpack_bf16_pairs(u):
    a = lax.bitcast_convert_type(u & jnp.uint32(0xFFFF0000), F32)
    b = lax.bitcast_convert_type(u << 16, F32)
    return jnp.concatenate([a, b], axis=1)


def _router_tail(h1, g_ref, wr_ref, br_ref, cnt0_ref,
                 h1_ref, xp_ref, gate_ref, eid_ref, rank_ref, cnt_out_ref, cnt_ref):
    h1_ref[...] = h1
    xn = _rms(h1, g_ref[...])
    xp_ref[...] = _pack_bf16_pairs(xn)
    x_hi = xn.astype(BF16)
    x_lo = (xn - x_hi.astype(F32)).astype(BF16)
    both = jnp.dot(x_hi, wr_ref[...], preferred_element_type=F32)
    logits = (both[:, 0:ROUTER_LANES] + both[:, ROUTER_LANES:] +
              jnp.dot(x_lo, wr_ref[:, 0:ROUTER_LANES], preferred_element_type=F32) + br_ref[...])
    lt = logits.T
    tm = lt.shape[1]
    row = lax.broadcasted_iota(I32, (EXP_PER_GROUP, tm), 0)
    neg = jnp.float32(-jnp.inf)

    coarse = jnp.where(row < N_GROUPS, lt[0:EXP_PER_GROUP], neg)
    cmax = jnp.max(coarse, axis=0, keepdims=True)
    gi = jnp.min(jnp.where(coarse == cmax, row, EXP_PER_GROUP), axis=0, keepdims=True)
    pg = 1.0 / jnp.sum(jnp.exp(coarse - cmax), axis=0, keepdims=True)

    fine = lt[FINE_ROW0:FINE_ROW0 + EXP_PER_GROUP]
    for grp in range(1, N_GROUPS):
        lo_r = FINE_ROW0 + grp * EXP_PER_GROUP
        fine = jnp.where(gi == grp, lt[lo_r:lo_r + EXP_PER_GROUP], fine)
    m1 = jnp.max(fine, axis=0, keepdims=True)
    i1 = jnp.min(jnp.where(fine == m1, row, EXP_PER_GROUP), axis=0, keepdims=True)
    rest = jnp.where(row == i1, neg, fine)
    m2 = jnp.max(rest, axis=0, keepdims=True)
    i2 = jnp.min(jnp.where(rest == m2, row, EXP_PER_GROUP), axis=0, keepdims=True)
    e21 = jnp.exp(m2 - m1)
    w1 = 1.0 / (1.0 + e21)
    gate_ref[...] = jnp.concatenate([pg * w1, pg * (e21 * w1)], axis=0)
    e0 = gi * EXP_PER_GROUP + i1
    e1 = gi * EXP_PER_GROUP + i2
    eid_ref[...] = jnp.concatenate([e0, e1], axis=0)

    @pl.when(pl.program_id(0) == 0)
    def _():
        cnt_ref[...] = cnt0_ref[...]

    erow = lax.broadcasted_iota(I32, (N_EXPERTS, tm), 0)
    oh0 = jnp.where(erow == e0, 1.0, 0.0)
    oh1 = jnp.where(erow == e1, 1.0, 0.0)
    r_i = lax.broadcasted_iota(I32, (tm, tm), 0)
    c_i = lax.broadcasted_iota(I32, (tm, tm), 1)
    before = jnp.where(r_i < c_i, 1.0, 0.0).astype(BF16)
    pre0 = jnp.dot(oh0.astype(BF16), before, preferred_element_type=F32)
    pre1 = jnp.dot(oh1.astype(BF16), before, preferred_element_type=F32)
    tot0 = jnp.sum(oh0, axis=1, keepdims=True)
    tot1 = jnp.sum(oh1, axis=1, keepdims=True)
    base = cnt_ref[:, 0:1]
    r0 = jnp.sum(oh0 * (pre0 + base), axis=0, keepdims=True)
    r1 = jnp.sum(oh1 * (pre1 + (base + tot0)), axis=0, keepdims=True)
    rank_ref[...] = jnp.concatenate([r0, r1], axis=0).astype(I32)
    total = jnp.broadcast_to(base + tot0 + tot1, cnt_ref.shape)
    cnt_ref[...] = total
    cnt_out_ref[...] = total


N_TAIL_ROW_OUTPUTS = 5


def _tail_specs(total_rows, d, tm, row0=0):
    blk0 = row0 // tm
    t = total_rows
    row = lambda w: pl.BlockSpec((tm, w), lambda i: (i + blk0, 0))
    lanes = pl.BlockSpec((TOP_K, tm), lambda i: (0, i + blk0))
    cnt = pl.BlockSpec((N_EXPERTS, ROUTER_LANES), lambda i: (0, 0))
    out_specs = [row(d), row(d // 2), lanes, lanes, lanes, cnt]
    out_shape = [jax.ShapeDtypeStruct((t, d), F32), jax.ShapeDtypeStruct((t, d // 2), U32),
                 jax.ShapeDtypeStruct((TOP_K, t), F32), jax.ShapeDtypeStruct((TOP_K, t), I32),
                 jax.ShapeDtypeStruct((TOP_K, t), I32),
                 jax.ShapeDtypeStruct((N_EXPERTS, ROUTER_LANES), F32)]
    scratch = [pltpu.VMEM((N_EXPERTS, ROUTER_LANES), F32)]
    return out_specs, out_shape, scratch


def _no_slots():
    return jnp.zeros((N_EXPERTS, ROUTER_LANES), F32)


def _even_out_kernel(ya_ref, yb_ref, h_ref, w_ref, g_ref, wr_ref, br_ref, cnt0_ref, *tail_refs):
    a_width = ya_ref.shape[1]
    mix = (jnp.dot(ya_ref[...], w_ref[0:a_width, :], preferred_element_type=F32) +
           jnp.dot(yb_ref[...], w_ref[a_width:, :], preferred_element_type=F32))
    _router_tail(h_ref[...] + mix, g_ref, wr_ref, br_ref, cnt0_ref, *tail_refs)


def _even_out(ya, yb, h, w_out, g_ffn, wr, br, *, tm):
    t, d = h.shape
    out_specs, out_shape, scratch = _tail_specs(t, d, tm)
    row = lambda w: pl.BlockSpec((tm, w), lambda i: (i, 0))
    cnt0 = _no_slots()
    return pl.pallas_call(
        _even_out_kernel,
        grid=(t // tm,),
        in_specs=[row(ya.shape[1]), row(yb.shape[1]), row(d), _const_spec(w_out.shape),
                  _const_spec((1, d)), _const_spec(wr.shape), _const_spec(br.shape),
                  _const_spec(cnt0.shape)],
        out_specs=out_specs,
        out_shape=out_shape,
        scratch_shapes=scratch,
        compiler_params=_params("arbitrary"),
        name="even_out_router",
    )(ya, yb, h, w_out, g_ffn, wr, br, cnt0)


def _gelu_tanh(x):
    return 0.5 * x * (1.0 + jnp.tanh(0.7978845608028654 * (x + 0.044715 * (x * x * x))))


def _odd_kernel(*refs, n_prev, n_alias):
    h_refs, refs = refs[:max(n_prev, 1)], refs[max(n_prev, 1):]
    (g_ref, w_in_ref, gv_ref, ws_ref, bs_ref, w_out_ref, gf_ref, wr_ref, br_ref, cnt0_ref,
     *rest) = refs
    *tail_refs, y_ref = rest[n_alias:]
    h = _post_value(*h_refs) if n_prev else h_refs[0][...]
    tm = h.shape[0]
    c_width = gv_ref.shape[1]
    n_grp = ws_ref.shape[0]
    gdim = c_width // n_grp
    xn = _rms(h, g_ref[...]).astype(BF16)
    u = _gelu_tanh(jnp.dot(xn, w_in_ref[:, 0:c_width], preferred_element_type=F32))
    v = _gelu_tanh(jnp.dot(xn, w_in_ref[:, c_width:], preferred_element_type=F32))
    v = _rms(v, gv_ref[...]).astype(BF16)
    r_i = lax.broadcasted_iota(I32, (CHUNK, CHUNK), 0)
    c_i = lax.broadcasted_iota(I32, (CHUNK, CHUNK), 1)
    causal = c_i <= r_i
    bs = bs_ref[...]
    for grp in range(n_grp):
        w_m = jnp.where(causal, ws_ref[grp], 0.0).astype(BF16)
        bias = bs[:, grp:grp + 1]
        cols = slice(grp * gdim, (grp + 1) * gdim)
        for c in range(tm // CHUNK):
            rows = slice(c * CHUNK, (c + 1) * CHUNK)
            gate = jnp.dot(w_m, v[rows, cols], preferred_element_type=F32) + bias
            y_ref[rows, cols] = (u[rows, cols] * gate).astype(BF16)
    mix = jnp.dot(y_ref[...], w_out_ref[...], preferred_element_type=F32)
    _router_tail(h + mix, gf_ref, wr_ref, br_ref, cnt0_ref, *tail_refs)


def _odd_mixer(h_or_prev, g_mix, w_in, g_v, w_s, b_s_t, w_out, g_ffn, wr, br, *, tm,
               total_rows=None, earlier=None):
    cnt0 = _no_slots() if earlier is None else earlier[N_TAIL_ROW_OUTPUTS]
    consts = [g_mix, w_in, g_v, w_s, b_s_t, w_out, g_ffn, wr, br, cnt0]
    pending = isinstance(h_or_prev, PendingTail)
    if pending:
        head_specs, head_args = h_or_prev.operands(tm)
        t, d, row0 = h_or_prev.rows, h_or_prev.width, h_or_prev.row0
    else:
        t, d = h_or_prev.shape
        row0 = 0
        head_specs, head_args = [pl.BlockSpec((tm, d), lambda i: (i, 0))], [h_or_prev]
    out_specs, out_shape, scratch = _tail_specs(total_rows or t, d, tm, row0)
    n_prev = len(head_args) if pending else 0
    in_specs = head_specs + [_const_spec(c.shape) for c in consts]
    operands = [*head_args, *consts]
    aliases = {}
    if earlier is not None:
        aliases = {len(operands) + k: k for k in range(N_TAIL_ROW_OUTPUTS)}
        in_specs += [pl.BlockSpec(memory_space=pl.ANY)] * N_TAIL_ROW_OUTPUTS
        operands += list(earlier[:N_TAIL_ROW_OUTPUTS])
    return pl.pallas_call(
        functools.partial(_odd_kernel, n_prev=n_prev, n_alias=len(aliases)),
        grid=(t // tm,),
        in_specs=in_specs,
        out_specs=out_specs,
        out_shape=out_shape,
        input_output_aliases=aliases,
        scratch_shapes=scratch + [pltpu.VMEM((tm, g_v.shape[1]), BF16)],
        compiler_params=_params("arbitrary"),
        name="odd_mixer_router",
    )(*operands)


def _sc_mesh():
    return plsc.VectorSubcoreMesh(core_axis_name="c", subcore_axis_name="s")


def _sc_worker_base(rows_per_worker):
    wid = lax.axis_index("s") * V7X_SC_CORES + lax.axis_index("c")
    return wid * rows_per_worker


def _dispatch(xp, dest, n_slots):
    t, w = xp.shape
    per_worker = t // SC_WORKERS
    n_chunks = per_worker // SC_ROWS
    assert per_worker * SC_WORKERS == t and n_chunks * SC_ROWS == per_worker and n_chunks % 2 == 0

    idx_t = pltpu.VMEM((SC_ROWS,), I32)

    @functools.partial(
        pl.kernel, mesh=_sc_mesh(),
        out_type=jax.ShapeDtypeStruct((n_slots, w), xp.dtype),
        scratch_types=[pltpu.VMEM((2, SC_ROWS, w), xp.dtype), idx_t, idx_t, idx_t, idx_t,
                       pltpu.SemaphoreType.DMA((2,)), pltpu.SemaphoreType.DMA((2,))],
    )
    def kern(x_hbm, d_hbm, out_hbm, rows_v, i00, i01, i10, i11, rsem, wsem):
        base = _sc_worker_base(per_worker)
        idx = ((i00, i01), (i10, i11))

        def load(c, b):
            off = pl.multiple_of(base + c * SC_ROWS, 8)
            pltpu.async_copy(x_hbm.at[pl.ds(off, SC_ROWS)], rows_v.at[b], rsem.at[b])
            for k in range(TOP_K):
                pltpu.sync_copy(d_hbm.at[k, pl.ds(off, SC_ROWS)], idx[b][k])

        def wait_load(b):
            pltpu.make_async_copy(x_hbm.at[pl.ds(0, SC_ROWS)], rows_v.at[b], rsem.at[b]).wait()

        def scatter(b):
            for k in range(TOP_K):
                pltpu.async_copy(rows_v.at[b], out_hbm.at[idx[b][k]], wsem.at[b])

        def wait_scatter(b):
            for k in range(TOP_K):
                pltpu.make_async_copy(rows_v.at[b], out_hbm.at[idx[b][k]], wsem.at[b]).wait()

        load(0, 0)

        @pl.loop(0, n_chunks, step=2)
        def _(c0):
            for b in range(2):
                c = c0 + b
                wait_load(b)
                scatter(b)

                @pl.when(c + 1 < n_chunks)
                def _():
                    @pl.when(c >= 1)
                    def _():
                        wait_scatter(1 - b)
                    load(c + 1, 1 - b)

        wait_scatter(0)
        wait_scatter(1)

    return kern(xp, dest)


def _combine(y, idx):
    n = idx.shape[0]
    w = y.shape[1]
    rows = 2 * SC_ROWS
    per_worker = n // SC_WORKERS
    n_chunks = per_worker // rows
    assert per_worker * SC_WORKERS == n and n_chunks * rows == per_worker and n_chunks % 2 == 0

    idx_t = pltpu.VMEM((rows,), I32)

    @functools.partial(
        pl.kernel, mesh=_sc_mesh(),
        out_type=jax.ShapeDtypeStruct((n, w), y.dtype),
        scratch_types=[pltpu.VMEM((2, rows, w), y.dtype), idx_t, idx_t,
                       pltpu.SemaphoreType.DMA((2,)), pltpu.SemaphoreType.DMA((2,))],
    )
    def kern(y_hbm, i_hbm, out_hbm, rows_v, i0, i1, gsem, wsem):
        base = _sc_worker_base(per_worker)
        ibuf = (i0, i1)

        def gather(c, b):
            off = pl.multiple_of(base + c * rows, 8)
            pltpu.sync_copy(i_hbm.at[pl.ds(off, rows)], ibuf[b])
            pltpu.async_copy(y_hbm.at[ibuf[b]], rows_v.at[b], gsem.at[b])

        def wait_gather(b):
            pltpu.make_async_copy(y_hbm.at[ibuf[b]], rows_v.at[b], gsem.at[b]).wait()

        def write(c, b):
            off = pl.multiple_of(base + c * rows, 8)
            pltpu.async_copy(rows_v.at[b], out_hbm.at[pl.ds(off, rows)], wsem.at[b])

        def wait_write(b):
            pltpu.make_async_copy(rows_v.at[b], out_hbm.at[pl.ds(0, rows)], wsem.at[b]).wait()

        gather(0, 0)

        @pl.loop(0, n_chunks, step=2)
        def _(c0):
            for b in range(2):
                c = c0 + b

                @pl.when(c + 1 < n_chunks)
                def _():
                    @pl.when(c >= 1)
                    def _():
                        wait_write(1 - b)
                    gather(c + 1, 1 - b)

                wait_gather(b)
                write(c, b)

        wait_write(0)
        wait_write(1)

    return kern(y, idx)


def _expert_kernel(te_ref, tv_ref, nu_ref, tg_ref, tn_ref, x_hbm, w1_hbm, w3_hbm, w2_hbm, y_ref,
                   xbuf_ref, xsem, w1f_ref, w3f_ref, w2f_ref, wsem, w1b_ref, w3b_ref, w2b_ref,
                   *, layer):
    i = pl.program_id(0)
    n_used = nu_ref[0]
    ts = xbuf_ref.shape[1]

    def x_copy(tile):
        slot = tile % X_RING
        rows = pl.ds(pl.multiple_of(tile * ts, ts), ts)
        return pltpu.make_async_copy(x_hbm.at[rows], xbuf_ref.at[slot], xsem.at[slot])

    def w_copies(expert, slot):
        pairs = ((w1_hbm, w1f_ref), (w3_hbm, w3f_ref), (w2_hbm, w2f_ref))
        return [pltpu.make_async_copy(src.at[layer, expert], dst.at[slot], wsem.at[slot, j])
                for j, (src, dst) in enumerate(pairs)]

    @pl.when(jnp.logical_and(i == 0, n_used > 0))
    def _():
        for tile in range(X_RING - 1):
            @pl.when(tile < n_used)
            def _():
                x_copy(tile).start()
        for c in w_copies(te_ref[0], 0):
            c.start()

    @pl.when(i < n_used)
    def _():
        @pl.when(i + (X_RING - 1) < n_used)
        def _():
            x_copy(i + (X_RING - 1)).start()

        prev = te_ref[jnp.maximum(i - 1, 0)]

        @pl.when(jnp.logical_or(i == 0, te_ref[i] != prev))
        def _():
            slot = tg_ref[i] % 2
            for c in w_copies(te_ref[i], slot):
                c.wait()
            nxt = tn_ref[i]

            @pl.when(nxt >= 0)
            def _():
                for c in w_copies(nxt, 1 - slot):
                    c.start()

            w1b_ref[...] = w1f_ref[slot].astype(BF16)
            w3b_ref[...] = w3f_ref[slot].astype(BF16)
            w2b_ref[...] = w2f_ref[slot].astype(BF16)

        x_copy(i).wait()
        xu = xbuf_ref[i % X_RING]
        rowid = lax.broadcasted_iota(I32, xu.shape, 0)
        xu = jnp.where(rowid < tv_ref[i], xu, jnp.zeros_like(xu))
        x = _unpack_bf16_pairs(xu).astype(BF16)
        h1 = jnp.dot(x, w1b_ref[...], preferred_element_type=F32)
        h3 = jnp.dot(x, w3b_ref[...], preferred_element_type=F32)
        he = (h1 * jax.nn.sigmoid(h1)) * h3
        y = jnp.dot(he.astype(BF16), w2b_ref[...], preferred_element_type=F32)
        y_ref[...] = _pack_bf16_pairs(y)


def _experts(x_sorted, plan, w1, w3, w2, layer):
    n_slots, wp = x_sorted.shape
    _, n_e, d, d_e = w1.shape
    ts = SLOT_TILE
    any_spec = pl.BlockSpec(memory_space=pl.ANY)
    yspec = pl.BlockSpec((ts, wp),
                         lambda i, te, tv, nu, tg, tn: (jnp.minimum(i, jnp.maximum(nu[0] - 1, 0)), 0))
    return pl.pallas_call(
        functools.partial(_expert_kernel, layer=layer),
        grid_spec=pltpu.PrefetchScalarGridSpec(
            num_scalar_prefetch=len(plan),
            grid=(n_slots // ts,),
            in_specs=[any_spec] * 4,
            out_specs=yspec,
            scratch_shapes=[pltpu.VMEM((X_RING, ts, wp), x_sorted.dtype),
                            pltpu.SemaphoreType.DMA((X_RING,)),
                            pltpu.VMEM((2, d, d_e), F32), pltpu.VMEM((2, d, d_e), F32),
                            pltpu.VMEM((2, d_e, d), F32), pltpu.SemaphoreType.DMA((2, 3)),
                            pltpu.VMEM((d, d_e), BF16), pltpu.VMEM((d, d_e), BF16),
                            pltpu.VMEM((d_e, d), BF16)],
        ),
        out_shape=jax.ShapeDtypeStruct((n_slots, d // 2), U32),
        compiler_params=_params("arbitrary"),
        name="moe_experts",
    )(*plan, x_sorted, w1, w3, w2)


def _slot_plan(eids, ranks, counts, n_tiles):
    ts = SLOT_TILE
    cnt = counts[:, 0].astype(I32)
    padded = (cnt + ts - 1) // ts * ts
    e_ids = jnp.arange(N_EXPERTS, dtype=I32)
    ends = jnp.sum(jnp.where(e_ids[:, None] <= e_ids[None, :], padded[:, None], 0), axis=0)
    starts = ends - padded
    onehot = eids[None] == e_ids[:, None, None]
    dest = ranks + jnp.sum(jnp.where(onehot, starts[:, None, None], 0), axis=0)
    tile_lo = jnp.arange(n_tiles, dtype=I32) * ts
    tile_e = jnp.minimum(jnp.sum(tile_lo[:, None] >= ends[None, :], axis=1), N_EXPERTS - 1)
    tile_e = tile_e.astype(I32)
    valid_end = jnp.sum(jnp.where(tile_e[:, None] == e_ids[None, :], (starts + cnt)[None, :], 0),
                        axis=1)
    tile_valid = jnp.clip(valid_end - tile_lo, 0, ts).astype(I32)
    n_used = (ends[-1:] // ts).astype(I32)
    nonempty = cnt > 0
    earlier = jnp.logical_and(nonempty[None, :], e_ids[None, :] < tile_e[:, None])
    tile_grp = jnp.sum(earlier, axis=1).astype(I32)
    later = jnp.logical_and(nonempty[None, :], e_ids[None, :] > tile_e[:, None])
    nxt = jnp.min(jnp.where(later, e_ids[None, :], N_EXPERTS), axis=1)
    tile_next = jnp.where(nxt < N_EXPERTS, nxt, -1).astype(I32)
    return dest, (tile_e, tile_valid, n_used, tile_grp, tile_next)


class PendingTail(NamedTuple):
    operands: Callable
    rows: int
    row0: int
    width: int


def _post_value(h1_ref, y0_ref, y1_ref, gate_ref, p_ref, g_ref, wpg_ref, wpe_ref):
    gates = gate_ref[...]
    tm = gates.shape[1]
    gt = jnp.concatenate([gates, jnp.zeros((8 - TOP_K, tm), F32)], axis=0).T
    h2 = h1_ref[...] + (gt[:, 0:1] * _unpack_bf16_pairs(y0_ref[...]) +
                        gt[:, 1:2] * _unpack_bf16_pairs(y1_ref[...]))
    gate = jax.nn.sigmoid(jnp.dot(_rms(h2, g_ref[...]).astype(BF16), wpg_ref[...],
                                  preferred_element_type=F32))
    pe = jnp.dot(p_ref[...].astype(BF16), wpe_ref[...], preferred_element_type=F32)
    return h2 + gate * pe


def _post_operands(h1, yk, gates, p, g_pl, w_pg, w_pe, *, p_row0, row0=0):
    d = h1.shape[1]
    t = yk.shape[0] // TOP_K
    wy = yk.shape[1]

    def operands(tm):
        nt = t // tm
        blk0 = row0 // tm
        p_blk0 = (p_row0 + row0) // tm
        specs = [pl.BlockSpec((tm, d), lambda i: (i + blk0, 0)),
                 pl.BlockSpec((tm, wy), lambda i: (i, 0)),
                 pl.BlockSpec((tm, wy), lambda i: (i + nt, 0)),
                 pl.BlockSpec((TOP_K, tm), lambda i: (0, i + blk0)),
                 pl.BlockSpec((tm, p.shape[1]), lambda i: (i + p_blk0, 0)),
                 _const_spec((1, d)), _const_spec(w_pg.shape), _const_spec(w_pe.shape)]
        return specs, [h1, yk, yk, gates, p, g_pl, w_pg, w_pe]

    return PendingTail(operands, t, row0, d)


def _post_kernel(*refs):
    *post_refs, o_ref = refs
    o_ref[...] = _post_value(*post_refs)


def _post(pending, *, tm):
    specs, args = pending.operands(tm)
    t, d = pending.rows, pending.width
    return pl.pallas_call(
        _post_kernel,
        grid=(t // tm,),
        in_specs=specs,
        out_specs=pl.BlockSpec((tm, d), lambda i: (i, 0)),
        out_shape=jax.ShapeDtypeStruct((t, d), F32),
        compiler_params=_params("arbitrary"),
        name="layer_post",
    )(*args)


def _final_kernel(*refs, n_post):
    post_refs, gfin_ref, o_ref = refs[:n_post], refs[n_post], refs[-1]
    o_ref[...] = _rms(_post_value(*post_refs), gfin_ref[...])


def _final(pending, g_fin, *, tm, total_rows, out=None):
    specs, args = pending.operands(tm)
    d = pending.width
    blk0 = pending.row0 // tm
    in_specs = specs + [_const_spec((1, d))]
    operands = [*args, g_fin]
    aliases = {}
    if out is not None:
        aliases = {len(operands): 0}
        in_specs.append(pl.BlockSpec(memory_space=pl.ANY))
        operands.append(out)
    return pl.pallas_call(
        functools.partial(_final_kernel, n_post=len(args)),
        grid=(pending.rows // tm,),
        in_specs=in_specs,
        out_specs=pl.BlockSpec((tm, d), lambda i: (i + blk0, 0)),
        out_shape=jax.ShapeDtypeStruct((total_rows, d), F32),
        input_output_aliases=aliases,
        compiler_params=_params("arbitrary"),
        name="final_post",
    )(*operands)


def _router_weights(wc, bc, wf, bf):
    d = wc.shape[0]
    w = jnp.zeros((d, ROUTER_LANES), F32)
    w = w.at[:, 0:N_GROUPS].set(wc).at[:, FINE_ROW0:FINE_ROW0 + N_EXPERTS].set(wf)
    b = jnp.zeros((1, ROUTER_LANES), F32)
    b = b.at[0, 0:N_GROUPS].set(bc).at[0, FINE_ROW0:FINE_ROW0 + N_EXPERTS].set(bf)
    hi = w.astype(BF16)
    lo = (w - hi.astype(F32)).astype(BF16)
    return jnp.concatenate([hi, lo], axis=1), b


def _moe(xp, eids, ranks, counts, w1, w3, w2, layer, n_parts):
    t = xp.shape[0]
    n_tiles = TOP_K * t // SLOT_TILE + N_EXPERTS
    dest, plan = _slot_plan(eids, ranks, counts, n_tiles)
    x_sorted = _dispatch(xp, dest, n_tiles * SLOT_TILE)
    y_sorted = _experts(x_sorted, plan, w1, w3, w2, layer)
    tp = t // n_parts
    return [_combine(y_sorted, dest[:, s * tp:(s + 1) * tp].reshape(-1)) for s in range(n_parts)]


def kernel(x, p, norm_mix, norm_ffn, norm_pl, final_norm, w_in_even, conv_w_even, w_out_even, w_in_odd, g_v_odd, w_s_odd, b_s_odd, w_out_odd, router_c, router_c_b, router_f, router_f_b, moe_w1, moe_w3, moe_w2, w_pe, w_pg):
    batch, seq, d = x.shape
    depth = p.shape[0]
    t = batch * seq
    tm = min(TOKEN_TILE, seq)
    tm_big = min(LIGHT_TOKEN_TILE, seq)
    p_rows = p.reshape(depth * t, -1)
    row = lambda a: a.reshape(1, -1)
    h = x.reshape(t, d)
    for i in range(depth):
        j = i // 2
        wr, br = _router_weights(router_c[i], router_c_b[i], router_f[i], router_f_b[i])
        g_ffn = row(norm_ffn[i])
        if i % 2 == 0:
            if isinstance(h, list):
                h = _post(h[0], tm=tm_big)
            ya, q, k, v = _even_in(h, row(norm_mix[i]), w_in_even[j].astype(BF16), conv_w_even[j],
                                   seq=seq, tm=tm_big)
            yb = _attention(q, k, v, batch=batch, seq=seq, tq=min(Q_TILE, seq))
            mixed = _even_out(ya, yb, h, w_out_even[j].astype(BF16), g_ffn, wr, br, tm=tm_big)
        else:
            mixed = None
            for part in (h if isinstance(h, list) else [h]):
                mixed = _odd_mixer(part, row(norm_mix[i]), w_in_odd[j].astype(BF16),
                                   row(g_v_odd[j]), w_s_odd[j], b_s_odd[j].T,
                                   w_out_odd[j].astype(BF16), g_ffn, wr, br, tm=tm,
                                   total_rows=t, earlier=mixed)
        h1, xp, gates, eids, ranks, counts = mixed
        consumer_tile = tm_big if i == depth - 1 else tm if i % 2 == 0 else None
        split = consumer_tile is not None and t % (TAIL_PARTS * consumer_tile) == 0
        n_parts = TAIL_PARTS if split else 1
        yks = _moe(xp, eids, ranks, counts, moe_w1, moe_w3, moe_w2, i, n_parts)
        h = [_post_operands(h1, yk, gates, p_rows, row(norm_pl[i]), w_pg[i].astype(BF16),
                            w_pe[i].astype(BF16), p_row0=i * t, row0=s * (t // n_parts))
             for s, yk in enumerate(yks)]
    out = None
    for tail in h:
        out = _final(tail, row(final_norm), tm=tm_big, total_rows=t, out=out)
    return out.reshape(batch, seq, d)
```

```python
import functools
from typing import Callable, NamedTuple

import jax
import jax.numpy as jnp
from jax import lax
from jax.experimental import pallas as pl
from jax.experimental.pallas import tpu as pltpu
from jax.experimental.pallas import tpu_sc as plsc

EPS = 1e-6
HEAD_DIM = 64
HEADS_PER_SLAB = 2
ATTN_SLABS = 4
CHUNK = 128
N_GROUPS = 4
EXP_PER_GROUP = 8
N_EXPERTS = N_GROUPS * EXP_PER_GROUP
TOP_K = 2
ROUTER_LANES = 128
FINE_ROW0 = 8
RANK_BLOCK = 256

V7X_VMEM_BYTES = 64 * 1024 * 1024
VMEM_LIMIT = V7X_VMEM_BYTES - 8 * 1024 * 1024
V7X_SC_CORES = 2
V7X_SC_SUBCORES = 16
SC_WORKERS = V7X_SC_CORES * V7X_SC_SUBCORES
SC_ROWS = 32

TOKEN_TILE = 512
LIGHT_TOKEN_TILE = 1024
Q_TILE = 64
WINDOW_TILES = 4
SLOT_TILE = 512
X_RING = 3
MIXER_PARTS = 2
FINAL_PARTS = 4

LOG_F32_TINY_BOUND = -88.0
OFF_PENALTY = 1e30
Q_SUBTILES = 4

F32 = jnp.float32
BF16 = jnp.bfloat16
U32 = jnp.uint32
I32 = jnp.int32


def _rms(x, g):
    return x * lax.rsqrt(jnp.mean(x * x, axis=-1, keepdims=True) + EPS) * g


def _const_spec(shape):
    nd = len(shape)
    return pl.BlockSpec(shape, lambda *_: (0,) * nd, pipeline_mode=pl.Buffered(1))


def _params(*sem):
    return pltpu.CompilerParams(dimension_semantics=sem, vmem_limit_bytes=VMEM_LIMIT)


def _even_in_kernel(h_ref, g_ref, w_ref, cw_ref, ya_ref, q_ref, k_ref, v_ref, zs_ref,
                    *, tiles_per_seq, a_width, sb_width):
    tm = h_ref.shape[0]
    xn = _rms(h_ref[...], g_ref[...]).astype(BF16)
    a3 = 3 * a_width

    @pl.when(pl.program_id(0) % tiles_per_seq == 0)
    def _():
        zs_ref[0:8, :] = jnp.zeros((8, a_width), F32)

    pa = jnp.dot(xn, w_ref[:, 0:a3], preferred_element_type=F32)
    z = pa[:, 2 * a_width:a3] * pa[:, 0:a_width]
    zs_ref[8:tm + 8, :] = z
    z1 = zs_ref[7:tm + 7, :]
    z2 = zs_ref[6:tm + 6, :]
    cw = cw_ref[...]
    conv = cw[0:1, :] * z2 + cw[1:2, :] * z1 + cw[2:3, :] * z
    ya_ref[...] = (pa[:, a_width:2 * a_width] * conv).astype(BF16)
    zs_ref[0:8, :] = zs_ref[tm:tm + 8, :]

    pq = jnp.dot(xn, w_ref[:, a3:a3 + 3 * sb_width], preferred_element_type=F32)
    q_ref[...] = (pq[:, 0:sb_width] * (HEAD_DIM ** -0.5)).astype(BF16)
    k_ref[...] = pq[:, sb_width:2 * sb_width].astype(BF16)
    v_ref[...] = pq[:, 2 * sb_width:3 * sb_width].astype(BF16)


def _even_in(h, g, w_in, conv_w, *, seq, tm):
    t, d = h.shape
    a_width = conv_w.shape[1]
    sb_width = (w_in.shape[1] - 3 * a_width) // 3
    kern = functools.partial(_even_in_kernel, tiles_per_seq=seq // tm, a_width=a_width,
                             sb_width=sb_width)
    row = lambda w: pl.BlockSpec((tm, w), lambda i: (i, 0))
    return pl.pallas_call(
        kern,
        grid=(t // tm,),
        in_specs=[row(d), _const_spec((1, d)), _const_spec(w_in.shape), _const_spec(conv_w.shape)],
        out_specs=[row(a_width), row(sb_width), row(sb_width), row(sb_width)],
        out_shape=[jax.ShapeDtypeStruct((t, a_width), BF16)] +
                  [jax.ShapeDtypeStruct((t, sb_width), BF16)] * 3,
        scratch_shapes=[pltpu.VMEM((tm + 8, a_width), F32)],
        compiler_params=_params("arbitrary"),
        name="even_in",
    )(h, g, w_in, conv_w)


def _attn_kernel(q_ref, k_ref, v_ref, o_ref, acc_ref, c_ref, *, tq):
    qi = pl.program_id(2)
    slab = HEADS_PER_SLAB * HEAD_DIM
    rows_unit = HEADS_PER_SLAB * tq
    rows_tile = ATTN_SLABS * rows_unit
    rows = Q_SUBTILES * rows_tile
    units = [(j, sl) for j in range(Q_SUBTILES) for sl in range(ATTN_SLABS)]
    lane = lax.broadcasted_iota(I32, (tq, slab), 1)
    qs = []
    for j, sl in units:
        q = q_ref[j * tq:(j + 1) * tq, sl * slab:(sl + 1) * slab]
        zero = jnp.zeros_like(q)
        qs.append(jnp.concatenate([jnp.where(lane < HEAD_DIM, q, zero),
                                   jnp.where(lane >= HEAD_DIM, q, zero)], axis=0))

    def later_keys(width):
        r_i = lax.broadcasted_iota(I32, (width, width), 0)
        c_i = lax.broadcasted_iota(I32, (width, width), 1)
        return jnp.where(r_i > c_i, 1.0, 0.0).astype(BF16)

    def per_tile_column(vals):
        return jnp.concatenate([jnp.full((rows_tile, 1), v, F32) for v in vals], axis=0)

    def block(starts, width, c, mask):
        upper = later_keys(width)
        keys = [pl.ds(pl.multiple_of(s, tq), width) for s in starts]
        z = jnp.concatenate(
            [lax.dot_general(qs[u], k_ref[keys[j], sl * slab:(sl + 1) * slab],
                             (((1,), (1,)), ((), ())), preferred_element_type=F32)
             for u, (j, sl) in enumerate(units)], axis=0)
        sp = jnp.maximum(z, 0.0) + jnp.log(1.0 + jnp.exp(-jnp.abs(z)))
        spm = sp if mask is None else jnp.where(mask, sp, 0.0)
        hi = spm.astype(BF16)
        lo = (spm - hi.astype(F32)).astype(BF16)
        later = (jnp.dot(hi, upper, preferred_element_type=F32) +
                 jnp.dot(lo, upper, preferred_element_type=F32))
        a = jnp.exp(z - sp - later - c)
        if mask is not None:
            a = jnp.where(mask, a, 0.0)
        a = a.astype(BF16)
        out = jnp.concatenate(
            [jnp.dot(a[u * rows_unit:(u + 1) * rows_unit], v_ref[keys[j], sl * slab:(sl + 1) * slab],
                     preferred_element_type=F32) for u, (j, sl) in enumerate(units)], axis=0)
        return out, jnp.sum(spm, axis=1, keepdims=True)

    tile0 = qi * Q_SUBTILES
    win = WINDOW_TILES * tq
    firsts = [jnp.maximum(tile0 + j - (WINDOW_TILES - 1), 0) * tq for j in range(Q_SUBTILES)]
    col = lax.broadcasted_iota(I32, (tq, win), 1)
    row = lax.broadcasted_iota(I32, (tq, win), 0)
    mask = jnp.concatenate(
        [col + (firsts[j] - (tile0 + j) * tq) < row
         for j in range(Q_SUBTILES) for _ in range(rows_tile // tq)], axis=0)
    out0, c2 = block(firsts, win, jnp.zeros((rows, 1), F32), mask)
    acc_ref[...] = out0
    c_ref[...] = c2

    def cond(carry):
        n, cmin = carry
        return jnp.logical_and(tile0 + (Q_SUBTILES - 1) - WINDOW_TILES - n >= 0,
                               cmin < -LOG_F32_TINY_BOUND)

    def body(carry):
        n, _ = carry
        kbs = [tile0 + j - WINDOW_TILES - n for j in range(Q_SUBTILES)]
        c = c_ref[...]
        off = per_tile_column([jnp.where(kb >= 0, 0.0, OFF_PENALTY) for kb in kbs])
        out, ssum = block([jnp.maximum(kb, 0) * tq for kb in kbs], tq, c + off, None)
        acc_ref[...] += out
        c_new = c + ssum
        c_ref[...] = c_new
        return n + 1, jnp.min(c_new)

    lax.while_loop(cond, body, (jnp.int32(0), jnp.min(c2)))
    acc = acc_ref[...]
    for u, (j, sl) in enumerate(units):
        lo_r = u * rows_unit
        o_ref[j * tq:(j + 1) * tq, sl * slab:(sl + 1) * slab] = jnp.where(
            lane < HEAD_DIM, acc[lo_r:lo_r + tq], acc[lo_r + tq:lo_r + 2 * tq]).astype(o_ref.dtype)


def _attention(q, k, v, *, batch, seq, tq):
    t, w = q.shape
    wblk = ATTN_SLABS * HEADS_PER_SLAB * HEAD_DIM
    nq = seq // (Q_SUBTILES * tq)
    qspec = pl.BlockSpec((Q_SUBTILES * tq, wblk), lambda b, s, i: (b * nq + i, s))
    kvspec = pl.BlockSpec((seq, wblk), lambda b, s, i: (b, s))
    rows = Q_SUBTILES * ATTN_SLABS * HEADS_PER_SLAB * tq
    return pl.pallas_call(
        functools.partial(_attn_kernel, tq=tq),
        grid=(batch, w // wblk, nq),
        in_specs=[qspec, kvspec, kvspec],
        out_specs=qspec,
        out_shape=jax.ShapeDtypeStruct((t, w), BF16),
        scratch_shapes=[pltpu.VMEM((rows, HEADS_PER_SLAB * HEAD_DIM), F32),
                        pltpu.VMEM((rows, 1), F32)],
        compiler_params=_params("arbitrary", "arbitrary", "arbitrary"),
        name="stick_breaking_attention",
    )(q, k, v)


def _pack_bf16_pairs(x):
    n = x.shape[1] // 2
    bits = lax.bitcast_convert_type(x.astype(BF16).astype(F32), U32)
    return bits[:, 0:n] | (bits[:, n:] >> 16)


def _unpack_bf16_pairs(u):
    a = lax.bitcast_convert_type(u & jnp.uint32(0xFFFF0000), F32)
    b = lax.bitcast_convert_type(u << 16, F32)
    return jnp.concatenate([a, b], axis=1)


def _router_tail(h1, g_ref, wr_ref, br_ref, cnt0_ref,
                 h1_ref, xp_ref, gate_ref, eid_ref, rank_ref, cnt_out_ref, cnt_ref):
    h1_ref[...] = h1
    xn = _rms(h1, g_ref[...])
    xp_ref[...] = _pack_bf16_pairs(xn)
    x_hi = xn.astype(BF16)
    x_lo = (xn - x_hi.astype(F32)).astype(BF16)
    both = jnp.dot(x_hi, wr_ref[...], preferred_element_type=F32)
    logits = (both[:, 0:ROUTER_LANES] + both[:, ROUTER_LANES:] +
              jnp.dot(x_lo, wr_ref[:, 0:ROUTER_LANES], preferred_element_type=F32) + br_ref[...])
    lt = logits.T
    tm = lt.shape[1]
    row = lax.broadcasted_iota(I32, (EXP_PER_GROUP, tm), 0)
    neg = jnp.float32(-jnp.inf)

    coarse = jnp.where(row < N_GROUPS, lt[0:EXP_PER_GROUP], neg)
    cmax = jnp.max(coarse, axis=0, keepdims=True)
    gi = jnp.min(jnp.where(coarse == cmax, row, EXP_PER_GROUP), axis=0, keepdims=True)
    pg = 1.0 / jnp.sum(jnp.exp(coarse - cmax), axis=0, keepdims=True)

    fine = lt[FINE_ROW0:FINE_ROW0 + EXP_PER_GROUP]
    for grp in range(1, N_GROUPS):
        lo_r = FINE_ROW0 + grp * EXP_PER_GROUP
        fine = jnp.where(gi == grp, lt[lo_r:lo_r + EXP_PER_GROUP], fine)
    m1 = jnp.max(fine, axis=0, keepdims=True)
    i1 = jnp.min(jnp.where(fine == m1, row, EXP_PER_GROUP), axis=0, keepdims=True)
    rest = jnp.where(row == i1, neg, fine)
    m2 = jnp.max(rest, axis=0, keepdims=True)
    i2 = jnp.min(jnp.where(rest == m2, row, EXP_PER_GROUP), axis=0, keepdims=True)
    e21 = jnp.exp(m2 - m1)
    w1 = 1.0 / (1.0 + e21)
    gate_ref[...] = jnp.concatenate([pg * w1, pg * (e21 * w1)], axis=0)
    e0 = gi * EXP_PER_GROUP + i1
    e1 = gi * EXP_PER_GROUP + i2
    eid_ref[...] = jnp.concatenate([e0, e1], axis=0)

    @pl.when(pl.program_id(0) == 0)
    def _():
        cnt_ref[...] = cnt0_ref[...]

    erow = lax.broadcasted_iota(I32, (N_EXPERTS, tm), 0)
    oh0 = jnp.where(erow == e0, 1.0, 0.0)
    oh1 = jnp.where(erow == e1, 1.0, 0.0)
    pb = min(RANK_BLOCK, tm)
    r_i = lax.broadcasted_iota(I32, (pb, pb), 0)
    c_i = lax.broadcasted_iota(I32, (pb, pb), 1)
    before = jnp.where(r_i < c_i, 1.0, 0.0).astype(BF16)
    oh = jnp.concatenate([oh0, oh1], axis=0)
    pres, run = [], jnp.zeros((TOP_K * N_EXPERTS, 1), F32)
    for lo_l in range(0, tm, pb):
        ohb = oh[:, lo_l:lo_l + pb]
        pres.append(jnp.dot(ohb.astype(BF16), before, preferred_element_type=F32) + run)
        run = run + jnp.sum(ohb, axis=1, keepdims=True)
    pre = jnp.concatenate(pres, axis=1)
    pre0, pre1 = pre[0:N_EXPERTS], pre[N_EXPERTS:]
    tot0, tot1 = run[0:N_EXPERTS], run[N_EXPERTS:]
    base = cnt_ref[:, 0:1]
    r0 = jnp.sum(oh0 * (pre0 + base), axis=0, keepdims=True)
    r1 = jnp.sum(oh1 * (pre1 + (base + tot0)), axis=0, keepdims=True)
    rank_ref[...] = jnp.concatenate([r0, r1], axis=0).astype(I32)
    total = jnp.broadcast_to(base + tot0 + tot1, cnt_ref.shape)
    cnt_ref[...] = total
    cnt_out_ref[...] = total


N_TAIL_ROW_OUTPUTS = 5


def _tail_specs(total_rows, d, tm, row0=0):
    blk0 = row0 // tm
    t = total_rows
    row = lambda w: pl.BlockSpec((tm, w), lambda i: (i + blk0, 0))
    lanes = pl.BlockSpec((TOP_K, tm), lambda i: (0, i + blk0))
    cnt = pl.BlockSpec((N_EXPERTS, ROUTER_LANES), lambda i: (0, 0))
    out_specs = [row(d), row(d // 2), lanes, lanes, lanes, cnt]
    out_shape = [jax.ShapeDtypeStruct((t, d), F32), jax.ShapeDtypeStruct((t, d // 2), U32),
                 jax.ShapeDtypeStruct((TOP_K, t), F32), jax.ShapeDtypeStruct((TOP_K, t), I32),
                 jax.ShapeDtypeStruct((TOP_K, t), I32),
                 jax.ShapeDtypeStruct((N_EXPERTS, ROUTER_LANES), F32)]
    scratch = [pltpu.VMEM((N_EXPERTS, ROUTER_LANES), F32)]
    return out_specs, out_shape, scratch


def _no_slots():
    return jnp.zeros((N_EXPERTS, ROUTER_LANES), F32)


def _even_out_kernel(ya_ref, yb_ref, h_ref, w_ref, g_ref, wr_ref, br_ref, cnt0_ref, *tail_refs):
    a_width = ya_ref.shape[1]
    mix = (jnp.dot(ya_ref[...], w_ref[0:a_width, :], preferred_element_type=F32) +
           jnp.dot(yb_ref[...], w_ref[a_width:, :], preferred_element_type=F32))
    _router_tail(h_ref[...] + mix, g_ref, wr_ref, br_ref, cnt0_ref, *tail_refs)


def _even_out(ya, yb, h, w_out, g_ffn, wr, br, *, tm):
    t, d = h.shape
    out_specs, out_shape, scratch = _tail_specs(t, d, tm)
    row = lambda w: pl.BlockSpec((tm, w), lambda i: (i, 0))
    cnt0 = _no_slots()
    return pl.pallas_call(
        _even_out_kernel,
        grid=(t // tm,),
        in_specs=[row(ya.shape[1]), row(yb.shape[1]), row(d), _const_spec(w_out.shape),
                  _const_spec((1, d)), _const_spec(wr.shape), _const_spec(br.shape),
                  _const_spec(cnt0.shape)],
        out_specs=out_specs,
        out_shape=out_shape,
        scratch_shapes=scratch,
        compiler_params=_params("arbitrary"),
        name="even_out_router",
    )(ya, yb, h, w_out, g_ffn, wr, br, cnt0)


def _gelu_tanh(x):
    return 0.5 * x * (1.0 + jnp.tanh(0.7978845608028654 * (x + 0.044715 * (x * x * x))))


def _odd_kernel(*refs, n_prev, n_alias):
    h_refs, refs = refs[:max(n_prev, 1)], refs[max(n_prev, 1):]
    (g_ref, w_in_ref, gv_ref, ws_ref, bs_ref, w_out_ref, gf_ref, wr_ref, br_ref, cnt0_ref,
     *rest) = refs
    *tail_refs, y_ref = rest[n_alias:]
    h = _post_value(*h_refs) if n_prev else h_refs[0][...]
    tm = h.shape[0]
    c_width = gv_ref.shape[1]
    n_grp = ws_ref.shape[0]
    gdim = c_width // n_grp
    xn = _rms(h, g_ref[...]).astype(BF16)
    u = _gelu_tanh(jnp.dot(xn, w_in_ref[:, 0:c_width], preferred_element_type=F32))
    v = _gelu_tanh(jnp.dot(xn, w_in_ref[:, c_width:], preferred_element_type=F32))
    v = _rms(v, gv_ref[...]).astype(BF16)
    r_i = lax.broadcasted_iota(I32, (CHUNK, CHUNK), 0)
    c_i = lax.broadcasted_iota(I32, (CHUNK, CHUNK), 1)
    causal = c_i <= r_i
    bs = bs_ref[...]
    for grp in range(n_grp):
        w_m = jnp.where(causal, ws_ref[grp], 0.0).astype(BF16)
        bias = bs[:, grp:grp + 1]
        cols = slice(grp * gdim, (grp + 1) * gdim)
        for c in range(tm // CHUNK):
            rows = slice(c * CHUNK, (c + 1) * CHUNK)
            gate = jnp.dot(w_m, v[rows, cols], preferred_element_type=F32) + bias
            y_ref[rows, cols] = (u[rows, cols] * gate).astype(BF16)
    mix = jnp.dot(y_ref[...], w_out_ref[...], preferred_element_type=F32)
    _router_tail(h + mix, gf_ref, wr_ref, br_ref, cnt0_ref, *tail_refs)


def _odd_mixer(h_or_prev, g_mix, w_in, g_v, w_s, b_s_t, w_out, g_ffn, wr, br, *, tm,
               total_rows=None, earlier=None):
    cnt0 = _no_slots() if earlier is None else earlier[N_TAIL_ROW_OUTPUTS]
    consts = [g_mix, w_in, g_v, w_s, b_s_t, w_out, g_ffn, wr, br, cnt0]
    pending = isinstance(h_or_prev, PendingTail)
    if pending:
        head_specs, head_args = h_or_prev.operands(tm)
        t, d, row0 = h_or_prev.rows, h_or_prev.width, h_or_prev.row0
    else:
        t, d = h_or_prev.shape
        row0 = 0
        head_specs, head_args = [pl.BlockSpec((tm, d), lambda i: (i, 0))], [h_or_prev]
    out_specs, out_shape, scratch = _tail_specs(total_rows or t, d, tm, row0)
    n_prev = len(head_args) if pending else 0
    in_specs = head_specs + [_const_spec(c.shape) for c in consts]
    operands = [*head_args, *consts]
    aliases = {}
    if earlier is not None:
        aliases = {len(operands) + k: k for k in range(N_TAIL_ROW_OUTPUTS)}
        in_specs += [pl.BlockSpec(memory_space=pl.ANY)] * N_TAIL_ROW_OUTPUTS
        operands += list(earlier[:N_TAIL_ROW_OUTPUTS])
    return pl.pallas_call(
        functools.partial(_odd_kernel, n_prev=n_prev, n_alias=len(aliases)),
        grid=(t // tm,),
        in_specs=in_specs,
        out_specs=out_specs,
        out_shape=out_shape,
        input_output_aliases=aliases,
        scratch_shapes=scratch + [pltpu.VMEM((tm, g_v.shape[1]), BF16)],
        compiler_params=_params("arbitrary"),
        name="odd_mixer_router",
    )(*operands)


def _sc_mesh():
    return plsc.VectorSubcoreMesh(core_axis_name="c", subcore_axis_name="s")


def _sc_worker_base(rows_per_worker):
    wid = lax.axis_index("s") * V7X_SC_CORES + lax.axis_index("c")
    return wid * rows_per_worker


def _dispatch(xp, dest, n_slots):
    t, w = xp.shape
    per_worker = t // SC_WORKERS
    n_chunks = per_worker // SC_ROWS
    assert per_worker * SC_WORKERS == t and n_chunks * SC_ROWS == per_worker and n_chunks % 2 == 0

    idx_t = pltpu.VMEM((SC_ROWS,), I32)

    @functools.partial(
        pl.kernel, mesh=_sc_mesh(),
        out_type=jax.ShapeDtypeStruct((n_slots, w), xp.dtype),
        scratch_types=[pltpu.VMEM((2, SC_ROWS, w), xp.dtype), idx_t, idx_t, idx_t, idx_t,
                       pltpu.SemaphoreType.DMA((2,)), pltpu.SemaphoreType.DMA((2,))],
    )
    def kern(x_hbm, d_hbm, out_hbm, rows_v, i00, i01, i10, i11, rsem, wsem):
        base = _sc_worker_base(per_worker)
        idx = ((i00, i01), (i10, i11))

        def load(c, b):
            off = pl.multiple_of(base + c * SC_ROWS, 8)
            pltpu.async_copy(x_hbm.at[pl.ds(off, SC_ROWS)], rows_v.at[b], rsem.at[b])
            for k in range(TOP_K):
                pltpu.sync_copy(d_hbm.at[k, pl.ds(off, SC_ROWS)], idx[b][k])

        def wait_load(b):
            pltpu.make_async_copy(x_hbm.at[pl.ds(0, SC_ROWS)], rows_v.at[b], rsem.at[b]).wait()

        def scatter(b):
            for k in range(TOP_K):
                pltpu.async_copy(rows_v.at[b], out_hbm.at[idx[b][k]], wsem.at[b])

        def wait_scatter(b):
            for k in range(TOP_K):
                pltpu.make_async_copy(rows_v.at[b], out_hbm.at[idx[b][k]], wsem.at[b]).wait()

        load(0, 0)

        @pl.loop(0, n_chunks, step=2)
        def _(c0):
            for b in range(2):
                c = c0 + b
                wait_load(b)
                scatter(b)

                @pl.when(c + 1 < n_chunks)
                def _():
                    @pl.when(c >= 1)
                    def _():
                        wait_scatter(1 - b)
                    load(c + 1, 1 - b)

        wait_scatter(0)
        wait_scatter(1)

    return kern(xp, dest)


def _combine(y, idx):
    n = idx.shape[0]
    w = y.shape[1]
    rows = 2 * SC_ROWS
    per_worker = n // SC_WORKERS
    n_chunks = per_worker // rows
    assert per_worker * SC_WORKERS == n and n_chunks * rows == per_worker and n_chunks % 2 == 0

    idx_t = pltpu.VMEM((rows,), I32)

    @functools.partial(
        pl.kernel, mesh=_sc_mesh(),
        out_type=jax.ShapeDtypeStruct((n, w), y.dtype),
        scratch_types=[pltpu.VMEM((2, rows, w), y.dtype), idx_t, idx_t,
                       pltpu.SemaphoreType.DMA((2,)), pltpu.SemaphoreType.DMA((2,))],
    )
    def kern(y_hbm, i_hbm, out_hbm, rows_v, i0, i1, gsem, wsem):
        base = _sc_worker_base(per_worker)
        ibuf = (i0, i1)

        def gather(c, b):
            off = pl.multiple_of(base + c * rows, 8)
            pltpu.sync_copy(i_hbm.at[pl.ds(off, rows)], ibuf[b])
            pltpu.async_copy(y_hbm.at[ibuf[b]], rows_v.at[b], gsem.at[b])

        def wait_gather(b):
            pltpu.make_async_copy(y_hbm.at[ibuf[b]], rows_v.at[b], gsem.at[b]).wait()

        def write(c, b):
            off = pl.multiple_of(base + c * rows, 8)
            pltpu.async_copy(rows_v.at[b], out_hbm.at[pl.ds(off, rows)], wsem.at[b])

        def wait_write(b):
            pltpu.make_async_copy(rows_v.at[b], out_hbm.at[pl.ds(0, rows)], wsem.at[b]).wait()

        gather(0, 0)

        @pl.loop(0, n_chunks, step=2)
        def _(c0):
            for b in range(2):
                c = c0 + b

                @pl.when(c + 1 < n_chunks)
                def _():
                    @pl.when(c >= 1)
                    def _():
                        wait_write(1 - b)
                    gather(c + 1, 1 - b)

                wait_gather(b)
                write(c, b)

        wait_write(0)
        wait_write(1)

    return kern(y, idx)


def _expert_kernel(te_ref, tv_ref, nu_ref, tg_ref, tn_ref, x_hbm, w1_hbm, w3_hbm, w2_hbm, y_ref,
                   xbuf_ref, xsem, w1f_ref, w3f_ref, w2f_ref, wsem, w1b_ref, w3b_ref, w2b_ref,
                   *, layer):
    i = pl.program_id(0)
    n_used = nu_ref[0]
    ts = xbuf_ref.shape[1]

    def x_copy(tile):
        slot = tile % X_RING
        rows = pl.ds(pl.multiple_of(tile * ts, ts), ts)
        return pltpu.make_async_copy(x_hbm.at[rows], xbuf_ref.at[slot], xsem.at[slot])

    def w_copies(expert, slot):
        pairs = ((w1_hbm, w1f_ref), (w3_hbm, w3f_ref), (w2_hbm, w2f_ref))
        return [pltpu.make_async_copy(src.at[layer, expert], dst.at[slot], wsem.at[slot, j])
                for j, (src, dst) in enumerate(pairs)]

    @pl.when(jnp.logical_and(i == 0, n_used > 0))
    def _():
        for tile in range(X_RING - 1):
            @pl.when(tile < n_used)
            def _():
                x_copy(tile).start()
        for c in w_copies(te_ref[0], 0):
            c.start()

    @pl.when(i < n_used)
    def _():
        @pl.when(i + (X_RING - 1) < n_used)
        def _():
            x_copy(i + (X_RING - 1)).start()

        prev = te_ref[jnp.maximum(i - 1, 0)]

        @pl.when(jnp.logical_or(i == 0, te_ref[i] != prev))
        def _():
            slot = tg_ref[i] % 2
            for c in w_copies(te_ref[i], slot):
                c.wait()
            nxt = tn_ref[i]

            @pl.when(nxt >= 0)
            def _():
                for c in w_copies(nxt, 1 - slot):
                    c.start()

            w1b_ref[...] = w1f_ref[slot].astype(BF16)
            w3b_ref[...] = w3f_ref[slot].astype(BF16)
            w2b_ref[...] = w2f_ref[slot].astype(BF16)

        x_copy(i).wait()
        xu = xbuf_ref[i % X_RING]
        rowid = lax.broadcasted_iota(I32, xu.shape, 0)
        xu = jnp.where(rowid < tv_ref[i], xu, jnp.zeros_like(xu))
        x = _unpack_bf16_pairs(xu).astype(BF16)
        h1 = jnp.dot(x, w1b_ref[...], preferred_element_type=F32)
        h3 = jnp.dot(x, w3b_ref[...], preferred_element_type=F32)
        he = (h1 * jax.nn.sigmoid(h1)) * h3
        y = jnp.dot(he.astype(BF16), w2b_ref[...], preferred_element_type=F32)
        y_ref[...] = _pack_bf16_pairs(y)


def _experts(x_sorted, plan, w1, w3, w2, layer):
    n_slots, wp = x_sorted.shape
    _, n_e, d, d_e = w1.shape
    ts = SLOT_TILE
    any_spec = pl.BlockSpec(memory_space=pl.ANY)
    yspec = pl.BlockSpec((ts, wp),
                         lambda i, te, tv, nu, tg, tn: (jnp.minimum(i, jnp.maximum(nu[0] - 1, 0)), 0))
    return pl.pallas_call(
        functools.partial(_expert_kernel, layer=layer),
        grid_spec=pltpu.PrefetchScalarGridSpec(
            num_scalar_prefetch=len(plan),
            grid=(n_slots // ts,),
            in_specs=[any_spec] * 4,
            out_specs=yspec,
            scratch_shapes=[pltpu.VMEM((X_RING, ts, wp), x_sorted.dtype),
                            pltpu.SemaphoreType.DMA((X_RING,)),
                            pltpu.VMEM((2, d, d_e), F32), pltpu.VMEM((2, d, d_e), F32),
                            pltpu.VMEM((2, d_e, d), F32), pltpu.SemaphoreType.DMA((2, 3)),
                            pltpu.VMEM((d, d_e), BF16), pltpu.VMEM((d, d_e), BF16),
                            pltpu.VMEM((d_e, d), BF16)],
        ),
        out_shape=jax.ShapeDtypeStruct((n_slots, d // 2), U32),
        compiler_params=_params("arbitrary"),
        name="moe_experts",
    )(*plan, x_sorted, w1, w3, w2)


def _slot_plan(eids, ranks, counts, n_tiles):
    ts = SLOT_TILE
    cnt = counts[:, 0].astype(I32)
    padded = (cnt + ts - 1) // ts * ts
    e_ids = jnp.arange(N_EXPERTS, dtype=I32)
    ends = jnp.sum(jnp.where(e_ids[:, None] <= e_ids[None, :], padded[:, None], 0), axis=0)
    starts = ends - padded
    onehot = eids[None] == e_ids[:, None, None]
    dest = ranks + jnp.sum(jnp.where(onehot, starts[:, None, None], 0), axis=0)
    tile_lo = jnp.arange(n_tiles, dtype=I32) * ts
    tile_e = jnp.minimum(jnp.sum(tile_lo[:, None] >= ends[None, :], axis=1), N_EXPERTS - 1)
    tile_e = tile_e.astype(I32)
    valid_end = jnp.sum(jnp.where(tile_e[:, None] == e_ids[None, :], (starts + cnt)[None, :], 0),
                        axis=1)
    tile_valid = jnp.clip(valid_end - tile_lo, 0, ts).astype(I32)
    n_used = (ends[-1:] // ts).astype(I32)
    nonempty = cnt > 0
    earlier = jnp.logical_and(nonempty[None, :], e_ids[None, :] < tile_e[:, None])
    tile_grp = jnp.sum(earlier, axis=1).astype(I32)
    later = jnp.logical_and(nonempty[None, :], e_ids[None, :] > tile_e[:, None])
    nxt = jnp.min(jnp.where(later, e_ids[None, :], N_EXPERTS), axis=1)
    tile_next = jnp.where(nxt < N_EXPERTS, nxt, -1).astype(I32)
    return dest, (tile_e, tile_valid, n_used, tile_grp, tile_next)


class PendingTail(NamedTuple):
    operands: Callable
    rows: int
    row0: int
    width: int


def _post_value(h1_ref, y0_ref, y1_ref, gate_ref, p_ref, g_ref, wpg_ref, wpe_ref):
    gates = gate_ref[...]
    tm = gates.shape[1]
    gt = jnp.concatenate([gates, jnp.zeros((8 - TOP_K, tm), F32)], axis=0).T
    h2 = h1_ref[...] + (gt[:, 0:1] * _unpack_bf16_pairs(y0_ref[...]) +
                        gt[:, 1:2] * _unpack_bf16_pairs(y1_ref[...]))
    gate = jax.nn.sigmoid(jnp.dot(_rms(h2, g_ref[...]).astype(BF16), wpg_ref[...],
                                  preferred_element_type=F32))
    pe = jnp.dot(p_ref[...].astype(BF16), wpe_ref[...], preferred_element_type=F32)
    return h2 + gate * pe


def _post_operands(h1, yk, gates, p, g_pl, w_pg, w_pe, *, p_row0, row0=0):
    d = h1.shape[1]
    t = yk.shape[0] // TOP_K
    wy = yk.shape[1]

    def operands(tm):
        nt = t // tm
        blk0 = row0 // tm
        p_blk0 = (p_row0 + row0) // tm
        specs = [pl.BlockSpec((tm, d), lambda i: (i + blk0, 0)),
                 pl.BlockSpec((tm, wy), lambda i: (i, 0)),
                 pl.BlockSpec((tm, wy), lambda i: (i + nt, 0)),
                 pl.BlockSpec((TOP_K, tm), lambda i: (0, i + blk0)),
                 pl.BlockSpec((tm, p.shape[1]), lambda i: (i + p_blk0, 0)),
                 _const_spec((1, d)), _const_spec(w_pg.shape), _const_spec(w_pe.shape)]
        return specs, [h1, yk, yk, gates, p, g_pl, w_pg, w_pe]

    return PendingTail(operands, t, row0, d)


def _post_kernel(*refs):
    *post_refs, o_ref = refs
    o_ref[...] = _post_value(*post_refs)


def _post(pending, *, tm):
    specs, args = pending.operands(tm)
    t, d = pending.rows, pending.width
    return pl.pallas_call(
        _post_kernel,
        grid=(t // tm,),
        in_specs=specs,
        out_specs=pl.BlockSpec((tm, d), lambda i: (i, 0)),
        out_shape=jax.ShapeDtypeStruct((t, d), F32),
        compiler_params=_params("arbitrary"),
        name="layer_post",
    )(*args)


def _final_kernel(*refs, n_post):
    post_refs, gfin_ref, o_ref = refs[:n_post], refs[n_post], refs[-1]
    o_ref[...] = _rms(_post_value(*post_refs), gfin_ref[...])


def _final(pending, g_fin, *, tm, total_rows, out=None):
    specs, args = pending.operands(tm)
    d = pending.width
    blk0 = pending.row0 // tm
    in_specs = specs + [_const_spec((1, d))]
    operands = [*args, g_fin]
    aliases = {}
    if out is not None:
        aliases = {len(operands): 0}
        in_specs.append(pl.BlockSpec(memory_space=pl.ANY))
        operands.append(out)
    return pl.pallas_call(
        functools.partial(_final_kernel, n_post=len(args)),
        grid=(pending.rows // tm,),
        in_specs=in_specs,
        out_specs=pl.BlockSpec((tm, d), lambda i: (i + blk0, 0)),
        out_shape=jax.ShapeDtypeStruct((total_rows, d), F32),
        input_output_aliases=aliases,
        compiler_params=_params("arbitrary"),
        name="final_post",
    )(*operands)


def _router_weights(wc, bc, wf, bf):
    def lanes(coarse, fine):
        z = lambda n: jnp.zeros((coarse.shape[0], n), F32)
        return jnp.concatenate([coarse, z(FINE_ROW0 - N_GROUPS), fine,
                                z(ROUTER_LANES - FINE_ROW0 - N_EXPERTS)], axis=1)

    w = lanes(wc, wf)
    b = lanes(bc[None, :], bf[None, :])
    hi = w.astype(BF16)
    lo = (w - hi.astype(F32)).astype(BF16)
    return jnp.concatenate([hi, lo], axis=1), b


def _moe(xp, eids, ranks, counts, w1, w3, w2, layer, n_parts):
    t = xp.shape[0]
    n_tiles = TOP_K * t // SLOT_TILE + N_EXPERTS
    dest, plan = _slot_plan(eids, ranks, counts, n_tiles)
    x_sorted = _dispatch(xp, dest, n_tiles * SLOT_TILE)
    y_sorted = _experts(x_sorted, plan, w1, w3, w2, layer)
    tp = t // n_parts
    return [_combine(y_sorted, dest[:, s * tp:(s + 1) * tp].reshape(-1)) for s in range(n_parts)]


def kernel(x, p, norm_mix, norm_ffn, norm_pl, final_norm, w_in_even, conv_w_even, w_out_even, w_in_odd, g_v_odd, w_s_odd, b_s_odd, w_out_odd, router_c, router_c_b, router_f, router_f_b, moe_w1, moe_w3, moe_w2, w_pe, w_pg):
    batch, seq, d = x.shape
    depth = p.shape[0]
    t = batch * seq
    tm = min(TOKEN_TILE, seq)
    tm_big = min(LIGHT_TOKEN_TILE, seq)
    p_rows = p.reshape(depth * t, -1)
    row = lambda a: a.reshape(1, -1)
    h = x.reshape(t, d)
    for i in range(depth):
        j = i // 2
        wr, br = _router_weights(router_c[i], router_c_b[i], router_f[i], router_f_b[i])
        g_ffn = row(norm_ffn[i])
        if i % 2 == 0:
            if isinstance(h, list):
                h = _post(h[0], tm=tm_big)
            ya, q, k, v = _even_in(h, row(norm_mix[i]), w_in_even[j].astype(BF16), conv_w_even[j],
                                   seq=seq, tm=tm_big)
            yb = _attention(q, k, v, batch=batch, seq=seq, tq=min(Q_TILE, seq))
            mixed = _even_out(ya, yb, h, w_out_even[j].astype(BF16), g_ffn, wr, br, tm=tm_big)
        else:
            mixed = None
            for part in (h if isinstance(h, list) else [h]):
                mixed = _odd_mixer(part, row(norm_mix[i]), w_in_odd[j].astype(BF16),
                                   row(g_v_odd[j]), w_s_odd[j], b_s_odd[j].T,
                                   w_out_odd[j].astype(BF16), g_ffn, wr, br, tm=tm,
                                   total_rows=t, earlier=mixed)
        h1, xp, gates, eids, ranks, counts = mixed
        consumer_tile, parts = ((tm_big, FINAL_PARTS) if i == depth - 1 else
                                (tm, MIXER_PARTS) if i % 2 == 0 else (None, 1))
        n_parts = parts if consumer_tile and t % (parts * consumer_tile) == 0 else 1
        yks = _moe(xp, eids, ranks, counts, moe_w1, moe_w3, moe_w2, i, n_parts)
        h = [_post_operands(h1, yk, gates, p_rows, row(norm_pl[i]), w_pg[i].astype(BF16),
                            w_pe[i].astype(BF16), p_row0=i * t, row0=s * (t // n_parts))
             for s, yk in enumerate(yks)]
    out = None
    for tail in h:
        out = _final(tail, row(final_norm), tm=tm_big, total_rows=t, out=out)
    return out.reshape(batch, seq, d)
```

```python
import functools
from typing import Callable, NamedTuple

import jax
import jax.numpy as jnp
from jax import lax
from jax.experimental import pallas as pl
from jax.experimental.pallas import tpu as pltpu
from jax.experimental.pallas import tpu_sc as plsc

EPS = 1e-6
HEAD_DIM = 64
HEADS_PER_SLAB = 2
ATTN_SLABS = 4
CHUNK = 128
N_GROUPS = 4
EXP_PER_GROUP = 8
N_EXPERTS = N_GROUPS * EXP_PER_GROUP
TOP_K = 2
ROUTER_LANES = 128
FINE_ROW0 = 8
RANK_BLOCK = 256

V7X_VMEM_BYTES = 64 * 1024 * 1024
VMEM_LIMIT = V7X_VMEM_BYTES - 8 * 1024 * 1024
V7X_SC_CORES = 2
V7X_SC_SUBCORES = 16
SC_WORKERS = V7X_SC_CORES * V7X_SC_SUBCORES
SC_ROWS = 32

TOKEN_TILE = 512
LIGHT_TOKEN_TILE = 1024
Q_TILE = 64
WINDOW_TILES = 4
SLOT_TILE = 512
X_RING = 3
MIXER_PARTS = 2
FINAL_PARTS = 4

LOG_F32_TINY_BOUND = -88.0
OFF_PENALTY = 1e30
Q_SUBTILES = 8

F32 = jnp.float32
BF16 = jnp.bfloat16
U32 = jnp.uint32
I32 = jnp.int32


def _rms(x, g):
    return x * lax.rsqrt(jnp.mean(x * x, axis=-1, keepdims=True) + EPS) * g


def _const_spec(shape):
    nd = len(shape)
    return pl.BlockSpec(shape, lambda *_: (0,) * nd, pipeline_mode=pl.Buffered(1))


def _params(*sem):
    return pltpu.CompilerParams(dimension_semantics=sem, vmem_limit_bytes=VMEM_LIMIT)


def _even_in_kernel(h_ref, g_ref, w_ref, cw_ref, ya_ref, q_ref, k_ref, v_ref, zs_ref,
                    *, tiles_per_seq, a_width, sb_width):
    tm = h_ref.shape[0]
    xn = _rms(h_ref[...], g_ref[...]).astype(BF16)
    a3 = 3 * a_width

    @pl.when(pl.program_id(0) % tiles_per_seq == 0)
    def _():
        zs_ref[0:8, :] = jnp.zeros((8, a_width), F32)

    pa = jnp.dot(xn, w_ref[:, 0:a3], preferred_element_type=F32)
    z = pa[:, 2 * a_width:a3] * pa[:, 0:a_width]
    zs_ref[8:tm + 8, :] = z
    z1 = zs_ref[7:tm + 7, :]
    z2 = zs_ref[6:tm + 6, :]
    cw = cw_ref[...]
    conv = cw[0:1, :] * z2 + cw[1:2, :] * z1 + cw[2:3, :] * z
    ya_ref[...] = (pa[:, a_width:2 * a_width] * conv).astype(BF16)
    zs_ref[0:8, :] = zs_ref[tm:tm + 8, :]

    pq = jnp.dot(xn, w_ref[:, a3:a3 + 3 * sb_width], preferred_element_type=F32)
    q_ref[...] = (pq[:, 0:sb_width] * (HEAD_DIM ** -0.5)).astype(BF16)
    k_ref[...] = pq[:, sb_width:2 * sb_width].astype(BF16)
    v_ref[...] = pq[:, 2 * sb_width:3 * sb_width].astype(BF16)


def _even_in(h, g, w_in, conv_w, *, seq, tm):
    t, d = h.shape
    a_width = conv_w.shape[1]
    sb_width = (w_in.shape[1] - 3 * a_width) // 3
    kern = functools.partial(_even_in_kernel, tiles_per_seq=seq // tm, a_width=a_width,
                             sb_width=sb_width)
    row = lambda w: pl.BlockSpec((tm, w), lambda i: (i, 0))
    return pl.pallas_call(
        kern,
        grid=(t // tm,),
        in_specs=[row(d), _const_spec((1, d)), _const_spec(w_in.shape), _const_spec(conv_w.shape)],
        out_specs=[row(a_width), row(sb_width), row(sb_width), row(sb_width)],
        out_shape=[jax.ShapeDtypeStruct((t, a_width), BF16)] +
                  [jax.ShapeDtypeStruct((t, sb_width), BF16)] * 3,
        scratch_shapes=[pltpu.VMEM((tm + 8, a_width), F32)],
        compiler_params=_params("arbitrary"),
        name="even_in",
    )(h, g, w_in, conv_w)


def _attn_kernel(q_ref, k_ref, v_ref, o_ref, acc_ref, c_ref, *, tq):
    qi = pl.program_id(2)
    slab = HEADS_PER_SLAB * HEAD_DIM
    rows_unit = HEADS_PER_SLAB * tq
    rows_tile = ATTN_SLABS * rows_unit
    rows = Q_SUBTILES * rows_tile
    units = [(j, sl) for j in range(Q_SUBTILES) for sl in range(ATTN_SLABS)]
    lane = lax.broadcasted_iota(I32, (tq, slab), 1)
    qs = []
    for j, sl in units:
        q = q_ref[j * tq:(j + 1) * tq, sl * slab:(sl + 1) * slab]
        zero = jnp.zeros_like(q)
        qs.append(jnp.concatenate([jnp.where(lane < HEAD_DIM, q, zero),
                                   jnp.where(lane >= HEAD_DIM, q, zero)], axis=0))

    def later_keys(width):
        r_i = lax.broadcasted_iota(I32, (width, width), 0)
        c_i = lax.broadcasted_iota(I32, (width, width), 1)
        return jnp.where(r_i > c_i, 1.0, 0.0).astype(BF16)

    def per_tile_column(vals):
        return jnp.concatenate([jnp.full((rows_tile, 1), v, F32) for v in vals], axis=0)

    def block(starts, width, c, mask):
        upper = later_keys(width)
        keys = [pl.ds(pl.multiple_of(s, tq), width) for s in starts]
        z = jnp.concatenate(
            [lax.dot_general(qs[u], k_ref[keys[j], sl * slab:(sl + 1) * slab],
                             (((1,), (1,)), ((), ())), preferred_element_type=F32)
             for u, (j, sl) in enumerate(units)], axis=0)
        sp = jnp.maximum(z, 0.0) + jnp.log(1.0 + jnp.exp(-jnp.abs(z)))
        spm = sp if mask is None else jnp.where(mask, sp, 0.0)
        hi = spm.astype(BF16)
        lo = (spm - hi.astype(F32)).astype(BF16)
        later = (jnp.dot(hi, upper, preferred_element_type=F32) +
                 jnp.dot(lo, upper, preferred_element_type=F32))
        a = jnp.exp(z - sp - later - c)
        if mask is not None:
            a = jnp.where(mask, a, 0.0)
        a = a.astype(BF16)
        out = jnp.concatenate(
            [jnp.dot(a[u * rows_unit:(u + 1) * rows_unit], v_ref[keys[j], sl * slab:(sl + 1) * slab],
                     preferred_element_type=F32) for u, (j, sl) in enumerate(units)], axis=0)
        return out, jnp.sum(spm, axis=1, keepdims=True)

    tile0 = qi * Q_SUBTILES
    win = WINDOW_TILES * tq
    firsts = [jnp.maximum(tile0 + j - (WINDOW_TILES - 1), 0) * tq for j in range(Q_SUBTILES)]
    col = lax.broadcasted_iota(I32, (tq, win), 1)
    row = lax.broadcasted_iota(I32, (tq, win), 0)
    mask = jnp.concatenate(
        [col + (firsts[j] - (tile0 + j) * tq) < row
         for j in range(Q_SUBTILES) for _ in range(rows_tile // tq)], axis=0)
    out0, c2 = block(firsts, win, jnp.zeros((rows, 1), F32), mask)
    acc_ref[...] = out0
    c_ref[...] = c2

    def cond(carry):
        n, cmin = carry
        return jnp.logical_and(tile0 + (Q_SUBTILES - 1) - WINDOW_TILES - n >= 0,
                               cmin < -LOG_F32_TINY_BOUND)

    def body(carry):
        n, _ = carry
        kbs = [tile0 + j - WINDOW_TILES - n for j in range(Q_SUBTILES)]
        c = c_ref[...]
        off = per_tile_column([jnp.where(kb >= 0, 0.0, OFF_PENALTY) for kb in kbs])
        out, ssum = block([jnp.maximum(kb, 0) * tq for kb in kbs], tq, c + off, None)
        acc_ref[...] += out
        c_new = c + ssum
        c_ref[...] = c_new
        return n + 1, jnp.min(c_new)

    lax.while_loop(cond, body, (jnp.int32(0), jnp.min(c2)))
    acc = acc_ref[...]
    for u, (j, sl) in enumerate(units):
        lo_r = u * rows_unit
        o_ref[j * tq:(j + 1) * tq, sl * slab:(sl + 1) * slab] = jnp.where(
            lane < HEAD_DIM, acc[lo_r:lo_r + tq], acc[lo_r + tq:lo_r + 2 * tq]).astype(o_ref.dtype)


def _attention(q, k, v, *, batch, seq, tq):
    t, w = q.shape
    wblk = ATTN_SLABS * HEADS_PER_SLAB * HEAD_DIM
    nq = seq // (Q_SUBTILES * tq)
    qspec = pl.BlockSpec((Q_SUBTILES * tq, wblk), lambda b, s, i: (b * nq + i, s))
    kvspec = pl.BlockSpec((seq, wblk), lambda b, s, i: (b, s))
    rows = Q_SUBTILES * ATTN_SLABS * HEADS_PER_SLAB * tq
    return pl.pallas_call(
        functools.partial(_attn_kernel, tq=tq),
        grid=(batch, w // wblk, nq),
        in_specs=[qspec, kvspec, kvspec],
        out_specs=qspec,
        out_shape=jax.ShapeDtypeStruct((t, w), BF16),
        scratch_shapes=[pltpu.VMEM((rows, HEADS_PER_SLAB * HEAD_DIM), F32),
                        pltpu.VMEM((rows, 1), F32)],
        compiler_params=_params("arbitrary", "arbitrary", "arbitrary"),
        name="stick_breaking_attention",
    )(q, k, v)


def _pack_bf16_pairs(x):
    n = x.shape[1] // 2
    bits = lax.bitcast_convert_type(x.astype(BF16).astype(F32), U32)
    return bits[:, 0:n] | (bits[:, n:] >> 16)


def _unpack_bf16_pairs(u):
    a = lax.bitcast_convert_type(u & jnp.uint32(0xFFFF0000), F32)
    b = lax.bitcast_convert_type(u << 16, F32)
    return jnp.concatenate([a, b], axis=1)


def _router_tail(h1, g_ref, wr_ref, br_ref, cnt0_ref,
                 h1_ref, xp_ref, gate_ref, eid_ref, rank_ref, cnt_out_ref, cnt_ref):
    h1_ref[...] = h1
    xn = _rms(h1, g_ref[...])
    xp_ref[...] = _pack_bf16_pairs(xn)
    x_hi = xn.astype(BF16)
    x_lo = (xn - x_hi.astype(F32)).astype(BF16)
    both = jnp.dot(x_hi, wr_ref[...], preferred_element_type=F32)
    logits = (both[:, 0:ROUTER_LANES] + both[:, ROUTER_LANES:] +
              jnp.dot(x_lo, wr_ref[:, 0:ROUTER_LANES], preferred_element_type=F32) + br_ref[...])
    lt = logits.T
    tm = lt.shape[1]
    row = lax.broadcasted_iota(I32, (EXP_PER_GROUP, tm), 0)
    neg = jnp.float32(-jnp.inf)

    coarse = jnp.where(row < N_GROUPS, lt[0:EXP_PER_GROUP], neg)
    cmax = jnp.max(coarse, axis=0, keepdims=True)
    gi = jnp.min(jnp.where(coarse == cmax, row, EXP_PER_GROUP), axis=0, keepdims=True)
    pg = 1.0 / jnp.sum(jnp.exp(coarse - cmax), axis=0, keepdims=True)

    fine = lt[FINE_ROW0:FINE_ROW0 + EXP_PER_GROUP]
    for grp in range(1, N_GROUPS):
        lo_r = FINE_ROW0 + grp * EXP_PER_GROUP
        fine = jnp.where(gi == grp, lt[lo_r:lo_r + EXP_PER_GROUP], fine)
    m1 = jnp.max(fine, axis=0, keepdims=True)
    i1 = jnp.min(jnp.where(fine == m1, row, EXP_PER_GROUP), axis=0, keepdims=True)
    rest = jnp.where(row == i1, neg, fine)
    m2 = jnp.max(rest, axis=0, keepdims=True)
    i2 = jnp.min(jnp.where(rest == m2, row, EXP_PER_GROUP), axis=0, keepdims=True)
    e21 = jnp.exp(m2 - m1)
    w1 = 1.0 / (1.0 + e21)
    gate_ref[...] = jnp.concatenate([pg * w1, pg * (e21 * w1)], axis=0)
    e0 = gi * EXP_PER_GROUP + i1
    e1 = gi * EXP_PER_GROUP + i2
    eid_ref[...] = jnp.concatenate([e0, e1], axis=0)

    @pl.when(pl.program_id(0) == 0)
    def _():
        cnt_ref[...] = cnt0_ref[...]

    erow = lax.broadcasted_iota(I32, (N_EXPERTS, tm), 0)
    oh0 = jnp.where(erow == e0, 1.0, 0.0)
    oh1 = jnp.where(erow == e1, 1.0, 0.0)
    pb = min(RANK_BLOCK, tm)
    r_i = lax.broadcasted_iota(I32, (pb, pb), 0)
    c_i = lax.broadcasted_iota(I32, (pb, pb), 1)
    before = jnp.where(r_i < c_i, 1.0, 0.0).astype(BF16)
    oh = jnp.concatenate([oh0, oh1], axis=0)
    pres, run = [], jnp.zeros((TOP_K * N_EXPERTS, 1), F32)
    for lo_l in range(0, tm, pb):
        ohb = oh[:, lo_l:lo_l + pb]
        pres.append(jnp.dot(ohb.astype(BF16), before, preferred_element_type=F32) + run)
        run = run + jnp.sum(ohb, axis=1, keepdims=True)
    pre = jnp.concatenate(pres, axis=1)
    pre0, pre1 = pre[0:N_EXPERTS], pre[N_EXPERTS:]
    tot0, tot1 = run[0:N_EXPERTS], run[N_EXPERTS:]
    base = cnt_ref[:, 0:1]
    r0 = jnp.sum(oh0 * (pre0 + base), axis=0, keepdims=True)
    r1 = jnp.sum(oh1 * (pre1 + (base + tot0)), axis=0, keepdims=True)
    rank_ref[...] = jnp.concatenate([r0, r1], axis=0).astype(I32)
    total = jnp.broadcast_to(base + tot0 + tot1, cnt_ref.shape)
    cnt_ref[...] = total
    cnt_out_ref[...] = total


N_TAIL_ROW_OUTPUTS = 5


def _tail_specs(total_rows, d, tm, row0=0):
    blk0 = row0 // tm
    t = total_rows
    row = lambda w: pl.BlockSpec((tm, w), lambda i: (i + blk0, 0))
    lanes = pl.BlockSpec((TOP_K, tm), lambda i: (0, i + blk0))
    cnt = pl.BlockSpec((N_EXPERTS, ROUTER_LANES), lambda i: (0, 0))
    out_specs = [row(d), row(d // 2), lanes, lanes, lanes, cnt]
    out_shape = [jax.ShapeDtypeStruct((t, d), F32), jax.ShapeDtypeStruct((t, d // 2), U32),
                 jax.ShapeDtypeStruct((TOP_K, t), F32), jax.ShapeDtypeStruct((TOP_K, t), I32),
                 jax.ShapeDtypeStruct((TOP_K, t), I32),
                 jax.ShapeDtypeStruct((N_EXPERTS, ROUTER_LANES), F32)]
    scratch = [pltpu.VMEM((N_EXPERTS, ROUTER_LANES), F32)]
    return out_specs, out_shape, scratch


def _no_slots():
    return jnp.zeros((N_EXPERTS, ROUTER_LANES), F32)


def _even_out_kernel(ya_ref, yb_ref, h_ref, w_ref, g_ref, wr_ref, br_ref, cnt0_ref, *tail_refs):
    a_width = ya_ref.shape[1]
    mix = (jnp.dot(ya_ref[...], w_ref[0:a_width, :], preferred_element_type=F32) +
           jnp.dot(yb_ref[...], w_ref[a_width:, :], preferred_element_type=F32))
    _router_tail(h_ref[...] + mix, g_ref, wr_ref, br_ref, cnt0_ref, *tail_refs)


def _even_out(ya, yb, h, w_out, g_ffn, wr, br, *, tm):
    t, d = h.shape
    out_specs, out_shape, scratch = _tail_specs(t, d, tm)
    row = lambda w: pl.BlockSpec((tm, w), lambda i: (i, 0))
    cnt0 = _no_slots()
    return pl.pallas_call(
        _even_out_kernel,
        grid=(t // tm,),
        in_specs=[row(ya.shape[1]), row(yb.shape[1]), row(d), _const_spec(w_out.shape),
                  _const_spec((1, d)), _const_spec(wr.shape), _const_spec(br.shape),
                  _const_spec(cnt0.shape)],
        out_specs=out_specs,
        out_shape=out_shape,
        scratch_shapes=scratch,
        compiler_params=_params("arbitrary"),
        name="even_out_router",
    )(ya, yb, h, w_out, g_ffn, wr, br, cnt0)


def _gelu_tanh(x):
    return 0.5 * x * (1.0 + jnp.tanh(0.7978845608028654 * (x + 0.044715 * (x * x * x))))


def _odd_kernel(*refs, n_prev, n_alias):
    h_refs, refs = refs[:max(n_prev, 1)], refs[max(n_prev, 1):]
    (g_ref, w_in_ref, gv_ref, ws_ref, bs_ref, w_out_ref, gf_ref, wr_ref, br_ref, cnt0_ref,
     *rest) = refs
    *tail_refs, y_ref = rest[n_alias:]
    h = _post_value(*h_refs) if n_prev else h_refs[0][...]
    tm = h.shape[0]
    c_width = gv_ref.shape[1]
    n_grp = ws_ref.shape[0]
    gdim = c_width // n_grp
    xn = _rms(h, g_ref[...]).astype(BF16)
    u = _gelu_tanh(jnp.dot(xn, w_in_ref[:, 0:c_width], preferred_element_type=F32))
    v = _gelu_tanh(jnp.dot(xn, w_in_ref[:, c_width:], preferred_element_type=F32))
    v = _rms(v, gv_ref[...]).astype(BF16)
    r_i = lax.broadcasted_iota(I32, (CHUNK, CHUNK), 0)
    c_i = lax.broadcasted_iota(I32, (CHUNK, CHUNK), 1)
    causal = c_i <= r_i
    bs = bs_ref[...]
    for grp in range(n_grp):
        w_m = jnp.where(causal, ws_ref[grp], 0.0).astype(BF16)
        bias = bs[:, grp:grp + 1]
        cols = slice(grp * gdim, (grp + 1) * gdim)
        for c in range(tm // CHUNK):
            rows = slice(c * CHUNK, (c + 1) * CHUNK)
            gate = jnp.dot(w_m, v[rows, cols], preferred_element_type=F32) + bias
            y_ref[rows, cols] = (u[rows, cols] * gate).astype(BF16)
    mix = jnp.dot(y_ref[...], w_out_ref[...], preferred_element_type=F32)
    _router_tail(h + mix, gf_ref, wr_ref, br_ref, cnt0_ref, *tail_refs)


def _odd_mixer(h_or_prev, g_mix, w_in, g_v, w_s, b_s_t, w_out, g_ffn, wr, br, *, tm,
               total_rows=None, earlier=None):
    cnt0 = _no_slots() if earlier is None else earlier[N_TAIL_ROW_OUTPUTS]
    consts = [g_mix, w_in, g_v, w_s, b_s_t, w_out, g_ffn, wr, br, cnt0]
    pending = isinstance(h_or_prev, PendingTail)
    if pending:
        head_specs, head_args = h_or_prev.operands(tm)
        t, d, row0 = h_or_prev.rows, h_or_prev.width, h_or_prev.row0
    else:
        t, d = h_or_prev.shape
        row0 = 0
        head_specs, head_args = [pl.BlockSpec((tm, d), lambda i: (i, 0))], [h_or_prev]
    out_specs, out_shape, scratch = _tail_specs(total_rows or t, d, tm, row0)
    n_prev = len(head_args) if pending else 0
    in_specs = head_specs + [_const_spec(c.shape) for c in consts]
    operands = [*head_args, *consts]
    aliases = {}
    if earlier is not None:
        aliases = {len(operands) + k: k for k in range(N_TAIL_ROW_OUTPUTS)}
        in_specs += [pl.BlockSpec(memory_space=pl.ANY)] * N_TAIL_ROW_OUTPUTS
        operands += list(earlier[:N_TAIL_ROW_OUTPUTS])
    return pl.pallas_call(
        functools.partial(_odd_kernel, n_prev=n_prev, n_alias=len(aliases)),
        grid=(t // tm,),
        in_specs=in_specs,
        out_specs=out_specs,
        out_shape=out_shape,
        input_output_aliases=aliases,
        scratch_shapes=scratch + [pltpu.VMEM((tm, g_v.shape[1]), BF16)],
        compiler_params=_params("arbitrary"),
        name="odd_mixer_router",
    )(*operands)


def _sc_mesh():
    return plsc.VectorSubcoreMesh(core_axis_name="c", subcore_axis_name="s")


def _sc_worker_base(rows_per_worker):
    wid = lax.axis_index("s") * V7X_SC_CORES + lax.axis_index("c")
    return wid * rows_per_worker


def _dispatch(xp, dest, n_slots):
    t, w = xp.shape
    per_worker = t // SC_WORKERS
    n_chunks = per_worker // SC_ROWS
    assert per_worker * SC_WORKERS == t and n_chunks * SC_ROWS == per_worker and n_chunks % 2 == 0

    idx_t = pltpu.VMEM((SC_ROWS,), I32)

    @functools.partial(
        pl.kernel, mesh=_sc_mesh(),
        out_type=jax.ShapeDtypeStruct((n_slots, w), xp.dtype),
        scratch_types=[pltpu.VMEM((2, SC_ROWS, w), xp.dtype), idx_t, idx_t, idx_t, idx_t,
                       pltpu.SemaphoreType.DMA((2,)), pltpu.SemaphoreType.DMA((2,))],
    )
    def kern(x_hbm, d_hbm, out_hbm, rows_v, i00, i01, i10, i11, rsem, wsem):
        base = _sc_worker_base(per_worker)
        idx = ((i00, i01), (i10, i11))

        def load(c, b):
            off = pl.multiple_of(base + c * SC_ROWS, 8)
            pltpu.async_copy(x_hbm.at[pl.ds(off, SC_ROWS)], rows_v.at[b], rsem.at[b])
            for k in range(TOP_K):
                pltpu.sync_copy(d_hbm.at[k, pl.ds(off, SC_ROWS)], idx[b][k])

        def wait_load(b):
            pltpu.make_async_copy(x_hbm.at[pl.ds(0, SC_ROWS)], rows_v.at[b], rsem.at[b]).wait()

        def scatter(b):
            for k in range(TOP_K):
                pltpu.async_copy(rows_v.at[b], out_hbm.at[idx[b][k]], wsem.at[b])

        def wait_scatter(b):
            for k in range(TOP_K):
                pltpu.make_async_copy(rows_v.at[b], out_hbm.at[idx[b][k]], wsem.at[b]).wait()

        load(0, 0)

        @pl.loop(0, n_chunks, step=2)
        def _(c0):
            for b in range(2):
                c = c0 + b
                wait_load(b)
                scatter(b)

                @pl.when(c + 1 < n_chunks)
                def _():
                    @pl.when(c >= 1)
                    def _():
                        wait_scatter(1 - b)
                    load(c + 1, 1 - b)

        wait_scatter(0)
        wait_scatter(1)

    return kern(xp, dest)


def _combine(y, idx):
    n = idx.shape[0]
    w = y.shape[1]
    rows = 2 * SC_ROWS
    per_worker = n // SC_WORKERS
    n_chunks = per_worker // rows
    assert per_worker * SC_WORKERS == n and n_chunks * rows == per_worker and n_chunks % 2 == 0

    idx_t = pltpu.VMEM((rows,), I32)

    @functools.partial(
        pl.kernel, mesh=_sc_mesh(),
        out_type=jax.ShapeDtypeStruct((n, w), y.dtype),
        scratch_types=[pltpu.VMEM((2, rows, w), y.dtype), idx_t, idx_t,
                       pltpu.SemaphoreType.DMA((2,)), pltpu.SemaphoreType.DMA((2,))],
    )
    def kern(y_hbm, i_hbm, out_hbm, rows_v, i0, i1, gsem, wsem):
        base = _sc_worker_base(per_worker)
        ibuf = (i0, i1)

        def gather(c, b):
            off = pl.multiple_of(base + c * rows, 8)
            pltpu.sync_copy(i_hbm.at[pl.ds(off, rows)], ibuf[b])
            pltpu.async_copy(y_hbm.at[ibuf[b]], rows_v.at[b], gsem.at[b])

        def wait_gather(b):
            pltpu.make_async_copy(y_hbm.at[ibuf[b]], rows_v.at[b], gsem.at[b]).wait()

        def write(c, b):
            off = pl.multiple_of(base + c * rows, 8)
            pltpu.async_copy(rows_v.at[b], out_hbm.at[pl.ds(off, rows)], wsem.at[b])

        def wait_write(b):
            pltpu.make_async_copy(rows_v.at[b], out_hbm.at[pl.ds(0, rows)], wsem.at[b]).wait()

        gather(0, 0)

        @pl.loop(0, n_chunks, step=2)
        def _(c0):
            for b in range(2):
                c = c0 + b

                @pl.when(c + 1 < n_chunks)
                def _():
                    @pl.when(c >= 1)
                    def _():
                        wait_write(1 - b)
                    gather(c + 1, 1 - b)

                wait_gather(b)
                write(c, b)

        wait_write(0)
        wait_write(1)

    return kern(y, idx)


def _expert_kernel(te_ref, tv_ref, nu_ref, tg_ref, tn_ref, x_hbm, w1_hbm, w3_hbm, w2_hbm, y_ref,
                   xbuf_ref, xsem, w1f_ref, w3f_ref, w2f_ref, wsem, w1b_ref, w3b_ref, w2b_ref,
                   *, layer):
    i = pl.program_id(0)
    n_used = nu_ref[0]
    ts = xbuf_ref.shape[1]

    def x_copy(tile):
        slot = tile % X_RING
        rows = pl.ds(pl.multiple_of(tile * ts, ts), ts)
        return pltpu.make_async_copy(x_hbm.at[rows], xbuf_ref.at[slot], xsem.at[slot])

    def w_copies(expert, slot):
        pairs = ((w1_hbm, w1f_ref), (w3_hbm, w3f_ref), (w2_hbm, w2f_ref))
        return [pltpu.make_async_copy(src.at[layer, expert], dst.at[slot], wsem.at[slot, j])
                for j, (src, dst) in enumerate(pairs)]

    @pl.when(jnp.logical_and(i == 0, n_used > 0))
    def _():
        for tile in range(X_RING - 1):
            @pl.when(tile < n_used)
            def _():
                x_copy(tile).start()
        for c in w_copies(te_ref[0], 0):
            c.start()

    @pl.when(i < n_used)
    def _():
        @pl.when(i + (X_RING - 1) < n_used)
        def _():
            x_copy(i + (X_RING - 1)).start()

        prev = te_ref[jnp.maximum(i - 1, 0)]

        @pl.when(jnp.logical_or(i == 0, te_ref[i] != prev))
        def _():
            slot = tg_ref[i] % 2
            for c in w_copies(te_ref[i], slot):
                c.wait()
            nxt = tn_ref[i]

            @pl.when(nxt >= 0)
            def _():
                for c in w_copies(nxt, 1 - slot):
                    c.start()

            w1b_ref[...] = w1f_ref[slot].astype(BF16)
            w3b_ref[...] = w3f_ref[slot].astype(BF16)
            w2b_ref[...] = w2f_ref[slot].astype(BF16)

        x_copy(i).wait()
        xu = xbuf_ref[i % X_RING]
        rowid = lax.broadcasted_iota(I32, xu.shape, 0)
        xu = jnp.where(rowid < tv_ref[i], xu, jnp.zeros_like(xu))
        x = _unpack_bf16_pairs(xu).astype(BF16)
        h1 = jnp.dot(x, w1b_ref[...], preferred_element_type=F32)
        h3 = jnp.dot(x, w3b_ref[...], preferred_element_type=F32)
        he = (h1 * jax.nn.sigmoid(h1)) * h3
        y = jnp.dot(he.astype(BF16), w2b_ref[...], preferred_element_type=F32)
        y_ref[...] = _pack_bf16_pairs(y)


def _experts(x_sorted, plan, w1, w3, w2, layer):
    n_slots, wp = x_sorted.shape
    _, n_e, d, d_e = w1.shape
    ts = SLOT_TILE
    any_spec = pl.BlockSpec(memory_space=pl.ANY)
    yspec = pl.BlockSpec((ts, wp),
                         lambda i, te, tv, nu, tg, tn: (jnp.minimum(i, jnp.maximum(nu[0] - 1, 0)), 0))
    return pl.pallas_call(
        functools.partial(_expert_kernel, layer=layer),
        grid_spec=pltpu.PrefetchScalarGridSpec(
            num_scalar_prefetch=len(plan),
            grid=(n_slots // ts,),
            in_specs=[any_spec] * 4,
            out_specs=yspec,
            scratch_shapes=[pltpu.VMEM((X_RING, ts, wp), x_sorted.dtype),
                            pltpu.SemaphoreType.DMA((X_RING,)),
                            pltpu.VMEM((2, d, d_e), F32), pltpu.VMEM((2, d, d_e), F32),
                            pltpu.VMEM((2, d_e, d), F32), pltpu.SemaphoreType.DMA((2, 3)),
                            pltpu.VMEM((d, d_e), BF16), pltpu.VMEM((d, d_e), BF16),
                            pltpu.VMEM((d_e, d), BF16)],
        ),
        out_shape=jax.ShapeDtypeStruct((n_slots, d // 2), U32),
        compiler_params=_params("arbitrary"),
        name="moe_experts",
    )(*plan, x_sorted, w1, w3, w2)


def _slot_plan(eids, ranks, counts, n_tiles):
    ts = SLOT_TILE
    cnt = counts[:, 0].astype(I32)
    padded = (cnt + ts - 1) // ts * ts
    e_ids = jnp.arange(N_EXPERTS, dtype=I32)
    ends = jnp.sum(jnp.where(e_ids[:, None] <= e_ids[None, :], padded[:, None], 0), axis=0)
    starts = ends - padded
    onehot = eids[None] == e_ids[:, None, None]
    dest = ranks + jnp.sum(jnp.where(onehot, starts[:, None, None], 0), axis=0)
    tile_lo = jnp.arange(n_tiles, dtype=I32) * ts
    tile_e = jnp.minimum(jnp.sum(tile_lo[:, None] >= ends[None, :], axis=1), N_EXPERTS - 1)
    tile_e = tile_e.astype(I32)
    valid_end = jnp.sum(jnp.where(tile_e[:, None] == e_ids[None, :], (starts + cnt)[None, :], 0),
                        axis=1)
    tile_valid = jnp.clip(valid_end - tile_lo, 0, ts).astype(I32)
    n_used = (ends[-1:] // ts).astype(I32)
    nonempty = cnt > 0
    earlier = jnp.logical_and(nonempty[None, :], e_ids[None, :] < tile_e[:, None])
    tile_grp = jnp.sum(earlier, axis=1).astype(I32)
    later = jnp.logical_and(nonempty[None, :], e_ids[None, :] > tile_e[:, None])
    nxt = jnp.min(jnp.where(later, e_ids[None, :], N_EXPERTS), axis=1)
    tile_next = jnp.where(nxt < N_EXPERTS, nxt, -1).astype(I32)
    return dest, (tile_e, tile_valid, n_used, tile_grp, tile_next)


class PendingTail(NamedTuple):
    operands: Callable
    rows: int
    row0: int
    width: int


def _post_value(h1_ref, y0_ref, y1_ref, gate_ref, p_ref, g_ref, wpg_ref, wpe_ref):
    gates = gate_ref[...]
    tm = gates.shape[1]
    gt = jnp.concatenate([gates, jnp.zeros((8 - TOP_K, tm), F32)], axis=0).T
    h2 = h1_ref[...] + (gt[:, 0:1] * _unpack_bf16_pairs(y0_ref[...]) +
                        gt[:, 1:2] * _unpack_bf16_pairs(y1_ref[...]))
    gate = jax.nn.sigmoid(jnp.dot(_rms(h2, g_ref[...]).astype(BF16), wpg_ref[...],
                                  preferred_element_type=F32))
    pe = jnp.dot(p_ref[...].astype(BF16), wpe_ref[...], preferred_element_type=F32)
    return h2 + gate * pe


def _post_operands(h1, yk, gates, p, g_pl, w_pg, w_pe, *, p_row0, row0=0):
    d = h1.shape[1]
    t = yk.shape[0] // TOP_K
    wy = yk.shape[1]

    def operands(tm):
        nt = t // tm
        blk0 = row0 // tm
        p_blk0 = (p_row0 + row0) // tm
        specs = [pl.BlockSpec((tm, d), lambda i: (i + blk0, 0)),
                 pl.BlockSpec((tm, wy), lambda i: (i, 0)),
                 pl.BlockSpec((tm, wy), lambda i: (i + nt, 0)),
                 pl.BlockSpec((TOP_K, tm), lambda i: (0, i + blk0)),
                 pl.BlockSpec((tm, p.shape[1]), lambda i: (i + p_blk0, 0)),
                 _const_spec((1, d)), _const_spec(w_pg.shape), _const_spec(w_pe.shape)]
        return specs, [h1, yk, yk, gates, p, g_pl, w_pg, w_pe]

    return PendingTail(operands, t, row0, d)


def _post_kernel(*refs):
    *post_refs, o_ref = refs
    o_ref[...] = _post_value(*post_refs)


def _post(pending, *, tm):
    specs, args = pending.operands(tm)
    t, d = pending.rows, pending.width
    return pl.pallas_call(
        _post_kernel,
        grid=(t // tm,),
        in_specs=specs,
        out_specs=pl.BlockSpec((tm, d), lambda i: (i, 0)),
        out_shape=jax.ShapeDtypeStruct((t, d), F32),
        compiler_params=_params("arbitrary"),
        name="layer_post",
    )(*args)


def _final_kernel(*refs, n_post):
    post_refs, gfin_ref, o_ref = refs[:n_post], refs[n_post], refs[-1]
    o_ref[...] = _rms(_post_value(*post_refs), gfin_ref[...])


def _final(pending, g_fin, *, tm, total_rows, out=None):
    specs, args = pending.operands(tm)
    d = pending.width
    blk0 = pending.row0 // tm
    in_specs = specs + [_const_spec((1, d))]
    operands = [*args, g_fin]
    aliases = {}
    if out is not None:
        aliases = {len(operands): 0}
        in_specs.append(pl.BlockSpec(memory_space=pl.ANY))
        operands.append(out)
    return pl.pallas_call(
        functools.partial(_final_kernel, n_post=len(args)),
        grid=(pending.rows // tm,),
        in_specs=in_specs,
        out_specs=pl.BlockSpec((tm, d), lambda i: (i + blk0, 0)),
        out_shape=jax.ShapeDtypeStruct((total_rows, d), F32),
        input_output_aliases=aliases,
        compiler_params=_params("arbitrary"),
        name="final_post",
    )(*operands)


def _router_weights(wc, bc, wf, bf):
    def lanes(coarse, fine):
        z = lambda n: jnp.zeros((coarse.shape[0], n), F32)
        return jnp.concatenate([coarse, z(FINE_ROW0 - N_GROUPS), fine,
                                z(ROUTER_LANES - FINE_ROW0 - N_EXPERTS)], axis=1)

    w = lanes(wc, wf)
    b = lanes(bc[None, :], bf[None, :])
    hi = w.astype(BF16)
    lo = (w - hi.astype(F32)).astype(BF16)
    return jnp.concatenate([hi, lo], axis=1), b


def _moe(xp, eids, ranks, counts, w1, w3, w2, layer, n_parts):
    t = xp.shape[0]
    n_tiles = TOP_K * t // SLOT_TILE + N_EXPERTS
    dest, plan = _slot_plan(eids, ranks, counts, n_tiles)
    x_sorted = _dispatch(xp, dest, n_tiles * SLOT_TILE)
    y_sorted = _experts(x_sorted, plan, w1, w3, w2, layer)
    tp = t // n_parts
    return [_combine(y_sorted, dest[:, s * tp:(s + 1) * tp].reshape(-1)) for s in range(n_parts)]


def kernel(x, p, norm_mix, norm_ffn, norm_pl, final_norm, w_in_even, conv_w_even, w_out_even, w_in_odd, g_v_odd, w_s_odd, b_s_odd, w_out_odd, router_c, router_c_b, router_f, router_f_b, moe_w1, moe_w3, moe_w2, w_pe, w_pg):
    batch, seq, d = x.shape
    depth = p.shape[0]
    t = batch * seq
    tm = min(TOKEN_TILE, seq)
    tm_big = min(LIGHT_TOKEN_TILE, seq)
    p_rows = p.reshape(depth * t, -1)
    row = lambda a: a.reshape(1, -1)
    h = x.reshape(t, d)
    for i in range(depth):
        j = i // 2
        wr, br = _router_weights(router_c[i], router_c_b[i], router_f[i], router_f_b[i])
        g_ffn = row(norm_ffn[i])
        if i % 2 == 0:
            if isinstance(h, list):
                h = _post(h[0], tm=tm_big)
            ya, q, k, v = _even_in(h, row(norm_mix[i]), w_in_even[j].astype(BF16), conv_w_even[j],
                                   seq=seq, tm=tm_big)
            yb = _attention(q, k, v, batch=batch, seq=seq, tq=min(Q_TILE, seq))
            mixed = _even_out(ya, yb, h, w_out_even[j].astype(BF16), g_ffn, wr, br, tm=tm_big)
        else:
            mixed = None
            for part in (h if isinstance(h, list) else [h]):
                mixed = _odd_mixer(part, row(norm_mix[i]), w_in_odd[j].astype(BF16),
                                   row(g_v_odd[j]), w_s_odd[j], b_s_odd[j].T,
                                   w_out_odd[j].astype(BF16), g_ffn, wr, br, tm=tm,
                                   total_rows=t, earlier=mixed)
        h1, xp, gates, eids, ranks, counts = mixed
        consumer_tile, parts = ((tm_big, FINAL_PARTS) if i == depth - 1 else
                                (tm, MIXER_PARTS) if i % 2 == 0 else (None, 1))
        n_parts = parts if consumer_tile and t % (parts * consumer_tile) == 0 else 1
        yks = _moe(xp, eids, ranks, counts, moe_w1, moe_w3, moe_w2, i, n_parts)
        h = [_post_operands(h1, yk, gates, p_rows, row(norm_pl[i]), w_pg[i].astype(BF16),
                            w_pe[i].astype(BF16), p_row0=i * t, row0=s * (t // n_parts))
             for s, yk in enumerate(yks)]
    out = None
    for tail in h:
        out = _final(tail, row(final_norm), tm=tm_big, total_rows=t, out=out)
    return out.reshape(batch, seq, d)
```

```python
import functools
from typing import Callable, NamedTuple

import jax
import jax.numpy as jnp
from jax import lax
from jax.experimental import pallas as pl
from jax.experimental.pallas import tpu as pltpu
from jax.experimental.pallas import tpu_sc as plsc

EPS = 1e-6
HEAD_DIM = 64
HEADS_PER_SLAB = 2
ATTN_SLABS = 4
CHUNK = 128
N_GROUPS = 4
EXP_PER_GROUP = 8
N_EXPERTS = N_GROUPS * EXP_PER_GROUP
TOP_K = 2
ROUTER_LANES = 128
FINE_ROW0 = 8
RANK_BLOCK = 256

V7X_VMEM_BYTES = 64 * 1024 * 1024
VMEM_LIMIT = V7X_VMEM_BYTES - 8 * 1024 * 1024
V7X_SC_CORES = 2
V7X_SC_SUBCORES = 16
SC_WORKERS = V7X_SC_CORES * V7X_SC_SUBCORES
SC_ROWS = 32

TOKEN_TILE = 512
LIGHT_TOKEN_TILE = 1024
Q_TILE = 64
WINDOW_TILES = 4
SLOT_TILE = 512
X_RING = 3
MIXER_PARTS = 2
FINAL_PARTS = 4

LOG_F32_TINY_BOUND = -88.0
OFF_PENALTY = 1e30
Q_SUBTILES = 4

F32 = jnp.float32
BF16 = jnp.bfloat16
U32 = jnp.uint32
I32 = jnp.int32


def _rms(x, g):
    return x * lax.rsqrt(jnp.mean(x * x, axis=-1, keepdims=True) + EPS) * g


def _const_spec(shape):
    nd = len(shape)
    return pl.BlockSpec(shape, lambda *_: (0,) * nd, pipeline_mode=pl.Buffered(1))


def _params(*sem):
    return pltpu.CompilerParams(dimension_semantics=sem, vmem_limit_bytes=VMEM_LIMIT)


def _even_mix_kernel(h_ref, g_ref, w_ref, cw_ref, ya_ref, yb_ref, zs_ref, q_ref, k_ref, v_ref,
                     acc_ref, c_ref, *, tq, a_width, sb_width):
    qi = pl.program_id(1)
    tm = h_ref.shape[0]
    xn = _rms(h_ref[...], g_ref[...]).astype(BF16)
    a3 = 3 * a_width

    @pl.when(qi == 0)
    def _():
        zs_ref[0:8, :] = jnp.zeros((8, a_width), F32)

    pa = jnp.dot(xn, w_ref[:, 0:a3], preferred_element_type=F32)
    z = pa[:, 2 * a_width:a3] * pa[:, 0:a_width]
    zs_ref[8:tm + 8, :] = z
    z1 = zs_ref[7:tm + 7, :]
    z2 = zs_ref[6:tm + 6, :]
    cw = cw_ref[...]
    conv = cw[0:1, :] * z2 + cw[1:2, :] * z1 + cw[2:3, :] * z
    ya_ref[...] = (pa[:, a_width:2 * a_width] * conv).astype(BF16)
    zs_ref[0:8, :] = zs_ref[tm:tm + 8, :]

    pq = jnp.dot(xn, w_ref[:, a3:a3 + 3 * sb_width], preferred_element_type=F32)
    q_ref[...] = (pq[:, 0:sb_width] * (HEAD_DIM ** -0.5)).astype(BF16)
    seq_rows = pl.ds(pl.multiple_of(qi * tm, tm), tm)
    k_ref[seq_rows, :] = pq[:, sb_width:2 * sb_width].astype(BF16)
    v_ref[seq_rows, :] = pq[:, 2 * sb_width:3 * sb_width].astype(BF16)
    _attn_tiles(q_ref, k_ref, v_ref, yb_ref, acc_ref, c_ref, qi, tq=tq)


def _even_mix(h, g, w_in, conv_w, *, batch, seq, tq):
    t, d = h.shape
    a_width = conv_w.shape[1]
    sb_width = (w_in.shape[1] - 3 * a_width) // 3
    assert sb_width == ATTN_SLABS * HEADS_PER_SLAB * HEAD_DIM
    tm = Q_SUBTILES * tq
    nq = seq // tm
    rows = Q_SUBTILES * ATTN_SLABS * HEADS_PER_SLAB * tq
    row = lambda w: pl.BlockSpec((tm, w), lambda b, i: (b * nq + i, 0))
    return pl.pallas_call(
        functools.partial(_even_mix_kernel, tq=tq, a_width=a_width, sb_width=sb_width),
        grid=(batch, nq),
        in_specs=[row(d), _const_spec((1, d)), _const_spec(w_in.shape), _const_spec(conv_w.shape)],
        out_specs=[row(a_width), row(sb_width)],
        out_shape=[jax.ShapeDtypeStruct((t, a_width), BF16),
                   jax.ShapeDtypeStruct((t, sb_width), BF16)],
        scratch_shapes=[pltpu.VMEM((tm + 8, a_width), F32),
                        pltpu.VMEM((tm, sb_width), BF16),
                        pltpu.VMEM((seq, sb_width), BF16), pltpu.VMEM((seq, sb_width), BF16),
                        pltpu.VMEM((rows, HEADS_PER_SLAB * HEAD_DIM), F32),
                        pltpu.VMEM((rows, 1), F32)],
        compiler_params=_params("arbitrary", "arbitrary"),
        name="even_mix",
    )(h, g, w_in, conv_w)


def _attn_tiles(q_ref, k_ref, v_ref, o_ref, acc_ref, c_ref, qi, *, tq):
    slab = HEADS_PER_SLAB * HEAD_DIM
    rows_unit = HEADS_PER_SLAB * tq
    rows_tile = ATTN_SLABS * rows_unit
    rows = Q_SUBTILES * rows_tile
    units = [(j, sl) for j in range(Q_SUBTILES) for sl in range(ATTN_SLABS)]
    lane = lax.broadcasted_iota(I32, (tq, slab), 1)
    qs = []
    for j, sl in units:
        q = q_ref[j * tq:(j + 1) * tq, sl * slab:(sl + 1) * slab]
        zero = jnp.zeros_like(q)
        qs.append(jnp.concatenate([jnp.where(lane < HEAD_DIM, q, zero),
                                   jnp.where(lane >= HEAD_DIM, q, zero)], axis=0))

    def later_keys(width):
        r_i = lax.broadcasted_iota(I32, (width, width), 0)
        c_i = lax.broadcasted_iota(I32, (width, width), 1)
        return jnp.where(r_i > c_i, 1.0, 0.0).astype(BF16)

    def per_tile_column(vals):
        return jnp.concatenate([jnp.full((rows_tile, 1), v, F32) for v in vals], axis=0)

    def block(starts, width, c, mask):
        upper = later_keys(width)
        keys = [pl.ds(pl.multiple_of(s, tq), width) for s in starts]
        z = jnp.concatenate(
            [lax.dot_general(qs[u], k_ref[keys[j], sl * slab:(sl + 1) * slab],
                             (((1,), (1,)), ((), ())), preferred_element_type=F32)
             for u, (j, sl) in enumerate(units)], axis=0)
        sp = jnp.maximum(z, 0.0) + jnp.log(1.0 + jnp.exp(-jnp.abs(z)))
        spm = sp if mask is None else jnp.where(mask, sp, 0.0)
        hi = spm.astype(BF16)
        lo = (spm - hi.astype(F32)).astype(BF16)
        later = (jnp.dot(hi, upper, preferred_element_type=F32) +
                 jnp.dot(lo, upper, preferred_element_type=F32))
        a = jnp.exp(z - sp - later - c)
        if mask is not None:
            a = jnp.where(mask, a, 0.0)
        a = a.astype(BF16)
        out = jnp.concatenate(
            [jnp.dot(a[u * rows_unit:(u + 1) * rows_unit], v_ref[keys[j], sl * slab:(sl + 1) * slab],
                     preferred_element_type=F32) for u, (j, sl) in enumerate(units)], axis=0)
        return out, jnp.sum(spm, axis=1, keepdims=True)

    tile0 = qi * Q_SUBTILES
    win = WINDOW_TILES * tq
    firsts = [jnp.maximum(tile0 + j - (WINDOW_TILES - 1), 0) * tq for j in range(Q_SUBTILES)]
    col = lax.broadcasted_iota(I32, (tq, win), 1)
    row = lax.broadcasted_iota(I32, (tq, win), 0)
    mask = jnp.concatenate(
        [col + (firsts[j] - (tile0 + j) * tq) < row
         for j in range(Q_SUBTILES) for _ in range(rows_tile // tq)], axis=0)
    out0, c2 = block(firsts, win, jnp.zeros((rows, 1), F32), mask)
    acc_ref[...] = out0
    c_ref[...] = c2

    def cond(carry):
        n, cmin = carry
        return jnp.logical_and(tile0 + (Q_SUBTILES - 1) - WINDOW_TILES - n >= 0,
                               cmin < -LOG_F32_TINY_BOUND)

    def body(carry):
        n, _ = carry
        kbs = [tile0 + j - WINDOW_TILES - n for j in range(Q_SUBTILES)]
        c = c_ref[...]
        off = per_tile_column([jnp.where(kb >= 0, 0.0, OFF_PENALTY) for kb in kbs])
        out, ssum = block([jnp.maximum(kb, 0) * tq for kb in kbs], tq, c + off, None)
        acc_ref[...] += out
        c_new = c + ssum
        c_ref[...] = c_new
        return n + 1, jnp.min(c_new)

    lax.while_loop(cond, body, (jnp.int32(0), jnp.min(c2)))
    acc = acc_ref[...]
    for u, (j, sl) in enumerate(units):
        lo_r = u * rows_unit
        o_ref[j * tq:(j + 1) * tq, sl * slab:(sl + 1) * slab] = jnp.where(
            lane < HEAD_DIM, acc[lo_r:lo_r + tq], acc[lo_r + tq:lo_r + 2 * tq]).astype(o_ref.dtype)


def _pack_bf16_pairs(x):
    n = x.shape[1] // 2
    bits = lax.bitcast_convert_type(x.astype(BF16).astype(F32), U32)
    return bits[:, 0:n] | (bits[:, n:] >> 16)


def _unpack_bf16_pairs(u):
    a = lax.bitcast_convert_type(u & jnp.uint32(0xFFFF0000), F32)
    b = lax.bitcast_convert_type(u << 16, F32)
    return jnp.concatenate([a, b], axis=1)


def _router_tail(h1, g_ref, wr_ref, br_ref, cnt0_ref,
                 h1_ref, xp_ref, gate_ref, eid_ref, rank_ref, cnt_out_ref, cnt_ref):
    h1_ref[...] = h1
    xn = _rms(h1, g_ref[...])
    xp_ref[...] = _pack_bf16_pairs(xn)
    x_hi = xn.astype(BF16)
    x_lo = (xn - x_hi.astype(F32)).astype(BF16)
    both = jnp.dot(x_hi, wr_ref[...], preferred_element_type=F32)
    logits = (both[:, 0:ROUTER_LANES] + both[:, ROUTER_LANES:] +
              jnp.dot(x_lo, wr_ref[:, 0:ROUTER_LANES], preferred_element_type=F32) + br_ref[...])
    lt = logits.T
    tm = lt.shape[1]
    row = lax.broadcasted_iota(I32, (EXP_PER_GROUP, tm), 0)
    neg = jnp.float32(-jnp.inf)

    coarse = jnp.where(row < N_GROUPS, lt[0:EXP_PER_GROUP], neg)
    cmax = jnp.max(coarse, axis=0, keepdims=True)
    gi = jnp.min(jnp.where(coarse == cmax, row, EXP_PER_GROUP), axis=0, keepdims=True)
    pg = 1.0 / jnp.sum(jnp.exp(coarse - cmax), axis=0, keepdims=True)

    fine = lt[FINE_ROW0:FINE_ROW0 + EXP_PER_GROUP]
    for grp in range(1, N_GROUPS):
        lo_r = FINE_ROW0 + grp * EXP_PER_GROUP
        fine = jnp.where(gi == grp, lt[lo_r:lo_r + EXP_PER_GROUP], fine)
    m1 = jnp.max(fine, axis=0, keepdims=True)
    i1 = jnp.min(jnp.where(fine == m1, row, EXP_PER_GROUP), axis=0, keepdims=True)
    rest = jnp.where(row == i1, neg, fine)
    m2 = jnp.max(rest, axis=0, keepdims=True)
    i2 = jnp.min(jnp.where(rest == m2, row, EXP_PER_GROUP), axis=0, keepdims=True)
    e21 = jnp.exp(m2 - m1)
    w1 = 1.0 / (1.0 + e21)
    gate_ref[...] = jnp.concatenate([pg * w1, pg * (e21 * w1)], axis=0)
    e0 = gi * EXP_PER_GROUP + i1
    e1 = gi * EXP_PER_GROUP + i2
    eid_ref[...] = jnp.concatenate([e0, e1], axis=0)

    @pl.when(pl.program_id(0) == 0)
    def _():
        cnt_ref[...] = cnt0_ref[...]

    erow = lax.broadcasted_iota(I32, (N_EXPERTS, tm), 0)
    oh0 = jnp.where(erow == e0, 1.0, 0.0)
    oh1 = jnp.where(erow == e1, 1.0, 0.0)
    pb = min(RANK_BLOCK, tm)
    r_i = lax.broadcasted_iota(I32, (pb, pb), 0)
    c_i = lax.broadcasted_iota(I32, (pb, pb), 1)
    before = jnp.where(r_i < c_i, 1.0, 0.0).astype(BF16)
    oh = jnp.concatenate([oh0, oh1], axis=0)
    pres, run = [], jnp.zeros((TOP_K * N_EXPERTS, 1), F32)
    for lo_l in range(0, tm, pb):
        ohb = oh[:, lo_l:lo_l + pb]
        pres.append(jnp.dot(ohb.astype(BF16), before, preferred_element_type=F32) + run)
        run = run + jnp.sum(ohb, axis=1, keepdims=True)
    pre = jnp.concatenate(pres, axis=1)
    pre0, pre1 = pre[0:N_EXPERTS], pre[N_EXPERTS:]
    tot0, tot1 = run[0:N_EXPERTS], run[N_EXPERTS:]
    base = cnt_ref[:, 0:1]
    r0 = jnp.sum(oh0 * (pre0 + base), axis=0, keepdims=True)
    r1 = jnp.sum(oh1 * (pre1 + (base + tot0)), axis=0, keepdims=True)
    rank_ref[...] = jnp.concatenate([r0, r1], axis=0).astype(I32)
    total = jnp.broadcast_to(base + tot0 + tot1, cnt_ref.shape)
    cnt_ref[...] = total
    cnt_out_ref[...] = total


N_TAIL_ROW_OUTPUTS = 5


def _tail_specs(total_rows, d, tm, row0=0):
    blk0 = row0 // tm
    t = total_rows
    row = lambda w: pl.BlockSpec((tm, w), lambda i: (i + blk0, 0))
    lanes = pl.BlockSpec((TOP_K, tm), lambda i: (0, i + blk0))
    cnt = pl.BlockSpec((N_EXPERTS, ROUTER_LANES), lambda i: (0, 0))
    out_specs = [row(d), row(d // 2), lanes, lanes, lanes, cnt]
    out_shape = [jax.ShapeDtypeStruct((t, d), F32), jax.ShapeDtypeStruct((t, d // 2), U32),
                 jax.ShapeDtypeStruct((TOP_K, t), F32), jax.ShapeDtypeStruct((TOP_K, t), I32),
                 jax.ShapeDtypeStruct((TOP_K, t), I32),
                 jax.ShapeDtypeStruct((N_EXPERTS, ROUTER_LANES), F32)]
    scratch = [pltpu.VMEM((N_EXPERTS, ROUTER_LANES), F32)]
    return out_specs, out_shape, scratch


def _no_slots():
    return jnp.zeros((N_EXPERTS, ROUTER_LANES), F32)


def _even_out_kernel(ya_ref, yb_ref, h_ref, w_ref, g_ref, wr_ref, br_ref, cnt0_ref, *tail_refs):
    a_width = ya_ref.shape[1]
    mix = (jnp.dot(ya_ref[...], w_ref[0:a_width, :], preferred_element_type=F32) +
           jnp.dot(yb_ref[...], w_ref[a_width:, :], preferred_element_type=F32))
    _router_tail(h_ref[...] + mix, g_ref, wr_ref, br_ref, cnt0_ref, *tail_refs)


def _even_out(ya, yb, h, w_out, g_ffn, wr, br, *, tm):
    t, d = h.shape
    out_specs, out_shape, scratch = _tail_specs(t, d, tm)
    row = lambda w: pl.BlockSpec((tm, w), lambda i: (i, 0))
    cnt0 = _no_slots()
    return pl.pallas_call(
        _even_out_kernel,
        grid=(t // tm,),
        in_specs=[row(ya.shape[1]), row(yb.shape[1]), row(d), _const_spec(w_out.shape),
                  _const_spec((1, d)), _const_spec(wr.shape), _const_spec(br.shape),
                  _const_spec(cnt0.shape)],
        out_specs=out_specs,
        out_shape=out_shape,
        scratch_shapes=scratch,
        compiler_params=_params("arbitrary"),
        name="even_out_router",
    )(ya, yb, h, w_out, g_ffn, wr, br, cnt0)


def _gelu_tanh(x):
    return 0.5 * x * (1.0 + jnp.tanh(0.7978845608028654 * (x + 0.044715 * (x * x * x))))


def _odd_kernel(*refs, n_prev, n_alias):
    h_refs, refs = refs[:max(n_prev, 1)], refs[max(n_prev, 1):]
    (g_ref, w_in_ref, gv_ref, ws_ref, bs_ref, w_out_ref, gf_ref, wr_ref, br_ref, cnt0_ref,
     *rest) = refs
    *tail_refs, y_ref = rest[n_alias:]
    h = _post_value(*h_refs) if n_prev else h_refs[0][...]
    tm = h.shape[0]
    c_width = gv_ref.shape[1]
    n_grp = ws_ref.shape[0]
    gdim = c_width // n_grp
    xn = _rms(h, g_ref[...]).astype(BF16)
    u = _gelu_tanh(jnp.dot(xn, w_in_ref[:, 0:c_width], preferred_element_type=F32))
    v = _gelu_tanh(jnp.dot(xn, w_in_ref[:, c_width:], preferred_element_type=F32))
    v = _rms(v, gv_ref[...]).astype(BF16)
    r_i = lax.broadcasted_iota(I32, (CHUNK, CHUNK), 0)
    c_i = lax.broadcasted_iota(I32, (CHUNK, CHUNK), 1)
    causal = c_i <= r_i
    bs = bs_ref[...]
    for grp in range(n_grp):
        w_m = jnp.where(causal, ws_ref[grp], 0.0).astype(BF16)
        bias = bs[:, grp:grp + 1]
        cols = slice(grp * gdim, (grp + 1) * gdim)
        for c in range(tm // CHUNK):
            rows = slice(c * CHUNK, (c + 1) * CHUNK)
            gate = jnp.dot(w_m, v[rows, cols], preferred_element_type=F32) + bias
            y_ref[rows, cols] = (u[rows, cols] * gate).astype(BF16)
    mix = jnp.dot(y_ref[...], w_out_ref[...], preferred_element_type=F32)
    _router_tail(h + mix, gf_ref, wr_ref, br_ref, cnt0_ref, *tail_refs)


def _odd_mixer(h_or_prev, g_mix, w_in, g_v, w_s, b_s_t, w_out, g_ffn, wr, br, *, tm,
               total_rows=None, earlier=None):
    cnt0 = _no_slots() if earlier is None else earlier[N_TAIL_ROW_OUTPUTS]
    consts = [g_mix, w_in, g_v, w_s, b_s_t, w_out, g_ffn, wr, br, cnt0]
    pending = isinstance(h_or_prev, PendingTail)
    if pending:
        head_specs, head_args = h_or_prev.operands(tm)
        t, d, row0 = h_or_prev.rows, h_or_prev.width, h_or_prev.row0
    else:
        t, d = h_or_prev.shape
        row0 = 0
        head_specs, head_args = [pl.BlockSpec((tm, d), lambda i: (i, 0))], [h_or_prev]
    out_specs, out_shape, scratch = _tail_specs(total_rows or t, d, tm, row0)
    n_prev = len(head_args) if pending else 0
    in_specs = head_specs + [_const_spec(c.shape) for c in consts]
    operands = [*head_args, *consts]
    aliases = {}
    if earlier is not None:
        aliases = {len(operands) + k: k for k in range(N_TAIL_ROW_OUTPUTS)}
        in_specs += [pl.BlockSpec(memory_space=pl.ANY)] * N_TAIL_ROW_OUTPUTS
        operands += list(earlier[:N_TAIL_ROW_OUTPUTS])
    return pl.pallas_call(
        functools.partial(_odd_kernel, n_prev=n_prev, n_alias=len(aliases)),
        grid=(t // tm,),
        in_specs=in_specs,
        out_specs=out_specs,
        out_shape=out_shape,
        input_output_aliases=aliases,
        scratch_shapes=scratch + [pltpu.VMEM((tm, g_v.shape[1]), BF16)],
        compiler_params=_params("arbitrary"),
        name="odd_mixer_router",
    )(*operands)


def _sc_mesh():
    return plsc.VectorSubcoreMesh(core_axis_name="c", subcore_axis_name="s")


def _sc_worker_base(rows_per_worker):
    wid = lax.axis_index("s") * V7X_SC_CORES + lax.axis_index("c")
    return wid * rows_per_worker


def _dispatch(xp, dest, n_slots):
    t, w = xp.shape
    per_worker = t // SC_WORKERS
    n_chunks = per_worker // SC_ROWS
    assert per_worker * SC_WORKERS == t and n_chunks * SC_ROWS == per_worker and n_chunks % 2 == 0

    idx_t = pltpu.VMEM((SC_ROWS,), I32)

    @functools.partial(
        pl.kernel, mesh=_sc_mesh(),
        out_type=jax.ShapeDtypeStruct((n_slots, w), xp.dtype),
        scratch_types=[pltpu.VMEM((2, SC_ROWS, w), xp.dtype), idx_t, idx_t, idx_t, idx_t,
                       pltpu.SemaphoreType.DMA((2,)), pltpu.SemaphoreType.DMA((2,))],
    )
    def kern(x_hbm, d_hbm, out_hbm, rows_v, i00, i01, i10, i11, rsem, wsem):
        base = _sc_worker_base(per_worker)
        idx = ((i00, i01), (i10, i11))

        def load(c, b):
            off = pl.multiple_of(base + c * SC_ROWS, 8)
            pltpu.async_copy(x_hbm.at[pl.ds(off, SC_ROWS)], rows_v.at[b], rsem.at[b])
            for k in range(TOP_K):
                pltpu.sync_copy(d_hbm.at[k, pl.ds(off, SC_ROWS)], idx[b][k])

        def wait_load(b):
            pltpu.make_async_copy(x_hbm.at[pl.ds(0, SC_ROWS)], rows_v.at[b], rsem.at[b]).wait()

        def scatter(b):
            for k in range(TOP_K):
                pltpu.async_copy(rows_v.at[b], out_hbm.at[idx[b][k]], wsem.at[b])

        def wait_scatter(b):
            for k in range(TOP_K):
                pltpu.make_async_copy(rows_v.at[b], out_hbm.at[idx[b][k]], wsem.at[b]).wait()

        load(0, 0)

        @pl.loop(0, n_chunks, step=2)
        def _(c0):
            for b in range(2):
                c = c0 + b
                wait_load(b)
                scatter(b)

                @pl.when(c + 1 < n_chunks)
                def _():
                    @pl.when(c >= 1)
                    def _():
                        wait_scatter(1 - b)
                    load(c + 1, 1 - b)

        wait_scatter(0)
        wait_scatter(1)

    return kern(xp, dest)


def _combine(y, idx):
    n = idx.shape[0]
    w = y.shape[1]
    rows = 2 * SC_ROWS
    per_worker = n // SC_WORKERS
    n_chunks = per_worker // rows
    assert per_worker * SC_WORKERS == n and n_chunks * rows == per_worker and n_chunks % 2 == 0

    idx_t = pltpu.VMEM((rows,), I32)

    @functools.partial(
        pl.kernel, mesh=_sc_mesh(),
        out_type=jax.ShapeDtypeStruct((n, w), y.dtype),
        scratch_types=[pltpu.VMEM((2, rows, w), y.dtype), idx_t, idx_t,
                       pltpu.SemaphoreType.DMA((2,)), pltpu.SemaphoreType.DMA((2,))],
    )
    def kern(y_hbm, i_hbm, out_hbm, rows_v, i0, i1, gsem, wsem):
        base = _sc_worker_base(per_worker)
        ibuf = (i0, i1)

        def gather(c, b):
            off = pl.multiple_of(base + c * rows, 8)
            pltpu.sync_copy(i_hbm.at[pl.ds(off, rows)], ibuf[b])
            pltpu.async_copy(y_hbm.at[ibuf[b]], rows_v.at[b], gsem.at[b])

        def wait_gather(b):
            pltpu.make_async_copy(y_hbm.at[ibuf[b]], rows_v.at[b], gsem.at[b]).wait()

        def write(c, b):
            off = pl.multiple_of(base + c * rows, 8)
            pltpu.async_copy(rows_v.at[b], out_hbm.at[pl.ds(off, rows)], wsem.at[b])

        def wait_write(b):
            pltpu.make_async_copy(rows_v.at[b], out_hbm.at[pl.ds(0, rows)], wsem.at[b]).wait()

        gather(0, 0)

        @pl.loop(0, n_chunks, step=2)
        def _(c0):
            for b in range(2):
                c = c0 + b

                @pl.when(c + 1 < n_chunks)
                def _():
                    @pl.when(c >= 1)
                    def _():
                        wait_write(1 - b)
                    gather(c + 1, 1 - b)

                wait_gather(b)
                write(c, b)

        wait_write(0)
        wait_write(1)

    return kern(y, idx)


def _expert_kernel(te_ref, tv_ref, nu_ref, tg_ref, tn_ref, x_hbm, w1_hbm, w3_hbm, w2_hbm, y_ref,
                   xbuf_ref, xsem, w1f_ref, w3f_ref, w2f_ref, wsem, w1b_ref, w3b_ref, w2b_ref,
                   *, layer):
    i = pl.program_id(0)
    n_used = nu_ref[0]
    ts = xbuf_ref.shape[1]

    def x_copy(tile):
        slot = tile % X_RING
        rows = pl.ds(pl.multiple_of(tile * ts, ts), ts)
        return pltpu.make_async_copy(x_hbm.at[rows], xbuf_ref.at[slot], xsem.at[slot])

    def w_copies(expert, slot):
        pairs = ((w1_hbm, w1f_ref), (w3_hbm, w3f_ref), (w2_hbm, w2f_ref))
        return [pltpu.make_async_copy(src.at[layer, expert], dst.at[slot], wsem.at[slot, j])
                for j, (src, dst) in enumerate(pairs)]

    @pl.when(jnp.logical_and(i == 0, n_used > 0))
    def _():
        for tile in range(X_RING - 1):
            @pl.when(tile < n_used)
            def _():
                x_copy(tile).start()
        for c in w_copies(te_ref[0], 0):
            c.start()

    @pl.when(i < n_used)
    def _():
        @pl.when(i + (X_RING - 1) < n_used)
        def _():
            x_copy(i + (X_RING - 1)).start()

        prev = te_ref[jnp.maximum(i - 1, 0)]

        @pl.when(jnp.logical_or(i == 0, te_ref[i] != prev))
        def _():
            slot = tg_ref[i] % 2
            for c in w_copies(te_ref[i], slot):
                c.wait()
            nxt = tn_ref[i]

            @pl.when(nxt >= 0)
            def _():
                for c in w_copies(nxt, 1 - slot):
                    c.start()

            w1b_ref[...] = w1f_ref[slot].astype(BF16)
            w3b_ref[...] = w3f_ref[slot].astype(BF16)
            w2b_ref[...] = w2f_ref[slot].astype(BF16)

        x_copy(i).wait()
        xu = xbuf_ref[i % X_RING]
        rowid = lax.broadcasted_iota(I32, xu.shape, 0)
        xu = jnp.where(rowid < tv_ref[i], xu, jnp.zeros_like(xu))
        x = _unpack_bf16_pairs(xu).astype(BF16)
        h1 = jnp.dot(x, w1b_ref[...], preferred_element_type=F32)
        h3 = jnp.dot(x, w3b_ref[...], preferred_element_type=F32)
        he = (h1 * jax.nn.sigmoid(h1)) * h3
        y = jnp.dot(he.astype(BF16), w2b_ref[...], preferred_element_type=F32)
        y_ref[...] = _pack_bf16_pairs(y)


def _experts(x_sorted, plan, w1, w3, w2, layer):
    n_slots, wp = x_sorted.shape
    _, n_e, d, d_e = w1.shape
    ts = SLOT_TILE
    any_spec = pl.BlockSpec(memory_space=pl.ANY)
    yspec = pl.BlockSpec((ts, wp),
                         lambda i, te, tv, nu, tg, tn: (jnp.minimum(i, jnp.maximum(nu[0] - 1, 0)), 0))
    return pl.pallas_call(
        functools.partial(_expert_kernel, layer=layer),
        grid_spec=pltpu.PrefetchScalarGridSpec(
            num_scalar_prefetch=len(plan),
            grid=(n_slots // ts,),
            in_specs=[any_spec] * 4,
            out_specs=yspec,
            scratch_shapes=[pltpu.VMEM((X_RING, ts, wp), x_sorted.dtype),
                            pltpu.SemaphoreType.DMA((X_RING,)),
                            pltpu.VMEM((2, d, d_e), F32), pltpu.VMEM((2, d, d_e), F32),
                            pltpu.VMEM((2, d_e, d), F32), pltpu.SemaphoreType.DMA((2, 3)),
                            pltpu.VMEM((d, d_e), BF16), pltpu.VMEM((d, d_e), BF16),
                            pltpu.VMEM((d_e, d), BF16)],
        ),
        out_shape=jax.ShapeDtypeStruct((n_slots, d // 2), U32),
        compiler_params=_params("arbitrary"),
        name="moe_experts",
    )(*plan, x_sorted, w1, w3, w2)


def _slot_plan(eids, ranks, counts, n_tiles):
    ts = SLOT_TILE
    cnt = counts[:, 0].astype(I32)
    padded = (cnt + ts - 1) // ts * ts
    e_ids = jnp.arange(N_EXPERTS, dtype=I32)
    ends = jnp.sum(jnp.where(e_ids[:, None] <= e_ids[None, :], padded[:, None], 0), axis=0)
    starts = ends - padded
    onehot = eids[None] == e_ids[:, None, None]
    dest = ranks + jnp.sum(jnp.where(onehot, starts[:, None, None], 0), axis=0)
    tile_lo = jnp.arange(n_tiles, dtype=I32) * ts
    tile_e = jnp.minimum(jnp.sum(tile_lo[:, None] >= ends[None, :], axis=1), N_EXPERTS - 1)
    tile_e = tile_e.astype(I32)
    valid_end = jnp.sum(jnp.where(tile_e[:, None] == e_ids[None, :], (starts + cnt)[None, :], 0),
                        axis=1)
    tile_valid = jnp.clip(valid_end - tile_lo, 0, ts).astype(I32)
    n_used = (ends[-1:] // ts).astype(I32)
    nonempty = cnt > 0
    earlier = jnp.logical_and(nonempty[None, :], e_ids[None, :] < tile_e[:, None])
    tile_grp = jnp.sum(earlier, axis=1).astype(I32)
    later = jnp.logical_and(nonempty[None, :], e_ids[None, :] > tile_e[:, None])
    nxt = jnp.min(jnp.where(later, e_ids[None, :], N_EXPERTS), axis=1)
    tile_next = jnp.where(nxt < N_EXPERTS, nxt, -1).astype(I32)
    return dest, (tile_e, tile_valid, n_used, tile_grp, tile_next)


class PendingTail(NamedTuple):
    operands: Callable
    rows: int
    row0: int
    width: int


def _post_value(h1_ref, y0_ref, y1_ref, gate_ref, p_ref, g_ref, wpg_ref, wpe_ref):
    gates = gate_ref[...]
    tm = gates.shape[1]
    gt = jnp.concatenate([gates, jnp.zeros((8 - TOP_K, tm), F32)], axis=0).T
    h2 = h1_ref[...] + (gt[:, 0:1] * _unpack_bf16_pairs(y0_ref[...]) +
                        gt[:, 1:2] * _unpack_bf16_pairs(y1_ref[...]))
    gate = jax.nn.sigmoid(jnp.dot(_rms(h2, g_ref[...]).astype(BF16), wpg_ref[...],
                                  preferred_element_type=F32))
    pe = jnp.dot(p_ref[...].astype(BF16), wpe_ref[...], preferred_element_type=F32)
    return h2 + gate * pe


def _post_operands(h1, yk, gates, p, g_pl, w_pg, w_pe, *, p_row0, row0=0):
    d = h1.shape[1]
    t = yk.shape[0] // TOP_K
    wy = yk.shape[1]

    def operands(tm):
        nt = t // tm
        blk0 = row0 // tm
        p_blk0 = (p_row0 + row0) // tm
        specs = [pl.BlockSpec((tm, d), lambda i: (i + blk0, 0)),
                 pl.BlockSpec((tm, wy), lambda i: (i, 0)),
                 pl.BlockSpec((tm, wy), lambda i: (i + nt, 0)),
                 pl.BlockSpec((TOP_K, tm), lambda i: (0, i + blk0)),
                 pl.BlockSpec((tm, p.shape[1]), lambda i: (i + p_blk0, 0)),
                 _const_spec((1, d)), _const_spec(w_pg.shape), _const_spec(w_pe.shape)]
        return specs, [h1, yk, yk, gates, p, g_pl, w_pg, w_pe]

    return PendingTail(operands, t, row0, d)


def _post_kernel(*refs):
    *post_refs, o_ref = refs
    o_ref[...] = _post_value(*post_refs)


def _post(pending, *, tm):
    specs, args = pending.operands(tm)
    t, d = pending.rows, pending.width
    return pl.pallas_call(
        _post_kernel,
        grid=(t // tm,),
        in_specs=specs,
        out_specs=pl.BlockSpec((tm, d), lambda i: (i, 0)),
        out_shape=jax.ShapeDtypeStruct((t, d), F32),
        compiler_params=_params("arbitrary"),
        name="layer_post",
    )(*args)


def _final_kernel(*refs, n_post):
    post_refs, gfin_ref, o_ref = refs[:n_post], refs[n_post], refs[-1]
    o_ref[...] = _rms(_post_value(*post_refs), gfin_ref[...])


def _final(pending, g_fin, *, tm, total_rows, out=None):
    specs, args = pending.operands(tm)
    d = pending.width
    blk0 = pending.row0 // tm
    in_specs = specs + [_const_spec((1, d))]
    operands = [*args, g_fin]
    aliases = {}
    if out is not None:
        aliases = {len(operands): 0}
        in_specs.append(pl.BlockSpec(memory_space=pl.ANY))
        operands.append(out)
    return pl.pallas_call(
        functools.partial(_final_kernel, n_post=len(args)),
        grid=(pending.rows // tm,),
        in_specs=in_specs,
        out_specs=pl.BlockSpec((tm, d), lambda i: (i + blk0, 0)),
        out_shape=jax.ShapeDtypeStruct((total_rows, d), F32),
        input_output_aliases=aliases,
        compiler_params=_params("arbitrary"),
        name="final_post",
    )(*operands)


def _router_weights(wc, bc, wf, bf):
    def lanes(coarse, fine):
        z = lambda n: jnp.zeros((coarse.shape[0], n), F32)
        return jnp.concatenate([coarse, z(FINE_ROW0 - N_GROUPS), fine,
                                z(ROUTER_LANES - FINE_ROW0 - N_EXPERTS)], axis=1)

    w = lanes(wc, wf)
    b = lanes(bc[None, :], bf[None, :])
    hi = w.astype(BF16)
    lo = (w - hi.astype(F32)).astype(BF16)
    return jnp.concatenate([hi, lo], axis=1), b


def _moe(xp, eids, ranks, counts, w1, w3, w2, layer, n_parts):
    t = xp.shape[0]
    n_tiles = TOP_K * t // SLOT_TILE + N_EXPERTS
    dest, plan = _slot_plan(eids, ranks, counts, n_tiles)
    x_sorted = _dispatch(xp, dest, n_tiles * SLOT_TILE)
    y_sorted = _experts(x_sorted, plan, w1, w3, w2, layer)
    tp = t // n_parts
    return [_combine(y_sorted, dest[:, s * tp:(s + 1) * tp].reshape(-1)) for s in range(n_parts)]


def kernel(x, p, norm_mix, norm_ffn, norm_pl, final_norm, w_in_even, conv_w_even, w_out_even, w_in_odd, g_v_odd, w_s_odd, b_s_odd, w_out_odd, router_c, router_c_b, router_f, router_f_b, moe_w1, moe_w3, moe_w2, w_pe, w_pg):
    batch, seq, d = x.shape
    depth = p.shape[0]
    t = batch * seq
    tm = min(TOKEN_TILE, seq)
    tm_big = min(LIGHT_TOKEN_TILE, seq)
    p_rows = p.reshape(depth * t, -1)
    row = lambda a: a.reshape(1, -1)
    h = x.reshape(t, d)
    for i in range(depth):
        j = i // 2
        wr, br = _router_weights(router_c[i], router_c_b[i], router_f[i], router_f_b[i])
        g_ffn = row(norm_ffn[i])
        if i % 2 == 0:
            if isinstance(h, list):
                h = _post(h[0], tm=tm_big)
            ya, yb = _even_mix(h, row(norm_mix[i]), w_in_even[j].astype(BF16), conv_w_even[j],
                               batch=batch, seq=seq, tq=min(Q_TILE, seq // Q_SUBTILES))
            mixed = _even_out(ya, yb, h, w_out_even[j].astype(BF16), g_ffn, wr, br, tm=tm_big)
        else:
            mixed = None
            for part in (h if isinstance(h, list) else [h]):
                mixed = _odd_mixer(part, row(norm_mix[i]), w_in_odd[j].astype(BF16),
                                   row(g_v_odd[j]), w_s_odd[j], b_s_odd[j].T,
                                   w_out_odd[j].astype(BF16), g_ffn, wr, br, tm=tm,
                                   total_rows=t, earlier=mixed)
        h1, xp, gates, eids, ranks, counts = mixed
        consumer_tile, parts = ((tm_big, FINAL_PARTS) if i == depth - 1 else
                                (tm, MIXER_PARTS) if i % 2 == 0 else (None, 1))
        n_parts = parts if consumer_tile and t % (parts * consumer_tile) == 0 else 1
        yks = _moe(xp, eids, ranks, counts, moe_w1, moe_w3, moe_w2, i, n_parts)
        h = [_post_operands(h1, yk, gates, p_rows, row(norm_pl[i]), w_pg[i].astype(BF16),
                            w_pe[i].astype(BF16), p_row0=i * t, row0=s * (t // n_parts))
             for s, yk in enumerate(yks)]
    out = None
    for tail in h:
        out = _final(tail, row(final_norm), tm=tm_big, total_rows=t, out=out)
    return out.reshape(batch, seq, d)
```

```python
import functools
from typing import Callable, NamedTuple

import jax
import jax.numpy as jnp
from jax import lax
from jax.experimental import pallas as pl
from jax.experimental.pallas import tpu as pltpu
from jax.experimental.pallas import tpu_sc as plsc

EPS = 1e-6
HEAD_DIM = 64
HEADS_PER_SLAB = 2
ATTN_SLABS = 4
CHUNK = 128
N_GROUPS = 4
EXP_PER_GROUP = 8
N_EXPERTS = N_GROUPS * EXP_PER_GROUP
TOP_K = 2
ROUTER_LANES = 128
FINE_ROW0 = 8
RANK_BLOCK = 256

V7X_VMEM_BYTES = 64 * 1024 * 1024
VMEM_LIMIT = V7X_VMEM_BYTES - 8 * 1024 * 1024
V7X_SC_CORES = 2
V7X_SC_SUBCORES = 16
SC_WORKERS = V7X_SC_CORES * V7X_SC_SUBCORES
SC_ROWS = 32

TOKEN_TILE = 512
LIGHT_TOKEN_TILE = 1024
Q_TILE = 64
WINDOW_TILES = 4
SLOT_TILE = 512
X_RING = 3
MIXER_PARTS = 2
FINAL_PARTS = 4

LOG_F32_TINY_BOUND = -88.0
OFF_PENALTY = 1e30
Q_SUBTILES = 4

F32 = jnp.float32
BF16 = jnp.bfloat16
U32 = jnp.uint32
I32 = jnp.int32


def _rms(x, g):
    return x * lax.rsqrt(jnp.mean(x * x, axis=-1, keepdims=True) + EPS) * g


def _const_spec(shape):
    nd = len(shape)
    return pl.BlockSpec(shape, lambda *_: (0,) * nd, pipeline_mode=pl.Buffered(1))


def _params(*sem):
    return pltpu.CompilerParams(dimension_semantics=sem, vmem_limit_bytes=VMEM_LIMIT)


def _even_layer_kernel(h_ref, g_ref, w_ref, cw_ref, wo_ref, gf_ref, wr_ref, br_ref, cnt0_ref,
                       h1_ref, xp_ref, gate_ref, eid_ref, rank_ref, cnt_out_ref,
                       cnt_ref, zs_ref, q_ref, k_ref, v_ref, yb_ref, acc_ref, c_ref,
                       *, tq, a_width, sb_width):
    qi = pl.program_id(1)
    tm = h_ref.shape[0]
    h = h_ref[...]
    xn = _rms(h, g_ref[...]).astype(BF16)
    a3 = 3 * a_width

    @pl.when(qi == 0)
    def _():
        zs_ref[0:8, :] = jnp.zeros((8, a_width), F32)

    pa = jnp.dot(xn, w_ref[:, 0:a3], preferred_element_type=F32)
    z = pa[:, 2 * a_width:a3] * pa[:, 0:a_width]
    zs_ref[8:tm + 8, :] = z
    z1 = zs_ref[7:tm + 7, :]
    z2 = zs_ref[6:tm + 6, :]
    cw = cw_ref[...]
    conv = cw[0:1, :] * z2 + cw[1:2, :] * z1 + cw[2:3, :] * z
    ya = (pa[:, a_width:2 * a_width] * conv).astype(BF16)
    zs_ref[0:8, :] = zs_ref[tm:tm + 8, :]

    pq = jnp.dot(xn, w_ref[:, a3:a3 + 3 * sb_width], preferred_element_type=F32)
    q_ref[...] = (pq[:, 0:sb_width] * (HEAD_DIM ** -0.5)).astype(BF16)
    seq_rows = pl.ds(pl.multiple_of(qi * tm, tm), tm)
    k_ref[seq_rows, :] = pq[:, sb_width:2 * sb_width].astype(BF16)
    v_ref[seq_rows, :] = pq[:, 2 * sb_width:3 * sb_width].astype(BF16)
    _attn_tiles(q_ref, k_ref, v_ref, yb_ref, acc_ref, c_ref, qi, tq=tq)
    mix = (jnp.dot(ya, wo_ref[0:a_width, :], preferred_element_type=F32) +
           jnp.dot(yb_ref[...], wo_ref[a_width:, :], preferred_element_type=F32))
    first_step = jnp.logical_and(pl.program_id(0) == 0, qi == 0)
    _router_tail(h + mix, gf_ref, wr_ref, br_ref, cnt0_ref, h1_ref, xp_ref, gate_ref, eid_ref,
                 rank_ref, cnt_out_ref, cnt_ref, first_step=first_step)


def _even_layer(h, g, w_in, conv_w, w_out, g_ffn, wr, br, *, batch, seq, tq):
    t, d = h.shape
    a_width = conv_w.shape[1]
    sb_width = (w_in.shape[1] - 3 * a_width) // 3
    assert sb_width == ATTN_SLABS * HEADS_PER_SLAB * HEAD_DIM
    tm = Q_SUBTILES * tq
    nq = seq // tm
    rows = Q_SUBTILES * ATTN_SLABS * HEADS_PER_SLAB * tq
    out_specs, out_shape, scratch = _tail_specs(t, d, tm, tile_of=lambda b, i: b * nq + i)
    consts = [g, w_in, conv_w, w_out, g_ffn, wr, br, _no_slots()]
    return pl.pallas_call(
        functools.partial(_even_layer_kernel, tq=tq, a_width=a_width, sb_width=sb_width),
        grid=(batch, nq),
        in_specs=[pl.BlockSpec((tm, d), lambda b, i: (b * nq + i, 0))] +
                 [_const_spec(c.shape) for c in consts],
        out_specs=out_specs,
        out_shape=out_shape,
        scratch_shapes=scratch + [pltpu.VMEM((tm + 8, a_width), F32),
                                  pltpu.VMEM((tm, sb_width), BF16),
                                  pltpu.VMEM((seq, sb_width), BF16),
                                  pltpu.VMEM((seq, sb_width), BF16),
                                  pltpu.VMEM((tm, sb_width), BF16),
                                  pltpu.VMEM((rows, HEADS_PER_SLAB * HEAD_DIM), F32),
                                  pltpu.VMEM((rows, 1), F32)],
        compiler_params=_params("arbitrary", "arbitrary"),
        name="even_layer",
    )(h, *consts)


def _attn_tiles(q_ref, k_ref, v_ref, o_ref, acc_ref, c_ref, qi, *, tq):
    slab = HEADS_PER_SLAB * HEAD_DIM
    rows_unit = HEADS_PER_SLAB * tq
    rows_tile = ATTN_SLABS * rows_unit
    rows = Q_SUBTILES * rows_tile
    units = [(j, sl) for j in range(Q_SUBTILES) for sl in range(ATTN_SLABS)]
    lane = lax.broadcasted_iota(I32, (tq, slab), 1)
    qs = []
    for j, sl in units:
        q = q_ref[j * tq:(j + 1) * tq, sl * slab:(sl + 1) * slab]
        zero = jnp.zeros_like(q)
        qs.append(jnp.concatenate([jnp.where(lane < HEAD_DIM, q, zero),
                                   jnp.where(lane >= HEAD_DIM, q, zero)], axis=0))

    def later_keys(width):
        r_i = lax.broadcasted_iota(I32, (width, width), 0)
        c_i = lax.broadcasted_iota(I32, (width, width), 1)
        return jnp.where(r_i > c_i, 1.0, 0.0).astype(BF16)

    def per_tile_column(vals):
        return jnp.concatenate([jnp.full((rows_tile, 1), v, F32) for v in vals], axis=0)

    def block(starts, width, c, mask):
        upper = later_keys(width)
        keys = [pl.ds(pl.multiple_of(s, tq), width) for s in starts]
        z = jnp.concatenate(
            [lax.dot_general(qs[u], k_ref[keys[j], sl * slab:(sl + 1) * slab],
                             (((1,), (1,)), ((), ())), preferred_element_type=F32)
             for u, (j, sl) in enumerate(units)], axis=0)
        sp = jnp.maximum(z, 0.0) + jnp.log(1.0 + jnp.exp(-jnp.abs(z)))
        spm = sp if mask is None else jnp.where(mask, sp, 0.0)
        hi = spm.astype(BF16)
        lo = (spm - hi.astype(F32)).astype(BF16)
        later = (jnp.dot(hi, upper, preferred_element_type=F32) +
                 jnp.dot(lo, upper, preferred_element_type=F32))
        a = jnp.exp(z - sp - later - c)
        if mask is not None:
            a = jnp.where(mask, a, 0.0)
        a = a.astype(BF16)
        out = jnp.concatenate(
            [jnp.dot(a[u * rows_unit:(u + 1) * rows_unit], v_ref[keys[j], sl * slab:(sl + 1) * slab],
                     preferred_element_type=F32) for u, (j, sl) in enumerate(units)], axis=0)
        return out, jnp.sum(spm, axis=1, keepdims=True)

    tile0 = qi * Q_SUBTILES
    win = WINDOW_TILES * tq
    firsts = [jnp.maximum(tile0 + j - (WINDOW_TILES - 1), 0) * tq for j in range(Q_SUBTILES)]
    col = lax.broadcasted_iota(I32, (tq, win), 1)
    row = lax.broadcasted_iota(I32, (tq, win), 0)
    mask = jnp.concatenate(
        [col + (firsts[j] - (tile0 + j) * tq) < row
         for j in range(Q_SUBTILES) for _ in range(rows_tile // tq)], axis=0)
    out0, c2 = block(firsts, win, jnp.zeros((rows, 1), F32), mask)
    acc_ref[...] = out0
    c_ref[...] = c2

    def cond(carry):
        n, cmin = carry
        return jnp.logical_and(tile0 + (Q_SUBTILES - 1) - WINDOW_TILES - n >= 0,
                               cmin < -LOG_F32_TINY_BOUND)

    def body(carry):
        n, _ = carry
        kbs = [tile0 + j - WINDOW_TILES - n for j in range(Q_SUBTILES)]
        c = c_ref[...]
        off = per_tile_column([jnp.where(kb >= 0, 0.0, OFF_PENALTY) for kb in kbs])
        out, ssum = block([jnp.maximum(kb, 0) * tq for kb in kbs], tq, c + off, None)
        acc_ref[...] += out
        c_new = c + ssum
        c_ref[...] = c_new
        return n + 1, jnp.min(c_new)

    lax.while_loop(cond, body, (jnp.int32(0), jnp.min(c2)))
    acc = acc_ref[...]
    for u, (j, sl) in enumerate(units):
        lo_r = u * rows_unit
        o_ref[j * tq:(j + 1) * tq, sl * slab:(sl + 1) * slab] = jnp.where(
            lane < HEAD_DIM, acc[lo_r:lo_r + tq], acc[lo_r + tq:lo_r + 2 * tq]).astype(o_ref.dtype)


def _pack_bf16_pairs(x):
    n = x.shape[1] // 2
    bits = lax.bitcast_convert_type(x.astype(BF16).astype(F32), U32)
    return bits[:, 0:n] | (bits[:, n:] >> 16)


def _unpack_bf16_pairs(u):
    a = lax.bitcast_convert_type(u & jnp.uint32(0xFFFF0000), F32)
    b = lax.bitcast_convert_type(u << 16, F32)
    return jnp.concatenate([a, b], axis=1)


def _router_tail(h1, g_ref, wr_ref, br_ref, cnt0_ref,
                 h1_ref, xp_ref, gate_ref, eid_ref, rank_ref, cnt_out_ref, cnt_ref, *, first_step):
    h1_ref[...] = h1
    xn = _rms(h1, g_ref[...])
    xp_ref[...] = _pack_bf16_pairs(xn)
    x_hi = xn.astype(BF16)
    x_lo = (xn - x_hi.astype(F32)).astype(BF16)
    both = jnp.dot(x_hi, wr_ref[...], preferred_element_type=F32)
    logits = (both[:, 0:ROUTER_LANES] + both[:, ROUTER_LANES:] +
              jnp.dot(x_lo, wr_ref[:, 0:ROUTER_LANES], preferred_element_type=F32) + br_ref[...])
    lt = logits.T
    tm = lt.shape[1]
    row = lax.broadcasted_iota(I32, (EXP_PER_GROUP, tm), 0)
    neg = jnp.float32(-jnp.inf)

    coarse = jnp.where(row < N_GROUPS, lt[0:EXP_PER_GROUP], neg)
    cmax = jnp.max(coarse, axis=0, keepdims=True)
    gi = jnp.min(jnp.where(coarse == cmax, row, EXP_PER_GROUP), axis=0, keepdims=True)
    pg = 1.0 / jnp.sum(jnp.exp(coarse - cmax), axis=0, keepdims=True)

    fine = lt[FINE_ROW0:FINE_ROW0 + EXP_PER_GROUP]
    for grp in range(1, N_GROUPS):
        lo_r = FINE_ROW0 + grp * EXP_PER_GROUP
        fine = jnp.where(gi == grp, lt[lo_r:lo_r + EXP_PER_GROUP], fine)
    m1 = jnp.max(fine, axis=0, keepdims=True)
    i1 = jnp.min(jnp.where(fine == m1, row, EXP_PER_GROUP), axis=0, keepdims=True)
    rest = jnp.where(row == i1, neg, fine)
    m2 = jnp.max(rest, axis=0, keepdims=True)
    i2 = jnp.min(jnp.where(rest == m2, row, EXP_PER_GROUP), axis=0, keepdims=True)
    e21 = jnp.exp(m2 - m1)
    w1 = 1.0 / (1.0 + e21)
    gate_ref[...] = jnp.concatenate([pg * w1, pg * (e21 * w1)], axis=0)
    e0 = gi * EXP_PER_GROUP + i1
    e1 = gi * EXP_PER_GROUP + i2
    eid_ref[...] = jnp.concatenate([e0, e1], axis=0)

    @pl.when(first_step)
    def _():
        cnt_ref[...] = cnt0_ref[...]

    erow = lax.broadcasted_iota(I32, (N_EXPERTS, tm), 0)
    oh0 = jnp.where(erow == e0, 1.0, 0.0)
    oh1 = jnp.where(erow == e1, 1.0, 0.0)
    pb = min(RANK_BLOCK, tm)
    r_i = lax.broadcasted_iota(I32, (pb, pb), 0)
    c_i = lax.broadcasted_iota(I32, (pb, pb), 1)
    before = jnp.where(r_i < c_i, 1.0, 0.0).astype(BF16)
    oh = jnp.concatenate([oh0, oh1], axis=0)
    pres, run = [], jnp.zeros((TOP_K * N_EXPERTS, 1), F32)
    for lo_l in range(0, tm, pb):
        ohb = oh[:, lo_l:lo_l + pb]
        pres.append(jnp.dot(ohb.astype(BF16), before, preferred_element_type=F32) + run)
        run = run + jnp.sum(ohb, axis=1, keepdims=True)
    pre = jnp.concatenate(pres, axis=1)
    pre0, pre1 = pre[0:N_EXPERTS], pre[N_EXPERTS:]
    tot0, tot1 = run[0:N_EXPERTS], run[N_EXPERTS:]
    base = cnt_ref[:, 0:1]
    r0 = jnp.sum(oh0 * (pre0 + base), axis=0, keepdims=True)
    r1 = jnp.sum(oh1 * (pre1 + (base + tot0)), axis=0, keepdims=True)
    rank_ref[...] = jnp.concatenate([r0, r1], axis=0).astype(I32)
    total = jnp.broadcast_to(base + tot0 + tot1, cnt_ref.shape)
    cnt_ref[...] = total
    cnt_out_ref[...] = total


N_TAIL_ROW_OUTPUTS = 5


def _tail_specs(total_rows, d, tm, row0=0, tile_of=None):
    blk0 = row0 // tm
    t = total_rows
    tile_of = tile_of or (lambda i: i + blk0)
    row = lambda w: pl.BlockSpec((tm, w), lambda *g: (tile_of(*g), 0))
    lanes = pl.BlockSpec((TOP_K, tm), lambda *g: (0, tile_of(*g)))
    cnt = pl.BlockSpec((N_EXPERTS, ROUTER_LANES), lambda *g: (0, 0))
    out_specs = [row(d), row(d // 2), lanes, lanes, lanes, cnt]
    out_shape = [jax.ShapeDtypeStruct((t, d), F32), jax.ShapeDtypeStruct((t, d // 2), U32),
                 jax.ShapeDtypeStruct((TOP_K, t), F32), jax.ShapeDtypeStruct((TOP_K, t), I32),
                 jax.ShapeDtypeStruct((TOP_K, t), I32),
                 jax.ShapeDtypeStruct((N_EXPERTS, ROUTER_LANES), F32)]
    scratch = [pltpu.VMEM((N_EXPERTS, ROUTER_LANES), F32)]
    return out_specs, out_shape, scratch


def _no_slots():
    return jnp.zeros((N_EXPERTS, ROUTER_LANES), F32)


def _gelu_tanh(x):
    return 0.5 * x * (1.0 + jnp.tanh(0.7978845608028654 * (x + 0.044715 * (x * x * x))))


def _odd_kernel(*refs, n_prev, n_alias):
    h_refs, refs = refs[:max(n_prev, 1)], refs[max(n_prev, 1):]
    (g_ref, w_in_ref, gv_ref, ws_ref, bs_ref, w_out_ref, gf_ref, wr_ref, br_ref, cnt0_ref,
     *rest) = refs
    *tail_refs, y_ref = rest[n_alias:]
    h = _post_value(*h_refs) if n_prev else h_refs[0][...]
    tm = h.shape[0]
    c_width = gv_ref.shape[1]
    n_grp = ws_ref.shape[0]
    gdim = c_width // n_grp
    xn = _rms(h, g_ref[...]).astype(BF16)
    u = _gelu_tanh(jnp.dot(xn, w_in_ref[:, 0:c_width], preferred_element_type=F32))
    v = _gelu_tanh(jnp.dot(xn, w_in_ref[:, c_width:], preferred_element_type=F32))
    v = _rms(v, gv_ref[...]).astype(BF16)
    r_i = lax.broadcasted_iota(I32, (CHUNK, CHUNK), 0)
    c_i = lax.broadcasted_iota(I32, (CHUNK, CHUNK), 1)
    causal = c_i <= r_i
    bs = bs_ref[...]
    for grp in range(n_grp):
        w_m = jnp.where(causal, ws_ref[grp], 0.0).astype(BF16)
        bias = bs[:, grp:grp + 1]
        cols = slice(grp * gdim, (grp + 1) * gdim)
        for c in range(tm // CHUNK):
            rows = slice(c * CHUNK, (c + 1) * CHUNK)
            gate = jnp.dot(w_m, v[rows, cols], preferred_element_type=F32) + bias
            y_ref[rows, cols] = (u[rows, cols] * gate).astype(BF16)
    mix = jnp.dot(y_ref[...], w_out_ref[...], preferred_element_type=F32)
    _router_tail(h + mix, gf_ref, wr_ref, br_ref, cnt0_ref, *tail_refs,
                 first_step=pl.program_id(0) == 0)


def _odd_mixer(h_or_prev, g_mix, w_in, g_v, w_s, b_s_t, w_out, g_ffn, wr, br, *, tm,
               total_rows=None, earlier=None):
    cnt0 = _no_slots() if earlier is None else earlier[N_TAIL_ROW_OUTPUTS]
    consts = [g_mix, w_in, g_v, w_s, b_s_t, w_out, g_ffn, wr, br, cnt0]
    pending = isinstance(h_or_prev, PendingTail)
    if pending:
        head_specs, head_args = h_or_prev.operands(tm)
        t, d, row0 = h_or_prev.rows, h_or_prev.width, h_or_prev.row0
    else:
        t, d = h_or_prev.shape
        row0 = 0
        head_specs, head_args = [pl.BlockSpec((tm, d), lambda i: (i, 0))], [h_or_prev]
    out_specs, out_shape, scratch = _tail_specs(total_rows or t, d, tm, row0)
    n_prev = len(head_args) if pending else 0
    in_specs = head_specs + [_const_spec(c.shape) for c in consts]
    operands = [*head_args, *consts]
    aliases = {}
    if earlier is not None:
        aliases = {len(operands) + k: k for k in range(N_TAIL_ROW_OUTPUTS)}
        in_specs += [pl.BlockSpec(memory_space=pl.ANY)] * N_TAIL_ROW_OUTPUTS
        operands += list(earlier[:N_TAIL_ROW_OUTPUTS])
    return pl.pallas_call(
        functools.partial(_odd_kernel, n_prev=n_prev, n_alias=len(aliases)),
        grid=(t // tm,),
        in_specs=in_specs,
        out_specs=out_specs,
        out_shape=out_shape,
        input_output_aliases=aliases,
        scratch_shapes=scratch + [pltpu.VMEM((tm, g_v.shape[1]), BF16)],
        compiler_params=_params("arbitrary"),
        name="odd_mixer_router",
    )(*operands)


def _sc_mesh():
    return plsc.VectorSubcoreMesh(core_axis_name="c", subcore_axis_name="s")


def _sc_worker_base(rows_per_worker):
    wid = lax.axis_index("s") * V7X_SC_CORES + lax.axis_index("c")
    return wid * rows_per_worker


def _dispatch(xp, dest, n_slots):
    t, w = xp.shape
    per_worker = t // SC_WORKERS
    n_chunks = per_worker // SC_ROWS
    assert per_worker * SC_WORKERS == t and n_chunks * SC_ROWS == per_worker and n_chunks % 2 == 0

    idx_t = pltpu.VMEM((SC_ROWS,), I32)

    @functools.partial(
        pl.kernel, mesh=_sc_mesh(),
        out_type=jax.ShapeDtypeStruct((n_slots, w), xp.dtype),
        scratch_types=[pltpu.VMEM((2, SC_ROWS, w), xp.dtype), idx_t, idx_t, idx_t, idx_t,
                       pltpu.SemaphoreType.DMA((2,)), pltpu.SemaphoreType.DMA((2,))],
    )
    def kern(x_hbm, d_hbm, out_hbm, rows_v, i00, i01, i10, i11, rsem, wsem):
        base = _sc_worker_base(per_worker)
        idx = ((i00, i01), (i10, i11))

        def load(c, b):
            off = pl.multiple_of(base + c * SC_ROWS, 8)
            pltpu.async_copy(x_hbm.at[pl.ds(off, SC_ROWS)], rows_v.at[b], rsem.at[b])
            for k in range(TOP_K):
                pltpu.sync_copy(d_hbm.at[k, pl.ds(off, SC_ROWS)], idx[b][k])

        def wait_load(b):
            pltpu.make_async_copy(x_hbm.at[pl.ds(0, SC_ROWS)], rows_v.at[b], rsem.at[b]).wait()

        def scatter(b):
            for k in range(TOP_K):
                pltpu.async_copy(rows_v.at[b], out_hbm.at[idx[b][k]], wsem.at[b])

        def wait_scatter(b):
            for k in range(TOP_K):
                pltpu.make_async_copy(rows_v.at[b], out_hbm.at[idx[b][k]], wsem.at[b]).wait()

        load(0, 0)

        @pl.loop(0, n_chunks, step=2)
        def _(c0):
            for b in range(2):
                c = c0 + b
                wait_load(b)
                scatter(b)

                @pl.when(c + 1 < n_chunks)
                def _():
                    @pl.when(c >= 1)
                    def _():
                        wait_scatter(1 - b)
                    load(c + 1, 1 - b)

        wait_scatter(0)
        wait_scatter(1)

    return kern(xp, dest)


def _combine(y, idx):
    n = idx.shape[0]
    w = y.shape[1]
    rows = 2 * SC_ROWS
    per_worker = n // SC_WORKERS
    n_chunks = per_worker // rows
    assert per_worker * SC_WORKERS == n and n_chunks * rows == per_worker and n_chunks % 2 == 0

    idx_t = pltpu.VMEM((rows,), I32)

    @functools.partial(
        pl.kernel, mesh=_sc_mesh(),
        out_type=jax.ShapeDtypeStruct((n, w), y.dtype),
        scratch_types=[pltpu.VMEM((2, rows, w), y.dtype), idx_t, idx_t,
                       pltpu.SemaphoreType.DMA((2,)), pltpu.SemaphoreType.DMA((2,))],
    )
    def kern(y_hbm, i_hbm, out_hbm, rows_v, i0, i1, gsem, wsem):
        base = _sc_worker_base(per_worker)
        ibuf = (i0, i1)

        def gather(c, b):
            off = pl.multiple_of(base + c * rows, 8)
            pltpu.sync_copy(i_hbm.at[pl.ds(off, rows)], ibuf[b])
            pltpu.async_copy(y_hbm.at[ibuf[b]], rows_v.at[b], gsem.at[b])

        def wait_gather(b):
            pltpu.make_async_copy(y_hbm.at[ibuf[b]], rows_v.at[b], gsem.at[b]).wait()

        def write(c, b):
            off = pl.multiple_of(base + c * rows, 8)
            pltpu.async_copy(rows_v.at[b], out_hbm.at[pl.ds(off, rows)], wsem.at[b])

        def wait_write(b):
            pltpu.make_async_copy(rows_v.at[b], out_hbm.at[pl.ds(0, rows)], wsem.at[b]).wait()

        gather(0, 0)

        @pl.loop(0, n_chunks, step=2)
        def _(c0):
            for b in range(2):
                c = c0 + b

                @pl.when(c + 1 < n_chunks)
                def _():
                    @pl.when(c >= 1)
                    def _():
                        wait_write(1 - b)
                    gather(c + 1, 1 - b)

                wait_gather(b)
                write(c, b)

        wait_write(0)
        wait_write(1)

    return kern(y, idx)


def _expert_kernel(te_ref, tv_ref, nu_ref, tg_ref, tn_ref, x_hbm, w1_hbm, w3_hbm, w2_hbm, y_ref,
                   xbuf_ref, xsem, w1f_ref, w3f_ref, w2f_ref, wsem, w1b_ref, w3b_ref, w2b_ref,
                   *, layer):
    i = pl.program_id(0)
    n_used = nu_ref[0]
    ts = xbuf_ref.shape[1]

    def x_copy(tile):
        slot = tile % X_RING
        rows = pl.ds(pl.multiple_of(tile * ts, ts), ts)
        return pltpu.make_async_copy(x_hbm.at[rows], xbuf_ref.at[slot], xsem.at[slot])

    def w_copies(expert, slot):
        pairs = ((w1_hbm, w1f_ref), (w3_hbm, w3f_ref), (w2_hbm, w2f_ref))
        return [pltpu.make_async_copy(src.at[layer, expert], dst.at[slot], wsem.at[slot, j])
                for j, (src, dst) in enumerate(pairs)]

    @pl.when(jnp.logical_and(i == 0, n_used > 0))
    def _():
        for tile in range(X_RING - 1):
            @pl.when(tile < n_used)
            def _():
                x_copy(tile).start()
        for c in w_copies(te_ref[0], 0):
            c.start()

    @pl.when(i < n_used)
    def _():
        @pl.when(i + (X_RING - 1) < n_used)
        def _():
            x_copy(i + (X_RING - 1)).start()

        prev = te_ref[jnp.maximum(i - 1, 0)]

        @pl.when(jnp.logical_or(i == 0, te_ref[i] != prev))
        def _():
            slot = tg_ref[i] % 2
            for c in w_copies(te_ref[i], slot):
                c.wait()
            nxt = tn_ref[i]

            @pl.when(nxt >= 0)
            def _():
                for c in w_copies(nxt, 1 - slot):
                    c.start()

            w1b_ref[...] = w1f_ref[slot].astype(BF16)
            w3b_ref[...] = w3f_ref[slot].astype(BF16)
            w2b_ref[...] = w2f_ref[slot].astype(BF16)

        x_copy(i).wait()
        xu = xbuf_ref[i % X_RING]
        rowid = lax.broadcasted_iota(I32, xu.shape, 0)
        xu = jnp.where(rowid < tv_ref[i], xu, jnp.zeros_like(xu))
        x = _unpack_bf16_pairs(xu).astype(BF16)
        h1 = jnp.dot(x, w1b_ref[...], preferred_element_type=F32)
        h3 = jnp.dot(x, w3b_ref[...], preferred_element_type=F32)
        he = (h1 * jax.nn.sigmoid(h1)) * h3
        y = jnp.dot(he.astype(BF16), w2b_ref[...], preferred_element_type=F32)
        y_ref[...] = _pack_bf16_pairs(y)


def _experts(x_sorted, plan, w1, w3, w2, layer):
    n_slots, wp = x_sorted.shape
    _, n_e, d, d_e = w1.shape
    ts = SLOT_TILE
    any_spec = pl.BlockSpec(memory_space=pl.ANY)
    yspec = pl.BlockSpec((ts, wp),
                         lambda i, te, tv, nu, tg, tn: (jnp.minimum(i, jnp.maximum(nu[0] - 1, 0)), 0))
    return pl.pallas_call(
        functools.partial(_expert_kernel, layer=layer),
        grid_spec=pltpu.PrefetchScalarGridSpec(
            num_scalar_prefetch=len(plan),
            grid=(n_slots // ts,),
            in_specs=[any_spec] * 4,
            out_specs=yspec,
            scratch_shapes=[pltpu.VMEM((X_RING, ts, wp), x_sorted.dtype),
                            pltpu.SemaphoreType.DMA((X_RING,)),
                            pltpu.VMEM((2, d, d_e), F32), pltpu.VMEM((2, d, d_e), F32),
                            pltpu.VMEM((2, d_e, d), F32), pltpu.SemaphoreType.DMA((2, 3)),
                            pltpu.VMEM((d, d_e), BF16), pltpu.VMEM((d, d_e), BF16),
                            pltpu.VMEM((d_e, d), BF16)],
        ),
        out_shape=jax.ShapeDtypeStruct((n_slots, d // 2), U32),
        compiler_params=_params("arbitrary"),
        name="moe_experts",
    )(*plan, x_sorted, w1, w3, w2)


def _slot_plan(eids, ranks, counts, n_tiles):
    ts = SLOT_TILE
    cnt = counts[:, 0].astype(I32)
    padded = (cnt + ts - 1) // ts * ts
    e_ids = jnp.arange(N_EXPERTS, dtype=I32)
    ends = jnp.sum(jnp.where(e_ids[:, None] <= e_ids[None, :], padded[:, None], 0), axis=0)
    starts = ends - padded
    onehot = eids[None] == e_ids[:, None, None]
    dest = ranks + jnp.sum(jnp.where(onehot, starts[:, None, None], 0), axis=0)
    tile_lo = jnp.arange(n_tiles, dtype=I32) * ts
    tile_e = jnp.minimum(jnp.sum(tile_lo[:, None] >= ends[None, :], axis=1), N_EXPERTS - 1)
    tile_e = tile_e.astype(I32)
    valid_end = jnp.sum(jnp.where(tile_e[:, None] == e_ids[None, :], (starts + cnt)[None, :], 0),
                        axis=1)
    tile_valid = jnp.clip(valid_end - tile_lo, 0, ts).astype(I32)
    n_used = (ends[-1:] // ts).astype(I32)
    nonempty = cnt > 0
    earlier = jnp.logical_and(nonempty[None, :], e_ids[None, :] < tile_e[:, None])
    tile_grp = jnp.sum(earlier, axis=1).astype(I32)
    later = jnp.logical_and(nonempty[None, :], e_ids[None, :] > tile_e[:, None])
    nxt = jnp.min(jnp.where(later, e_ids[None, :], N_EXPERTS), axis=1)
    tile_next = jnp.where(nxt < N_EXPERTS, nxt, -1).astype(I32)
    return dest, (tile_e, tile_valid, n_used, tile_grp, tile_next)


class PendingTail(NamedTuple):
    operands: Callable
    rows: int
    row0: int
    width: int


def _post_value(h1_ref, y0_ref, y1_ref, gate_ref, p_ref, g_ref, wpg_ref, wpe_ref):
    gates = gate_ref[...]
    tm = gates.shape[1]
    gt = jnp.concatenate([gates, jnp.zeros((8 - TOP_K, tm), F32)], axis=0).T
    h2 = h1_ref[...] + (gt[:, 0:1] * _unpack_bf16_pairs(y0_ref[...]) +
                        gt[:, 1:2] * _unpack_bf16_pairs(y1_ref[...]))
    gate = jax.nn.sigmoid(jnp.dot(_rms(h2, g_ref[...]).astype(BF16), wpg_ref[...],
                                  preferred_element_type=F32))
    pe = jnp.dot(p_ref[...].astype(BF16), wpe_ref[...], preferred_element_type=F32)
    return h2 + gate * pe


def _post_operands(h1, yk, gates, p, g_pl, w_pg, w_pe, *, p_row0, row0=0):
    d = h1.shape[1]
    t = yk.shape[0] // TOP_K
    wy = yk.shape[1]

    def operands(tm):
        nt = t // tm
        blk0 = row0 // tm
        p_blk0 = (p_row0 + row0) // tm
        specs = [pl.BlockSpec((tm, d), lambda i: (i + blk0, 0)),
                 pl.BlockSpec((tm, wy), lambda i: (i, 0)),
                 pl.BlockSpec((tm, wy), lambda i: (i + nt, 0)),
                 pl.BlockSpec((TOP_K, tm), lambda i: (0, i + blk0)),
                 pl.BlockSpec((tm, p.shape[1]), lambda i: (i + p_blk0, 0)),
                 _const_spec((1, d)), _const_spec(w_pg.shape), _const_spec(w_pe.shape)]
        return specs, [h1, yk, yk, gates, p, g_pl, w_pg, w_pe]

    return PendingTail(operands, t, row0, d)


def _post_kernel(*refs):
    *post_refs, o_ref = refs
    o_ref[...] = _post_value(*post_refs)


def _post(pending, *, tm):
    specs, args = pending.operands(tm)
    t, d = pending.rows, pending.width
    return pl.pallas_call(
        _post_kernel,
        grid=(t // tm,),
        in_specs=specs,
        out_specs=pl.BlockSpec((tm, d), lambda i: (i, 0)),
        out_shape=jax.ShapeDtypeStruct((t, d), F32),
        compiler_params=_params("arbitrary"),
        name="layer_post",
    )(*args)


def _final_kernel(*refs, n_post):
    post_refs, gfin_ref, o_ref = refs[:n_post], refs[n_post], refs[-1]
    o_ref[...] = _rms(_post_value(*post_refs), gfin_ref[...])


def _final(pending, g_fin, *, tm, total_rows, out=None):
    specs, args = pending.operands(tm)
    d = pending.width
    blk0 = pending.row0 // tm
    in_specs = specs + [_const_spec((1, d))]
    operands = [*args, g_fin]
    aliases = {}
    if out is not None:
        aliases = {len(operands): 0}
        in_specs.append(pl.BlockSpec(memory_space=pl.ANY))
        operands.append(out)
    return pl.pallas_call(
        functools.partial(_final_kernel, n_post=len(args)),
        grid=(pending.rows // tm,),
        in_specs=in_specs,
        out_specs=pl.BlockSpec((tm, d), lambda i: (i + blk0, 0)),
        out_shape=jax.ShapeDtypeStruct((total_rows, d), F32),
        input_output_aliases=aliases,
        compiler_params=_params("arbitrary"),
        name="final_post",
    )(*operands)


def _router_weights(wc, bc, wf, bf):
    def lanes(coarse, fine):
        z = lambda n: jnp.zeros((coarse.shape[0], n), F32)
        return jnp.concatenate([coarse, z(FINE_ROW0 - N_GROUPS), fine,
                                z(ROUTER_LANES - FINE_ROW0 - N_EXPERTS)], axis=1)

    w = lanes(wc, wf)
    b = lanes(bc[None, :], bf[None, :])
    hi = w.astype(BF16)
    lo = (w - hi.astype(F32)).astype(BF16)
    return jnp.concatenate([hi, lo], axis=1), b


def _moe(xp, eids, ranks, counts, w1, w3, w2, layer, n_parts):
    t = xp.shape[0]
    n_tiles = TOP_K * t // SLOT_TILE + N_EXPERTS
    dest, plan = _slot_plan(eids, ranks, counts, n_tiles)
    x_sorted = _dispatch(xp, dest, n_tiles * SLOT_TILE)
    y_sorted = _experts(x_sorted, plan, w1, w3, w2, layer)
    tp = t // n_parts
    return [_combine(y_sorted, dest[:, s * tp:(s + 1) * tp].reshape(-1)) for s in range(n_parts)]


def kernel(x, p, norm_mix, norm_ffn, norm_pl, final_norm, w_in_even, conv_w_even, w_out_even, w_in_odd, g_v_odd, w_s_odd, b_s_odd, w_out_odd, router_c, router_c_b, router_f, router_f_b, moe_w1, moe_w3, moe_w2, w_pe, w_pg):
    batch, seq, d = x.shape
    depth = p.shape[0]
    t = batch * seq
    tm = min(TOKEN_TILE, seq)
    tm_big = min(LIGHT_TOKEN_TILE, seq)
    p_rows = p.reshape(depth * t, -1)
    row = lambda a: a.reshape(1, -1)
    h = x.reshape(t, d)
    for i in range(depth):
        j = i // 2
        wr, br = _router_weights(router_c[i], router_c_b[i], router_f[i], router_f_b[i])
        g_ffn = row(norm_ffn[i])
        if i % 2 == 0:
            if isinstance(h, list):
                h = _post(h[0], tm=tm_big)
            mixed = _even_layer(h, row(norm_mix[i]), w_in_even[j].astype(BF16), conv_w_even[j],
                                w_out_even[j].astype(BF16), g_ffn, wr, br,
                                batch=batch, seq=seq, tq=min(Q_TILE, seq // Q_SUBTILES))
        else:
            mixed = None
            for part in (h if isinstance(h, list) else [h]):
                mixed = _odd_mixer(part, row(norm_mix[i]), w_in_odd[j].astype(BF16),
                                   row(g_v_odd[j]), w_s_odd[j], b_s_odd[j].T,
                                   w_out_odd[j].astype(BF16), g_ffn, wr, br, tm=tm,
                                   total_rows=t, earlier=mixed)
        h1, xp, gates, eids, ranks, counts = mixed
        consumer_tile, parts = ((tm_big, FINAL_PARTS) if i == depth - 1 else
                                (tm, MIXER_PARTS) if i % 2 == 0 else (None, 1))
        n_parts = parts if consumer_tile and t % (parts * consumer_tile) == 0 else 1
        yks = _moe(xp, eids, ranks, counts, moe_w1, moe_w3, moe_w2, i, n_parts)
        h = [_post_operands(h1, yk, gates, p_rows, row(norm_pl[i]), w_pg[i].astype(BF16),
                            w_pe[i].astype(BF16), p_row0=i * t, row0=s * (t // n_parts))
             for s, yk in enumerate(yks)]
    out = None
    for tail in h:
        out = _final(tail, row(final_norm), tm=tm_big, total_rows=t, out=out)
    return out.reshape(batch, seq, d)
```

```python
import functools
from typing import Callable, NamedTuple

import jax
import jax.numpy as jnp
from jax import lax
from jax.experimental import pallas as pl
from jax.experimental.pallas import tpu as pltpu
from jax.experimental.pallas import tpu_sc as plsc

EPS = 1e-6
HEAD_DIM = 64
HEADS_PER_SLAB = 2
ATTN_SLABS = 4
CHUNK = 128
N_GROUPS = 4
EXP_PER_GROUP = 8
N_EXPERTS = N_GROUPS * EXP_PER_GROUP
TOP_K = 2
ROUTER_LANES = 128
FINE_ROW0 = 8
RANK_BLOCK = 256

V7X_VMEM_BYTES = 64 * 1024 * 1024
VMEM_LIMIT = V7X_VMEM_BYTES - 8 * 1024 * 1024
V7X_SC_CORES = 2
V7X_SC_SUBCORES = 16
SC_WORKERS = V7X_SC_CORES * V7X_SC_SUBCORES
SC_ROWS = 32

TOKEN_TILE = 512
LIGHT_TOKEN_TILE = 1024
Q_TILE = 64
WINDOW_TILES = 4
SLOT_TILE = 512
X_RING = 4
MIXER_PARTS = 2
FINAL_PARTS = 4

LOG_F32_TINY_BOUND = -88.0
OFF_PENALTY = 1e30
Q_SUBTILES = 4

F32 = jnp.float32
BF16 = jnp.bfloat16
U32 = jnp.uint32
I32 = jnp.int32


def _rms(x, g):
    return x * lax.rsqrt(jnp.mean(x * x, axis=-1, keepdims=True) + EPS) * g


def _const_spec(shape):
    nd = len(shape)
    return pl.BlockSpec(shape, lambda *_: (0,) * nd, pipeline_mode=pl.Buffered(1))


def _params(*sem):
    return pltpu.CompilerParams(dimension_semantics=sem, vmem_limit_bytes=VMEM_LIMIT)


def _even_mix_kernel(h_ref, g_ref, w_ref, cw_ref, ya_ref, yb_ref, zs_ref, q_ref, k_ref, v_ref,
                     acc_ref, c_ref, *, tq, a_width, sb_width):
    qi = pl.program_id(1)
    tm = h_ref.shape[0]
    xn = _rms(h_ref[...], g_ref[...]).astype(BF16)
    a3 = 3 * a_width

    @pl.when(qi == 0)
    def _():
        zs_ref[0:8, :] = jnp.zeros((8, a_width), F32)

    pa = jnp.dot(xn, w_ref[:, 0:a3], preferred_element_type=F32)
    z = pa[:, 2 * a_width:a3] * pa[:, 0:a_width]
    zs_ref[8:tm + 8, :] = z
    z1 = zs_ref[7:tm + 7, :]
    z2 = zs_ref[6:tm + 6, :]
    cw = cw_ref[...]
    conv = cw[0:1, :] * z2 + cw[1:2, :] * z1 + cw[2:3, :] * z
    ya_ref[...] = (pa[:, a_width:2 * a_width] * conv).astype(BF16)
    zs_ref[0:8, :] = zs_ref[tm:tm + 8, :]

    pq = jnp.dot(xn, w_ref[:, a3:a3 + 3 * sb_width], preferred_element_type=F32)
    q_ref[...] = (pq[:, 0:sb_width] * (HEAD_DIM ** -0.5)).astype(BF16)
    seq_rows = pl.ds(pl.multiple_of(qi * tm, tm), tm)
    k_ref[seq_rows, :] = pq[:, sb_width:2 * sb_width].astype(BF16)
    v_ref[seq_rows, :] = pq[:, 2 * sb_width:3 * sb_width].astype(BF16)
    _attn_tiles(q_ref, k_ref, v_ref, yb_ref, acc_ref, c_ref, qi, tq=tq)


def _even_mix(h, g, w_in, conv_w, *, batch, seq, tq):
    t, d = h.shape
    a_width = conv_w.shape[1]
    sb_width = (w_in.shape[1] - 3 * a_width) // 3
    assert sb_width == ATTN_SLABS * HEADS_PER_SLAB * HEAD_DIM
    tm = Q_SUBTILES * tq
    nq = seq // tm
    rows = Q_SUBTILES * ATTN_SLABS * HEADS_PER_SLAB * tq
    row = lambda w: pl.BlockSpec((tm, w), lambda b, i: (b * nq + i, 0))
    return pl.pallas_call(
        functools.partial(_even_mix_kernel, tq=tq, a_width=a_width, sb_width=sb_width),
        grid=(batch, nq),
        in_specs=[row(d), _const_spec((1, d)), _const_spec(w_in.shape), _const_spec(conv_w.shape)],
        out_specs=[row(a_width), row(sb_width)],
        out_shape=[jax.ShapeDtypeStruct((t, a_width), BF16),
                   jax.ShapeDtypeStruct((t, sb_width), BF16)],
        scratch_shapes=[pltpu.VMEM((tm + 8, a_width), F32),
                        pltpu.VMEM((tm, sb_width), BF16),
                        pltpu.VMEM((seq, sb_width), BF16), pltpu.VMEM((seq, sb_width), BF16),
                        pltpu.VMEM((rows, HEADS_PER_SLAB * HEAD_DIM), F32),
                        pltpu.VMEM((rows, 1), F32)],
        compiler_params=_params("arbitrary", "arbitrary"),
        name="even_mix",
    )(h, g, w_in, conv_w)


def _attn_tiles(q_ref, k_ref, v_ref, o_ref, acc_ref, c_ref, qi, *, tq):
    slab = HEADS_PER_SLAB * HEAD_DIM
    rows_unit = HEADS_PER_SLAB * tq
    rows_tile = ATTN_SLABS * rows_unit
    rows = Q_SUBTILES * rows_tile
    units = [(j, sl) for j in range(Q_SUBTILES) for sl in range(ATTN_SLABS)]
    lane = lax.broadcasted_iota(I32, (tq, slab), 1)
    qs = []
    for j, sl in units:
        q = q_ref[j * tq:(j + 1) * tq, sl * slab:(sl + 1) * slab]
        zero = jnp.zeros_like(q)
        qs.append(jnp.concatenate([jnp.where(lane < HEAD_DIM, q, zero),
                                   jnp.where(lane >= HEAD_DIM, q, zero)], axis=0))

    def later_keys(width):
        r_i = lax.broadcasted_iota(I32, (width, width), 0)
        c_i = lax.broadcasted_iota(I32, (width, width), 1)
        return jnp.where(r_i > c_i, 1.0, 0.0).astype(BF16)

    def per_tile_column(vals):
        return jnp.concatenate([jnp.full((rows_tile, 1), v, F32) for v in vals], axis=0)

    def block(starts, width, c, mask):
        upper = later_keys(width)
        keys = [pl.ds(pl.multiple_of(s, tq), width) for s in starts]
        z = jnp.concatenate(
            [lax.dot_general(qs[u], k_ref[keys[j], sl * slab:(sl + 1) * slab],
                             (((1,), (1,)), ((), ())), preferred_element_type=F32)
             for u, (j, sl) in enumerate(units)], axis=0)
        sp = jnp.maximum(z, 0.0) + jnp.log(1.0 + jnp.exp(-jnp.abs(z)))
        spm = sp if mask is None else jnp.where(mask, sp, 0.0)
        hi = spm.astype(BF16)
        lo = (spm - hi.astype(F32)).astype(BF16)
        later = (jnp.dot(hi, upper, preferred_element_type=F32) +
                 jnp.dot(lo, upper, preferred_element_type=F32))
        a = jnp.exp(z - sp - later - c)
        if mask is not None:
            a = jnp.where(mask, a, 0.0)
        a = a.astype(BF16)
        out = jnp.concatenate(
            [jnp.dot(a[u * rows_unit:(u + 1) * rows_unit], v_ref[keys[j], sl * slab:(sl + 1) * slab],
                     preferred_element_type=F32) for u, (j, sl) in enumerate(units)], axis=0)
        return out, jnp.sum(spm, axis=1, keepdims=True)

    tile0 = qi * Q_SUBTILES
    win = WINDOW_TILES * tq
    firsts = [jnp.maximum(tile0 + j - (WINDOW_TILES - 1), 0) * tq for j in range(Q_SUBTILES)]
    col = lax.broadcasted_iota(I32, (tq, win), 1)
    row = lax.broadcasted_iota(I32, (tq, win), 0)
    mask = jnp.concatenate(
        [col + (firsts[j] - (tile0 + j) * tq) < row
         for j in range(Q_SUBTILES) for _ in range(rows_tile // tq)], axis=0)
    out0, c2 = block(firsts, win, jnp.zeros((rows, 1), F32), mask)
    acc_ref[...] = out0
    c_ref[...] = c2

    def cond(carry):
        n, cmin = carry
        return jnp.logical_and(tile0 + (Q_SUBTILES - 1) - WINDOW_TILES - n >= 0,
                               cmin < -LOG_F32_TINY_BOUND)

    def body(carry):
        n, _ = carry
        kbs = [tile0 + j - WINDOW_TILES - n for j in range(Q_SUBTILES)]
        c = c_ref[...]
        off = per_tile_column([jnp.where(kb >= 0, 0.0, OFF_PENALTY) for kb in kbs])
        out, ssum = block([jnp.maximum(kb, 0) * tq for kb in kbs], tq, c + off, None)
        acc_ref[...] += out
        c_new = c + ssum
        c_ref[...] = c_new
        return n + 1, jnp.min(c_new)

    lax.while_loop(cond, body, (jnp.int32(0), jnp.min(c2)))
    acc = acc_ref[...]
    for u, (j, sl) in enumerate(units):
        lo_r = u * rows_unit
        o_ref[j * tq:(j + 1) * tq, sl * slab:(sl + 1) * slab] = jnp.where(
            lane < HEAD_DIM, acc[lo_r:lo_r + tq], acc[lo_r + tq:lo_r + 2 * tq]).astype(o_ref.dtype)


def _pack_bf16_pairs(x):
    n = x.shape[1] // 2
    bits = lax.bitcast_convert_type(x.astype(BF16).astype(F32), U32)
    return bits[:, 0:n] | (bits[:, n:] >> 16)


def _unpack_bf16_pairs(u):
    a = lax.bitcast_convert_type(u & jnp.uint32(0xFFFF0000), F32)
    b = lax.bitcast_convert_type(u << 16, F32)
    return jnp.concatenate([a, b], axis=1)


def _router_tail(h1, g_ref, wr_ref, br_ref, cnt0_ref,
                 h1_ref, xp_ref, gate_ref, eid_ref, rank_ref, cnt_out_ref, cnt_ref):
    h1_ref[...] = h1
    xn = _rms(h1, g_ref[...])
    xp_ref[...] = _pack_bf16_pairs(xn)
    x_hi = xn.astype(BF16)
    x_lo = (xn - x_hi.astype(F32)).astype(BF16)
    both = jnp.dot(x_hi, wr_ref[...], preferred_element_type=F32)
    logits = (both[:, 0:ROUTER_LANES] + both[:, ROUTER_LANES:] +
              jnp.dot(x_lo, wr_ref[:, 0:ROUTER_LANES], preferred_element_type=F32) + br_ref[...])
    lt = logits.T
    tm = lt.shape[1]
    row = lax.broadcasted_iota(I32, (EXP_PER_GROUP, tm), 0)
    neg = jnp.float32(-jnp.inf)

    coarse = jnp.where(row < N_GROUPS, lt[0:EXP_PER_GROUP], neg)
    cmax = jnp.max(coarse, axis=0, keepdims=True)
    gi = jnp.min(jnp.where(coarse == cmax, row, EXP_PER_GROUP), axis=0, keepdims=True)
    pg = 1.0 / jnp.sum(jnp.exp(coarse - cmax), axis=0, keepdims=True)

    fine = lt[FINE_ROW0:FINE_ROW0 + EXP_PER_GROUP]
    for grp in range(1, N_GROUPS):
        lo_r = FINE_ROW0 + grp * EXP_PER_GROUP
        fine = jnp.where(gi == grp, lt[lo_r:lo_r + EXP_PER_GROUP], fine)
    m1 = jnp.max(fine, axis=0, keepdims=True)
    i1 = jnp.min(jnp.where(fine == m1, row, EXP_PER_GROUP), axis=0, keepdims=True)
    rest = jnp.where(row == i1, neg, fine)
    m2 = jnp.max(rest, axis=0, keepdims=True)
    i2 = jnp.min(jnp.where(rest == m2, row, EXP_PER_GROUP), axis=0, keepdims=True)
    e21 = jnp.exp(m2 - m1)
    w1 = 1.0 / (1.0 + e21)
    gate_ref[...] = jnp.concatenate([pg * w1, pg * (e21 * w1)], axis=0)
    e0 = gi * EXP_PER_GROUP + i1
    e1 = gi * EXP_PER_GROUP + i2
    eid_ref[...] = jnp.concatenate([e0, e1], axis=0)

    @pl.when(pl.program_id(0) == 0)
    def _():
        cnt_ref[...] = cnt0_ref[...]

    erow = lax.broadcasted_iota(I32, (N_EXPERTS, tm), 0)
    oh0 = jnp.where(erow == e0, 1.0, 0.0)
    oh1 = jnp.where(erow == e1, 1.0, 0.0)
    pb = min(RANK_BLOCK, tm)
    r_i = lax.broadcasted_iota(I32, (pb, pb), 0)
    c_i = lax.broadcasted_iota(I32, (pb, pb), 1)
    before = jnp.where(r_i < c_i, 1.0, 0.0).astype(BF16)
    oh = jnp.concatenate([oh0, oh1], axis=0)
    pres, run = [], jnp.zeros((TOP_K * N_EXPERTS, 1), F32)
    for lo_l in range(0, tm, pb):
        ohb = oh[:, lo_l:lo_l + pb]
        pres.append(jnp.dot(ohb.astype(BF16), before, preferred_element_type=F32) + run)
        run = run + jnp.sum(ohb, axis=1, keepdims=True)
    pre = jnp.concatenate(pres, axis=1)
    pre0, pre1 = pre[0:N_EXPERTS], pre[N_EXPERTS:]
    tot0, tot1 = run[0:N_EXPERTS], run[N_EXPERTS:]
    base = cnt_ref[:, 0:1]
    r0 = jnp.sum(oh0 * (pre0 + base), axis=0, keepdims=True)
    r1 = jnp.sum(oh1 * (pre1 + (base + tot0)), axis=0, keepdims=True)
    rank_ref[...] = jnp.concatenate([r0, r1], axis=0).astype(I32)
    total = jnp.broadcast_to(base + tot0 + tot1, cnt_ref.shape)
    cnt_ref[...] = total
    cnt_out_ref[...] = total


N_TAIL_ROW_OUTPUTS = 5


def _tail_specs(total_rows, d, tm, row0=0):
    blk0 = row0 // tm
    t = total_rows
    row = lambda w: pl.BlockSpec((tm, w), lambda i: (i + blk0, 0))
    lanes = pl.BlockSpec((TOP_K, tm), lambda i: (0, i + blk0))
    cnt = pl.BlockSpec((N_EXPERTS, ROUTER_LANES), lambda i: (0, 0))
    out_specs = [row(d), row(d // 2), lanes, lanes, lanes, cnt]
    out_shape = [jax.ShapeDtypeStruct((t, d), F32), jax.ShapeDtypeStruct((t, d // 2), U32),
                 jax.ShapeDtypeStruct((TOP_K, t), F32), jax.ShapeDtypeStruct((TOP_K, t), I32),
                 jax.ShapeDtypeStruct((TOP_K, t), I32),
                 jax.ShapeDtypeStruct((N_EXPERTS, ROUTER_LANES), F32)]
    scratch = [pltpu.VMEM((N_EXPERTS, ROUTER_LANES), F32)]
    return out_specs, out_shape, scratch


def _no_slots():
    return jnp.zeros((N_EXPERTS, ROUTER_LANES), F32)


def _even_out_kernel(ya_ref, yb_ref, h_ref, w_ref, g_ref, wr_ref, br_ref, cnt0_ref, *tail_refs):
    a_width = ya_ref.shape[1]
    mix = (jnp.dot(ya_ref[...], w_ref[0:a_width, :], preferred_element_type=F32) +
           jnp.dot(yb_ref[...], w_ref[a_width:, :], preferred_element_type=F32))
    _router_tail(h_ref[...] + mix, g_ref, wr_ref, br_ref, cnt0_ref, *tail_refs)


def _even_out(ya, yb, h, w_out, g_ffn, wr, br, *, tm):
    t, d = h.shape
    out_specs, out_shape, scratch = _tail_specs(t, d, tm)
    row = lambda w: pl.BlockSpec((tm, w), lambda i: (i, 0))
    cnt0 = _no_slots()
    return pl.pallas_call(
        _even_out_kernel,
        grid=(t // tm,),
        in_specs=[row(ya.shape[1]), row(yb.shape[1]), row(d), _const_spec(w_out.shape),
                  _const_spec((1, d)), _const_spec(wr.shape), _const_spec(br.shape),
                  _const_spec(cnt0.shape)],
        out_specs=out_specs,
        out_shape=out_shape,
        scratch_shapes=scratch,
        compiler_params=_params("arbitrary"),
        name="even_out_router",
    )(ya, yb, h, w_out, g_ffn, wr, br, cnt0)


def _gelu_tanh(x):
    return 0.5 * x * (1.0 + jnp.tanh(0.7978845608028654 * (x + 0.044715 * (x * x * x))))


def _odd_kernel(*refs, n_prev, n_alias):
    h_refs, refs = refs[:max(n_prev, 1)], refs[max(n_prev, 1):]
    (g_ref, w_in_ref, gv_ref, ws_ref, bs_ref, w_out_ref, gf_ref, wr_ref, br_ref, cnt0_ref,
     *rest) = refs
    *tail_refs, y_ref = rest[n_alias:]
    h = _post_value(*h_refs) if n_prev else h_refs[0][...]
    tm = h.shape[0]
    c_width = gv_ref.shape[1]
    n_grp = ws_ref.shape[0]
    gdim = c_width // n_grp
    xn = _rms(h, g_ref[...]).astype(BF16)
    u = _gelu_tanh(jnp.dot(xn, w_in_ref[:, 0:c_width], preferred_element_type=F32))
    v = _gelu_tanh(jnp.dot(xn, w_in_ref[:, c_width:], preferred_element_type=F32))
    v = _rms(v, gv_ref[...]).astype(BF16)
    r_i = lax.broadcasted_iota(I32, (CHUNK, CHUNK), 0)
    c_i = lax.broadcasted_iota(I32, (CHUNK, CHUNK), 1)
    causal = c_i <= r_i
    bs = bs_ref[...]
    for grp in range(n_grp):
        w_m = jnp.where(causal, ws_ref[grp], 0.0).astype(BF16)
        bias = bs[:, grp:grp + 1]
        cols = slice(grp * gdim, (grp + 1) * gdim)
        for c in range(tm // CHUNK):
            rows = slice(c * CHUNK, (c + 1) * CHUNK)
            gate = jnp.dot(w_m, v[rows, cols], preferred_element_type=F32) + bias
            y_ref[rows, cols] = (u[rows, cols] * gate).astype(BF16)
    mix = jnp.dot(y_ref[...], w_out_ref[...], preferred_element_type=F32)
    _router_tail(h + mix, gf_ref, wr_ref, br_ref, cnt0_ref, *tail_refs)


def _odd_mixer(h_or_prev, g_mix, w_in, g_v, w_s, b_s_t, w_out, g_ffn, wr, br, *, tm,
               total_rows=None, earlier=None):
    cnt0 = _no_slots() if earlier is None else earlier[N_TAIL_ROW_OUTPUTS]
    consts = [g_mix, w_in, g_v, w_s, b_s_t, w_out, g_ffn, wr, br, cnt0]
    pending = isinstance(h_or_prev, PendingTail)
    if pending:
        head_specs, head_args = h_or_prev.operands(tm)
        t, d, row0 = h_or_prev.rows, h_or_prev.width, h_or_prev.row0
    else:
        t, d = h_or_prev.shape
        row0 = 0
        head_specs, head_args = [pl.BlockSpec((tm, d), lambda i: (i, 0))], [h_or_prev]
    out_specs, out_shape, scratch = _tail_specs(total_rows or t, d, tm, row0)
    n_prev = len(head_args) if pending else 0
    in_specs = head_specs + [_const_spec(c.shape) for c in consts]
    operands = [*head_args, *consts]
    aliases = {}
    if earlier is not None:
        aliases = {len(operands) + k: k for k in range(N_TAIL_ROW_OUTPUTS)}
        in_specs += [pl.BlockSpec(memory_space=pl.ANY)] * N_TAIL_ROW_OUTPUTS
        operands += list(earlier[:N_TAIL_ROW_OUTPUTS])
    return pl.pallas_call(
        functools.partial(_odd_kernel, n_prev=n_prev, n_alias=len(aliases)),
        grid=(t // tm,),
        in_specs=in_specs,
        out_specs=out_specs,
        out_shape=out_shape,
        input_output_aliases=aliases,
        scratch_shapes=scratch + [pltpu.VMEM((tm, g_v.shape[1]), BF16)],
        compiler_params=_params("arbitrary"),
        name="odd_mixer_router",
    )(*operands)


def _sc_mesh():
    return plsc.VectorSubcoreMesh(core_axis_name="c", subcore_axis_name="s")


def _sc_worker_base(rows_per_worker):
    wid = lax.axis_index("s") * V7X_SC_CORES + lax.axis_index("c")
    return wid * rows_per_worker


def _dispatch(xp, dest, n_slots):
    t, w = xp.shape
    per_worker = t // SC_WORKERS
    n_chunks = per_worker // SC_ROWS
    assert per_worker * SC_WORKERS == t and n_chunks * SC_ROWS == per_worker and n_chunks % 2 == 0

    idx_t = pltpu.VMEM((SC_ROWS,), I32)

    @functools.partial(
        pl.kernel, mesh=_sc_mesh(),
        out_type=jax.ShapeDtypeStruct((n_slots, w), xp.dtype),
        scratch_types=[pltpu.VMEM((2, SC_ROWS, w), xp.dtype), idx_t, idx_t, idx_t, idx_t,
                       pltpu.SemaphoreType.DMA((2,)), pltpu.SemaphoreType.DMA((2,))],
    )
    def kern(x_hbm, d_hbm, out_hbm, rows_v, i00, i01, i10, i11, rsem, wsem):
        base = _sc_worker_base(per_worker)
        idx = ((i00, i01), (i10, i11))

        def load(c, b):
            off = pl.multiple_of(base + c * SC_ROWS, 8)
            pltpu.async_copy(x_hbm.at[pl.ds(off, SC_ROWS)], rows_v.at[b], rsem.at[b])
            for k in range(TOP_K):
                pltpu.sync_copy(d_hbm.at[k, pl.ds(off, SC_ROWS)], idx[b][k])

        def wait_load(b):
            pltpu.make_async_copy(x_hbm.at[pl.ds(0, SC_ROWS)], rows_v.at[b], rsem.at[b]).wait()

        def scatter(b):
            for k in range(TOP_K):
                pltpu.async_copy(rows_v.at[b], out_hbm.at[idx[b][k]], wsem.at[b])

        def wait_scatter(b):
            for k in range(TOP_K):
                pltpu.make_async_copy(rows_v.at[b], out_hbm.at[idx[b][k]], wsem.at[b]).wait()

        load(0, 0)

        @pl.loop(0, n_chunks, step=2)
        def _(c0):
            for b in range(2):
                c = c0 + b
                wait_load(b)
                scatter(b)

                @pl.when(c + 1 < n_chunks)
                def _():
                    @pl.when(c >= 1)
                    def _():
                        wait_scatter(1 - b)
                    load(c + 1, 1 - b)

        wait_scatter(0)
        wait_scatter(1)

    return kern(xp, dest)


def _combine(y, idx):
    n = idx.shape[0]
    w = y.shape[1]
    rows = 2 * SC_ROWS
    per_worker = n // SC_WORKERS
    n_chunks = per_worker // rows
    assert per_worker * SC_WORKERS == n and n_chunks * rows == per_worker and n_chunks % 2 == 0

    idx_t = pltpu.VMEM((rows,), I32)

    @functools.partial(
        pl.kernel, mesh=_sc_mesh(),
        out_type=jax.ShapeDtypeStruct((n, w), y.dtype),
        scratch_types=[pltpu.VMEM((2, rows, w), y.dtype), idx_t, idx_t,
                       pltpu.SemaphoreType.DMA((2,)), pltpu.SemaphoreType.DMA((2,))],
    )
    def kern(y_hbm, i_hbm, out_hbm, rows_v, i0, i1, gsem, wsem):
        base = _sc_worker_base(per_worker)
        ibuf = (i0, i1)

        def gather(c, b):
            off = pl.multiple_of(base + c * rows, 8)
            pltpu.sync_copy(i_hbm.at[pl.ds(off, rows)], ibuf[b])
            pltpu.async_copy(y_hbm.at[ibuf[b]], rows_v.at[b], gsem.at[b])

        def wait_gather(b):
            pltpu.make_async_copy(y_hbm.at[ibuf[b]], rows_v.at[b], gsem.at[b]).wait()

        def write(c, b):
            off = pl.multiple_of(base + c * rows, 8)
            pltpu.async_copy(rows_v.at[b], out_hbm.at[pl.ds(off, rows)], wsem.at[b])

        def wait_write(b):
            pltpu.make_async_copy(rows_v.at[b], out_hbm.at[pl.ds(0, rows)], wsem.at[b]).wait()

        gather(0, 0)

        @pl.loop(0, n_chunks, step=2)
        def _(c0):
            for b in range(2):
                c = c0 + b

                @pl.when(c + 1 < n_chunks)
                def _():
                    @pl.when(c >= 1)
                    def _():
                        wait_write(1 - b)
                    gather(c + 1, 1 - b)

                wait_gather(b)
                write(c, b)

        wait_write(0)
        wait_write(1)

    return kern(y, idx)


def _expert_kernel(te_ref, tv_ref, nu_ref, tg_ref, tn_ref, x_hbm, w1_hbm, w3_hbm, w2_hbm, y_ref,
                   xbuf_ref, xsem, w1f_ref, w3f_ref, w2f_ref, wsem, w1b_ref, w3b_ref, w2b_ref,
                   *, layer):
    i = pl.program_id(0)
    n_used = nu_ref[0]
    ts = xbuf_ref.shape[1]

    def x_copy(tile):
        slot = tile % X_RING
        rows = pl.ds(pl.multiple_of(tile * ts, ts), ts)
        return pltpu.make_async_copy(x_hbm.at[rows], xbuf_ref.at[slot], xsem.at[slot])

    def w_copies(expert, slot):
        pairs = ((w1_hbm, w1f_ref), (w3_hbm, w3f_ref), (w2_hbm, w2f_ref))
        return [pltpu.make_async_copy(src.at[layer, expert], dst.at[slot], wsem.at[slot, j])
                for j, (src, dst) in enumerate(pairs)]

    @pl.when(jnp.logical_and(i == 0, n_used > 0))
    def _():
        for tile in range(X_RING - 1):
            @pl.when(tile < n_used)
            def _():
                x_copy(tile).start()
        for c in w_copies(te_ref[0], 0):
            c.start()

    @pl.when(i < n_used)
    def _():
        @pl.when(i + (X_RING - 1) < n_used)
        def _():
            x_copy(i + (X_RING - 1)).start()

        prev = te_ref[jnp.maximum(i - 1, 0)]

        @pl.when(jnp.logical_or(i == 0, te_ref[i] != prev))
        def _():
            slot = tg_ref[i] % 2
            for c in w_copies(te_ref[i], slot):
                c.wait()
            nxt = tn_ref[i]

            @pl.when(nxt >= 0)
            def _():
                for c in w_copies(nxt, 1 - slot):
                    c.start()

            w1b_ref[...] = w1f_ref[slot].astype(BF16)
            w3b_ref[...] = w3f_ref[slot].astype(BF16)
            w2b_ref[...] = w2f_ref[slot].astype(BF16)

        x_copy(i).wait()
        xu = xbuf_ref[i % X_RING]
        rowid = lax.broadcasted_iota(I32, xu.shape, 0)
        xu = jnp.where(rowid < tv_ref[i], xu, jnp.zeros_like(xu))
        x = _unpack_bf16_pairs(xu).astype(BF16)
        h1 = jnp.dot(x, w1b_ref[...], preferred_element_type=F32)
        h3 = jnp.dot(x, w3b_ref[...], preferred_element_type=F32)
        he = (h1 * jax.nn.sigmoid(h1)) * h3
        y = jnp.dot(he.astype(BF16), w2b_ref[...], preferred_element_type=F32)
        y_ref[...] = _pack_bf16_pairs(y)


def _experts(x_sorted, plan, w1, w3, w2, layer):
    n_slots, wp = x_sorted.shape
    _, n_e, d, d_e = w1.shape
    ts = SLOT_TILE
    any_spec = pl.BlockSpec(memory_space=pl.ANY)
    yspec = pl.BlockSpec((ts, wp),
                         lambda i, te, tv, nu, tg, tn: (jnp.minimum(i, jnp.maximum(nu[0] - 1, 0)), 0))
    return pl.pallas_call(
        functools.partial(_expert_kernel, layer=layer),
        grid_spec=pltpu.PrefetchScalarGridSpec(
            num_scalar_prefetch=len(plan),
            grid=(n_slots // ts,),
            in_specs=[any_spec] * 4,
            out_specs=yspec,
            scratch_shapes=[pltpu.VMEM((X_RING, ts, wp), x_sorted.dtype),
                            pltpu.SemaphoreType.DMA((X_RING,)),
                            pltpu.VMEM((2, d, d_e), F32), pltpu.VMEM((2, d, d_e), F32),
                            pltpu.VMEM((2, d_e, d), F32), pltpu.SemaphoreType.DMA((2, 3)),
                            pltpu.VMEM((d, d_e), BF16), pltpu.VMEM((d, d_e), BF16),
                            pltpu.VMEM((d_e, d), BF16)],
        ),
        out_shape=jax.ShapeDtypeStruct((n_slots, d // 2), U32),
        compiler_params=_params("arbitrary"),
        name="moe_experts",
    )(*plan, x_sorted, w1, w3, w2)


def _slot_plan(eids, ranks, counts, n_tiles):
    ts = SLOT_TILE
    cnt = counts[:, 0].astype(I32)
    padded = (cnt + ts - 1) // ts * ts
    e_ids = jnp.arange(N_EXPERTS, dtype=I32)
    ends = jnp.sum(jnp.where(e_ids[:, None] <= e_ids[None, :], padded[:, None], 0), axis=0)
    starts = ends - padded
    onehot = eids[None] == e_ids[:, None, None]
    dest = ranks + jnp.sum(jnp.where(onehot, starts[:, None, None], 0), axis=0)
    tile_lo = jnp.arange(n_tiles, dtype=I32) * ts
    tile_e = jnp.minimum(jnp.sum(tile_lo[:, None] >= ends[None, :], axis=1), N_EXPERTS - 1)
    tile_e = tile_e.astype(I32)
    valid_end = jnp.sum(jnp.where(tile_e[:, None] == e_ids[None, :], (starts + cnt)[None, :], 0),
                        axis=1)
    tile_valid = jnp.clip(valid_end - tile_lo, 0, ts).astype(I32)
    n_used = (ends[-1:] // ts).astype(I32)
    nonempty = cnt > 0
    earlier = jnp.logical_and(nonempty[None, :], e_ids[None, :] < tile_e[:, None])
    tile_grp = jnp.sum(earlier, axis=1).astype(I32)
    later = jnp.logical_and(nonempty[None, :], e_ids[None, :] > tile_e[:, None])
    nxt = jnp.min(jnp.where(later, e_ids[None, :], N_EXPERTS), axis=1)
    tile_next = jnp.where(nxt < N_EXPERTS, nxt, -1).astype(I32)
    return dest, (tile_e, tile_valid, n_used, tile_grp, tile_next)


class PendingTail(NamedTuple):
    operands: Callable
    rows: int
    row0: int
    width: int


def _post_value(h1_ref, y0_ref, y1_ref, gate_ref, p_ref, g_ref, wpg_ref, wpe_ref):
    gates = gate_ref[...]
    tm = gates.shape[1]
    gt = jnp.concatenate([gates, jnp.zeros((8 - TOP_K, tm), F32)], axis=0).T
    h2 = h1_ref[...] + (gt[:, 0:1] * _unpack_bf16_pairs(y0_ref[...]) +
                        gt[:, 1:2] * _unpack_bf16_pairs(y1_ref[...]))
    gate = jax.nn.sigmoid(jnp.dot(_rms(h2, g_ref[...]).astype(BF16), wpg_ref[...],
                                  preferred_element_type=F32))
    pe = jnp.dot(p_ref[...].astype(BF16), wpe_ref[...], preferred_element_type=F32)
    return h2 + gate * pe


def _post_operands(h1, yk, gates, p, g_pl, w_pg, w_pe, *, p_row0, row0=0):
    d = h1.shape[1]
    t = yk.shape[0] // TOP_K
    wy = yk.shape[1]

    def operands(tm):
        nt = t // tm
        blk0 = row0 // tm
        p_blk0 = (p_row0 + row0) // tm
        specs = [pl.BlockSpec((tm, d), lambda i: (i + blk0, 0)),
                 pl.BlockSpec((tm, wy), lambda i: (i, 0)),
                 pl.BlockSpec((tm, wy), lambda i: (i + nt, 0)),
                 pl.BlockSpec((TOP_K, tm), lambda i: (0, i + blk0)),
                 pl.BlockSpec((tm, p.shape[1]), lambda i: (i + p_blk0, 0)),
                 _const_spec((1, d)), _const_spec(w_pg.shape), _const_spec(w_pe.shape)]
        return specs, [h1, yk, yk, gates, p, g_pl, w_pg, w_pe]

    return PendingTail(operands, t, row0, d)


def _post_kernel(*refs):
    *post_refs, o_ref = refs
    o_ref[...] = _post_value(*post_refs)


def _post(pending, *, tm):
    specs, args = pending.operands(tm)
    t, d = pending.rows, pending.width
    return pl.pallas_call(
        _post_kernel,
        grid=(t // tm,),
        in_specs=specs,
        out_specs=pl.BlockSpec((tm, d), lambda i: (i, 0)),
        out_shape=jax.ShapeDtypeStruct((t, d), F32),
        compiler_params=_params("arbitrary"),
        name="layer_post",
    )(*args)


def _final_kernel(*refs, n_post):
    post_refs, gfin_ref, o_ref = refs[:n_post], refs[n_post], refs[-1]
    o_ref[...] = _rms(_post_value(*post_refs), gfin_ref[...])


def _final(pending, g_fin, *, tm, total_rows, out=None):
    specs, args = pending.operands(tm)
    d = pending.width
    blk0 = pending.row0 // tm
    in_specs = specs + [_const_spec((1, d))]
    operands = [*args, g_fin]
    aliases = {}
    if out is not None:
        aliases = {len(operands): 0}
        in_specs.append(pl.BlockSpec(memory_space=pl.ANY))
        operands.append(out)
    return pl.pallas_call(
        functools.partial(_final_kernel, n_post=len(args)),
        grid=(pending.rows // tm,),
        in_specs=in_specs,
        out_specs=pl.BlockSpec((tm, d), lambda i: (i + blk0, 0)),
        out_shape=jax.ShapeDtypeStruct((total_rows, d), F32),
        input_output_aliases=aliases,
        compiler_params=_params("arbitrary"),
        name="final_post",
    )(*operands)


def _router_weights(wc, bc, wf, bf):
    def lanes(coarse, fine):
        z = lambda n: jnp.zeros((coarse.shape[0], n), F32)
        return jnp.concatenate([coarse, z(FINE_ROW0 - N_GROUPS), fine,
                                z(ROUTER_LANES - FINE_ROW0 - N_EXPERTS)], axis=1)

    w = lanes(wc, wf)
    b = lanes(bc[None, :], bf[None, :])
    hi = w.astype(BF16)
    lo = (w - hi.astype(F32)).astype(BF16)
    return jnp.concatenate([hi, lo], axis=1), b


def _moe(xp, eids, ranks, counts, w1, w3, w2, layer, n_parts):
    t = xp.shape[0]
    n_tiles = TOP_K * t // SLOT_TILE + N_EXPERTS
    dest, plan = _slot_plan(eids, ranks, counts, n_tiles)
    x_sorted = _dispatch(xp, dest, n_tiles * SLOT_TILE)
    y_sorted = _experts(x_sorted, plan, w1, w3, w2, layer)
    tp = t // n_parts
    return [_combine(y_sorted, dest[:, s * tp:(s + 1) * tp].reshape(-1)) for s in range(n_parts)]


def kernel(x, p, norm_mix, norm_ffn, norm_pl, final_norm, w_in_even, conv_w_even, w_out_even, w_in_odd, g_v_odd, w_s_odd, b_s_odd, w_out_odd, router_c, router_c_b, router_f, router_f_b, moe_w1, moe_w3, moe_w2, w_pe, w_pg):
    batch, seq, d = x.shape
    depth = p.shape[0]
    t = batch * seq
    tm = min(TOKEN_TILE, seq)
    tm_big = min(LIGHT_TOKEN_TILE, seq)
    p_rows = p.reshape(depth * t, -1)
    row = lambda a: a.reshape(1, -1)
    h = x.reshape(t, d)
    for i in range(depth):
        j = i // 2
        wr, br = _router_weights(router_c[i], router_c_b[i], router_f[i], router_f_b[i])
        g_ffn = row(norm_ffn[i])
        if i % 2 == 0:
            if isinstance(h, list):
                h = _post(h[0], tm=tm_big)
            ya, yb = _even_mix(h, row(norm_mix[i]), w_in_even[j].astype(BF16), conv_w_even[j],
                               batch=batch, seq=seq, tq=min(Q_TILE, seq // Q_SUBTILES))
            mixed = _even_out(ya, yb, h, w_out_even[j].astype(BF16), g_ffn, wr, br, tm=tm_big)
        else:
            mixed = None
            for part in (h if isinstance(h, list) else [h]):
                mixed = _odd_mixer(part, row(norm_mix[i]), w_in_odd[j].astype(BF16),
                                   row(g_v_odd[j]), w_s_odd[j], b_s_odd[j].T,
                                   w_out_odd[j].astype(BF16), g_ffn, wr, br, tm=tm,
                                   total_rows=t, earlier=mixed)
        h1, xp, gates, eids, ranks, counts = mixed
        consumer_tile, parts = ((tm_big, FINAL_PARTS) if i == depth - 1 else
                                (tm, MIXER_PARTS) if i % 2 == 0 else (None, 1))
        n_parts = parts if consumer_tile and t % (parts * consumer_tile) == 0 else 1
        yks = _moe(xp, eids, ranks, counts, moe_w1, moe_w3, moe_w2, i, n_parts)
        h = [_post_operands(h1, yk, gates, p_rows, row(norm_pl[i]), w_pg[i].astype(BF16),
                            w_pe[i].astype(BF16), p_row0=i * t, row0=s * (t // n_parts))
             for s, yk in enumerate(yks)]
    out = None
    for tail in h:
        out = _final(tail, row(final_norm), tm=tm_big, total_rows=t, out=out)
    return out.reshape(batch, seq, d)
```
